```python
import math
import jax, jax.numpy as jnp
from jax import lax
import numpy as np

D_MODEL = 1024
BATCH = 4
SEQ = 8192
DEPTH = 4

GRID_W = 64
CTX_LEN = 256
HEAD_DIM = 64
NA_WIDTH = D_MODEL // 2
NA_HEADS = NA_WIDTH // HEAD_DIM
NA_WIN_ROWS = 8
NA_WIN_COLS = 16
LRU_WIDTH = D_MODEL - NA_WIDTH
LRU_BLOCKS = 8
LRU_BLOCK = LRU_WIDTH // LRU_BLOCKS
CONV_W = 4
CONV_LEFT = (CONV_W - 1) // 2
LRU_C = 8.0
MIX_WIDTH = NA_WIDTH + LRU_WIDTH
IN_COLS = 3 * NA_WIDTH + 2 * LRU_WIDTH
SPLITS = (NA_WIDTH, 2 * NA_WIDTH, 3 * NA_WIDTH, 3 * NA_WIDTH + LRU_WIDTH)
N_EXPERTS = 16
N_GROUPS = 4
EXPERTS_PER_GROUP = N_EXPERTS // N_GROUPS
TOP_K = 2
EXPERT_FF = D_MODEL // 2
N_MOD = 6
ATTN_SCALE = HEAD_DIM ** -0.5
EPS = 1e-6
NEG_INF = -1e30

kernel_name = "hybrid_natten_rglru_groupmoe_dit"


def _rms(x, g):
    xf = x.astype(jnp.float32)
    y = xf * lax.rsqrt(jnp.mean(xf * xf, axis=-1, keepdims=True) + EPS)
    return (y * g.astype(jnp.float32)).astype(x.dtype)


def _ada(cvec, w, b):
    m = jax.nn.silu(cvec) @ w + b
    return jnp.split(m[..., None, :], N_MOD, axis=-1)


def _qk_heads(t, g):
    B, L, _ = t.shape
    t = _rms(t.reshape(B, L, NA_HEADS, HEAD_DIM), g)
    return t.transpose(0, 2, 1, 3)


def _heads(t):
    B, L, _ = t.shape
    return t.reshape(B, L, NA_HEADS, HEAD_DIM).transpose(0, 2, 1, 3)


def _merge(t):
    B, H, L, hd = t.shape
    return t.transpose(0, 2, 1, 3).reshape(B, L, H * hd)


def _na_mixer(qx, kx, vx, qc, kc, vc, bias_table):
    B, H, N, hd = qx.shape
    rows = N // GRID_W
    kh = min(NA_WIN_ROWS, rows)
    kw = NA_WIN_COLS
    r = jnp.arange(rows)
    row0 = jnp.clip(r - kh // 2, 0, rows - kh)
    key_rows = row0[:, None] + jnp.arange(kh)[None, :]
    col = jnp.arange(GRID_W)
    col0 = jnp.clip(col - kw // 2, 0, GRID_W - kw)
    in_win = (col[None, :] >= col0[:, None]) & (col[None, :] < col0[:, None] + kw)
    r_idx = key_rows - r[:, None] + (NA_WIN_ROWS - 1)
    c_idx = jnp.clip(col[None, :] - col[:, None], -(kw - 1), kw - 1) + (NA_WIN_COLS - 1)
    bias = bias_table[:, r_idx[:, None, :, None], c_idx[None, :, None, :]]
    bias = jnp.where(in_win[None, None, :, None, :], bias.astype(jnp.float32), NEG_INF)
    bias = bias.reshape(H, rows, GRID_W, kh * GRID_W)

    qg = qx.reshape(B, H, rows, GRID_W, hd)
    kg = kx.reshape(B, H, rows, GRID_W, hd)[:, :, key_rows].reshape(B, H, rows, kh * GRID_W, hd)
    vg = vx.reshape(B, H, rows, GRID_W, hd)[:, :, key_rows].reshape(B, H, rows, kh * GRID_W, hd)
    n_loc = kh * GRID_W
    s_loc = jnp.einsum('bhrqd,bhrkd->bhrqk', qg, kg).astype(jnp.float32) + bias[None]
    s_ctx = jnp.einsum('bhrqd,bhcd->bhrqc', qg, kc).astype(jnp.float32)
    p = jax.nn.softmax(jnp.concatenate([s_loc, s_ctx], axis=-1), axis=-1).astype(vx.dtype)
    o = (jnp.einsum('bhrqk,bhrkd->bhrqd', p[..., :n_loc], vg)
         + jnp.einsum('bhrqc,bhcd->bhrqd', p[..., n_loc:], vc))
    yx = _merge(o.reshape(B, H, N, hd))
    yc = None
    if qc is not None:
        pc = jax.nn.softmax(jnp.einsum('bhqd,bhkd->bhqk', qc, kc).astype(jnp.float32), axis=-1)
        yc = _merge(jnp.einsum('bhqk,bhkd->bhqd', pc.astype(vc.dtype), vc))
    return yx, yc


def _dwconv(u, w, b):
    L = u.shape[1]
    up = jnp.pad(u, ((0, 0), (CONV_LEFT, CONV_W - 1 - CONV_LEFT), (0, 0)))
    out = up[:, 0:L] * w[0]
    for k in range(1, CONV_W):
        out = out + up[:, k:k + L] * w[k]
    return out + b


def _block_diag(u, w, b):
    B, L, _ = u.shape
    ub = u.reshape(B, L, LRU_BLOCKS, LRU_BLOCK)
    return jnp.einsum('blnd,nde->blne', ub, w.astype(jnp.float32)).reshape(B, L, LRU_WIDTH) + b.astype(jnp.float32)


def _rglru_coeffs(u, w_r, b_r, w_i, b_i, lam):
    uf = u.astype(jnp.float32)
    r_t = jax.nn.sigmoid(_block_diag(uf, w_r, b_r))
    i_t = jax.nn.sigmoid(_block_diag(uf, w_i, b_i))
    log_a = -LRU_C * r_t * jax.nn.softplus(-lam.astype(jnp.float32))
    a = jnp.exp(log_a)
    b = jnp.sqrt(-jnp.expm1(2.0 * log_a)) * (i_t * uf)
    return a, b


def _comb(left, right):
    a_l, b_l = left
    a_r, b_r = right
    return a_l * a_r, a_r * b_l + b_r


def _linear_scan(a, b, h0, reverse):
    if reverse:
        a, b = jnp.flip(a, 1), jnp.flip(b, 1)
    A, H = lax.associative_scan(_comb, (a, b), axis=1)
    if h0 is not None:
        H = H + A * h0[:, None, :]
    final = H[:, -1]
    if reverse:
        H = jnp.flip(H, 1)
    return H, final


def _rglru_mixer(ux, gx, uc, gc, conv_w, conv_b, w_r, b_r, w_i, b_i, lam):
    ux = _dwconv(ux, conv_w, conv_b)
    uc = _dwconv(uc, conv_w, conv_b)
    hx_sum = None
    hc_sum = None
    for d, rev in enumerate((False, True)):
        ac, bc = _rglru_coeffs(uc, w_r[d], b_r[d], w_i[d], b_i[d], lam[d])
        Hc, hc_final = _linear_scan(ac, bc, None, rev)
        ax, bx = _rglru_coeffs(ux, w_r[d], b_r[d], w_i[d], b_i[d], lam[d])
        Hx, _ = _linear_scan(ax, bx, hc_final, rev)
        hx_sum = Hx if hx_sum is None else hx_sum + Hx
        hc_sum = Hc if hc_sum is None else hc_sum + Hc
    yx = (jax.nn.gelu(gx.astype(jnp.float32)) * hx_sum).astype(gx.dtype)
    yc = None
    if gc is not None:
        yc = (jax.nn.gelu(gc.astype(jnp.float32)) * hc_sum).astype(gc.dtype)
    return yx, yc


def _moe(h, router_w, router_b, w1, w3, w2):
    B, L, _ = h.shape
    aff = jax.nn.sigmoid(jnp.einsum('bld,de->ble', h.astype(jnp.float32), router_w.astype(jnp.float32)))
    sel = aff + router_b.astype(jnp.float32)
    grp_score = lax.top_k(sel.reshape(B, L, N_GROUPS, EXPERTS_PER_GROUP), TOP_K)[0].sum(-1)
    best = jnp.argmax(grp_score, axis=-1)
    expert_group = jnp.arange(N_EXPERTS) // EXPERTS_PER_GROUP
    sel = jnp.where(expert_group == best[..., None], sel, NEG_INF)
    _, idx = lax.top_k(sel, TOP_K)
    wts = jnp.take_along_axis(aff, idx, axis=-1)
    wts = wts / jnp.sum(wts, axis=-1, keepdims=True)
    gates = jnp.einsum('blk,blke->ble', wts, jax.nn.one_hot(idx, N_EXPERTS, dtype=jnp.float32)).astype(h.dtype)
    out = jnp.zeros_like(h)
    for e in range(N_EXPERTS):
        y = (jax.nn.silu(h @ w1[e]) * (h @ w3[e])) @ w2[e]
        out = out + gates[..., e:e + 1] * y
    return out


def setup_inputs(seed: int = 0) -> dict:
    key = jax.random.key(seed)
    ks = jax.random.split(key, 26)
    f32 = jnp.float32

    def nrm(k, shape, fan_in, scale=1.0):
        return jax.random.normal(k, shape, f32) * (scale * fan_in ** -0.5)

    x = jax.random.normal(ks[0], (BATCH, SEQ, D_MODEL), f32)
    c = jax.random.normal(ks[1], (BATCH, D_MODEL), f32)
    ctx = jax.random.normal(ks[2], (BATCH, CTX_LEN, D_MODEL), f32)
    c_ctx = jax.random.normal(ks[3], (D_MODEL,), f32)
    w_mod = nrm(ks[4], (DEPTH, D_MODEL, N_MOD * D_MODEL), D_MODEL, 0.5)
    b_mod = 0.02 * jax.random.normal(ks[5], (DEPTH, N_MOD * D_MODEL), f32)
    norm_mix = 1.0 + 0.05 * jax.random.normal(ks[6], (DEPTH, D_MODEL), f32)
    norm_ffn = 1.0 + 0.05 * jax.random.normal(ks[7], (DEPTH, D_MODEL), f32)
    w_in = nrm(ks[8], (DEPTH, D_MODEL, IN_COLS), D_MODEL)
    w_out = nrm(ks[9], (DEPTH, MIX_WIDTH, D_MODEL), MIX_WIDTH)
    q_gain = 1.0 + 0.05 * jax.random.normal(ks[10], (DEPTH, HEAD_DIM), f32)
    k_gain = 1.0 + 0.05 * jax.random.normal(ks[11], (DEPTH, HEAD_DIM), f32)
    na_bias = 0.1 * jax.random.normal(ks[12], (DEPTH, NA_HEADS, 2 * NA_WIN_ROWS - 1, 2 * NA_WIN_COLS - 1), f32)
    conv_w = nrm(ks[13], (DEPTH, CONV_W, LRU_WIDTH), CONV_W)
    conv_b = 0.02 * jax.random.normal(ks[14], (DEPTH, LRU_WIDTH), f32)
    lru_w_r = nrm(ks[15], (DEPTH, 2, LRU_BLOCKS, LRU_BLOCK, LRU_BLOCK), LRU_BLOCK)
    lru_b_r = 0.02 * jax.random.normal(ks[16], (DEPTH, 2, LRU_WIDTH), f32)
    lru_w_i = nrm(ks[17], (DEPTH, 2, LRU_BLOCKS, LRU_BLOCK, LRU_BLOCK), LRU_BLOCK)
    lru_b_i = 0.02 * jax.random.normal(ks[18], (DEPTH, 2, LRU_WIDTH), f32)
    a_c = jax.random.uniform(ks[19], (DEPTH, 2, LRU_WIDTH), f32, 0.9, 0.999)
    a0 = a_c ** (1.0 / LRU_C)
    lru_lambda = jnp.log(a0) - jnp.log1p(-a0)
    router_w = nrm(ks[20], (D_MODEL, N_EXPERTS), D_MODEL)
    router_b = 0.01 * jax.random.normal(ks[21], (N_EXPERTS,), f32)
    exp_w1 = nrm(ks[22], (DEPTH, N_EXPERTS, D_MODEL, EXPERT_FF), D_MODEL)
    exp_w3 = nrm(ks[23], (DEPTH, N_EXPERTS, D_MODEL, EXPERT_FF), D_MODEL)
    exp_w2 = nrm(ks[24], (DEPTH, N_EXPERTS, EXPERT_FF, D_MODEL), EXPERT_FF)
    return {"x": x, "c": c, "ctx": ctx, "c_ctx": c_ctx, "w_mod": w_mod, "b_mod": b_mod,
            "norm_mix": norm_mix, "norm_ffn": norm_ffn, "w_in": w_in, "w_out": w_out,
            "q_gain": q_gain, "k_gain": k_gain, "na_bias": na_bias, "conv_w": conv_w, "conv_b": conv_b,
            "lru_w_r": lru_w_r, "lru_b_r": lru_b_r, "lru_w_i": lru_w_i, "lru_b_i": lru_b_i,
            "lru_lambda": lru_lambda, "router_w": router_w, "router_b": router_b,
            "exp_w1": exp_w1, "exp_w3": exp_w3, "exp_w2": exp_w2}


def reference(x, c, ctx, c_ctx, w_mod, b_mod, norm_mix, norm_ffn, w_in, w_out, q_gain, k_gain,
              na_bias, conv_w, conv_b, lru_w_r, lru_b_r, lru_w_i, lru_b_i, lru_lambda,
              router_w, router_b, exp_w1, exp_w3, exp_w2):
    xc = ctx
    n_ctx = ctx.shape[1]
    for l in range(DEPTH):
        last = l == DEPTH - 1
        sh_ax, sc_ax, g_ax, sh_fx, sc_fx, g_fx = _ada(c, w_mod[l], b_mod[l])
        sh_ac, sc_ac, g_ac, sh_fc, sc_fc, g_fc = _ada(c_ctx, w_mod[l], b_mod[l])

        hx = _rms(x, norm_mix[l]) * (1 + sc_ax) + sh_ax
        hc = _rms(xc, norm_mix[l]) * (1 + sc_ac) + sh_ac
        qx, kx, vx, ux, gx = jnp.split(hx @ w_in[l], SPLITS, axis=-1)
        if last:
            kc, vc, uc = jnp.split(hc @ w_in[l][:, NA_WIDTH:3 * NA_WIDTH + LRU_WIDTH],
                                   (NA_WIDTH, 2 * NA_WIDTH), axis=-1)
            qc_h = None
            gc = None
        else:
            qc, kc, vc, uc, gc = jnp.split(hc @ w_in[l], SPLITS, axis=-1)
            qc_h = _qk_heads(qc, q_gain[l]) * ATTN_SCALE
        ya_x, ya_c = _na_mixer(_qk_heads(qx, q_gain[l]) * ATTN_SCALE, _qk_heads(kx, k_gain[l]), _heads(vx),
                               qc_h, _qk_heads(kc, k_gain[l]), _heads(vc), na_bias[l])
        yb_x, yb_c = _rglru_mixer(ux, gx, uc, gc, conv_w[l], conv_b[l], lru_w_r[l], lru_b_r[l],
                                  lru_w_i[l], lru_b_i[l], lru_lambda[l])
        x = x + g_ax * (jnp.concatenate([ya_x, yb_x], axis=-1) @ w_out[l])

        hx = _rms(x, norm_ffn[l]) * (1 + sc_fx) + sh_fx
        if last:
            x = x + g_fx * _moe(hx, router_w, router_b, exp_w1[l], exp_w3[l], exp_w2[l])
        else:
            xc = xc + g_ac * (jnp.concatenate([ya_c, yb_c], axis=-1) @ w_out[l])
            hc = _rms(xc, norm_ffn[l]) * (1 + sc_fc) + sh_fc
            f = _moe(jnp.concatenate([hc, hx], axis=1), router_w, router_b, exp_w1[l], exp_w3[l], exp_w2[l])
            xc = xc + g_fc * f[:, :n_ctx]
            x = x + g_fx * f[:, n_ctx:]
    return x
```

```python
import functools

import jax
import jax.numpy as jnp
import numpy as np
from jax import lax
from jax.experimental import pallas as pl
from jax.experimental.pallas import tpu as pltpu

F32 = jnp.float32
BF16 = jnp.bfloat16

D_MODEL = 1024
BATCH = 4
SEQ = 8192
DEPTH = 4
GRID_W = 64
CTX_LEN = 256
HEAD_DIM = 64
NA_WIDTH = 512
NA_HEADS = 8
NA_WIN_ROWS = 8
NA_WIN_COLS = 16
LRU_WIDTH = 512
LRU_BLOCKS = 8
LRU_BLOCK = 64
CONV_W = 4
LRU_C = 8.0
IN_COLS = 3 * NA_WIDTH + 2 * LRU_WIDTH
N_EXPERTS = 16
N_GROUPS = 4
EXPERTS_PER_GROUP = 4
EXPERT_FF = 512
N_MOD = 6
ATTN_SCALE = HEAD_DIM ** -0.5
EPS = 1e-6
NEG_INF = -1e30

TILE = 256
SUB = 8
GROUPS = TILE // SUB
L_TOT = CTX_LEN + SEQ
N_TILES = L_TOT // TILE
N_LAT_TILES = SEQ // TILE
N_TOK = BATCH * L_TOT
MOE_TILE = 1024
LRU_CH = 256
GATE_LANES = 128
MOD_ROWS = 8
VMEM_LIMIT = 56 * 1024 * 1024


def _cparams(sem):
    return pltpu.CompilerParams(dimension_semantics=sem, vmem_limit_bytes=VMEM_LIMIT)


def _mod_kernel(c_ref, w_ref, b_ref, o_ref):
    c = c_ref[...]
    s = c * jax.nn.sigmoid(c)
    o_ref[0] = jnp.dot(s.astype(BF16), w_ref[0].astype(BF16), preferred_element_type=F32) + b_ref[0]


def _modulation(cs, w_mod, b_mod):
    nb = N_MOD
    return pl.pallas_call(
        _mod_kernel,
        grid=(DEPTH, nb),
        in_specs=[
            pl.BlockSpec((MOD_ROWS, D_MODEL), lambda l, n: (0, 0)),
            pl.BlockSpec((1, D_MODEL, D_MODEL), lambda l, n: (l, 0, n)),
            pl.BlockSpec((1, 1, D_MODEL), lambda l, n: (l, 0, n)),
        ],
        out_specs=pl.BlockSpec((1, MOD_ROWS, D_MODEL), lambda l, n: (l, 0, n)),
        out_shape=jax.ShapeDtypeStruct((DEPTH, MOD_ROWS, N_MOD * D_MODEL), F32),
        compiler_params=_cparams(("arbitrary", "arbitrary")),
        name="modulation",
    )(cs, w_mod, b_mod.reshape(DEPTH, 1, N_MOD * D_MODEL))


def _mod_row(b, i):
    return jnp.where(i == 0, BATCH, b)


def _in_kernel(has_prev, *refs):
    if has_prev:
        (x_ref, f_ref, mprev_ref, mod_ref, nrm_ref, w_ref, bd_ref, qg_ref, kg_ref,
         xo_ref, q_ref, k_ref, v_ref, u_ref, g_ref) = refs
    else:
        (x_ref, mod_ref, nrm_ref, w_ref, bd_ref, qg_ref, kg_ref,
         q_ref, k_ref, v_ref, u_ref, g_ref) = refs
    x = x_ref[0]
    if has_prev:
        x = x + mprev_ref[0, 0][5:6, :] * f_ref[0].astype(F32)
        xo_ref[0] = x
    m = mod_ref[0, 0]
    ms = jnp.mean(x * x, axis=-1, keepdims=True)
    h = (x * lax.rsqrt(ms + EPS)) * nrm_ref[...]
    h = h * (1.0 + m[1:2, :]) + m[0:1, :]
    acc = jnp.dot(h.astype(BF16), w_ref[...], preferred_element_type=F32)
    bd = bd_ref[...]

    def head_norm(t, gain):
        ss = jnp.dot((t * t).astype(BF16), bd, preferred_element_type=F32)
        return (t * lax.rsqrt(ss + EPS)) * gain

    q_ref[0] = head_norm(acc[:, 0:NA_WIDTH], qg_ref[...]).astype(BF16)
    k_ref[0] = head_norm(acc[:, NA_WIDTH:2 * NA_WIDTH], kg_ref[...]).astype(BF16)
    v_ref[0] = acc[:, 2 * NA_WIDTH:3 * NA_WIDTH].astype(BF16)
    u_ref[0] = acc[:, 3 * NA_WIDTH:3 * NA_WIDTH + LRU_WIDTH].astype(BF16)
    g_ref[0] = acc[:, 3 * NA_WIDTH + LRU_WIDTH:].astype(BF16)


def _in_proj(l, xs, f_prev, mod, nrm, w_in, bd, qg, kg):
    has_prev = f_prev is not None
    tok = lambda width: pl.BlockSpec((1, TILE, width), lambda b, i: (b, i, 0))
    full = lambda shape: pl.BlockSpec(shape, lambda b, i: tuple(0 for _ in shape))
    mod_spec = lambda layer: pl.BlockSpec((1, 1, N_MOD, D_MODEL), lambda b, i: (layer, _mod_row(b, i), 0, 0))
    in_specs = [tok(D_MODEL)]
    args = [xs]
    if has_prev:
        in_specs += [tok(D_MODEL), mod_spec(l - 1)]
        args += [f_prev, mod]
    in_specs += [mod_spec(l), full((1, D_MODEL)), full((D_MODEL, IN_COLS)), full((NA_WIDTH, NA_WIDTH)),
                 full((1, NA_WIDTH)), full((1, NA_WIDTH))]
    args += [mod, nrm, w_in, bd, qg, kg]
    half = jax.ShapeDtypeStruct((BATCH, L_TOT, NA_WIDTH), BF16)
    out_shape = [half] * 5
    out_specs = [tok(NA_WIDTH)] * 5
    if has_prev:
        out_shape = [jax.ShapeDtypeStruct((BATCH, L_TOT, D_MODEL), F32)] + out_shape
        out_specs = [tok(D_MODEL)] + out_specs
    outs = pl.pallas_call(
        functools.partial(_in_kernel, has_prev),
        grid=(BATCH, N_TILES),
        in_specs=in_specs,
        out_specs=out_specs,
        out_shape=out_shape,
        compiler_params=_cparams(("parallel", "parallel")),
        name="in_proj",
    )(*args)
    if has_prev:
        return outs[0], outs[1:]
    return xs, outs


def _attn_kernel(q_ref, kp_ref, kc_ref, kn_ref, kx_ref, vp_ref, vc_ref, vn_ref, vx_ref, bias_ref, o_ref):
    q = q_ref[0]
    lane = lax.broadcasted_iota(jnp.int32, (TILE, 2 * HEAD_DIM), 1)
    low = lane < HEAD_DIM
    kb = [kp_ref[0], kc_ref[0], kn_ref[0], kx_ref[0]]
    vb = [vp_ref[0], vc_ref[0], vn_ref[0], vx_ref[0]]
    outs = []
    for hh in range(2):
        qh = jnp.where(low if hh == 0 else jnp.logical_not(low), q, jnp.zeros_like(q))
        s = [lax.dot_general(qh, kb[i], (((1,), (1,)), ((), ())), preferred_element_type=F32) for i in range(4)]
        for i in range(3):
            s[i] = s[i] + bias_ref[0, hh, :, i * TILE:(i + 1) * TILE]
        m = jnp.max(s[0], axis=-1, keepdims=True)
        for i in range(1, 4):
            m = jnp.maximum(m, jnp.max(s[i], axis=-1, keepdims=True))
        p = [jnp.exp(s[i] - m) for i in range(4)]
        den = jnp.sum(p[0], axis=-1, keepdims=True)
        for i in range(1, 4):
            den = den + jnp.sum(p[i], axis=-1, keepdims=True)
        o = jnp.dot(p[0].astype(BF16), vb[0], preferred_element_type=F32)
        for i in range(1, 4):
            o = o + jnp.dot(p[i].astype(BF16), vb[i], preferred_element_type=F32)
        outs.append(o / den)
    o_ref[0] = jnp.where(low, outs[0], outs[1]).astype(BF16)


def _attention(q, k, v, bias_tiles):
    n_hp = NA_HEADS // 2
    last = N_LAT_TILES - 1
    blk = (1, TILE, 2 * HEAD_DIM)
    prev_map = lambda b, hp, j: (b, 1 + jnp.clip(j - 2, 0, last), hp)
    cur_map = lambda b, hp, j: (b, jnp.maximum(j, 1), hp)
    next_map = lambda b, hp, j: (b, 1 + jnp.clip(j, 0, last), hp)
    ctx_map = lambda b, hp, j: (b, 0, hp)
    var_map = lambda b, hp, j: (jnp.where(j == 0, 3, jnp.where(j == 1, 0, jnp.where(j == N_TILES - 1, 2, 1))),
                                hp, 0, 0)
    kv_specs = [pl.BlockSpec(blk, prev_map), pl.BlockSpec(blk, cur_map), pl.BlockSpec(blk, next_map),
                pl.BlockSpec(blk, ctx_map)]
    return pl.pallas_call(
        _attn_kernel,
        grid=(BATCH, n_hp, N_TILES),
        in_specs=[pl.BlockSpec(blk, lambda b, hp, j: (b, j, hp))] + kv_specs + kv_specs
        + [pl.BlockSpec((1, 2, TILE, 3 * TILE), var_map)],
        out_specs=pl.BlockSpec(blk, lambda b, hp, j: (b, j, hp)),
        out_shape=jax.ShapeDtypeStruct((BATCH, L_TOT, NA_WIDTH), BF16),
        compiler_params=_cparams(("parallel", "parallel", "arbitrary")),
        name="na_attention",
    )(q, k, k, k, k, v, v, v, v, bias_tiles)


def _softplus(x):
    return jnp.maximum(x, 0.0) + jnp.log1p(jnp.exp(-jnp.abs(x)))


def _lru_kernel(u_ref, g_ref, cw_ref, cb_ref, wr_ref, wi_ref, br_ref, bi_ref, lam_ref, y_ref,
                hf_scr, hl_scr, p_scr):
    C = LRU_CH
    cw = cw_ref[...]
    cb = cb_ref[...]
    sub = lax.broadcasted_iota(jnp.int32, (SUB, C), 0)

    def conv_tile(c, r0):
        U = u_ref[0, pl.ds(r0, TILE), :].astype(F32)
        pstart = pl.multiple_of(jnp.maximum(r0 - 16, 0), 16)
        nstart = pl.multiple_of(jnp.minimum(r0 + TILE, L_TOT - 16), 16)
        prev16 = u_ref[0, pl.ds(pstart, 16), :].astype(F32)
        next16 = u_ref[0, pl.ds(nstart, 16), :].astype(F32)
        has_prev = (c >= 2).astype(F32)
        has_next = jnp.logical_and(c >= 1, c <= N_TILES - 2).astype(F32)
        prow = prev16[15:16, :] * has_prev
        n0 = next16[0:1, :] * has_next
        n8 = next16[8:9, :] * has_next
        first8 = jnp.where(sub == 0, prow, pltpu.roll(U[TILE - SUB:TILE, :], 1, 0))
        last_a = jnp.where(sub == SUB - 1, n0, pltpu.roll(U[0:SUB, :], SUB - 1, 0))
        last_b = jnp.where(sub == SUB - 1, n8, pltpu.roll(U[SUB:2 * SUB, :], SUB - 1, 0))
        um1 = jnp.concatenate([first8, U[0:TILE - SUB, :]], axis=0)
        up1 = jnp.concatenate([U[SUB:TILE, :], last_a], axis=0)
        up2 = jnp.concatenate([U[2 * SUB:TILE, :], last_a, last_b], axis=0)
        return cw[0:1, :] * um1 + cw[1:2, :] * U + cw[2:3, :] * up1 + cw[3:4, :] * up2 + cb

    def coeffs(v, d):
        vb = v.astype(BF16)
        r = jax.nn.sigmoid(jnp.dot(vb, wr_ref[d, 0], preferred_element_type=F32) + br_ref[d:d + 1, :])
        i = jax.nn.sigmoid(jnp.dot(vb, wi_ref[d, 0], preferred_element_type=F32) + bi_ref[d:d + 1, :])
        log_a = (-LRU_C * r) * _softplus(-lam_ref[d:d + 1, :])
        a = jnp.exp(log_a)
        b = jnp.sqrt(1.0 - a * a) * (i * v)
        return a, b

    def scan_tile(a, b, h_in, reverse):
        order = range(GROUPS - 1, -1, -1) if reverse else range(GROUPS)
        hl = None
        for g in order:
            ag = a[g * SUB:(g + 1) * SUB, :]
            bg = b[g * SUB:(g + 1) * SUB, :]
            if hl is None:
                hl, p = bg, ag
            else:
                hl = ag * hl + bg
                p = ag * p
            hl_scr[g * SUB:(g + 1) * SUB, :] = hl
            p_scr[g * SUB:(g + 1) * SUB, :] = p
        blocks = range(SUB - 1, -1, -1) if reverse else range(SUB)
        carry = h_in
        cins = {}
        for s in blocks:
            cins[s] = carry
            carry = hl[s:s + 1, :] + p[s:s + 1, :] * carry
        cin = jnp.concatenate([cins[s] for s in range(SUB)], axis=0)
        hfull = hl_scr[...] + p_scr[...] * jnp.tile(cin, (GROUPS, 1))
        return hfull, carry

    zero = jnp.zeros((1, C), F32)

    def fwd_body(c, h):
        r0 = pl.multiple_of(c * TILE, TILE)
        a, b = coeffs(conv_tile(c, r0), 0)
        hfull, h = scan_tile(a, b, h, False)
        hf_scr[pl.ds(r0, TILE), :] = hfull
        return h

    lax.fori_loop(0, N_TILES, fwd_body, zero)

    def rev_tile(c, h):
        r0 = pl.multiple_of(c * TILE, TILE)
        a, b = coeffs(conv_tile(c, r0), 1)
        hfull, h = scan_tile(a, b, h, True)
        gate = jax.nn.gelu(g_ref[0, pl.ds(r0, TILE), :].astype(F32))
        y_ref[0, pl.ds(r0, TILE), :] = (gate * (hf_scr[pl.ds(r0, TILE), :] + hfull)).astype(BF16)
        return h

    h_ctx = rev_tile(jnp.int32(0), zero)
    lax.fori_loop(0, N_LAT_TILES, lambda i, h: rev_tile(N_TILES - 1 - i, h), h_ctx)


def _rglru(u, g, cw, cb, wr, wi, br, bi, lam):
    n_cb = LRU_WIDTH // LRU_CH
    seq = pl.BlockSpec((1, L_TOT, LRU_CH), lambda b, c: (b, 0, c))
    vec = lambda rows: pl.BlockSpec((rows, LRU_CH), lambda b, c: (0, c))
    mat = pl.BlockSpec((2, 1, LRU_CH, LRU_CH), lambda b, c: (0, c, 0, 0))
    return pl.pallas_call(
        _lru_kernel,
        grid=(BATCH, n_cb),
        in_specs=[seq, seq, vec(CONV_W), vec(1), mat, mat, vec(2), vec(2), vec(2)],
        out_specs=seq,
        out_shape=jax.ShapeDtypeStruct((BATCH, L_TOT, LRU_WIDTH), BF16),
        scratch_shapes=[pltpu.VMEM((L_TOT, LRU_CH), F32), pltpu.VMEM((TILE, LRU_CH), F32),
                        pltpu.VMEM((TILE, LRU_CH), F32)],
        compiler_params=_cparams(("parallel", "parallel")),
        name="rglru",
    )(u, g, cw, cb, wr, wi, br, bi, lam)


def _route(sel, aff):
    def top2_sum(a, b, c, d):
        hi1, lo1 = jnp.maximum(a, b), jnp.minimum(a, b)
        hi2, lo2 = jnp.maximum(c, d), jnp.minimum(c, d)
        return jnp.maximum(hi1, hi2) + jnp.maximum(jnp.minimum(hi1, hi2), jnp.maximum(lo1, lo2))

    scores = [top2_sum(*sel[EXPERTS_PER_GROUP * g:EXPERTS_PER_GROUP * (g + 1)]) for g in range(N_GROUPS)]
    best = jnp.zeros_like(scores[0], dtype=jnp.int32)
    best_v = scores[0]
    for g in range(1, N_GROUPS):
        upd = scores[g] > best_v
        best = jnp.where(upd, g, best)
        best_v = jnp.where(upd, scores[g], best_v)
    chosen = []
    for e in range(N_EXPERTS):
        g = e // EXPERTS_PER_GROUP
        rank = jnp.zeros_like(best)
        for o in range(EXPERTS_PER_GROUP * g, EXPERTS_PER_GROUP * (g + 1)):
            if o == e:
                continue
            ahead = sel[o] > sel[e]
            if o < e:
                ahead = jnp.logical_or(ahead, sel[o] == sel[e])
            rank = rank + ahead.astype(jnp.int32)
        chosen.append(jnp.logical_and(best == g, rank < 2))
    total = jnp.zeros_like(aff[0])
    for e in range(N_EXPERTS):
        total = total + jnp.where(chosen[e], aff[e], 0.0)
    return [jnp.where(chosen[e], aff[e] / total, 0.0) for e in range(N_EXPERTS)]


def _out_kernel(x_ref, ya_ref, yb_ref, mod_ref, nrm_ref, w_ref, rwh_ref, rwl_ref, rb_ref,
                xo_ref, h_ref, gt_ref):
    m = mod_ref[0, 0]
    y = jnp.dot(ya_ref[0], w_ref[0:NA_WIDTH, :], preferred_element_type=F32)
    y = y + jnp.dot(yb_ref[0], w_ref[NA_WIDTH:, :], preferred_element_type=F32)
    x = x_ref[0] + m[2:3, :] * y
    xo_ref[0] = x
    ms = jnp.mean(x * x, axis=-1, keepdims=True)
    h = (x * lax.rsqrt(ms + EPS)) * nrm_ref[...]
    h = h * (1.0 + m[4:5, :]) + m[3:4, :]
    h_hi = h.astype(BF16)
    h_ref[0] = h_hi
    h_lo = (h - h_hi.astype(F32)).astype(BF16)
    nt = (((1,), (1,)), ((), ()))
    lg = (lax.dot_general(rwh_ref[...], h_hi, nt, preferred_element_type=F32)
          + lax.dot_general(rwh_ref[...], h_lo, nt, preferred_element_type=F32)
          + lax.dot_general(rwl_ref[...], h_hi, nt, preferred_element_type=F32))
    aff_all = jax.nn.sigmoid(lg)
    sel_all = aff_all + rb_ref[...]
    aff = [aff_all[e:e + 1, :] for e in range(N_EXPERTS)]
    sel = [sel_all[e:e + 1, :] for e in range(N_EXPERTS)]
    gates = _route(sel, aff)
    gt = jnp.concatenate(gates + [jnp.zeros((GATE_LANES - N_EXPERTS, TILE), F32)], axis=0)
    gt_ref[0] = jnp.transpose(gt)


def _out_proj(l, xs, ya, yb, mod, nrm, w_out, rwh, rwl, rb):
    tok = lambda width: pl.BlockSpec((1, TILE, width), lambda b, i: (b, i, 0))
    full = lambda shape: pl.BlockSpec(shape, lambda b, i: tuple(0 for _ in shape))
    return pl.pallas_call(
        _out_kernel,
        grid=(BATCH, N_TILES),
        in_specs=[tok(D_MODEL), tok(NA_WIDTH), tok(LRU_WIDTH),
                  pl.BlockSpec((1, 1, N_MOD, D_MODEL), lambda b, i: (l, _mod_row(b, i), 0, 0)),
                  full((1, D_MODEL)), full((D_MODEL, D_MODEL)), full((N_EXPERTS, D_MODEL)),
                  full((N_EXPERTS, D_MODEL)), full((N_EXPERTS, 1))],
        out_specs=[tok(D_MODEL), tok(D_MODEL), tok(GATE_LANES)],
        out_shape=[jax.ShapeDtypeStruct((BATCH, L_TOT, D_MODEL), F32),
                   jax.ShapeDtypeStruct((BATCH, L_TOT, D_MODEL), BF16),
                   jax.ShapeDtypeStruct((BATCH, L_TOT, GATE_LANES), F32)],
        compiler_params=_cparams(("parallel", "parallel")),
        name="out_proj_router",
    )(xs, ya, yb, mod, nrm, w_out, rwh, rwl, rb)


def _moe_kernel(h_ref, gt_ref, w1_ref, w3_ref, w2_ref, f_ref, acc_ref):
    e = pl.program_id(1)
    h = h_ref[...]
    a = jnp.dot(h, w1_ref[0], preferred_element_type=F32)
    b = jnp.dot(h, w3_ref[0], preferred_element_type=F32)
    t = ((a * jax.nn.sigmoid(a)) * b).astype(BF16)
    y = jnp.dot(t, w2_ref[0], preferred_element_type=F32)
    lane = lax.broadcasted_iota(jnp.int32, (MOE_TILE, GATE_LANES), 1)
    ge = jnp.sum(jnp.where(lane == e, gt_ref[...], 0.0), axis=-1, keepdims=True)

    @pl.when(e == 0)
    def _():
        acc_ref[...] = ge * y

    @pl.when(e > 0)
    def _():
        acc_ref[...] += ge * y

    @pl.when(e == N_EXPERTS - 1)
    def _():
        f_ref[...] = acc_ref[...].astype(BF16)


def _moe(h2, gates, w1, w3, w2):
    n_t = N_TOK // MOE_TILE
    return pl.pallas_call(
        _moe_kernel,
        grid=(n_t, N_EXPERTS),
        in_specs=[pl.BlockSpec((MOE_TILE, D_MODEL), lambda i, e: (i, 0)),
                  pl.BlockSpec((MOE_TILE, GATE_LANES), lambda i, e: (i, 0)),
                  pl.BlockSpec((1, D_MODEL, EXPERT_FF), lambda i, e: (e, 0, 0)),
                  pl.BlockSpec((1, D_MODEL, EXPERT_FF), lambda i, e: (e, 0, 0)),
                  pl.BlockSpec((1, EXPERT_FF, D_MODEL), lambda i, e: (e, 0, 0))],
        out_specs=pl.BlockSpec((MOE_TILE, D_MODEL), lambda i, e: (i, 0)),
        out_shape=jax.ShapeDtypeStruct((N_TOK, D_MODEL), BF16),
        scratch_shapes=[pltpu.VMEM((MOE_TILE, D_MODEL), F32)],
        compiler_params=_cparams(("parallel", "arbitrary")),
        name="moe_experts",
    )(h2, gates, w1, w3, w2)


def _final_kernel(x_ref, f_ref, mod_ref, o_ref):
    o_ref[0] = x_ref[0] + mod_ref[0, 0][5:6, :] * f_ref[0].astype(F32)


def _final(xs, f, mod):
    lat = lambda b, j: (b, j + 1, 0)
    return pl.pallas_call(
        _final_kernel,
        grid=(BATCH, N_LAT_TILES),
        in_specs=[pl.BlockSpec((1, TILE, D_MODEL), lat), pl.BlockSpec((1, TILE, D_MODEL), lat),
                  pl.BlockSpec((1, 1, N_MOD, D_MODEL), lambda b, j: (DEPTH - 1, b, 0, 0))],
        out_specs=pl.BlockSpec((1, TILE, D_MODEL), lambda b, j: (b, j, 0)),
        out_shape=jax.ShapeDtypeStruct((BATCH, SEQ, D_MODEL), F32),
        compiler_params=_cparams(("parallel", "parallel")),
        name="final_residual",
    )(xs, f, mod)


def _to_scan_major(t):
    b, n, d = t.shape
    return t.reshape(b, n // TILE, SUB, GROUPS, d).transpose(0, 1, 3, 2, 4).reshape(b, n, d)


def _from_scan_major(t):
    b, n, d = t.shape
    return t.reshape(b, n // TILE, GROUPS, SUB, d).transpose(0, 1, 3, 2, 4).reshape(b, n, d)


def _bias_tiles(table):
    rows_q = TILE // GRID_W
    rows_k = 3 * rows_q
    a = np.arange(rows_q)[:, None]
    c = np.arange(rows_k)[None, :]
    r_idx = np.clip(c - a + 3, 0, 2 * NA_WIN_ROWS - 2)
    qc = np.arange(GRID_W)[:, None]
    kc = np.arange(GRID_W)[None, :]
    col0 = np.clip(qc - NA_WIN_COLS // 2, 0, GRID_W - NA_WIN_COLS)
    in_win = (kc >= col0) & (kc < col0 + NA_WIN_COLS)
    c_idx = np.clip(kc - qc, -(NA_WIN_COLS - 1), NA_WIN_COLS - 1) + (NA_WIN_COLS - 1)
    blk = table[:, r_idx[:, :, None, None], c_idx[None, None, :, :]].astype(F32)
    allowed = np.stack([
        np.broadcast_to(c >= rows_q, (rows_q, rows_k)),
        (c >= a) & (c <= a + NA_WIN_ROWS - 1),
        np.broadcast_to(c < 2 * rows_q, (rows_q, rows_k)),
        np.zeros((rows_q, rows_k), bool),
    ])
    mask = allowed[:, None, :, :, None, None] & in_win[None, None, None, None, :, :]
    tiles = jnp.where(mask, blk[None], NEG_INF)
    tiles = tiles.transpose(0, 1, 2, 4, 3, 5).reshape(4, NA_HEADS, TILE, 3 * TILE)
    perm = (np.arange(TILE) % SUB) * GROUPS + np.arange(TILE) // SUB
    cols = (np.arange(3)[:, None] * TILE + perm[None, :]).reshape(-1)
    return tiles[:, :, perm, :][:, :, :, cols]


def _block_diag(w, n_chunks):
    per = LRU_BLOCKS // n_chunks
    w = w.reshape(2, n_chunks, per, LRU_BLOCK, LRU_BLOCK)
    eye = jnp.eye(per, dtype=w.dtype)
    out = jnp.einsum('dcpij,pq->dcpiqj', w, eye)
    return out.reshape(2, n_chunks, per * LRU_BLOCK, per * LRU_BLOCK)


def kernel(x, c, ctx, c_ctx, w_mod, b_mod, norm_mix, norm_ffn, w_in, w_out, q_gain, k_gain, na_bias,
           conv_w, conv_b, lru_w_r, lru_b_r, lru_w_i, lru_b_i, lru_lambda, router_w, router_b,
           exp_w1, exp_w3, exp_w2):
    cs = jnp.concatenate([c, c_ctx[None, :], jnp.zeros((MOD_ROWS - BATCH - 1, D_MODEL), F32)], axis=0)
    mod = _modulation(cs, w_mod, b_mod).reshape(DEPTH, MOD_ROWS, N_MOD, D_MODEL)

    xs = jnp.concatenate([_to_scan_major(ctx), _to_scan_major(x)], axis=1)
    head_of = np.arange(NA_WIDTH) // HEAD_DIM
    bd = jnp.asarray((head_of[:, None] == head_of[None, :]).astype(np.float32) / HEAD_DIM, BF16)
    rwt = router_w.T
    rwh = rwt.astype(BF16)
    rwl = (rwt - rwh.astype(F32)).astype(BF16)
    rb = router_b.reshape(N_EXPERTS, 1)
    n_cb = LRU_WIDTH // LRU_CH

    f = None
    for l in range(DEPTH):
        qg = jnp.tile(q_gain[l] * ATTN_SCALE, NA_HEADS)[None, :]
        kg = jnp.tile(k_gain[l], NA_HEADS)[None, :]
        xs, (q, k, v, u, g) = _in_proj(l, xs, f, mod, norm_mix[l][None, :], w_in[l].astype(BF16), bd, qg, kg)
        ya = _attention(q, k, v, _bias_tiles(na_bias[l]))
        yb = _rglru(u, g, conv_w[l], conv_b[l][None, :],
                    _block_diag(lru_w_r[l], n_cb).astype(BF16), _block_diag(lru_w_i[l], n_cb).astype(BF16),
                    lru_b_r[l], lru_b_i[l], lru_lambda[l])
        xs, h2, gates = _out_proj(l, xs, ya, yb, mod, norm_ffn[l][None, :], w_out[l].astype(BF16), rwh, rwl, rb)
        f = _moe(h2.reshape(N_TOK, D_MODEL), gates.reshape(N_TOK, GATE_LANES),
                 exp_w1[l].astype(BF16), exp_w3[l].astype(BF16), exp_w2[l].astype(BF16))
        f = f.reshape(BATCH, L_TOT, D_MODEL)
    out = _final(xs, f, mod)
    return _from_scan_major(out)
```

```python
import functools

import jax
import jax.numpy as jnp
import numpy as np
from jax import lax
from jax.experimental import pallas as pl
from jax.experimental.pallas import tpu as pltpu
from jax.experimental.pallas import tpu_sc as plsc

F32 = jnp.float32
BF16 = jnp.bfloat16
I32 = jnp.int32

D_MODEL = 1024
BATCH = 4
SEQ = 8192
DEPTH = 4
GRID_W = 64
CTX_LEN = 256
HEAD_DIM = 64
NA_WIDTH = 512
NA_HEADS = 8
NA_WIN_ROWS = 8
NA_WIN_COLS = 16
LRU_WIDTH = 512
LRU_BLOCKS = 8
LRU_BLOCK = 64
CONV_W = 4
LRU_C = 8.0
IN_COLS = 3 * NA_WIDTH + 2 * LRU_WIDTH
N_EXPERTS = 16
N_GROUPS = 4
EXPERTS_PER_GROUP = 4
TOP_K = 2
EXPERT_FF = 512
N_MOD = 6
ATTN_SCALE = HEAD_DIM ** -0.5
EPS = 1e-6
NEG_INF = -1e30

TILE = 256
SUB = 8
LANES = 128
GROUPS = TILE // SUB
L_TOT = CTX_LEN + SEQ
N_TILES = L_TOT // TILE
N_LAT_TILES = SEQ // TILE
N_TOK = BATCH * L_TOT
N_TOK_TILES = N_TOK // TILE
LRU_CH = 256
MOD_ROWS = 8
VMEM_LIMIT = 56 * 1024 * 1024

PACK_W = D_MODEL // 2
HI_MASK = -65536
N_ASSIGN = TOP_K * N_TOK
EXP_TILE = 512
N_EXP_TILES = N_ASSIGN // EXP_TILE
N_ITEMS = N_EXP_TILES + N_EXPERTS - 1
ROUTE_ROWS = 2 * N_EXPERTS

SC_CORES = 2
SC_SUBCORES = 16
SC_WORKERS = SC_CORES * SC_SUBCORES
SC_ROWS = N_TOK // SC_WORKERS
SC_CHUNK = 96
SC_N_CHUNKS = SC_ROWS // SC_CHUNK


def _cparams(sem):
    return pltpu.CompilerParams(dimension_semantics=sem, vmem_limit_bytes=VMEM_LIMIT)


def _pack_rows(v):
    lo = pltpu.bitcast(v[:, :PACK_W].astype(BF16).astype(F32), I32)
    hi = pltpu.bitcast(v[:, PACK_W:].astype(BF16).astype(F32), I32)
    return ((lo >> 16) & 0xFFFF) | (hi & HI_MASK)


def _unpack_rows(p):
    return pltpu.bitcast(p << 16, F32), pltpu.bitcast(p & HI_MASK, F32)


def _mod_kernel(c_ref, w_ref, b_ref, o_ref):
    c = c_ref[...]
    s = c * jax.nn.sigmoid(c)
    o_ref[0] = jnp.dot(s.astype(BF16), w_ref[0].astype(BF16), preferred_element_type=F32) + b_ref[0]


def _modulation(cs, w_mod, b_mod):
    return pl.pallas_call(
        _mod_kernel,
        grid=(DEPTH, N_MOD),
        in_specs=[
            pl.BlockSpec((MOD_ROWS, D_MODEL), lambda l, n: (0, 0)),
            pl.BlockSpec((1, D_MODEL, D_MODEL), lambda l, n: (l, 0, n)),
            pl.BlockSpec((1, 1, D_MODEL), lambda l, n: (l, 0, n)),
        ],
        out_specs=pl.BlockSpec((1, MOD_ROWS, D_MODEL), lambda l, n: (l, 0, n)),
        out_shape=jax.ShapeDtypeStruct((DEPTH, MOD_ROWS, N_MOD * D_MODEL), F32),
        compiler_params=_cparams(("arbitrary", "arbitrary")),
        name="modulation",
    )(cs, w_mod, b_mod.reshape(DEPTH, 1, N_MOD * D_MODEL))


def _mod_row(b, i):
    return jnp.where(i == 0, BATCH, b)


def _moe_residual(x, g0_ref, g1_ref, wt_ref, gate_row):
    wt = wt_ref[0]
    lo0, hi0 = _unpack_rows(g0_ref[0])
    lo1, hi1 = _unpack_rows(g1_ref[0])
    w0 = wt[:, 0:1]
    w1 = wt[:, 1:2]
    f = jnp.concatenate([w0 * lo0 + w1 * lo1, w0 * hi0 + w1 * hi1], axis=1)
    return x + gate_row * f


def _in_kernel(has_prev, *refs):
    if has_prev:
        (x_ref, g0_ref, g1_ref, wt_ref, mprev_ref, mod_ref, nrm_ref, w_ref, bd_ref, qg_ref, kg_ref,
         xo_ref, q_ref, k_ref, v_ref, u_ref, g_ref) = refs
    else:
        (x_ref, mod_ref, nrm_ref, w_ref, bd_ref, qg_ref, kg_ref,
         q_ref, k_ref, v_ref, u_ref, g_ref) = refs
    x = x_ref[0]
    if has_prev:
        x = _moe_residual(x, g0_ref, g1_ref, wt_ref, mprev_ref[0, 0][5:6, :])
        xo_ref[0] = x
    m = mod_ref[0, 0]
    ms = jnp.mean(x * x, axis=-1, keepdims=True)
    h = (x * lax.rsqrt(ms + EPS)) * nrm_ref[...]
    h = h * (1.0 + m[1:2, :]) + m[0:1, :]
    acc = jnp.dot(h.astype(BF16), w_ref[...], preferred_element_type=F32)
    bd = bd_ref[...]

    def head_norm(t, gain):
        ss = jnp.dot((t * t).astype(BF16), bd, preferred_element_type=F32)
        return (t * lax.rsqrt(ss + EPS)) * gain

    q_ref[0] = head_norm(acc[:, 0:NA_WIDTH], qg_ref[...]).astype(BF16)
    k_ref[0] = head_norm(acc[:, NA_WIDTH:2 * NA_WIDTH], kg_ref[...]).astype(BF16)
    v_ref[0] = acc[:, 2 * NA_WIDTH:3 * NA_WIDTH].astype(BF16)
    u_ref[0] = acc[:, 3 * NA_WIDTH:3 * NA_WIDTH + LRU_WIDTH].astype(BF16)
    g_ref[0] = acc[:, 3 * NA_WIDTH + LRU_WIDTH:].astype(BF16)


def _in_proj(l, xs, moe_prev, mod, nrm, w_in, bd, qg, kg):
    has_prev = moe_prev is not None
    tok = lambda width: pl.BlockSpec((1, TILE, width), lambda b, i: (b, i, 0))
    full = lambda shape: pl.BlockSpec(shape, lambda b, i: tuple(0 for _ in shape))
    mod_spec = lambda layer: pl.BlockSpec((1, 1, N_MOD, D_MODEL), lambda b, i: (layer, _mod_row(b, i), 0, 0))
    in_specs = [tok(D_MODEL)]
    args = [xs]
    if has_prev:
        in_specs += [tok(PACK_W), tok(PACK_W), tok(LANES), mod_spec(l - 1)]
        args += list(moe_prev) + [mod]
    in_specs += [mod_spec(l), full((1, D_MODEL)), full((D_MODEL, IN_COLS)), full((NA_WIDTH, NA_WIDTH)),
                 full((1, NA_WIDTH)), full((1, NA_WIDTH))]
    args += [mod, nrm, w_in, bd, qg, kg]
    half = jax.ShapeDtypeStruct((BATCH, L_TOT, NA_WIDTH), BF16)
    out_shape = [half] * 5
    out_specs = [tok(NA_WIDTH)] * 5
    if has_prev:
        out_shape = [jax.ShapeDtypeStruct((BATCH, L_TOT, D_MODEL), F32)] + out_shape
        out_specs = [tok(D_MODEL)] + out_specs
    outs = pl.pallas_call(
        functools.partial(_in_kernel, has_prev),
        grid=(BATCH, N_TILES),
        in_specs=in_specs,
        out_specs=out_specs,
        out_shape=out_shape,
        compiler_params=_cparams(("parallel", "parallel")),
        name="in_proj",
    )(*args)
    if has_prev:
        return outs[0], outs[1:]
    return xs, outs


def _attn_kernel(q_ref, kp_ref, kc_ref, kn_ref, kx_ref, vp_ref, vc_ref, vn_ref, vx_ref, bias_ref, o_ref):
    q = q_ref[0]
    lane = lax.broadcasted_iota(jnp.int32, (TILE, 2 * HEAD_DIM), 1)
    low = lane < HEAD_DIM
    kb = [kp_ref[0], kc_ref[0], kn_ref[0], kx_ref[0]]
    vb = [vp_ref[0], vc_ref[0], vn_ref[0], vx_ref[0]]
    outs = []
    for hh in range(2):
        qh = jnp.where(low if hh == 0 else jnp.logical_not(low), q, jnp.zeros_like(q))
        s = [lax.dot_general(qh, kb[i], (((1,), (1,)), ((), ())), preferred_element_type=F32) for i in range(4)]
        for i in range(3):
            s[i] = s[i] + bias_ref[0, hh, :, i * TILE:(i + 1) * TILE]
        m = jnp.max(s[0], axis=-1, keepdims=True)
        for i in range(1, 4):
            m = jnp.maximum(m, jnp.max(s[i], axis=-1, keepdims=True))
        p = [jnp.exp(s[i] - m) for i in range(4)]
        den = jnp.sum(p[0], axis=-1, keepdims=True)
        for i in range(1, 4):
            den = den + jnp.sum(p[i], axis=-1, keepdims=True)
        o = jnp.dot(p[0].astype(BF16), vb[0], preferred_element_type=F32)
        for i in range(1, 4):
            o = o + jnp.dot(p[i].astype(BF16), vb[i], preferred_element_type=F32)
        outs.append(o / den)
    o_ref[0] = jnp.where(low, outs[0], outs[1]).astype(BF16)


def _attention(q, k, v, bias_tiles):
    n_hp = NA_HEADS // 2
    last = N_LAT_TILES - 1
    blk = (1, TILE, 2 * HEAD_DIM)
    prev_map = lambda b, hp, j: (b, 1 + jnp.clip(j - 2, 0, last), hp)
    cur_map = lambda b, hp, j: (b, jnp.maximum(j, 1), hp)
    next_map = lambda b, hp, j: (b, 1 + jnp.clip(j, 0, last), hp)
    ctx_map = lambda b, hp, j: (b, 0, hp)
    var_map = lambda b, hp, j: (jnp.where(j == 0, 3, jnp.where(j == 1, 0, jnp.where(j == N_TILES - 1, 2, 1))),
                                hp, 0, 0)
    kv_specs = [pl.BlockSpec(blk, prev_map), pl.BlockSpec(blk, cur_map), pl.BlockSpec(blk, next_map),
                pl.BlockSpec(blk, ctx_map)]
    return pl.pallas_call(
        _attn_kernel,
        grid=(BATCH, n_hp, N_TILES),
        in_specs=[pl.BlockSpec(blk, lambda b, hp, j: (b, j, hp))] + kv_specs + kv_specs
        + [pl.BlockSpec((1, 2, TILE, 3 * TILE), var_map)],
        out_specs=pl.BlockSpec(blk, lambda b, hp, j: (b, j, hp)),
        out_shape=jax.ShapeDtypeStruct((BATCH, L_TOT, NA_WIDTH), BF16),
        compiler_params=_cparams(("parallel", "parallel", "arbitrary")),
        name="na_attention",
    )(q, k, k, k, k, v, v, v, v, bias_tiles)


def _softplus(x):
    return jnp.maximum(x, 0.0) + jnp.log1p(jnp.exp(-jnp.abs(x)))


def _lru_kernel(u_ref, g_ref, cw_ref, cb_ref, wr_ref, wi_ref, br_ref, bi_ref, lam_ref, y_ref,
                hf_scr, hl_scr, p_scr):
    C = LRU_CH
    cw = cw_ref[...]
    cb = cb_ref[...]
    sub = lax.broadcasted_iota(jnp.int32, (SUB, C), 0)

    def conv_tile(c, r0):
        U = u_ref[0, pl.ds(r0, TILE), :].astype(F32)
        pstart = pl.multiple_of(jnp.maximum(r0 - 16, 0), 16)
        nstart = pl.multiple_of(jnp.minimum(r0 + TILE, L_TOT - 16), 16)
        prev16 = u_ref[0, pl.ds(pstart, 16), :].astype(F32)
        next16 = u_ref[0, pl.ds(nstart, 16), :].astype(F32)
        has_prev = jnp.where(c >= 2, 1.0, 0.0).astype(F32)
        has_next = jnp.where(jnp.logical_and(c >= 1, c <= N_TILES - 2), 1.0, 0.0).astype(F32)
        prow = prev16[15:16, :] * has_prev
        n0 = next16[0:1, :] * has_next
        n8 = next16[8:9, :] * has_next
        first8 = jnp.where(sub == 0, prow, pltpu.roll(U[TILE - SUB:TILE, :], 1, 0))
        last_a = jnp.where(sub == SUB - 1, n0, pltpu.roll(U[0:SUB, :], SUB - 1, 0))
        last_b = jnp.where(sub == SUB - 1, n8, pltpu.roll(U[SUB:2 * SUB, :], SUB - 1, 0))
        um1 = jnp.concatenate([first8, U[0:TILE - SUB, :]], axis=0)
        up1 = jnp.concatenate([U[SUB:TILE, :], last_a], axis=0)
        up2 = jnp.concatenate([U[2 * SUB:TILE, :], last_a, last_b], axis=0)
        return cw[0:1, :] * um1 + cw[1:2, :] * U + cw[2:3, :] * up1 + cw[3:4, :] * up2 + cb

    def coeffs(v, d):
        vb = v.astype(BF16)
        r = jax.nn.sigmoid(jnp.dot(vb, wr_ref[d, 0], preferred_element_type=F32) + br_ref[d:d + 1, :])
        i = jax.nn.sigmoid(jnp.dot(vb, wi_ref[d, 0], preferred_element_type=F32) + bi_ref[d:d + 1, :])
        log_a = (-LRU_C * r) * _softplus(-lam_ref[d:d + 1, :])
        a = jnp.exp(log_a)
        b = jnp.sqrt(1.0 - a * a) * (i * v)
        return a, b

    def scan_tile(a, b, h_in, reverse):
        order = range(GROUPS - 1, -1, -1) if reverse else range(GROUPS)
        hl = None
        for g in order:
            ag = a[g * SUB:(g + 1) * SUB, :]
            bg = b[g * SUB:(g + 1) * SUB, :]
            if hl is None:
                hl, p = bg, ag
            else:
                hl = ag * hl + bg
                p = ag * p
            hl_scr[g * SUB:(g + 1) * SUB, :] = hl
            p_scr[g * SUB:(g + 1) * SUB, :] = p
        blocks = range(SUB - 1, -1, -1) if reverse else range(SUB)
        carry = h_in
        cins = {}
        for s in blocks:
            cins[s] = carry
            carry = hl[s:s + 1, :] + p[s:s + 1, :] * carry
        cin = jnp.concatenate([cins[s] for s in range(SUB)], axis=0)
        hfull = hl_scr[...] + p_scr[...] * jnp.tile(cin, (GROUPS, 1))
        return hfull, carry

    zero = jnp.zeros((1, C), F32)

    def fwd_body(c, h):
        r0 = pl.multiple_of(c * TILE, TILE)
        a, b = coeffs(conv_tile(c, r0), 0)
        hfull, h = scan_tile(a, b, h, False)
        hf_scr[pl.ds(r0, TILE), :] = hfull
        return h

    lax.fori_loop(0, N_TILES, fwd_body, zero)

    def rev_tile(c, h):
        r0 = pl.multiple_of(c * TILE, TILE)
        a, b = coeffs(conv_tile(c, r0), 1)
        hfull, h = scan_tile(a, b, h, True)
        gate = jax.nn.gelu(g_ref[0, pl.ds(r0, TILE), :].astype(F32))
        y_ref[0, pl.ds(r0, TILE), :] = (gate * (hf_scr[pl.ds(r0, TILE), :] + hfull)).astype(BF16)
        return h

    h_ctx = rev_tile(jnp.int32(0), zero)
    lax.fori_loop(0, N_LAT_TILES, lambda i, h: rev_tile(N_TILES - 1 - i, h), h_ctx)


def _rglru(u, g, cw, cb, wr, wi, br, bi, lam):
    n_cb = LRU_WIDTH // LRU_CH
    seq = pl.BlockSpec((1, L_TOT, LRU_CH), lambda b, c: (b, 0, c))
    vec = lambda rows: pl.BlockSpec((rows, LRU_CH), lambda b, c: (0, c))
    mat = pl.BlockSpec((2, 1, LRU_CH, LRU_CH), lambda b, c: (0, c, 0, 0))
    return pl.pallas_call(
        _lru_kernel,
        grid=(BATCH, n_cb),
        in_specs=[seq, seq, vec(CONV_W), vec(1), mat, mat, vec(2), vec(2), vec(2)],
        out_specs=seq,
        out_shape=jax.ShapeDtypeStruct((BATCH, L_TOT, LRU_WIDTH), BF16),
        scratch_shapes=[pltpu.VMEM((L_TOT, LRU_CH), F32), pltpu.VMEM((TILE, LRU_CH), F32),
                        pltpu.VMEM((TILE, LRU_CH), F32)],
        compiler_params=_cparams(("parallel", "parallel")),
        name="rglru",
    )(u, g, cw, cb, wr, wi, br, bi, lam)


def _route(sel, aff):
    def top2_sum(a, b, c, d):
        hi1, lo1 = jnp.maximum(a, b), jnp.minimum(a, b)
        hi2, lo2 = jnp.maximum(c, d), jnp.minimum(c, d)
        return jnp.maximum(hi1, hi2) + jnp.maximum(jnp.minimum(hi1, hi2), jnp.maximum(lo1, lo2))

    scores = [top2_sum(*sel[EXPERTS_PER_GROUP * g:EXPERTS_PER_GROUP * (g + 1)]) for g in range(N_GROUPS)]
    best = jnp.zeros_like(scores[0], dtype=jnp.int32)
    best_v = scores[0]
    for g in range(1, N_GROUPS):
        upd = scores[g] > best_v
        best = jnp.where(upd, g, best)
        best_v = jnp.where(upd, scores[g], best_v)
    chosen = []
    for e in range(N_EXPERTS):
        g = e // EXPERTS_PER_GROUP
        rank = jnp.zeros_like(best)
        for o in range(EXPERTS_PER_GROUP * g, EXPERTS_PER_GROUP * (g + 1)):
            if o == e:
                continue
            ahead = sel[o] > sel[e]
            if o < e:
                ahead = jnp.logical_or(ahead, sel[o] == sel[e])
            rank = rank + ahead.astype(jnp.int32)
        chosen.append(jnp.logical_and(best == g, rank < TOP_K))
    total = jnp.zeros_like(aff[0])
    for e in range(N_EXPERTS):
        total = total + jnp.where(chosen[e], aff[e], 0.0)
    gates = [jnp.where(chosen[e], aff[e] / total, 0.0) for e in range(N_EXPERTS)]
    return gates, [c.astype(F32) for c in chosen]


def _out_kernel(x_ref, ya_ref, yb_ref, mod_ref, nrm_ref, w_ref, rwh_ref, rwl_ref, rb_ref,
                xo_ref, hp_ref, rt_ref):
    m = mod_ref[0, 0]
    y = jnp.dot(ya_ref[0], w_ref[0:NA_WIDTH, :], preferred_element_type=F32)
    y = y + jnp.dot(yb_ref[0], w_ref[NA_WIDTH:, :], preferred_element_type=F32)
    x = x_ref[0] + m[2:3, :] * y
    xo_ref[0] = x
    ms = jnp.mean(x * x, axis=-1, keepdims=True)
    h = (x * lax.rsqrt(ms + EPS)) * nrm_ref[...]
    h = h * (1.0 + m[4:5, :]) + m[3:4, :]
    h_hi = h.astype(BF16)
    hp_ref[0] = _pack_rows(h)
    h_lo = (h - h_hi.astype(F32)).astype(BF16)
    nt = (((1,), (1,)), ((), ()))
    lg = (lax.dot_general(rwh_ref[...], h_hi, nt, preferred_element_type=F32)
          + lax.dot_general(rwh_ref[...], h_lo, nt, preferred_element_type=F32)
          + lax.dot_general(rwl_ref[...], h_hi, nt, preferred_element_type=F32))
    aff_all = jax.nn.sigmoid(lg)
    sel_all = aff_all + rb_ref[...]
    aff = [aff_all[e:e + 1, :] for e in range(N_EXPERTS)]
    sel = [sel_all[e:e + 1, :] for e in range(N_EXPERTS)]
    gates, chosen = _route(sel, aff)
    rt_ref[0] = jnp.concatenate(gates + chosen, axis=0)


def _out_proj(l, xs, ya, yb, mod, nrm, w_out, rwh, rwl, rb):
    tok = lambda width: pl.BlockSpec((1, TILE, width), lambda b, i: (b, i, 0))
    full = lambda shape: pl.BlockSpec(shape, lambda b, i: tuple(0 for _ in shape))
    return pl.pallas_call(
        _out_kernel,
        grid=(BATCH, N_TILES),
        in_specs=[tok(D_MODEL), tok(NA_WIDTH), tok(LRU_WIDTH),
                  pl.BlockSpec((1, 1, N_MOD, D_MODEL), lambda b, i: (l, _mod_row(b, i), 0, 0)),
                  full((1, D_MODEL)), full((D_MODEL, D_MODEL)), full((N_EXPERTS, D_MODEL)),
                  full((N_EXPERTS, D_MODEL)), full((N_EXPERTS, 1))],
        out_specs=[tok(D_MODEL), tok(PACK_W),
                   pl.BlockSpec((1, ROUTE_ROWS, TILE), lambda b, i: (b * N_TILES + i, 0, 0))],
        out_shape=[jax.ShapeDtypeStruct((BATCH, L_TOT, D_MODEL), F32),
                   jax.ShapeDtypeStruct((BATCH, L_TOT, PACK_W), I32),
                   jax.ShapeDtypeStruct((N_TOK_TILES, ROUTE_ROWS, TILE), F32)],
        compiler_params=_cparams(("parallel", "parallel")),
        name="out_proj_router",
    )(xs, ya, yb, mod, nrm, w_out, rwh, rwl, rb)


def _pos_kernel(rt_ref, tri_ref, pos_ref, wt_ref, cnt_ref, run_scr, start_scr):
    phase = pl.program_id(0)
    i = pl.program_id(1)
    rt = rt_ref[0]
    gates = rt[0:N_EXPERTS, :]
    chosen = rt[N_EXPERTS:, :]

    @pl.when(jnp.logical_and(phase == 0, i == 0))
    def _():
        run_scr[...] = jnp.zeros_like(run_scr)

    @pl.when(jnp.logical_and(phase == 1, i == 0))
    def _():
        tot = run_scr[...]
        cnt_ref[...] = jnp.broadcast_to(tot, (N_EXPERTS, LANES))
        acc = jnp.zeros((1, 1), F32)
        rows = []
        for e in range(N_EXPERTS):
            rows.append(acc)
            acc = acc + tot[e:e + 1, :]
        start_scr[...] = jnp.concatenate(rows, axis=0)
        run_scr[...] = jnp.zeros_like(run_scr)

    @pl.when(phase == 1)
    def _():
        rank = jnp.dot(chosen.astype(BF16), tri_ref[...], preferred_element_type=F32)
        posf = rank + (start_scr[...] + run_scr[...])
        seen = jnp.zeros((1, TILE), F32)
        p0 = jnp.zeros((1, TILE), F32)
        p1 = jnp.zeros((1, TILE), F32)
        w0 = jnp.zeros((1, TILE), F32)
        w1 = jnp.zeros((1, TILE), F32)
        for e in range(N_EXPERTS):
            ch = chosen[e:e + 1, :]
            first = ch * (1.0 - seen)
            second = ch * seen
            p0 = p0 + first * posf[e:e + 1, :]
            p1 = p1 + second * posf[e:e + 1, :]
            w0 = w0 + first * gates[e:e + 1, :]
            w1 = w1 + second * gates[e:e + 1, :]
            seen = jnp.minimum(seen + ch, 1.0)
        pos_ref[0] = jnp.concatenate([p0, p1], axis=0).astype(I32)
        wpad = jnp.concatenate([w0, w1, jnp.zeros((LANES - TOP_K, TILE), F32)], axis=0)
        wt_ref[...] = jnp.transpose(wpad)

    run_scr[...] += jnp.sum(chosen, axis=1, keepdims=True)


def _positions(route, tri):
    return pl.pallas_call(
        _pos_kernel,
        grid=(2, N_TOK_TILES),
        in_specs=[pl.BlockSpec((1, ROUTE_ROWS, TILE), lambda p, i: (i, 0, 0)),
                  pl.BlockSpec((TILE, TILE), lambda p, i: (0, 0))],
        out_specs=[pl.BlockSpec((1, TOP_K, TILE), lambda p, i: (i * p, 0, 0)),
                   pl.BlockSpec((TILE, LANES), lambda p, i: (i * p, 0)),
                   pl.BlockSpec((N_EXPERTS, LANES), lambda p, i: (0, 0))],
        out_shape=[jax.ShapeDtypeStruct((N_TOK_TILES, TOP_K, TILE), I32),
                   jax.ShapeDtypeStruct((N_TOK, LANES), F32),
                   jax.ShapeDtypeStruct((N_EXPERTS, LANES), F32)],
        scratch_shapes=[pltpu.VMEM((N_EXPERTS, 1), F32), pltpu.VMEM((N_EXPERTS, 1), F32)],
        compiler_params=_cparams(("arbitrary", "arbitrary")),
        name="moe_positions",
    )(route, tri)


def _work_items(counts):
    ends = jnp.cumsum(counts)
    starts = ends - counts
    first_tile = starts // EXP_TILE
    n_tiles = jnp.where(counts > 0, (ends - 1) // EXP_TILE - first_tile + 1, 0)
    item_end = jnp.cumsum(n_tiles)
    item_start = item_end - n_tiles
    idx = jnp.arange(N_ITEMS, dtype=I32)
    n_valid = item_end[-1]
    valid = idx < n_valid
    e = jnp.minimum(jnp.searchsorted(item_end, jnp.minimum(idx, n_valid - 1), side='right'), N_EXPERTS - 1).astype(I32)
    tile = jnp.where(valid, first_tile[e] + (idx - item_start[e]), N_EXP_TILES - 1).astype(I32)
    lo = jnp.where(valid, jnp.clip(starts[e] - tile * EXP_TILE, 0, EXP_TILE), 0).astype(I32)
    hi = jnp.where(valid, jnp.clip(ends[e] - tile * EXP_TILE, 0, EXP_TILE), 0).astype(I32)
    return tile, e, lo, hi


def _sc_worker_base():
    return (lax.axis_index("s") * SC_CORES + lax.axis_index("c")) * SC_ROWS


def _sc_dispatch_body(h_hbm, p0_hbm, p1_hbm, out_hbm, i0_v, i1_v, rows_v, sem):
    base = _sc_worker_base()

    @pl.loop(0, SC_N_CHUNKS)
    def _(j):
        off = base + j * SC_CHUNK
        pltpu.sync_copy(h_hbm.at[pl.ds(off, SC_CHUNK)], rows_v)
        pltpu.sync_copy(p0_hbm.at[pl.ds(off, SC_CHUNK)], i0_v)
        pltpu.sync_copy(p1_hbm.at[pl.ds(off, SC_CHUNK)], i1_v)
        c0 = pltpu.async_copy(rows_v, out_hbm.at[i0_v], sem)
        c1 = pltpu.async_copy(rows_v, out_hbm.at[i1_v], sem)
        c0.wait()
        c1.wait()


def _sc_combine_body(y_hbm, p0_hbm, p1_hbm, g0_hbm, g1_hbm, i_v, rows_v, sem):
    base = _sc_worker_base()

    @pl.loop(0, SC_N_CHUNKS)
    def _(j):
        off = base + j * SC_CHUNK
        for p_hbm, g_hbm in ((p0_hbm, g0_hbm), (p1_hbm, g1_hbm)):
            pltpu.sync_copy(p_hbm.at[pl.ds(off, SC_CHUNK)], i_v)
            pltpu.async_copy(y_hbm.at[i_v], rows_v, sem).wait()
            pltpu.sync_copy(rows_v, g_hbm.at[pl.ds(off, SC_CHUNK)])


def _sc_mesh():
    return plsc.VectorSubcoreMesh(core_axis_name="c", subcore_axis_name="s")


def _dispatch(hp, pos0, pos1):
    return pl.kernel(
        _sc_dispatch_body,
        out_type=jax.ShapeDtypeStruct((N_ASSIGN, PACK_W), I32),
        mesh=_sc_mesh(),
        scratch_types=[pltpu.VMEM((SC_CHUNK,), I32), pltpu.VMEM((SC_CHUNK,), I32),
                       pltpu.VMEM((SC_CHUNK, PACK_W), I32), pltpu.SemaphoreType.DMA],
        name="moe_dispatch",
    )(hp, pos0, pos1)


def _combine(ys, pos0, pos1):
    row = jax.ShapeDtypeStruct((N_TOK, PACK_W), I32)
    return pl.kernel(
        _sc_combine_body,
        out_type=[row, row],
        mesh=_sc_mesh(),
        scratch_types=[pltpu.VMEM((SC_CHUNK,), I32), pltpu.VMEM((SC_CHUNK, PACK_W), I32),
                       pltpu.SemaphoreType.DMA],
        name="moe_combine",
    )(ys, pos0, pos1)


def _expert_kernel(tile_ref, exp_ref, lo_ref, hi_ref, x_ref, w1_ref, w3_ref, w2_ref, o_ref):
    i = pl.program_id(0)
    lo_f, hi_f = _unpack_rows(x_ref[...])
    h = jnp.concatenate([lo_f.astype(BF16), hi_f.astype(BF16)], axis=1)
    a = jnp.dot(h, w1_ref[0], preferred_element_type=F32)
    b = jnp.dot(h, w3_ref[0], preferred_element_type=F32)
    t = ((a * jax.nn.sigmoid(a)) * b).astype(BF16)
    y = _pack_rows(jnp.dot(t, w2_ref[0], preferred_element_type=F32))
    row = lax.broadcasted_iota(jnp.int32, (EXP_TILE, PACK_W), 0)
    mine = jnp.logical_and(row >= lo_ref[i], row < hi_ref[i])
    revisit = jnp.logical_and(i > 0, tile_ref[i] == tile_ref[jnp.maximum(i - 1, 0)])

    @pl.when(jnp.logical_not(revisit))
    def _():
        o_ref[...] = jnp.where(mine, y, 0)

    @pl.when(revisit)
    def _():
        o_ref[...] = jnp.where(mine, y, o_ref[...])


def _experts(xs_sorted, items, w1, w3, w2):
    tile, e, lo, hi = items
    grid_spec = pltpu.PrefetchScalarGridSpec(
        num_scalar_prefetch=4,
        grid=(N_ITEMS,),
        in_specs=[pl.BlockSpec((EXP_TILE, PACK_W), lambda i, t, e, lo, hi: (t[i], 0)),
                  pl.BlockSpec((1, D_MODEL, EXPERT_FF), lambda i, t, e, lo, hi: (e[i], 0, 0)),
                  pl.BlockSpec((1, D_MODEL, EXPERT_FF), lambda i, t, e, lo, hi: (e[i], 0, 0)),
                  pl.BlockSpec((1, EXPERT_FF, D_MODEL), lambda i, t, e, lo, hi: (e[i], 0, 0))],
        out_specs=pl.BlockSpec((EXP_TILE, PACK_W), lambda i, t, e, lo, hi: (t[i], 0)),
    )
    return pl.pallas_call(
        _expert_kernel,
        grid_spec=grid_spec,
        out_shape=jax.ShapeDtypeStruct((N_ASSIGN, PACK_W), I32),
        compiler_params=_cparams(("arbitrary",)),
        name="moe_experts",
    )(tile, e, lo, hi, xs_sorted, w1, w3, w2)


def _moe(hp, route, tri, w1, w3, w2):
    pos, wt, cnt = _positions(route, tri)
    pos0 = pos[:, 0, :].reshape(N_TOK)
    pos1 = pos[:, 1, :].reshape(N_TOK)
    items = _work_items(cnt[:, 0].astype(I32))
    xs_sorted = _dispatch(hp.reshape(N_TOK, PACK_W), pos0, pos1)
    ys = _experts(xs_sorted, items, w1, w3, w2)
    g0, g1 = _combine(ys, pos0, pos1)
    shape = (BATCH, L_TOT, PACK_W)
    return g0.reshape(shape), g1.reshape(shape), wt.reshape(BATCH, L_TOT, LANES)


def _final_kernel(x_ref, g0_ref, g1_ref, wt_ref, mod_ref, o_ref):
    o_ref[0] = _moe_residual(x_ref[0], g0_ref, g1_ref, wt_ref, mod_ref[0, 0][5:6, :])


def _final(xs, moe, mod):
    lat = lambda width: pl.BlockSpec((1, TILE, width), lambda b, j: (b, j + 1, 0))
    return pl.pallas_call(
        _final_kernel,
        grid=(BATCH, N_LAT_TILES),
        in_specs=[lat(D_MODEL), lat(PACK_W), lat(PACK_W), lat(LANES),
                  pl.BlockSpec((1, 1, N_MOD, D_MODEL), lambda b, j: (DEPTH - 1, b, 0, 0))],
        out_specs=pl.BlockSpec((1, TILE, D_MODEL), lambda b, j: (b, j, 0)),
        out_shape=jax.ShapeDtypeStruct((BATCH, SEQ, D_MODEL), F32),
        compiler_params=_cparams(("parallel", "parallel")),
        name="final_residual",
    )(xs, *moe, mod)


def _to_scan_major(t):
    b, n, d = t.shape
    return t.reshape(b, n // TILE, SUB, GROUPS, d).transpose(0, 1, 3, 2, 4).reshape(b, n, d)


def _from_scan_major(t):
    b, n, d = t.shape
    return t.reshape(b, n // TILE, GROUPS, SUB, d).transpose(0, 1, 3, 2, 4).reshape(b, n, d)


def _bias_tiles(table):
    rows_q = TILE // GRID_W
    rows_k = 3 * rows_q
    a = np.arange(rows_q)[:, None]
    c = np.arange(rows_k)[None, :]
    r_idx = np.clip(c - a + 3, 0, 2 * NA_WIN_ROWS - 2)
    qc = np.arange(GRID_W)[:, None]
    kc = np.arange(GRID_W)[None, :]
    col0 = np.clip(qc - NA_WIN_COLS // 2, 0, GRID_W - NA_WIN_COLS)
    in_win = (kc >= col0) & (kc < col0 + NA_WIN_COLS)
    pad = GRID_W - NA_WIN_COLS
    tp = jnp.pad(table.astype(F32), ((0, 0), (0, 0), (pad, pad)))
    toeplitz = jnp.stack([tp[:, :, GRID_W - 1 - q:2 * GRID_W - 1 - q] for q in range(GRID_W)], axis=2)
    blk = jnp.stack([jnp.stack([toeplitz[:, int(r_idx[i, j])] for j in range(rows_k)], axis=1)
                     for i in range(rows_q)], axis=1)
    allowed = np.stack([
        np.broadcast_to(c >= rows_q, (rows_q, rows_k)),
        (c >= a) & (c <= a + NA_WIN_ROWS - 1),
        np.broadcast_to(c < 2 * rows_q, (rows_q, rows_k)),
        np.zeros((rows_q, rows_k), bool),
    ])
    mask = allowed[:, None, :, :, None, None] & in_win[None, None, None, None, :, :]
    tiles = jnp.where(mask, blk[None], NEG_INF)
    half = GRID_W // GROUPS
    tiles = tiles.reshape(4, NA_HEADS, rows_q, 3, rows_q, half, GROUPS, half, GROUPS)
    tiles = tiles.transpose(0, 1, 6, 2, 5, 3, 8, 4, 7)
    return tiles.reshape(4, NA_HEADS, TILE, 3 * TILE)


def _block_diag(w, n_chunks):
    per = LRU_BLOCKS // n_chunks
    w = w.reshape(2, n_chunks, per, LRU_BLOCK, LRU_BLOCK)
    eye = jnp.eye(per, dtype=w.dtype)
    out = jnp.einsum('dcpij,pq->dcpiqj', w, eye)
    return out.reshape(2, n_chunks, per * LRU_BLOCK, per * LRU_BLOCK)


def kernel(x, c, ctx, c_ctx, w_mod, b_mod, norm_mix, norm_ffn, w_in, w_out, q_gain, k_gain, na_bias,
           conv_w, conv_b, lru_w_r, lru_b_r, lru_w_i, lru_b_i, lru_lambda, router_w, router_b,
           exp_w1, exp_w3, exp_w2):
    cs = jnp.concatenate([c, c_ctx[None, :], jnp.zeros((MOD_ROWS - BATCH - 1, D_MODEL), F32)], axis=0)
    mod = _modulation(cs, w_mod, b_mod).reshape(DEPTH, MOD_ROWS, N_MOD, D_MODEL)

    xs = jnp.concatenate([_to_scan_major(ctx), _to_scan_major(x)], axis=1)
    head_of = np.arange(NA_WIDTH) // HEAD_DIM
    bd = jnp.asarray((head_of[:, None] == head_of[None, :]).astype(np.float32) / HEAD_DIM, BF16)
    tri = jnp.asarray(np.triu(np.ones((TILE, TILE), np.float32), 1), BF16)
    rwt = router_w.T
    rwh = rwt.astype(BF16)
    rwl = (rwt - rwh.astype(F32)).astype(BF16)
    rb = router_b.reshape(N_EXPERTS, 1)
    n_cb = LRU_WIDTH // LRU_CH

    moe = None
    for l in range(DEPTH):
        qg = jnp.tile(q_gain[l] * ATTN_SCALE, NA_HEADS)[None, :]
        kg = jnp.tile(k_gain[l], NA_HEADS)[None, :]
        xs, (q, k, v, u, g) = _in_proj(l, xs, moe, mod, norm_mix[l][None, :], w_in[l].astype(BF16), bd, qg, kg)
        ya = _attention(q, k, v, _bias_tiles(na_bias[l]))
        yb = _rglru(u, g, conv_w[l], conv_b[l][None, :],
                    _block_diag(lru_w_r[l], n_cb).astype(BF16), _block_diag(lru_w_i[l], n_cb).astype(BF16),
                    lru_b_r[l], lru_b_i[l], lru_lambda[l])
        xs, hp, route = _out_proj(l, xs, ya, yb, mod, norm_ffn[l][None, :], w_out[l].astype(BF16), rwh, rwl, rb)
        moe = _moe(hp, route, tri, exp_w1[l].astype(BF16), exp_w3[l].astype(BF16), exp_w2[l].astype(BF16))
    out = _final(xs, moe, mod)
    return _from_scan_major(out)
```

```python
import functools

import jax
import jax.numpy as jnp
import numpy as np
from jax import lax
from jax.experimental import pallas as pl
from jax.experimental.pallas import tpu as pltpu
from jax.experimental.pallas import tpu_sc as plsc

F32 = jnp.float32
BF16 = jnp.bfloat16
I32 = jnp.int32

D_MODEL = 1024
BATCH = 4
SEQ = 8192
DEPTH = 4
GRID_W = 64
CTX_LEN = 256
HEAD_DIM = 64
NA_WIDTH = 512
NA_HEADS = 8
NA_WIN_ROWS = 8
NA_WIN_COLS = 16
LRU_WIDTH = 512
LRU_BLOCKS = 8
LRU_BLOCK = 64
CONV_W = 4
LRU_C = 8.0
IN_COLS = 3 * NA_WIDTH + 2 * LRU_WIDTH
N_EXPERTS = 16
N_GROUPS = 4
EXPERTS_PER_GROUP = 4
TOP_K = 2
EXPERT_FF = 512
N_MOD = 6
ATTN_SCALE = HEAD_DIM ** -0.5
LOG2E = 1.4426950408889634
EPS = 1e-6
NEG_INF = -1e30

TILE = 256
SUB = 8
LANES = 128
GROUPS = TILE // SUB
L_TOT = CTX_LEN + SEQ
N_TILES = L_TOT // TILE
N_LAT_TILES = SEQ // TILE
N_TOK = BATCH * L_TOT
N_TOK_TILES = N_TOK // TILE
PAIR = 2
LRU_CH = 256
MOD_ROWS = 8
VMEM_LIMIT = 56 * 1024 * 1024

PACK_W = D_MODEL // 2
HI_MASK = -65536
N_ASSIGN = TOP_K * N_TOK
EXP_TILE = 512
N_EXP_TILES = N_ASSIGN // EXP_TILE
N_ITEMS = N_EXP_TILES + N_EXPERTS - 1
ROUTE_ROWS = 2 * N_EXPERTS
POS_TILES = 4

SC_CORES = 2
SC_SUBCORES = 16
SC_WORKERS = SC_CORES * SC_SUBCORES
SC_ROWS = N_TOK // SC_WORKERS
SC_CHUNK = 96
SC_N_CHUNKS = SC_ROWS // SC_CHUNK


def _cparams(sem):
    return pltpu.CompilerParams(dimension_semantics=sem, vmem_limit_bytes=VMEM_LIMIT)


def _pack_rows(v):
    lo = pltpu.bitcast(v[:, :PACK_W].astype(BF16).astype(F32), I32)
    hi = pltpu.bitcast(v[:, PACK_W:].astype(BF16).astype(F32), I32)
    return ((lo >> 16) & 0xFFFF) | (hi & HI_MASK)


def _unpack_rows(p):
    return pltpu.bitcast(p << 16, F32), pltpu.bitcast(p & HI_MASK, F32)


def _mod_kernel(c_ref, w_ref, b_ref, o_ref):
    c = c_ref[...]
    s = c * jax.nn.sigmoid(c)
    o_ref[0] = jnp.dot(s.astype(BF16), w_ref[0].astype(BF16), preferred_element_type=F32) + b_ref[0]


def _modulation(cs, w_mod, b_mod):
    return pl.pallas_call(
        _mod_kernel,
        grid=(DEPTH, N_MOD),
        in_specs=[
            pl.BlockSpec((MOD_ROWS, D_MODEL), lambda l, n: (0, 0)),
            pl.BlockSpec((1, D_MODEL, D_MODEL), lambda l, n: (l, 0, n)),
            pl.BlockSpec((1, 1, D_MODEL), lambda l, n: (l, 0, n)),
        ],
        out_specs=pl.BlockSpec((1, MOD_ROWS, D_MODEL), lambda l, n: (l, 0, n)),
        out_shape=jax.ShapeDtypeStruct((DEPTH, MOD_ROWS, N_MOD * D_MODEL), F32),
        compiler_params=_cparams(("arbitrary", "arbitrary")),
        name="modulation",
    )(cs, w_mod, b_mod.reshape(DEPTH, 1, N_MOD * D_MODEL))


def _mod_row(b, i):
    return jnp.where(i == 0, BATCH, b)


def _moe_residual(x, g0, g1, wt, gate_row):
    lo0, hi0 = _unpack_rows(g0)
    lo1, hi1 = _unpack_rows(g1)
    w0 = wt[:, 0:1]
    w1 = wt[:, 1:2]
    f = jnp.concatenate([w0 * lo0 + w1 * lo1, w0 * hi0 + w1 * hi1], axis=1)
    return x + gate_row * f


def _pair_tok_spec(width):
    return pl.BlockSpec((PAIR, TILE, width), lambda bp, i: (bp, i, 0))


def _pair_mod_specs(layer):
    return [pl.BlockSpec((1, 1, N_MOD, D_MODEL), lambda bp, i, k=k: (layer, _mod_row(PAIR * bp + k, i), 0, 0))
            for k in range(PAIR)]


def _full_spec(shape):
    return pl.BlockSpec(shape, lambda bp, i: tuple(0 for _ in shape))


def _in_kernel(has_prev, *refs):
    if has_prev:
        x_ref, g0_ref, g1_ref, wt_ref = refs[:4]
        mprev_refs = refs[4:4 + PAIR]
        refs = (x_ref,) + refs[4 + PAIR:]
    mod_refs = refs[1:1 + PAIR]
    nrm_ref, w_ref, bd_ref, qg_ref, kg_ref = refs[1 + PAIR:6 + PAIR]
    outs = refs[6 + PAIR:]
    if has_prev:
        xo_ref, outs = outs[0], outs[1:]
    q_ref, k_ref, v_ref, u_ref, g_ref = outs
    x_ref = refs[0]
    hs = []
    for k in range(PAIR):
        x = x_ref[k]
        if has_prev:
            x = _moe_residual(x, g0_ref[k], g1_ref[k], wt_ref[k], mprev_refs[k][0, 0][5:6, :])
            xo_ref[k] = x
        m = mod_refs[k][0, 0]
        ms = jnp.mean(x * x, axis=-1, keepdims=True)
        h = (x * lax.rsqrt(ms + EPS)) * nrm_ref[...]
        hs.append((h * (1.0 + m[1:2, :]) + m[0:1, :]).astype(BF16))
    acc = jnp.dot(jnp.concatenate(hs, axis=0), w_ref[...], preferred_element_type=F32)
    bd = bd_ref[...]

    def head_norm(t, gain):
        ss = jnp.dot((t * t).astype(BF16), bd, preferred_element_type=F32)
        return (t * lax.rsqrt(ss + EPS)) * gain

    def put(ref, val):
        for k in range(PAIR):
            ref[k] = val[k * TILE:(k + 1) * TILE, :].astype(BF16)

    put(q_ref, head_norm(acc[:, 0:NA_WIDTH], qg_ref[...]))
    put(k_ref, head_norm(acc[:, NA_WIDTH:2 * NA_WIDTH], kg_ref[...]))
    put(v_ref, acc[:, 2 * NA_WIDTH:3 * NA_WIDTH])
    put(u_ref, acc[:, 3 * NA_WIDTH:3 * NA_WIDTH + LRU_WIDTH])
    put(g_ref, acc[:, 3 * NA_WIDTH + LRU_WIDTH:])


def _in_proj(l, xs, moe_prev, mod, nrm, w_in, bd, qg, kg):
    has_prev = moe_prev is not None
    in_specs = [_pair_tok_spec(D_MODEL)]
    args = [xs]
    if has_prev:
        in_specs += [_pair_tok_spec(PACK_W), _pair_tok_spec(PACK_W), _pair_tok_spec(LANES)] + _pair_mod_specs(l - 1)
        args += list(moe_prev) + [mod] * PAIR
    in_specs += _pair_mod_specs(l) + [_full_spec((1, D_MODEL)), _full_spec((D_MODEL, IN_COLS)),
                                      _full_spec((NA_WIDTH, NA_WIDTH)), _full_spec((1, NA_WIDTH)),
                                      _full_spec((1, NA_WIDTH))]
    args += [mod] * PAIR + [nrm, w_in, bd, qg, kg]
    half = jax.ShapeDtypeStruct((BATCH, L_TOT, NA_WIDTH), BF16)
    out_shape = [half] * 5
    out_specs = [_pair_tok_spec(NA_WIDTH)] * 5
    if has_prev:
        out_shape = [jax.ShapeDtypeStruct((BATCH, L_TOT, D_MODEL), F32)] + out_shape
        out_specs = [_pair_tok_spec(D_MODEL)] + out_specs
    outs = pl.pallas_call(
        functools.partial(_in_kernel, has_prev),
        grid=(BATCH // PAIR, N_TILES),
        in_specs=in_specs,
        out_specs=out_specs,
        out_shape=out_shape,
        compiler_params=_cparams(("parallel", "parallel")),
        name="in_proj",
    )(*args)
    if has_prev:
        return outs[0], outs[1:]
    return xs, outs


def _attn_kernel(q_ref, kp_ref, kc_ref, kn_ref, kx_ref, vp_ref, vc_ref, vn_ref, vx_ref, bias_ref, o_ref):
    pair_w = 2 * HEAD_DIM
    lane = lax.broadcasted_iota(jnp.int32, (TILE, pair_w), 1)
    low = lane < HEAD_DIM
    k_refs = (kp_ref, kc_ref, kn_ref, kx_ref)
    v_refs = (vp_ref, vc_ref, vn_ref, vx_ref)
    nt = (((1,), (1,)), ((), ()))
    for hp in range(NA_HEADS // 2):
        cols = slice(hp * pair_w, (hp + 1) * pair_w)
        q = q_ref[0, :, cols]
        kb = [r[0, :, cols] for r in k_refs]
        vb = [r[0, :, cols] for r in v_refs]
        outs = []
        for hh in range(2):
            own = low if hh == 0 else jnp.logical_not(low)
            qh = jnp.where(own, q, jnp.zeros_like(q))
            s = [lax.dot_general(qh, kb[i], nt, preferred_element_type=F32) for i in range(4)]
            for i in range(3):
                s[i] = s[i] + bias_ref[0, 2 * hp + hh, :, i * TILE:(i + 1) * TILE]
            m = jnp.max(jnp.maximum(jnp.maximum(s[0], s[1]), jnp.maximum(s[2], s[3])), axis=-1, keepdims=True)
            o = None
            for i in range(4):
                p = jnp.exp2((s[i] - m).astype(BF16))
                va = jnp.where(own, vb[i], jnp.ones_like(vb[i]))
                part = jnp.dot(p, va, preferred_element_type=F32)
                o = part if o is None else o + part
            outs.append(o / pltpu.roll(o, HEAD_DIM, 1))
        o_ref[0, :, cols] = jnp.where(low, outs[0], outs[1]).astype(BF16)


def _attention(q, k, v, bias_tiles):
    last = N_LAT_TILES - 1
    blk = (1, TILE, NA_WIDTH)
    prev_map = lambda b, j: (b, 1 + jnp.clip(j - 2, 0, last), 0)
    cur_map = lambda b, j: (b, jnp.maximum(j, 1), 0)
    next_map = lambda b, j: (b, 1 + jnp.clip(j, 0, last), 0)
    ctx_map = lambda b, j: (b, 0, 0)
    var_map = lambda b, j: (jnp.where(j == 0, 3, jnp.where(j == 1, 0, jnp.where(j == N_TILES - 1, 2, 1))), 0, 0, 0)
    kv_specs = [pl.BlockSpec(blk, prev_map), pl.BlockSpec(blk, cur_map), pl.BlockSpec(blk, next_map),
                pl.BlockSpec(blk, ctx_map)]
    return pl.pallas_call(
        _attn_kernel,
        grid=(BATCH, N_TILES),
        in_specs=[pl.BlockSpec(blk, lambda b, j: (b, j, 0))] + kv_specs + kv_specs
        + [pl.BlockSpec((1, NA_HEADS, TILE, 3 * TILE), var_map)],
        out_specs=pl.BlockSpec(blk, lambda b, j: (b, j, 0)),
        out_shape=jax.ShapeDtypeStruct((BATCH, L_TOT, NA_WIDTH), BF16),
        compiler_params=_cparams(("parallel", "arbitrary")),
        name="na_attention",
    )(q, k, k, k, k, v, v, v, v, bias_tiles)


def _softplus(x):
    return jnp.maximum(x, 0.0) + jnp.log1p(jnp.exp(-jnp.abs(x)))


def _lru_kernel(u_ref, g_ref, cw_ref, cb_ref, wr_ref, wi_ref, br_ref, bi_ref, lam_ref, y_ref,
                hf_scr, hl_scr, p_scr):
    C = LRU_CH
    cw = cw_ref[...]
    cb = cb_ref[...]
    sub = lax.broadcasted_iota(jnp.int32, (SUB, C), 0)

    def conv_tile(c, r0):
        U = u_ref[0, pl.ds(r0, TILE), :].astype(F32)
        pstart = pl.multiple_of(jnp.maximum(r0 - 16, 0), 16)
        nstart = pl.multiple_of(jnp.minimum(r0 + TILE, L_TOT - 16), 16)
        prev16 = u_ref[0, pl.ds(pstart, 16), :].astype(F32)
        next16 = u_ref[0, pl.ds(nstart, 16), :].astype(F32)
        has_prev = jnp.where(c >= 2, 1.0, 0.0).astype(F32)
        has_next = jnp.where(jnp.logical_and(c >= 1, c <= N_TILES - 2), 1.0, 0.0).astype(F32)
        prow = prev16[15:16, :] * has_prev
        n0 = next16[0:1, :] * has_next
        n8 = next16[8:9, :] * has_next
        first8 = jnp.where(sub == 0, prow, pltpu.roll(U[TILE - SUB:TILE, :], 1, 0))
        last_a = jnp.where(sub == SUB - 1, n0, pltpu.roll(U[0:SUB, :], SUB - 1, 0))
        last_b = jnp.where(sub == SUB - 1, n8, pltpu.roll(U[SUB:2 * SUB, :], SUB - 1, 0))
        um1 = jnp.concatenate([first8, U[0:TILE - SUB, :]], axis=0)
        up1 = jnp.concatenate([U[SUB:TILE, :], last_a], axis=0)
        up2 = jnp.concatenate([U[2 * SUB:TILE, :], last_a, last_b], axis=0)
        return cw[0:1, :] * um1 + cw[1:2, :] * U + cw[2:3, :] * up1 + cw[3:4, :] * up2 + cb

    def coeffs(v, d):
        vb = v.astype(BF16)
        r = jax.nn.sigmoid(jnp.dot(vb, wr_ref[d, 0], preferred_element_type=F32) + br_ref[d:d + 1, :])
        i = jax.nn.sigmoid(jnp.dot(vb, wi_ref[d, 0], preferred_element_type=F32) + bi_ref[d:d + 1, :])
        log_a = (-LRU_C * r) * _softplus(-lam_ref[d:d + 1, :])
        a = jnp.exp(log_a)
        b = jnp.sqrt(1.0 - a * a) * (i * v)
        return a, b

    def scan_tile(a, b, h_in, reverse):
        order = range(GROUPS - 1, -1, -1) if reverse else range(GROUPS)
        hl = None
        for g in order:
            ag = a[g * SUB:(g + 1) * SUB, :]
            bg = b[g * SUB:(g + 1) * SUB, :]
            if hl is None:
                hl, p = bg, ag
            else:
                hl = ag * hl + bg
                p = ag * p
            hl_scr[g * SUB:(g + 1) * SUB, :] = hl
            p_scr[g * SUB:(g + 1) * SUB, :] = p
        blocks = range(SUB - 1, -1, -1) if reverse else range(SUB)
        carry = h_in
        cins = {}
        for s in blocks:
            cins[s] = carry
            carry = hl[s:s + 1, :] + p[s:s + 1, :] * carry
        cin = jnp.concatenate([cins[s] for s in range(SUB)], axis=0)
        hfull = hl_scr[...] + p_scr[...] * jnp.tile(cin, (GROUPS, 1))
        return hfull, carry

    zero = jnp.zeros((1, C), F32)

    def fwd_body(c, h):
        r0 = pl.multiple_of(c * TILE, TILE)
        a, b = coeffs(conv_tile(c, r0), 0)
        hfull, h = scan_tile(a, b, h, False)
        hf_scr[pl.ds(r0, TILE), :] = hfull
        return h

    lax.fori_loop(0, N_TILES, fwd_body, zero)

    def rev_tile(c, h):
        r0 = pl.multiple_of(c * TILE, TILE)
        a, b = coeffs(conv_tile(c, r0), 1)
        hfull, h = scan_tile(a, b, h, True)
        gate = jax.nn.gelu(g_ref[0, pl.ds(r0, TILE), :].astype(F32))
        y_ref[0, pl.ds(r0, TILE), :] = (gate * (hf_scr[pl.ds(r0, TILE), :] + hfull)).astype(BF16)
        return h

    h_ctx = rev_tile(jnp.int32(0), zero)
    lax.fori_loop(0, N_LAT_TILES, lambda i, h: rev_tile(N_TILES - 1 - i, h), h_ctx)


def _rglru(u, g, cw, cb, wr, wi, br, bi, lam):
    n_cb = LRU_WIDTH // LRU_CH
    seq = pl.BlockSpec((1, L_TOT, LRU_CH), lambda b, c: (b, 0, c))
    vec = lambda rows: pl.BlockSpec((rows, LRU_CH), lambda b, c: (0, c))
    mat = pl.BlockSpec((2, 1, LRU_CH, LRU_CH), lambda b, c: (0, c, 0, 0))
    return pl.pallas_call(
        _lru_kernel,
        grid=(BATCH, n_cb),
        in_specs=[seq, seq, vec(CONV_W), vec(1), mat, mat, vec(2), vec(2), vec(2)],
        out_specs=seq,
        out_shape=jax.ShapeDtypeStruct((BATCH, L_TOT, LRU_WIDTH), BF16),
        scratch_shapes=[pltpu.VMEM((L_TOT, LRU_CH), F32), pltpu.VMEM((TILE, LRU_CH), F32),
                        pltpu.VMEM((TILE, LRU_CH), F32)],
        compiler_params=_cparams(("parallel", "parallel")),
        name="rglru",
    )(u, g, cw, cb, wr, wi, br, bi, lam)


def _route(sel, aff):
    def top2_sum(a, b, c, d):
        hi1, lo1 = jnp.maximum(a, b), jnp.minimum(a, b)
        hi2, lo2 = jnp.maximum(c, d), jnp.minimum(c, d)
        return jnp.maximum(hi1, hi2) + jnp.maximum(jnp.minimum(hi1, hi2), jnp.maximum(lo1, lo2))

    scores = [top2_sum(*sel[EXPERTS_PER_GROUP * g:EXPERTS_PER_GROUP * (g + 1)]) for g in range(N_GROUPS)]
    best = jnp.zeros_like(scores[0], dtype=jnp.int32)
    best_v = scores[0]
    for g in range(1, N_GROUPS):
        upd = scores[g] > best_v
        best = jnp.where(upd, g, best)
        best_v = jnp.where(upd, scores[g], best_v)
    chosen = []
    for e in range(N_EXPERTS):
        g = e // EXPERTS_PER_GROUP
        rank = jnp.zeros_like(best)
        for o in range(EXPERTS_PER_GROUP * g, EXPERTS_PER_GROUP * (g + 1)):
            if o == e:
                continue
            ahead = sel[o] > sel[e]
            if o < e:
                ahead = jnp.logical_or(ahead, sel[o] == sel[e])
            rank = rank + ahead.astype(jnp.int32)
        chosen.append(jnp.logical_and(best == g, rank < TOP_K))
    total = jnp.zeros_like(aff[0])
    for e in range(N_EXPERTS):
        total = total + jnp.where(chosen[e], aff[e], 0.0)
    gates = [jnp.where(chosen[e], aff[e] / total, 0.0) for e in range(N_EXPERTS)]
    return gates, [c.astype(F32) for c in chosen]


def _out_kernel(x_ref, ya_ref, yb_ref, *refs):
    mod_refs = refs[:PAIR]
    nrm_ref, w_ref, rwh_ref, rwl_ref, rb_ref, xo_ref, hp_ref, rt_ref = refs[PAIR:]
    rows = PAIR * TILE
    ya = ya_ref[...].reshape(rows, NA_WIDTH)
    yb = yb_ref[...].reshape(rows, LRU_WIDTH)
    y = jnp.dot(ya, w_ref[0:NA_WIDTH, :], preferred_element_type=F32)
    y = y + jnp.dot(yb, w_ref[NA_WIDTH:, :], preferred_element_type=F32)
    hs = []
    for k in range(PAIR):
        m = mod_refs[k][0, 0]
        x = x_ref[k] + m[2:3, :] * y[k * TILE:(k + 1) * TILE, :]
        xo_ref[k] = x
        ms = jnp.mean(x * x, axis=-1, keepdims=True)
        h = (x * lax.rsqrt(ms + EPS)) * nrm_ref[...]
        h = h * (1.0 + m[4:5, :]) + m[3:4, :]
        hp_ref[k] = _pack_rows(h)
        hs.append(h)
    h = jnp.concatenate(hs, axis=0)
    h_hi = h.astype(BF16)
    h_lo = (h - h_hi.astype(F32)).astype(BF16)
    nt = (((1,), (1,)), ((), ()))
    lg = (lax.dot_general(rwh_ref[...], h_hi, nt, preferred_element_type=F32)
          + lax.dot_general(rwh_ref[...], h_lo, nt, preferred_element_type=F32)
          + lax.dot_general(rwl_ref[...], h_hi, nt, preferred_element_type=F32))
    aff_all = jax.nn.sigmoid(lg)
    sel_all = aff_all + rb_ref[...]
    aff = [aff_all[e:e + 1, :] for e in range(N_EXPERTS)]
    sel = [sel_all[e:e + 1, :] for e in range(N_EXPERTS)]
    gates, chosen = _route(sel, aff)
    rt = jnp.concatenate(gates + chosen, axis=0)
    for k in range(PAIR):
        rt_ref[k, 0] = rt[:, k * TILE:(k + 1) * TILE]


def _out_proj(l, xs, ya, yb, mod, nrm, w_out, rwh, rwl, rb):
    return pl.pallas_call(
        _out_kernel,
        grid=(BATCH // PAIR, N_TILES),
        in_specs=[_pair_tok_spec(D_MODEL), _pair_tok_spec(NA_WIDTH), _pair_tok_spec(LRU_WIDTH)]
        + _pair_mod_specs(l)
        + [_full_spec((1, D_MODEL)), _full_spec((D_MODEL, D_MODEL)), _full_spec((N_EXPERTS, D_MODEL)),
           _full_spec((N_EXPERTS, D_MODEL)), _full_spec((N_EXPERTS, 1))],
        out_specs=[_pair_tok_spec(D_MODEL), _pair_tok_spec(PACK_W),
                   pl.BlockSpec((PAIR, 1, ROUTE_ROWS, TILE), lambda bp, i: (bp, i, 0, 0))],
        out_shape=[jax.ShapeDtypeStruct((BATCH, L_TOT, D_MODEL), F32),
                   jax.ShapeDtypeStruct((BATCH, L_TOT, PACK_W), I32),
                   jax.ShapeDtypeStruct((BATCH, N_TILES, ROUTE_ROWS, TILE), F32)],
        compiler_params=_cparams(("parallel", "parallel")),
        name="out_proj_router",
    )(xs, ya, yb, *([mod] * PAIR), nrm, w_out, rwh, rwl, rb)


def _pos_kernel(rt_ref, tri_ref, pos_ref, wt_ref, cnt_ref, run_scr, start_scr):
    phase = pl.program_id(0)
    i = pl.program_id(1)

    @pl.when(jnp.logical_and(phase == 0, i == 0))
    def _():
        run_scr[...] = jnp.zeros_like(run_scr)

    @pl.when(jnp.logical_and(phase == 1, i == 0))
    def _():
        tot = run_scr[...]
        cnt_ref[...] = jnp.broadcast_to(tot, (N_EXPERTS, LANES))
        acc = jnp.zeros((1, 1), F32)
        rows = []
        for e in range(N_EXPERTS):
            rows.append(acc)
            acc = acc + tot[e:e + 1, :]
        start_scr[...] = jnp.concatenate(rows, axis=0)
        run_scr[...] = jnp.zeros_like(run_scr)

    @pl.when(phase == 0)
    def _():
        tot = run_scr[...]
        for k in range(POS_TILES):
            tot = tot + jnp.sum(rt_ref[k, N_EXPERTS:, :], axis=1, keepdims=True)
        run_scr[...] = tot

    @pl.when(phase == 1)
    def _():
        base = start_scr[...] + run_scr[...]
        for k in range(POS_TILES):
            gates = rt_ref[k, 0:N_EXPERTS, :]
            chosen = rt_ref[k, N_EXPERTS:, :]
            rank = jnp.dot(chosen.astype(BF16), tri_ref[...], preferred_element_type=F32)
            posf = rank + base
            seen = jnp.zeros((1, TILE), F32)
            p0 = jnp.zeros((1, TILE), F32)
            p1 = jnp.zeros((1, TILE), F32)
            w0 = jnp.zeros((1, TILE), F32)
            w1 = jnp.zeros((1, TILE), F32)
            for e in range(N_EXPERTS):
                ch = chosen[e:e + 1, :]
                first = ch * (1.0 - seen)
                second = ch * seen
                p0 = p0 + first * posf[e:e + 1, :]
                p1 = p1 + second * posf[e:e + 1, :]
                w0 = w0 + first * gates[e:e + 1, :]
                w1 = w1 + second * gates[e:e + 1, :]
                seen = jnp.minimum(seen + ch, 1.0)
            pos_ref[k] = jnp.concatenate([p0, p1], axis=0).astype(I32)
            wpad = jnp.concatenate([w0, w1, jnp.zeros((LANES - TOP_K, TILE), F32)], axis=0)
            wt_ref[k * TILE:(k + 1) * TILE, :] = jnp.transpose(wpad)
            base = base + jnp.sum(chosen, axis=1, keepdims=True)
        run_scr[...] = base - start_scr[...]


def _positions(route, tri):
    return pl.pallas_call(
        _pos_kernel,
        grid=(2, N_TOK_TILES // POS_TILES),
        in_specs=[pl.BlockSpec((POS_TILES, ROUTE_ROWS, TILE), lambda p, i: (i, 0, 0)),
                  pl.BlockSpec((TILE, TILE), lambda p, i: (0, 0))],
        out_specs=[pl.BlockSpec((POS_TILES, TOP_K, TILE), lambda p, i: (i * p, 0, 0)),
                   pl.BlockSpec((POS_TILES * TILE, LANES), lambda p, i: (i * p, 0)),
                   pl.BlockSpec((N_EXPERTS, LANES), lambda p, i: (0, 0))],
        out_shape=[jax.ShapeDtypeStruct((N_TOK_TILES, TOP_K, TILE), I32),
                   jax.ShapeDtypeStruct((N_TOK, LANES), F32),
                   jax.ShapeDtypeStruct((N_EXPERTS, LANES), F32)],
        scratch_shapes=[pltpu.VMEM((N_EXPERTS, 1), F32), pltpu.VMEM((N_EXPERTS, 1), F32)],
        compiler_params=_cparams(("arbitrary", "arbitrary")),
        name="moe_positions",
    )(route, tri)


def _work_items(counts):
    ends = jnp.cumsum(counts)
    starts = ends - counts
    first_tile = starts // EXP_TILE
    n_tiles = jnp.where(counts > 0, (ends - 1) // EXP_TILE - first_tile + 1, 0)
    item_end = jnp.cumsum(n_tiles)
    item_start = item_end - n_tiles
    idx = jnp.arange(N_ITEMS, dtype=I32)
    n_valid = item_end[-1]
    valid = idx < n_valid
    e = jnp.sum(jnp.minimum(idx, n_valid - 1)[:, None] >= item_end[None, :], axis=1).astype(I32)
    e = jnp.minimum(e, N_EXPERTS - 1)
    onehot = (e[:, None] == jnp.arange(N_EXPERTS, dtype=I32)[None, :]).astype(I32)
    pick = lambda vec: jnp.sum(onehot * vec[None, :].astype(I32), axis=1)
    tile = jnp.where(valid, pick(first_tile) + (idx - pick(item_start)), N_EXP_TILES - 1).astype(I32)
    lo = jnp.where(valid, jnp.clip(pick(starts) - tile * EXP_TILE, 0, EXP_TILE), 0).astype(I32)
    hi = jnp.where(valid, jnp.clip(pick(ends) - tile * EXP_TILE, 0, EXP_TILE), 0).astype(I32)
    return tile, e, lo, hi


def _sc_worker_base():
    return (lax.axis_index("s") * SC_CORES + lax.axis_index("c")) * SC_ROWS


def _sc_dispatch_body(h_hbm, p0_hbm, p1_hbm, out_hbm, i0_v, i1_v, rows_v, sem):
    base = _sc_worker_base()

    @pl.loop(0, SC_N_CHUNKS)
    def _(j):
        off = base + j * SC_CHUNK
        pltpu.sync_copy(h_hbm.at[pl.ds(off, SC_CHUNK)], rows_v)
        pltpu.sync_copy(p0_hbm.at[pl.ds(off, SC_CHUNK)], i0_v)
        pltpu.sync_copy(p1_hbm.at[pl.ds(off, SC_CHUNK)], i1_v)
        c0 = pltpu.async_copy(rows_v, out_hbm.at[i0_v], sem)
        c1 = pltpu.async_copy(rows_v, out_hbm.at[i1_v], sem)
        c0.wait()
        c1.wait()


def _sc_combine_body(y_hbm, p0_hbm, p1_hbm, g0_hbm, g1_hbm, i_v, rows_v, sem):
    base = _sc_worker_base()

    @pl.loop(0, SC_N_CHUNKS)
    def _(j):
        off = base + j * SC_CHUNK
        for p_hbm, g_hbm in ((p0_hbm, g0_hbm), (p1_hbm, g1_hbm)):
            pltpu.sync_copy(p_hbm.at[pl.ds(off, SC_CHUNK)], i_v)
            pltpu.async_copy(y_hbm.at[i_v], rows_v, sem).wait()
            pltpu.sync_copy(rows_v, g_hbm.at[pl.ds(off, SC_CHUNK)])


def _sc_mesh():
    return plsc.VectorSubcoreMesh(core_axis_name="c", subcore_axis_name="s")


def _dispatch(hp, pos0, pos1):
    return pl.kernel(
        _sc_dispatch_body,
        out_type=jax.ShapeDtypeStruct((N_ASSIGN, PACK_W), I32),
        mesh=_sc_mesh(),
        scratch_types=[pltpu.VMEM((SC_CHUNK,), I32), pltpu.VMEM((SC_CHUNK,), I32),
                       pltpu.VMEM((SC_CHUNK, PACK_W), I32), pltpu.SemaphoreType.DMA],
        name="moe_dispatch",
    )(hp, pos0, pos1)


def _combine(ys, pos0, pos1):
    row = jax.ShapeDtypeStruct((N_TOK, PACK_W), I32)
    return pl.kernel(
        _sc_combine_body,
        out_type=[row, row],
        mesh=_sc_mesh(),
        scratch_types=[pltpu.VMEM((SC_CHUNK,), I32), pltpu.VMEM((SC_CHUNK, PACK_W), I32),
                       pltpu.SemaphoreType.DMA],
        name="moe_combine",
    )(ys, pos0, pos1)


def _expert_kernel(tile_ref, exp_ref, lo_ref, hi_ref, x_ref, w1_ref, w3_ref, w2_ref, o_ref):
    i = pl.program_id(0)
    lo_f, hi_f = _unpack_rows(x_ref[...])
    h = jnp.concatenate([lo_f.astype(BF16), hi_f.astype(BF16)], axis=1)
    a = jnp.dot(h, w1_ref[0], preferred_element_type=F32)
    b = jnp.dot(h, w3_ref[0], preferred_element_type=F32)
    t = ((a * jax.nn.sigmoid(a)) * b).astype(BF16)
    y = _pack_rows(jnp.dot(t, w2_ref[0], preferred_element_type=F32))
    row = lax.broadcasted_iota(jnp.int32, (EXP_TILE, PACK_W), 0)
    mine = jnp.logical_and(row >= lo_ref[i], row < hi_ref[i])
    revisit = jnp.logical_and(i > 0, tile_ref[i] == tile_ref[jnp.maximum(i - 1, 0)])

    @pl.when(jnp.logical_not(revisit))
    def _():
        o_ref[...] = jnp.where(mine, y, 0)

    @pl.when(revisit)
    def _():
        o_ref[...] = jnp.where(mine, y, o_ref[...])


def _experts(xs_sorted, items, w1, w3, w2):
    tile, e, lo, hi = items
    grid_spec = pltpu.PrefetchScalarGridSpec(
        num_scalar_prefetch=4,
        grid=(N_ITEMS,),
        in_specs=[pl.BlockSpec((EXP_TILE, PACK_W), lambda i, t, e, lo, hi: (t[i], 0)),
                  pl.BlockSpec((1, D_MODEL, EXPERT_FF), lambda i, t, e, lo, hi: (e[i], 0, 0)),
                  pl.BlockSpec((1, D_MODEL, EXPERT_FF), lambda i, t, e, lo, hi: (e[i], 0, 0)),
                  pl.BlockSpec((1, EXPERT_FF, D_MODEL), lambda i, t, e, lo, hi: (e[i], 0, 0))],
        out_specs=pl.BlockSpec((EXP_TILE, PACK_W), lambda i, t, e, lo, hi: (t[i], 0)),
    )
    return pl.pallas_call(
        _expert_kernel,
        grid_spec=grid_spec,
        out_shape=jax.ShapeDtypeStruct((N_ASSIGN, PACK_W), I32),
        compiler_params=_cparams(("arbitrary",)),
        name="moe_experts",
    )(tile, e, lo, hi, xs_sorted, w1, w3, w2)


def _moe(hp, route, tri, w1, w3, w2):
    pos, wt, cnt = _positions(route, tri)
    pos0 = pos[:, 0, :].reshape(N_TOK)
    pos1 = pos[:, 1, :].reshape(N_TOK)
    items = _work_items(cnt[:, 0].astype(I32))
    xs_sorted = _dispatch(hp.reshape(N_TOK, PACK_W), pos0, pos1)
    ys = _experts(xs_sorted, items, w1, w3, w2)
    g0, g1 = _combine(ys, pos0, pos1)
    shape = (BATCH, L_TOT, PACK_W)
    return g0.reshape(shape), g1.reshape(shape), wt.reshape(BATCH, L_TOT, LANES)


def _final_kernel(x_ref, g0_ref, g1_ref, wt_ref, mod_ref, o_ref):
    o_ref[0] = _moe_residual(x_ref[0], g0_ref[0], g1_ref[0], wt_ref[0], mod_ref[0, 0][5:6, :])


def _final(xs, moe, mod):
    lat = lambda width: pl.BlockSpec((1, TILE, width), lambda b, j: (b, j + 1, 0))
    return pl.pallas_call(
        _final_kernel,
        grid=(BATCH, N_LAT_TILES),
        in_specs=[lat(D_MODEL), lat(PACK_W), lat(PACK_W), lat(LANES),
                  pl.BlockSpec((1, 1, N_MOD, D_MODEL), lambda b, j: (DEPTH - 1, b, 0, 0))],
        out_specs=pl.BlockSpec((1, TILE, D_MODEL), lambda b, j: (b, j, 0)),
        out_shape=jax.ShapeDtypeStruct((BATCH, SEQ, D_MODEL), F32),
        compiler_params=_cparams(("parallel", "parallel")),
        name="final_residual",
    )(xs, *moe, mod)


def _to_scan_major(t):
    b, n, d = t.shape
    return t.reshape(b, n // TILE, SUB, GROUPS, d).transpose(0, 1, 3, 2, 4).reshape(b, n, d)


def _from_scan_major(t):
    b, n, d = t.shape
    return t.reshape(b, n // TILE, GROUPS, SUB, d).transpose(0, 1, 3, 2, 4).reshape(b, n, d)


def _bias_tiles(table):
    rows_q = TILE // GRID_W
    rows_k = 3 * rows_q
    a = np.arange(rows_q)[:, None]
    c = np.arange(rows_k)[None, :]
    r_idx = np.clip(c - a + 3, 0, 2 * NA_WIN_ROWS - 2)
    qc = np.arange(GRID_W)[:, None]
    kc = np.arange(GRID_W)[None, :]
    col0 = np.clip(qc - NA_WIN_COLS // 2, 0, GRID_W - NA_WIN_COLS)
    in_win = (kc >= col0) & (kc < col0 + NA_WIN_COLS)
    pad = GRID_W - NA_WIN_COLS
    tp = jnp.pad(table.astype(F32), ((0, 0), (0, 0), (pad, pad)))
    toeplitz = jnp.stack([tp[:, :, GRID_W - 1 - q:2 * GRID_W - 1 - q] for q in range(GRID_W)], axis=2)
    blk = jnp.stack([jnp.stack([toeplitz[:, int(r_idx[i, j])] for j in range(rows_k)], axis=1)
                     for i in range(rows_q)], axis=1)
    allowed = np.stack([
        np.broadcast_to(c >= rows_q, (rows_q, rows_k)),
        (c >= a) & (c <= a + NA_WIN_ROWS - 1),
        np.broadcast_to(c < 2 * rows_q, (rows_q, rows_k)),
        np.zeros((rows_q, rows_k), bool),
    ])
    mask = allowed[:, None, :, :, None, None] & in_win[None, None, None, None, :, :]
    tiles = jnp.where(mask, blk[None] * LOG2E, NEG_INF)
    half = GRID_W // GROUPS
    tiles = tiles.reshape(4, NA_HEADS, rows_q, 3, rows_q, half, GROUPS, half, GROUPS)
    tiles = tiles.transpose(0, 1, 6, 2, 5, 3, 8, 4, 7)
    return tiles.reshape(4, NA_HEADS, TILE, 3 * TILE)


def _block_diag(w, n_chunks):
    per = LRU_BLOCKS // n_chunks
    w = w.reshape(2, n_chunks, per, LRU_BLOCK, LRU_BLOCK)
    eye = jnp.eye(per, dtype=w.dtype)
    out = jnp.einsum('dcpij,pq->dcpiqj', w, eye)
    return out.reshape(2, n_chunks, per * LRU_BLOCK, per * LRU_BLOCK)


def kernel(x, c, ctx, c_ctx, w_mod, b_mod, norm_mix, norm_ffn, w_in, w_out, q_gain, k_gain, na_bias,
           conv_w, conv_b, lru_w_r, lru_b_r, lru_w_i, lru_b_i, lru_lambda, router_w, router_b,
           exp_w1, exp_w3, exp_w2):
    cs = jnp.concatenate([c, c_ctx[None, :], jnp.zeros((MOD_ROWS - BATCH - 1, D_MODEL), F32)], axis=0)
    mod = _modulation(cs, w_mod, b_mod).reshape(DEPTH, MOD_ROWS, N_MOD, D_MODEL)

    xs = jnp.concatenate([_to_scan_major(ctx), _to_scan_major(x)], axis=1)
    head_of = np.arange(NA_WIDTH) // HEAD_DIM
    bd = jnp.asarray((head_of[:, None] == head_of[None, :]).astype(np.float32) / HEAD_DIM, BF16)
    tri = jnp.asarray(np.triu(np.ones((TILE, TILE), np.float32), 1), BF16)
    rwt = router_w.T
    rwh = rwt.astype(BF16)
    rwl = (rwt - rwh.astype(F32)).astype(BF16)
    rb = router_b.reshape(N_EXPERTS, 1)
    n_cb = LRU_WIDTH // LRU_CH

    moe = None
    for l in range(DEPTH):
        qg = jnp.tile(q_gain[l] * (ATTN_SCALE * LOG2E), NA_HEADS)[None, :]
        kg = jnp.tile(k_gain[l], NA_HEADS)[None, :]
        xs, (q, k, v, u, g) = _in_proj(l, xs, moe, mod, norm_mix[l][None, :], w_in[l].astype(BF16), bd, qg, kg)
        ya = _attention(q, k, v, _bias_tiles(na_bias[l]))
        yb = _rglru(u, g, conv_w[l], conv_b[l][None, :],
                    _block_diag(lru_w_r[l], n_cb).astype(BF16), _block_diag(lru_w_i[l], n_cb).astype(BF16),
                    lru_b_r[l], lru_b_i[l], lru_lambda[l])
        xs, hp, route = _out_proj(l, xs, ya, yb, mod, norm_ffn[l][None, :], w_out[l].astype(BF16), rwh, rwl, rb)
        moe = _moe(hp, route.reshape(N_TOK_TILES, ROUTE_ROWS, TILE), tri,
                   exp_w1[l].astype(BF16), exp_w3[l].astype(BF16), exp_w2[l].astype(BF16))
    out = _final(xs, moe, mod)
    return _from_scan_major(out)
```

```python
import functools

import jax
import jax.numpy as jnp
import numpy as np
from jax import lax
from jax.experimental import pallas as pl
from jax.experimental.pallas import tpu as pltpu
from jax.experimental.pallas import tpu_sc as plsc

F32 = jnp.float32
BF16 = jnp.bfloat16
I32 = jnp.int32

D_MODEL = 1024
BATCH = 4
SEQ = 8192
DEPTH = 4
GRID_W = 64
CTX_LEN = 256
HEAD_DIM = 64
NA_WIDTH = 512
NA_HEADS = 8
NA_WIN_ROWS = 8
NA_WIN_COLS = 16
LRU_WIDTH = 512
LRU_BLOCKS = 8
LRU_BLOCK = 64
CONV_W = 4
LRU_C = 8.0
IN_COLS = 3 * NA_WIDTH + 2 * LRU_WIDTH
N_EXPERTS = 16
N_GROUPS = 4
EXPERTS_PER_GROUP = 4
TOP_K = 2
EXPERT_FF = 512
N_MOD = 6
ATTN_SCALE = HEAD_DIM ** -0.5
LOG2E = 1.4426950408889634
EPS = 1e-6
NEG_INF = -1e30

TILE = 256
SUB = 8
LANES = 128
GROUPS = TILE // SUB
L_TOT = CTX_LEN + SEQ
N_TILES = L_TOT // TILE
N_LAT_TILES = SEQ // TILE
N_TOK = BATCH * L_TOT
N_TOK_TILES = N_TOK // TILE
PAIR = 2
LRU_CH = 256
MOD_ROWS = 8
VMEM_LIMIT = 56 * 1024 * 1024

PACK_W = D_MODEL // 2
HI_MASK = -65536
N_ASSIGN = TOP_K * N_TOK
EXP_TILE = 512
N_EXP_TILES = N_ASSIGN // EXP_TILE
N_ITEMS = N_EXP_TILES + N_EXPERTS - 1
ROUTE_ROWS = 2 * N_EXPERTS
POS_TILES = 4

SC_CORES = 2
SC_SUBCORES = 16
SC_WORKERS = SC_CORES * SC_SUBCORES
SC_ROWS = N_TOK // SC_WORKERS
SC_CHUNK = 96
SC_N_CHUNKS = SC_ROWS // SC_CHUNK


def _cparams(sem):
    return pltpu.CompilerParams(dimension_semantics=sem, vmem_limit_bytes=VMEM_LIMIT)


def _pack_rows(v):
    lo = pltpu.bitcast(v[:, :PACK_W].astype(BF16).astype(F32), I32)
    hi = pltpu.bitcast(v[:, PACK_W:].astype(BF16).astype(F32), I32)
    return ((lo >> 16) & 0xFFFF) | (hi & HI_MASK)


def _unpack_rows(p):
    return pltpu.bitcast(p << 16, F32), pltpu.bitcast(p & HI_MASK, F32)


def _mod_kernel(c_ref, w_ref, b_ref, o_ref):
    c = c_ref[...]
    s = c * jax.nn.sigmoid(c)
    o_ref[0] = jnp.dot(s.astype(BF16), w_ref[0].astype(BF16), preferred_element_type=F32) + b_ref[0]


def _modulation(cs, w_mod, b_mod):
    return pl.pallas_call(
        _mod_kernel,
        grid=(DEPTH, N_MOD),
        in_specs=[
            pl.BlockSpec((MOD_ROWS, D_MODEL), lambda l, n: (0, 0)),
            pl.BlockSpec((1, D_MODEL, D_MODEL), lambda l, n: (l, 0, n)),
            pl.BlockSpec((1, 1, D_MODEL), lambda l, n: (l, 0, n)),
        ],
        out_specs=pl.BlockSpec((1, MOD_ROWS, D_MODEL), lambda l, n: (l, 0, n)),
        out_shape=jax.ShapeDtypeStruct((DEPTH, MOD_ROWS, N_MOD * D_MODEL), F32),
        compiler_params=_cparams(("arbitrary", "arbitrary")),
        name="modulation",
    )(cs, w_mod, b_mod.reshape(DEPTH, 1, N_MOD * D_MODEL))


def _mod_row(b, i):
    return jnp.where(i == 0, BATCH, b)


def _moe_residual(x, g0, g1, wt, gate_row):
    lo0, hi0 = _unpack_rows(g0)
    lo1, hi1 = _unpack_rows(g1)
    w0 = wt[:, 0:1]
    w1 = wt[:, 1:2]
    f = jnp.concatenate([w0 * lo0 + w1 * lo1, w0 * hi0 + w1 * hi1], axis=1)
    return x + gate_row * f


def _pair_tok_spec(width):
    return pl.BlockSpec((PAIR, TILE, width), lambda bp, i: (bp, i, 0))


def _pair_mod_specs(layer):
    return [pl.BlockSpec((1, 1, N_MOD, D_MODEL), lambda bp, i, k=k: (layer, _mod_row(PAIR * bp + k, i), 0, 0))
            for k in range(PAIR)]


def _full_spec(shape):
    return pl.BlockSpec(shape, lambda bp, i: tuple(0 for _ in shape))


def _in_kernel(has_prev, *refs):
    if has_prev:
        x_ref, g0_ref, g1_ref, wt_ref = refs[:4]
        mprev_refs = refs[4:4 + PAIR]
        refs = (x_ref,) + refs[4 + PAIR:]
    mod_refs = refs[1:1 + PAIR]
    nrm_ref, w_ref, bd_ref, qg_ref, kg_ref = refs[1 + PAIR:6 + PAIR]
    outs = refs[6 + PAIR:]
    if has_prev:
        xo_ref, outs = outs[0], outs[1:]
    q_ref, k_ref, v_ref, u_ref, g_ref = outs
    x_ref = refs[0]
    hs = []
    for k in range(PAIR):
        x = x_ref[k]
        if has_prev:
            x = _moe_residual(x, g0_ref[k], g1_ref[k], wt_ref[k], mprev_refs[k][0, 0][5:6, :])
            xo_ref[k] = x
        m = mod_refs[k][0, 0]
        ms = jnp.mean(x * x, axis=-1, keepdims=True)
        h = (x * lax.rsqrt(ms + EPS)) * nrm_ref[...]
        hs.append((h * (1.0 + m[1:2, :]) + m[0:1, :]).astype(BF16))
    acc = jnp.dot(jnp.concatenate(hs, axis=0), w_ref[...], preferred_element_type=F32)
    bd = bd_ref[...]

    def head_norm(t, gain):
        ss = jnp.dot((t * t).astype(BF16), bd, preferred_element_type=F32)
        return (t * lax.rsqrt(ss + EPS)) * gain

    def put(ref, val):
        for k in range(PAIR):
            ref[k] = val[k * TILE:(k + 1) * TILE, :].astype(BF16)

    put(q_ref, head_norm(acc[:, 0:NA_WIDTH], qg_ref[...]))
    put(k_ref, head_norm(acc[:, NA_WIDTH:2 * NA_WIDTH], kg_ref[...]))
    put(v_ref, acc[:, 2 * NA_WIDTH:3 * NA_WIDTH])
    put(u_ref, acc[:, 3 * NA_WIDTH:3 * NA_WIDTH + LRU_WIDTH])
    put(g_ref, acc[:, 3 * NA_WIDTH + LRU_WIDTH:])


def _in_proj(l, xs, moe_prev, mod, nrm, w_in, bd, qg, kg):
    has_prev = moe_prev is not None
    in_specs = [_pair_tok_spec(D_MODEL)]
    args = [xs]
    if has_prev:
        in_specs += [_pair_tok_spec(PACK_W), _pair_tok_spec(PACK_W), _pair_tok_spec(LANES)] + _pair_mod_specs(l - 1)
        args += list(moe_prev) + [mod] * PAIR
    in_specs += _pair_mod_specs(l) + [_full_spec((1, D_MODEL)), _full_spec((D_MODEL, IN_COLS)),
                                      _full_spec((NA_WIDTH, NA_WIDTH)), _full_spec((1, NA_WIDTH)),
                                      _full_spec((1, NA_WIDTH))]
    args += [mod] * PAIR + [nrm, w_in, bd, qg, kg]
    half = jax.ShapeDtypeStruct((BATCH, L_TOT, NA_WIDTH), BF16)
    out_shape = [half] * 5
    out_specs = [_pair_tok_spec(NA_WIDTH)] * 5
    if has_prev:
        out_shape = [jax.ShapeDtypeStruct((BATCH, L_TOT, D_MODEL), F32)] + out_shape
        out_specs = [_pair_tok_spec(D_MODEL)] + out_specs
    outs = pl.pallas_call(
        functools.partial(_in_kernel, has_prev),
        grid=(BATCH // PAIR, N_TILES),
        in_specs=in_specs,
        out_specs=out_specs,
        out_shape=out_shape,
        compiler_params=_cparams(("parallel", "parallel")),
        name="in_proj",
    )(*args)
    if has_prev:
        return outs[0], outs[1:]
    return xs, outs


def _attn_kernel(q_ref, kp_ref, kc_ref, kn_ref, kx_ref, vp_ref, vc_ref, vn_ref, vx_ref, bias_ref, o_ref):
    pair_w = 2 * HEAD_DIM
    lane = lax.broadcasted_iota(jnp.int32, (TILE, pair_w), 1)
    low = lane < HEAD_DIM
    k_refs = (kp_ref, kc_ref, kn_ref, kx_ref)
    v_refs = (vp_ref, vc_ref, vn_ref, vx_ref)
    nt = (((1,), (1,)), ((), ()))
    for hp in range(NA_HEADS // 2):
        cols = slice(hp * pair_w, (hp + 1) * pair_w)
        q = q_ref[0, :, cols]
        kb = [r[0, :, cols] for r in k_refs]
        vb = [r[0, :, cols] for r in v_refs]
        outs = []
        for hh in range(2):
            own = low if hh == 0 else jnp.logical_not(low)
            qh = jnp.where(own, q, jnp.zeros_like(q))
            s = [lax.dot_general(qh, kb[i], nt, preferred_element_type=F32) for i in range(4)]
            for i in range(3):
                s[i] = s[i] + bias_ref[0, 2 * hp + hh, :, i * TILE:(i + 1) * TILE]
            m = jnp.max(jnp.maximum(jnp.maximum(s[0], s[1]), jnp.maximum(s[2], s[3])), axis=-1, keepdims=True)
            o = None
            for i in range(4):
                p = jnp.exp2((s[i] - m).astype(BF16))
                va = jnp.where(own, vb[i], jnp.ones_like(vb[i]))
                part = jnp.dot(p, va, preferred_element_type=F32)
                o = part if o is None else o + part
            outs.append(o / pltpu.roll(o, HEAD_DIM, 1))
        o_ref[0, :, cols] = jnp.where(low, outs[0], outs[1]).astype(BF16)


def _attention(l, q, k, v, bias_tiles):
    last = N_LAT_TILES - 1
    blk = (1, TILE, NA_WIDTH)
    prev_map = lambda b, j: (b, 1 + jnp.clip(j - 2, 0, last), 0)
    cur_map = lambda b, j: (b, jnp.maximum(j, 1), 0)
    next_map = lambda b, j: (b, 1 + jnp.clip(j, 0, last), 0)
    ctx_map = lambda b, j: (b, 0, 0)
    var_map = lambda b, j: (jnp.where(j == 0, 3, jnp.where(j == 1, 0, jnp.where(j == N_TILES - 1, 2, 1))), l, 0, 0)
    kv_specs = [pl.BlockSpec(blk, prev_map), pl.BlockSpec(blk, cur_map), pl.BlockSpec(blk, next_map),
                pl.BlockSpec(blk, ctx_map)]
    return pl.pallas_call(
        _attn_kernel,
        grid=(BATCH, N_TILES),
        in_specs=[pl.BlockSpec(blk, lambda b, j: (b, j, 0))] + kv_specs + kv_specs
        + [pl.BlockSpec((1, NA_HEADS, TILE, 3 * TILE), var_map)],
        out_specs=pl.BlockSpec(blk, lambda b, j: (b, j, 0)),
        out_shape=jax.ShapeDtypeStruct((BATCH, L_TOT, NA_WIDTH), BF16),
        compiler_params=_cparams(("parallel", "arbitrary")),
        name="na_attention",
    )(q, k, k, k, k, v, v, v, v, bias_tiles)


def _softplus(x):
    return jnp.maximum(x, 0.0) + jnp.log1p(jnp.exp(-jnp.abs(x)))


def _lru_kernel(u_ref, g_ref, cw_ref, cb_ref, wr_ref, wi_ref, br_ref, bi_ref, lam_ref, y_ref,
                hf_scr, hl_scr, p_scr):
    C = LRU_CH
    cw = cw_ref[...]
    cb = cb_ref[...]
    sub = lax.broadcasted_iota(jnp.int32, (SUB, C), 0)

    def conv_tile(c, r0):
        U = u_ref[0, pl.ds(r0, TILE), :].astype(F32)
        pstart = pl.multiple_of(jnp.maximum(r0 - 16, 0), 16)
        nstart = pl.multiple_of(jnp.minimum(r0 + TILE, L_TOT - 16), 16)
        prev16 = u_ref[0, pl.ds(pstart, 16), :].astype(F32)
        next16 = u_ref[0, pl.ds(nstart, 16), :].astype(F32)
        has_prev = jnp.where(c >= 2, 1.0, 0.0).astype(F32)
        has_next = jnp.where(jnp.logical_and(c >= 1, c <= N_TILES - 2), 1.0, 0.0).astype(F32)
        prow = prev16[15:16, :] * has_prev
        n0 = next16[0:1, :] * has_next
        n8 = next16[8:9, :] * has_next
        first8 = jnp.where(sub == 0, prow, pltpu.roll(U[TILE - SUB:TILE, :], 1, 0))
        last_a = jnp.where(sub == SUB - 1, n0, pltpu.roll(U[0:SUB, :], SUB - 1, 0))
        last_b = jnp.where(sub == SUB - 1, n8, pltpu.roll(U[SUB:2 * SUB, :], SUB - 1, 0))
        um1 = jnp.concatenate([first8, U[0:TILE - SUB, :]], axis=0)
        up1 = jnp.concatenate([U[SUB:TILE, :], last_a], axis=0)
        up2 = jnp.concatenate([U[2 * SUB:TILE, :], last_a, last_b], axis=0)
        return cw[0:1, :] * um1 + cw[1:2, :] * U + cw[2:3, :] * up1 + cw[3:4, :] * up2 + cb

    def coeffs(v, d):
        vb = v.astype(BF16)
        r = jax.nn.sigmoid(jnp.dot(vb, wr_ref[d, 0], preferred_element_type=F32) + br_ref[d:d + 1, :])
        i = jax.nn.sigmoid(jnp.dot(vb, wi_ref[d, 0], preferred_element_type=F32) + bi_ref[d:d + 1, :])
        log_a = (-LRU_C * r) * _softplus(-lam_ref[d:d + 1, :])
        a = jnp.exp(log_a)
        b = jnp.sqrt(1.0 - a * a) * (i * v)
        return a, b

    def scan_tile(a, b, h_in, reverse):
        order = range(GROUPS - 1, -1, -1) if reverse else range(GROUPS)
        hl = None
        for g in order:
            ag = a[g * SUB:(g + 1) * SUB, :]
            bg = b[g * SUB:(g + 1) * SUB, :]
            if hl is None:
                hl, p = bg, ag
            else:
                hl = ag * hl + bg
                p = ag * p
            hl_scr[g * SUB:(g + 1) * SUB, :] = hl
            p_scr[g * SUB:(g + 1) * SUB, :] = p
        blocks = range(SUB - 1, -1, -1) if reverse else range(SUB)
        carry = h_in
        cins = {}
        for s in blocks:
            cins[s] = carry
            carry = hl[s:s + 1, :] + p[s:s + 1, :] * carry
        cin = jnp.concatenate([cins[s] for s in range(SUB)], axis=0)
        hfull = hl_scr[...] + p_scr[...] * jnp.tile(cin, (GROUPS, 1))
        return hfull, carry

    zero = jnp.zeros((1, C), F32)

    def fwd_body(c, h):
        r0 = pl.multiple_of(c * TILE, TILE)
        a, b = coeffs(conv_tile(c, r0), 0)
        hfull, h = scan_tile(a, b, h, False)
        hf_scr[pl.ds(r0, TILE), :] = hfull
        return h

    lax.fori_loop(0, N_TILES, fwd_body, zero)

    def rev_tile(c, h):
        r0 = pl.multiple_of(c * TILE, TILE)
        a, b = coeffs(conv_tile(c, r0), 1)
        hfull, h = scan_tile(a, b, h, True)
        gate = jax.nn.gelu(g_ref[0, pl.ds(r0, TILE), :].astype(F32))
        y_ref[0, pl.ds(r0, TILE), :] = (gate * (hf_scr[pl.ds(r0, TILE), :] + hfull)).astype(BF16)
        return h

    h_ctx = rev_tile(jnp.int32(0), zero)
    lax.fori_loop(0, N_LAT_TILES, lambda i, h: rev_tile(N_TILES - 1 - i, h), h_ctx)


def _rglru(u, g, cw, cb, wr, wi, br, bi, lam):
    n_cb = LRU_WIDTH // LRU_CH
    seq = pl.BlockSpec((1, L_TOT, LRU_CH), lambda b, c: (b, 0, c))
    vec = lambda rows: pl.BlockSpec((rows, LRU_CH), lambda b, c: (0, c))
    mat = pl.BlockSpec((2, 1, LRU_CH, LRU_CH), lambda b, c: (0, c, 0, 0))
    return pl.pallas_call(
        _lru_kernel,
        grid=(BATCH, n_cb),
        in_specs=[seq, seq, vec(CONV_W), vec(1), mat, mat, vec(2), vec(2), vec(2)],
        out_specs=seq,
        out_shape=jax.ShapeDtypeStruct((BATCH, L_TOT, LRU_WIDTH), BF16),
        scratch_shapes=[pltpu.VMEM((L_TOT, LRU_CH), F32), pltpu.VMEM((TILE, LRU_CH), F32),
                        pltpu.VMEM((TILE, LRU_CH), F32)],
        compiler_params=_cparams(("parallel", "parallel")),
        name="rglru",
    )(u, g, cw, cb, wr, wi, br, bi, lam)


def _route(sel, aff):
    def top2_sum(a, b, c, d):
        hi1, lo1 = jnp.maximum(a, b), jnp.minimum(a, b)
        hi2, lo2 = jnp.maximum(c, d), jnp.minimum(c, d)
        return jnp.maximum(hi1, hi2) + jnp.maximum(jnp.minimum(hi1, hi2), jnp.maximum(lo1, lo2))

    scores = [top2_sum(*sel[EXPERTS_PER_GROUP * g:EXPERTS_PER_GROUP * (g + 1)]) for g in range(N_GROUPS)]
    best = jnp.zeros_like(scores[0], dtype=jnp.int32)
    best_v = scores[0]
    for g in range(1, N_GROUPS):
        upd = scores[g] > best_v
        best = jnp.where(upd, g, best)
        best_v = jnp.where(upd, scores[g], best_v)
    chosen = []
    for e in range(N_EXPERTS):
        g = e // EXPERTS_PER_GROUP
        rank = jnp.zeros_like(best)
        for o in range(EXPERTS_PER_GROUP * g, EXPERTS_PER_GROUP * (g + 1)):
            if o == e:
                continue
            ahead = sel[o] > sel[e]
            if o < e:
                ahead = jnp.logical_or(ahead, sel[o] == sel[e])
            rank = rank + ahead.astype(jnp.int32)
        chosen.append(jnp.logical_and(best == g, rank < TOP_K))
    total = jnp.zeros_like(aff[0])
    for e in range(N_EXPERTS):
        total = total + jnp.where(chosen[e], aff[e], 0.0)
    gates = [jnp.where(chosen[e], aff[e] / total, 0.0) for e in range(N_EXPERTS)]
    return gates, [c.astype(F32) for c in chosen]


def _out_kernel(x_ref, ya_ref, yb_ref, *refs):
    mod_refs = refs[:PAIR]
    nrm_ref, w_ref, rwh_ref, rwl_ref, rb_ref, xo_ref, hp_ref, rt_ref = refs[PAIR:]
    rows = PAIR * TILE
    ya = ya_ref[...].reshape(rows, NA_WIDTH)
    yb = yb_ref[...].reshape(rows, LRU_WIDTH)
    y = jnp.dot(ya, w_ref[0:NA_WIDTH, :], preferred_element_type=F32)
    y = y + jnp.dot(yb, w_ref[NA_WIDTH:, :], preferred_element_type=F32)
    hs = []
    for k in range(PAIR):
        m = mod_refs[k][0, 0]
        x = x_ref[k] + m[2:3, :] * y[k * TILE:(k + 1) * TILE, :]
        xo_ref[k] = x
        ms = jnp.mean(x * x, axis=-1, keepdims=True)
        h = (x * lax.rsqrt(ms + EPS)) * nrm_ref[...]
        h = h * (1.0 + m[4:5, :]) + m[3:4, :]
        hp_ref[k] = _pack_rows(h)
        hs.append(h)
    h = jnp.concatenate(hs, axis=0)
    h_hi = h.astype(BF16)
    h_lo = (h - h_hi.astype(F32)).astype(BF16)
    nt = (((1,), (1,)), ((), ()))
    lg = (lax.dot_general(rwh_ref[...], h_hi, nt, preferred_element_type=F32)
          + lax.dot_general(rwh_ref[...], h_lo, nt, preferred_element_type=F32)
          + lax.dot_general(rwl_ref[...], h_hi, nt, preferred_element_type=F32))
    aff_all = jax.nn.sigmoid(lg)
    sel_all = aff_all + rb_ref[...]
    aff = [aff_all[e:e + 1, :] for e in range(N_EXPERTS)]
    sel = [sel_all[e:e + 1, :] for e in range(N_EXPERTS)]
    gates, chosen = _route(sel, aff)
    rt = jnp.concatenate(gates + chosen, axis=0)
    for k in range(PAIR):
        rt_ref[k, 0] = rt[:, k * TILE:(k + 1) * TILE]


def _out_proj(l, xs, ya, yb, mod, nrm, w_out, rwh, rwl, rb):
    return pl.pallas_call(
        _out_kernel,
        grid=(BATCH // PAIR, N_TILES),
        in_specs=[_pair_tok_spec(D_MODEL), _pair_tok_spec(NA_WIDTH), _pair_tok_spec(LRU_WIDTH)]
        + _pair_mod_specs(l)
        + [_full_spec((1, D_MODEL)), _full_spec((D_MODEL, D_MODEL)), _full_spec((N_EXPERTS, D_MODEL)),
           _full_spec((N_EXPERTS, D_MODEL)), _full_spec((N_EXPERTS, 1))],
        out_specs=[_pair_tok_spec(D_MODEL), _pair_tok_spec(PACK_W),
                   pl.BlockSpec((PAIR, 1, ROUTE_ROWS, TILE), lambda bp, i: (bp, i, 0, 0))],
        out_shape=[jax.ShapeDtypeStruct((BATCH, L_TOT, D_MODEL), F32),
                   jax.ShapeDtypeStruct((BATCH, L_TOT, PACK_W), I32),
                   jax.ShapeDtypeStruct((BATCH, N_TILES, ROUTE_ROWS, TILE), F32)],
        compiler_params=_cparams(("parallel", "parallel")),
        name="out_proj_router",
    )(xs, ya, yb, *([mod] * PAIR), nrm, w_out, rwh, rwl, rb)


def _pos_kernel(rt_ref, tri_ref, pos_ref, wt_ref, cnt_ref, run_scr, start_scr):
    phase = pl.program_id(0)
    i = pl.program_id(1)

    @pl.when(jnp.logical_and(phase == 0, i == 0))
    def _():
        run_scr[...] = jnp.zeros_like(run_scr)

    @pl.when(jnp.logical_and(phase == 1, i == 0))
    def _():
        tot = run_scr[...]
        cnt_ref[...] = jnp.broadcast_to(tot, (N_EXPERTS, LANES))
        acc = jnp.zeros((1, 1), F32)
        rows = []
        for e in range(N_EXPERTS):
            rows.append(acc)
            acc = acc + tot[e:e + 1, :]
        start_scr[...] = jnp.concatenate(rows, axis=0)
        run_scr[...] = jnp.zeros_like(run_scr)

    @pl.when(phase == 0)
    def _():
        tot = run_scr[...]
        for k in range(POS_TILES):
            tot = tot + jnp.sum(rt_ref[k, N_EXPERTS:, :], axis=1, keepdims=True)
        run_scr[...] = tot

    @pl.when(phase == 1)
    def _():
        base = start_scr[...] + run_scr[...]
        for k in range(POS_TILES):
            gates = rt_ref[k, 0:N_EXPERTS, :]
            chosen = rt_ref[k, N_EXPERTS:, :]
            rank = jnp.dot(chosen.astype(BF16), tri_ref[...], preferred_element_type=F32)
            posf = rank + base
            seen = jnp.zeros((1, TILE), F32)
            p0 = jnp.zeros((1, TILE), F32)
            p1 = jnp.zeros((1, TILE), F32)
            w0 = jnp.zeros((1, TILE), F32)
            w1 = jnp.zeros((1, TILE), F32)
            for e in range(N_EXPERTS):
                ch = chosen[e:e + 1, :]
                first = ch * (1.0 - seen)
                second = ch * seen
                p0 = p0 + first * posf[e:e + 1, :]
                p1 = p1 + second * posf[e:e + 1, :]
                w0 = w0 + first * gates[e:e + 1, :]
                w1 = w1 + second * gates[e:e + 1, :]
                seen = jnp.minimum(seen + ch, 1.0)
            pos_ref[k] = jnp.concatenate([p0, p1], axis=0).astype(I32)
            wpad = jnp.concatenate([w0, w1, jnp.zeros((LANES - TOP_K, TILE), F32)], axis=0)
            wt_ref[k * TILE:(k + 1) * TILE, :] = jnp.transpose(wpad)
            base = base + jnp.sum(chosen, axis=1, keepdims=True)
        run_scr[...] = base - start_scr[...]


def _positions(route, tri):
    return pl.pallas_call(
        _pos_kernel,
        grid=(2, N_TOK_TILES // POS_TILES),
        in_specs=[pl.BlockSpec((POS_TILES, ROUTE_ROWS, TILE), lambda p, i: (i, 0, 0)),
                  pl.BlockSpec((TILE, TILE), lambda p, i: (0, 0))],
        out_specs=[pl.BlockSpec((POS_TILES, TOP_K, TILE), lambda p, i: (i * p, 0, 0)),
                   pl.BlockSpec((POS_TILES * TILE, LANES), lambda p, i: (i * p, 0)),
                   pl.BlockSpec((N_EXPERTS, LANES), lambda p, i: (0, 0))],
        out_shape=[jax.ShapeDtypeStruct((N_TOK_TILES, TOP_K, TILE), I32),
                   jax.ShapeDtypeStruct((N_TOK, LANES), F32),
                   jax.ShapeDtypeStruct((N_EXPERTS, LANES), F32)],
        scratch_shapes=[pltpu.VMEM((N_EXPERTS, 1), F32), pltpu.VMEM((N_EXPERTS, 1), F32)],
        compiler_params=_cparams(("arbitrary", "arbitrary")),
        name="moe_positions",
    )(route, tri)


def _work_items(counts):
    smem = pl.BlockSpec(memory_space=pltpu.SMEM)
    item = jax.ShapeDtypeStruct((N_ITEMS,), I32)
    return pl.pallas_call(
        _items_kernel,
        in_specs=[smem],
        out_specs=[smem] * 4,
        out_shape=[item] * 4,
        name="moe_work_items",
    )(counts)


def _items_kernel(cnt_ref, tile_ref, exp_ref, lo_ref, hi_ref):
    n = jnp.int32(0)
    start = jnp.int32(0)
    last_e = jnp.int32(0)
    for e in range(N_EXPERTS):
        cnt = cnt_ref[e]
        end = start + cnt
        first = start // EXP_TILE
        n_tiles = jnp.where(cnt > 0, (end - 1) // EXP_TILE - first + 1, 0)

        def put(j, carry, e=e, n=n, start=start, end=end, first=first):
            tile = first + j
            tile_ref[n + j] = tile
            exp_ref[n + j] = jnp.int32(e)
            lo_ref[n + j] = jnp.maximum(start - tile * EXP_TILE, 0)
            hi_ref[n + j] = jnp.minimum(end - tile * EXP_TILE, EXP_TILE)
            return carry

        lax.fori_loop(0, n_tiles, put, 0)
        n = n + n_tiles
        start = end
        last_e = jnp.where(cnt > 0, e, last_e)

    def pad(j, carry):
        tile_ref[j] = jnp.int32(N_EXP_TILES - 1)
        exp_ref[j] = last_e
        lo_ref[j] = jnp.int32(0)
        hi_ref[j] = jnp.int32(0)
        return carry

    lax.fori_loop(n, N_ITEMS, pad, 0)


def _sc_worker_base():
    return (lax.axis_index("s") * SC_CORES + lax.axis_index("c")) * SC_ROWS


def _sc_dispatch_body(h_hbm, p0_hbm, p1_hbm, out_hbm, i0_v, i1_v, rows_v, sem):
    base = _sc_worker_base()

    @pl.loop(0, SC_N_CHUNKS)
    def _(j):
        off = base + j * SC_CHUNK
        pltpu.sync_copy(h_hbm.at[pl.ds(off, SC_CHUNK)], rows_v)
        pltpu.sync_copy(p0_hbm.at[pl.ds(off, SC_CHUNK)], i0_v)
        pltpu.sync_copy(p1_hbm.at[pl.ds(off, SC_CHUNK)], i1_v)
        c0 = pltpu.async_copy(rows_v, out_hbm.at[i0_v], sem)
        c1 = pltpu.async_copy(rows_v, out_hbm.at[i1_v], sem)
        c0.wait()
        c1.wait()


def _sc_combine_body(y_hbm, p0_hbm, p1_hbm, g0_hbm, g1_hbm, i_v, rows_v, sem):
    base = _sc_worker_base()

    @pl.loop(0, SC_N_CHUNKS)
    def _(j):
        off = base + j * SC_CHUNK
        for p_hbm, g_hbm in ((p0_hbm, g0_hbm), (p1_hbm, g1_hbm)):
            pltpu.sync_copy(p_hbm.at[pl.ds(off, SC_CHUNK)], i_v)
            pltpu.async_copy(y_hbm.at[i_v], rows_v, sem).wait()
            pltpu.sync_copy(rows_v, g_hbm.at[pl.ds(off, SC_CHUNK)])


def _sc_mesh():
    return plsc.VectorSubcoreMesh(core_axis_name="c", subcore_axis_name="s")


def _dispatch(hp, pos0, pos1):
    return pl.kernel(
        _sc_dispatch_body,
        out_type=jax.ShapeDtypeStruct((N_ASSIGN, PACK_W), I32),
        mesh=_sc_mesh(),
        scratch_types=[pltpu.VMEM((SC_CHUNK,), I32), pltpu.VMEM((SC_CHUNK,), I32),
                       pltpu.VMEM((SC_CHUNK, PACK_W), I32), pltpu.SemaphoreType.DMA],
        name="moe_dispatch",
    )(hp, pos0, pos1)


def _combine(ys, pos0, pos1):
    row = jax.ShapeDtypeStruct((N_TOK, PACK_W), I32)
    return pl.kernel(
        _sc_combine_body,
        out_type=[row, row],
        mesh=_sc_mesh(),
        scratch_types=[pltpu.VMEM((SC_CHUNK,), I32), pltpu.VMEM((SC_CHUNK, PACK_W), I32),
                       pltpu.SemaphoreType.DMA],
        name="moe_combine",
    )(ys, pos0, pos1)


def _expert_kernel(tile_ref, exp_ref, lo_ref, hi_ref, x_ref, w1_ref, w3_ref, w2_ref, o_ref, w13_scr, w2_scr):
    i = pl.program_id(0)
    prev = jnp.maximum(i - 1, 0)

    @pl.when(jnp.logical_or(i == 0, exp_ref[i] != exp_ref[prev]))
    def _():
        w13_scr[:, 0:EXPERT_FF] = w1_ref[0, 0].astype(BF16)
        w13_scr[:, EXPERT_FF:] = w3_ref[0, 0].astype(BF16)
        w2_scr[...] = w2_ref[0, 0].astype(BF16)

    lo_f, hi_f = _unpack_rows(x_ref[...])
    h = jnp.concatenate([lo_f.astype(BF16), hi_f.astype(BF16)], axis=1)
    ab = jnp.dot(h, w13_scr[...], preferred_element_type=F32)
    a = ab[:, 0:EXPERT_FF]
    b = ab[:, EXPERT_FF:]
    t = ((a * jax.nn.sigmoid(a)) * b).astype(BF16)
    y = _pack_rows(jnp.dot(t, w2_scr[...], preferred_element_type=F32))
    row = lax.broadcasted_iota(jnp.int32, (EXP_TILE, PACK_W), 0)
    mine = jnp.logical_and(row >= lo_ref[i], row < hi_ref[i])
    revisit = jnp.logical_and(i > 0, tile_ref[i] == tile_ref[prev])

    @pl.when(jnp.logical_not(revisit))
    def _():
        o_ref[...] = jnp.where(mine, y, 0)

    @pl.when(revisit)
    def _():
        o_ref[...] = jnp.where(mine, y, o_ref[...])


def _experts(l, xs_sorted, items, w1, w3, w2):
    tile, e, lo, hi = items
    grid_spec = pltpu.PrefetchScalarGridSpec(
        num_scalar_prefetch=4,
        grid=(N_ITEMS,),
        in_specs=[pl.BlockSpec((EXP_TILE, PACK_W), lambda i, t, e, lo, hi: (t[i], 0)),
                  pl.BlockSpec((1, 1, D_MODEL, EXPERT_FF), lambda i, t, e, lo, hi: (l, e[i], 0, 0)),
                  pl.BlockSpec((1, 1, D_MODEL, EXPERT_FF), lambda i, t, e, lo, hi: (l, e[i], 0, 0)),
                  pl.BlockSpec((1, 1, EXPERT_FF, D_MODEL), lambda i, t, e, lo, hi: (l, e[i], 0, 0))],
        out_specs=pl.BlockSpec((EXP_TILE, PACK_W), lambda i, t, e, lo, hi: (t[i], 0)),
        scratch_shapes=[pltpu.VMEM((D_MODEL, 2 * EXPERT_FF), BF16), pltpu.VMEM((EXPERT_FF, D_MODEL), BF16)],
    )
    return pl.pallas_call(
        _expert_kernel,
        grid_spec=grid_spec,
        out_shape=jax.ShapeDtypeStruct((N_ASSIGN, PACK_W), I32),
        compiler_params=_cparams(("arbitrary",)),
        name="moe_experts",
    )(tile, e, lo, hi, xs_sorted, w1, w3, w2)


def _moe(l, hp, route, tri, w1, w3, w2):
    pos, wt, cnt = _positions(route, tri)
    pos0 = pos[:, 0, :].reshape(N_TOK)
    pos1 = pos[:, 1, :].reshape(N_TOK)
    items = _work_items(cnt[:, 0].astype(I32))
    xs_sorted = _dispatch(hp.reshape(N_TOK, PACK_W), pos0, pos1)
    ys = _experts(l, xs_sorted, items, w1, w3, w2)
    g0, g1 = _combine(ys, pos0, pos1)
    shape = (BATCH, L_TOT, PACK_W)
    return g0.reshape(shape), g1.reshape(shape), wt.reshape(BATCH, L_TOT, LANES)


def _final_kernel(x_ref, g0_ref, g1_ref, wt_ref, mod_ref, o_ref):
    o_ref[0] = _moe_residual(x_ref[0], g0_ref[0], g1_ref[0], wt_ref[0], mod_ref[0, 0][5:6, :])


def _final(xs, moe, mod):
    lat = lambda width: pl.BlockSpec((1, TILE, width), lambda b, j: (b, j + 1, 0))
    return pl.pallas_call(
        _final_kernel,
        grid=(BATCH, N_LAT_TILES),
        in_specs=[lat(D_MODEL), lat(PACK_W), lat(PACK_W), lat(LANES),
                  pl.BlockSpec((1, 1, N_MOD, D_MODEL), lambda b, j: (DEPTH - 1, b, 0, 0))],
        out_specs=pl.BlockSpec((1, TILE, D_MODEL), lambda b, j: (b, j, 0)),
        out_shape=jax.ShapeDtypeStruct((BATCH, SEQ, D_MODEL), F32),
        compiler_params=_cparams(("parallel", "parallel")),
        name="final_residual",
    )(xs, *moe, mod)


def _to_scan_major(t):
    b, n, d = t.shape
    return t.reshape(b, n // TILE, SUB, GROUPS, d).transpose(0, 1, 3, 2, 4).reshape(b, n, d)


def _from_scan_major(t):
    b, n, d = t.shape
    return t.reshape(b, n // TILE, GROUPS, SUB, d).transpose(0, 1, 3, 2, 4).reshape(b, n, d)


def _bias_tiles(table):
    rows_q = TILE // GRID_W
    rows_k = 3 * rows_q
    a = np.arange(rows_q)[:, None]
    c = np.arange(rows_k)[None, :]
    r_idx = np.clip(c - a + 3, 0, 2 * NA_WIN_ROWS - 2)
    qc = np.arange(GRID_W)[:, None]
    kc = np.arange(GRID_W)[None, :]
    col0 = np.clip(qc - NA_WIN_COLS // 2, 0, GRID_W - NA_WIN_COLS)
    in_win = (kc >= col0) & (kc < col0 + NA_WIN_COLS)
    pad = GRID_W - NA_WIN_COLS
    tp = jnp.pad(table.astype(F32), ((0, 0), (0, 0), (pad, pad)))
    toeplitz = jnp.stack([tp[:, :, GRID_W - 1 - q:2 * GRID_W - 1 - q] for q in range(GRID_W)], axis=2)
    blk = jnp.stack([jnp.stack([toeplitz[:, int(r_idx[i, j])] for j in range(rows_k)], axis=1)
                     for i in range(rows_q)], axis=1)
    allowed = np.stack([
        np.broadcast_to(c >= rows_q, (rows_q, rows_k)),
        (c >= a) & (c <= a + NA_WIN_ROWS - 1),
        np.broadcast_to(c < 2 * rows_q, (rows_q, rows_k)),
        np.zeros((rows_q, rows_k), bool),
    ])
    mask = allowed[:, None, :, :, None, None] & in_win[None, None, None, None, :, :]
    tiles = jnp.where(mask, blk[None] * LOG2E, NEG_INF)
    half = GRID_W // GROUPS
    n_h = table.shape[0]
    tiles = tiles.reshape(4, n_h, rows_q, 3, rows_q, half, GROUPS, half, GROUPS)
    tiles = tiles.transpose(0, 1, 6, 2, 5, 3, 8, 4, 7)
    return tiles.reshape(4, n_h, TILE, 3 * TILE)


def _block_diag(w, n_chunks):
    per = LRU_BLOCKS // n_chunks
    w = w.reshape(2, n_chunks, per, LRU_BLOCK, LRU_BLOCK)
    eye = jnp.eye(per, dtype=w.dtype)
    out = jnp.einsum('dcpij,pq->dcpiqj', w, eye)
    return out.reshape(2, n_chunks, per * LRU_BLOCK, per * LRU_BLOCK)


def kernel(x, c, ctx, c_ctx, w_mod, b_mod, norm_mix, norm_ffn, w_in, w_out, q_gain, k_gain, na_bias,
           conv_w, conv_b, lru_w_r, lru_b_r, lru_w_i, lru_b_i, lru_lambda, router_w, router_b,
           exp_w1, exp_w3, exp_w2):
    cs = jnp.concatenate([c, c_ctx[None, :], jnp.zeros((MOD_ROWS - BATCH - 1, D_MODEL), F32)], axis=0)
    mod = _modulation(cs, w_mod, b_mod).reshape(DEPTH, MOD_ROWS, N_MOD, D_MODEL)

    xs = jnp.concatenate([_to_scan_major(ctx), _to_scan_major(x)], axis=1)
    head_of = np.arange(NA_WIDTH) // HEAD_DIM
    bd = jnp.asarray((head_of[:, None] == head_of[None, :]).astype(np.float32) / HEAD_DIM, BF16)
    tri = jnp.asarray(np.triu(np.ones((TILE, TILE), np.float32), 1), BF16)
    rwt = router_w.T
    rwh = rwt.astype(BF16)
    rwl = (rwt - rwh.astype(F32)).astype(BF16)
    rb = router_b.reshape(N_EXPERTS, 1)
    n_cb = LRU_WIDTH // LRU_CH

    bias_tiles = _bias_tiles(na_bias.reshape(DEPTH * NA_HEADS, 2 * NA_WIN_ROWS - 1, 2 * NA_WIN_COLS - 1))
    moe = None
    for l in range(DEPTH):
        qg = jnp.tile(q_gain[l] * (ATTN_SCALE * LOG2E), NA_HEADS)[None, :]
        kg = jnp.tile(k_gain[l], NA_HEADS)[None, :]
        xs, (q, k, v, u, g) = _in_proj(l, xs, moe, mod, norm_mix[l][None, :], w_in[l].astype(BF16), bd, qg, kg)
        ya = _attention(l, q, k, v, bias_tiles)
        yb = _rglru(u, g, conv_w[l], conv_b[l][None, :],
                    _block_diag(lru_w_r[l], n_cb).astype(BF16), _block_diag(lru_w_i[l], n_cb).astype(BF16),
                    lru_b_r[l], lru_b_i[l], lru_lambda[l])
        xs, hp, route = _out_proj(l, xs, ya, yb, mod, norm_ffn[l][None, :], w_out[l].astype(BF16), rwh, rwl, rb)
        moe = _moe(l, hp, route.reshape(N_TOK_TILES, ROUTE_ROWS, TILE), tri, exp_w1, exp_w3, exp_w2)
    out = _final(xs, moe, mod)
    return _from_scan_major(out)
```

```python
import functools

import jax
import jax.numpy as jnp
import numpy as np
from jax import lax
from jax.experimental import pallas as pl
from jax.experimental.pallas import tpu as pltpu
from jax.experimental.pallas import tpu_sc as plsc

F32 = jnp.float32
BF16 = jnp.bfloat16
I32 = jnp.int32

D_MODEL = 1024
BATCH = 4
SEQ = 8192
DEPTH = 4
GRID_W = 64
CTX_LEN = 256
HEAD_DIM = 64
NA_WIDTH = 512
NA_HEADS = 8
NA_WIN_ROWS = 8
NA_WIN_COLS = 16
LRU_WIDTH = 512
LRU_BLOCKS = 8
LRU_BLOCK = 64
CONV_W = 4
LRU_C = 8.0
IN_COLS = 3 * NA_WIDTH + 2 * LRU_WIDTH
N_EXPERTS = 16
N_GROUPS = 4
EXPERTS_PER_GROUP = 4
TOP_K = 2
EXPERT_FF = 512
N_MOD = 6
ATTN_SCALE = HEAD_DIM ** -0.5
LOG2E = 1.4426950408889634
EPS = 1e-6
NEG_INF = -1e30

TILE = 256
SUB = 8
LANES = 128
GROUPS = TILE // SUB
L_TOT = CTX_LEN + SEQ
N_TILES = L_TOT // TILE
N_LAT_TILES = SEQ // TILE
STREAMS = 2
SB = BATCH // STREAMS
N_TOK = SB * L_TOT
N_TOK_TILES = N_TOK // TILE
PAIR = 2
LRU_CH = 256
MOD_ROWS = 8
VMEM_LIMIT = 56 * 1024 * 1024

PACK_W = D_MODEL // 2
HI_MASK = -65536
N_ASSIGN = TOP_K * N_TOK
EXP_TILE = 512
N_EXP_TILES = N_ASSIGN // EXP_TILE
N_ITEMS = N_EXP_TILES + N_EXPERTS - 1
ROUTE_ROWS = 2 * N_EXPERTS
POS_TILES = 6

SC_CORES = 2
SC_SUBCORES = 16
SC_WORKERS = SC_CORES * SC_SUBCORES
SC_ROWS = N_TOK // SC_WORKERS
SC_CHUNK = 88
SC_N_CHUNKS = SC_ROWS // SC_CHUNK


def _cparams(sem):
    return pltpu.CompilerParams(dimension_semantics=sem, vmem_limit_bytes=VMEM_LIMIT)


def _pack_rows(v):
    lo = pltpu.bitcast(v[:, :PACK_W].astype(BF16).astype(F32), I32)
    hi = pltpu.bitcast(v[:, PACK_W:].astype(BF16).astype(F32), I32)
    return ((lo >> 16) & 0xFFFF) | (hi & HI_MASK)


def _unpack_rows(p):
    return pltpu.bitcast(p << 16, F32), pltpu.bitcast(p & HI_MASK, F32)


def _mod_kernel(c_ref, w_ref, b_ref, o_ref):
    c = c_ref[...]
    s = c * jax.nn.sigmoid(c)
    o_ref[0] = jnp.dot(s.astype(BF16), w_ref[0].astype(BF16), preferred_element_type=F32) + b_ref[0]


def _modulation(cs, w_mod, b_mod):
    return pl.pallas_call(
        _mod_kernel,
        grid=(DEPTH, N_MOD),
        in_specs=[
            pl.BlockSpec((MOD_ROWS, D_MODEL), lambda l, n: (0, 0)),
            pl.BlockSpec((1, D_MODEL, D_MODEL), lambda l, n: (l, 0, n)),
            pl.BlockSpec((1, 1, D_MODEL), lambda l, n: (l, 0, n)),
        ],
        out_specs=pl.BlockSpec((1, MOD_ROWS, D_MODEL), lambda l, n: (l, 0, n)),
        out_shape=jax.ShapeDtypeStruct((DEPTH, MOD_ROWS, N_MOD * D_MODEL), F32),
        compiler_params=_cparams(("arbitrary", "arbitrary")),
        name="modulation",
    )(cs, w_mod, b_mod.reshape(DEPTH, 1, N_MOD * D_MODEL))


def _mod_row(b, i):
    return jnp.where(i == 0, BATCH, b)


def _moe_residual(x, g0, g1, wt, gate_row):
    lo0, hi0 = _unpack_rows(g0)
    lo1, hi1 = _unpack_rows(g1)
    w0 = wt[:, 0:1]
    w1 = wt[:, 1:2]
    f = jnp.concatenate([w0 * lo0 + w1 * lo1, w0 * hi0 + w1 * hi1], axis=1)
    return x + gate_row * f


def _pair_tok_spec(width):
    return pl.BlockSpec((PAIR, TILE, width), lambda bp, i: (bp, i, 0))


def _pair_mod_specs(layer, boff):
    return [pl.BlockSpec((1, 1, N_MOD, D_MODEL),
                         lambda bp, i, k=k: (layer, _mod_row(boff + PAIR * bp + k, i), 0, 0))
            for k in range(PAIR)]


def _full_spec(shape):
    return pl.BlockSpec(shape, lambda bp, i: tuple(0 for _ in shape))


def _in_kernel(has_prev, *refs):
    if has_prev:
        x_ref, g0_ref, g1_ref, wt_ref = refs[:4]
        mprev_refs = refs[4:4 + PAIR]
        refs = (x_ref,) + refs[4 + PAIR:]
    mod_refs = refs[1:1 + PAIR]
    nrm_ref, w_ref, bd_ref, qg_ref, kg_ref = refs[1 + PAIR:6 + PAIR]
    outs = refs[6 + PAIR:]
    if has_prev:
        xo_ref, outs = outs[0], outs[1:]
    q_ref, k_ref, v_ref, u_ref, g_ref = outs
    x_ref = refs[0]
    hs = []
    for k in range(PAIR):
        x = x_ref[k]
        if has_prev:
            x = _moe_residual(x, g0_ref[k], g1_ref[k], wt_ref[k], mprev_refs[k][0, 0][5:6, :])
            xo_ref[k] = x
        m = mod_refs[k][0, 0]
        ms = jnp.mean(x * x, axis=-1, keepdims=True)
        h = (x * lax.rsqrt(ms + EPS)) * nrm_ref[...]
        hs.append((h * (1.0 + m[1:2, :]) + m[0:1, :]).astype(BF16))
    acc = jnp.dot(jnp.concatenate(hs, axis=0), w_ref[...], preferred_element_type=F32)
    bd = bd_ref[...]

    def head_norm(t, gain):
        ss = jnp.dot((t * t).astype(BF16), bd, preferred_element_type=F32)
        return (t * lax.rsqrt(ss + EPS)) * gain

    def put(ref, val):
        for k in range(PAIR):
            ref[k] = val[k * TILE:(k + 1) * TILE, :].astype(BF16)

    put(q_ref, head_norm(acc[:, 0:NA_WIDTH], qg_ref[...]))
    put(k_ref, head_norm(acc[:, NA_WIDTH:2 * NA_WIDTH], kg_ref[...]))
    put(v_ref, acc[:, 2 * NA_WIDTH:3 * NA_WIDTH])
    put(u_ref, acc[:, 3 * NA_WIDTH:3 * NA_WIDTH + LRU_WIDTH])
    put(g_ref, acc[:, 3 * NA_WIDTH + LRU_WIDTH:])


def _in_proj(l, boff, xs, moe_prev, mod, nrm, w_in, bd, qg, kg):
    has_prev = moe_prev is not None
    in_specs = [_pair_tok_spec(D_MODEL)]
    args = [xs]
    if has_prev:
        in_specs += [_pair_tok_spec(PACK_W), _pair_tok_spec(PACK_W), _pair_tok_spec(LANES)]
        in_specs += _pair_mod_specs(l - 1, boff)
        args += list(moe_prev) + [mod] * PAIR
    in_specs += _pair_mod_specs(l, boff) + [_full_spec((1, D_MODEL)), _full_spec((D_MODEL, IN_COLS)),
                                            _full_spec((NA_WIDTH, NA_WIDTH)), _full_spec((1, NA_WIDTH)),
                                            _full_spec((1, NA_WIDTH))]
    args += [mod] * PAIR + [nrm, w_in, bd, qg, kg]
    half = jax.ShapeDtypeStruct((SB, L_TOT, NA_WIDTH), BF16)
    out_shape = [half] * 5
    out_specs = [_pair_tok_spec(NA_WIDTH)] * 5
    if has_prev:
        out_shape = [jax.ShapeDtypeStruct((SB, L_TOT, D_MODEL), F32)] + out_shape
        out_specs = [_pair_tok_spec(D_MODEL)] + out_specs
    outs = pl.pallas_call(
        functools.partial(_in_kernel, has_prev),
        grid=(SB // PAIR, N_TILES),
        in_specs=in_specs,
        out_specs=out_specs,
        out_shape=out_shape,
        compiler_params=_cparams(("parallel", "parallel")),
        name="in_proj",
    )(*args)
    if has_prev:
        return outs[0], outs[1:]
    return xs, outs


def _attn_kernel(q_ref, kp_ref, kc_ref, kn_ref, kx_ref, vp_ref, vc_ref, vn_ref, vx_ref, bias_ref, o_ref):
    pair_w = 2 * HEAD_DIM
    lane = lax.broadcasted_iota(jnp.int32, (TILE, pair_w), 1)
    low = lane < HEAD_DIM
    k_refs = (kp_ref, kc_ref, kn_ref, kx_ref)
    v_refs = (vp_ref, vc_ref, vn_ref, vx_ref)
    nt = (((1,), (1,)), ((), ()))
    for hp in range(NA_HEADS // 2):
        cols = slice(hp * pair_w, (hp + 1) * pair_w)
        q = q_ref[0, :, cols]
        kb = [r[0, :, cols] for r in k_refs]
        vb = [r[0, :, cols] for r in v_refs]
        outs = []
        for hh in range(2):
            own = low if hh == 0 else jnp.logical_not(low)
            qh = jnp.where(own, q, jnp.zeros_like(q))
            s = [lax.dot_general(qh, kb[i], nt, preferred_element_type=F32) for i in range(4)]
            for i in range(3):
                s[i] = s[i] + bias_ref[0, 2 * hp + hh, :, i * TILE:(i + 1) * TILE]
            m = jnp.max(jnp.maximum(jnp.maximum(s[0], s[1]), jnp.maximum(s[2], s[3])), axis=-1, keepdims=True)
            o = None
            for i in range(4):
                p = jnp.exp2((s[i] - m).astype(BF16))
                va = jnp.where(own, vb[i], jnp.ones_like(vb[i]))
                part = jnp.dot(p, va, preferred_element_type=F32)
                o = part if o is None else o + part
            outs.append(o / pltpu.roll(o, HEAD_DIM, 1))
        o_ref[0, :, cols] = jnp.where(low, outs[0], outs[1]).astype(BF16)


def _attention(l, q, k, v, bias_tiles):
    last = N_LAT_TILES - 1
    blk = (1, TILE, NA_WIDTH)
    prev_map = lambda b, j: (b, 1 + jnp.clip(j - 2, 0, last), 0)
    cur_map = lambda b, j: (b, jnp.maximum(j, 1), 0)
    next_map = lambda b, j: (b, 1 + jnp.clip(j, 0, last), 0)
    ctx_map = lambda b, j: (b, 0, 0)
    var_map = lambda b, j: (jnp.where(j == 0, 3, jnp.where(j == 1, 0, jnp.where(j == N_TILES - 1, 2, 1))), l, 0, 0)
    kv_specs = [pl.BlockSpec(blk, prev_map), pl.BlockSpec(blk, cur_map), pl.BlockSpec(blk, next_map),
                pl.BlockSpec(blk, ctx_map)]
    return pl.pallas_call(
        _attn_kernel,
        grid=(SB, N_TILES),
        in_specs=[pl.BlockSpec(blk, lambda b, j: (b, j, 0))] + kv_specs + kv_specs
        + [pl.BlockSpec((1, NA_HEADS, TILE, 3 * TILE), var_map)],
        out_specs=pl.BlockSpec(blk, lambda b, j: (b, j, 0)),
        out_shape=jax.ShapeDtypeStruct((SB, L_TOT, NA_WIDTH), BF16),
        compiler_params=_cparams(("parallel", "arbitrary")),
        name="na_attention",
    )(q, k, k, k, k, v, v, v, v, bias_tiles)


def _softplus(x):
    return jnp.maximum(x, 0.0) + jnp.log1p(jnp.exp(-jnp.abs(x)))


def _lru_kernel(u_ref, g_ref, cw_ref, cb_ref, wr_ref, wi_ref, br_ref, bi_ref, lam_ref, y_ref,
                hf_scr, v_scr, hl_scr, p_scr):
    C = LRU_CH
    cw = cw_ref[...]
    cb = cb_ref[...]
    sub = lax.broadcasted_iota(jnp.int32, (SUB, C), 0)

    def conv_tile(c, r0):
        U = u_ref[0, pl.ds(r0, TILE), :].astype(F32)
        pstart = pl.multiple_of(jnp.maximum(r0 - 16, 0), 16)
        nstart = pl.multiple_of(jnp.minimum(r0 + TILE, L_TOT - 16), 16)
        prev16 = u_ref[0, pl.ds(pstart, 16), :].astype(F32)
        next16 = u_ref[0, pl.ds(nstart, 16), :].astype(F32)
        has_prev = jnp.where(c >= 2, 1.0, 0.0).astype(F32)
        has_next = jnp.where(jnp.logical_and(c >= 1, c <= N_TILES - 2), 1.0, 0.0).astype(F32)
        prow = prev16[15:16, :] * has_prev
        n0 = next16[0:1, :] * has_next
        n8 = next16[8:9, :] * has_next
        first8 = jnp.where(sub == 0, prow, pltpu.roll(U[TILE - SUB:TILE, :], 1, 0))
        last_a = jnp.where(sub == SUB - 1, n0, pltpu.roll(U[0:SUB, :], SUB - 1, 0))
        last_b = jnp.where(sub == SUB - 1, n8, pltpu.roll(U[SUB:2 * SUB, :], SUB - 1, 0))
        um1 = jnp.concatenate([first8, U[0:TILE - SUB, :]], axis=0)
        up1 = jnp.concatenate([U[SUB:TILE, :], last_a], axis=0)
        up2 = jnp.concatenate([U[2 * SUB:TILE, :], last_a, last_b], axis=0)
        return cw[0:1, :] * um1 + cw[1:2, :] * U + cw[2:3, :] * up1 + cw[3:4, :] * up2 + cb

    def coeffs(v, d):
        vb = v.astype(BF16)
        tr = jnp.tanh(jnp.dot(vb, wr_ref[d, 0], preferred_element_type=F32) + br_ref[d:d + 1, :])
        ti = jnp.tanh(jnp.dot(vb, wi_ref[d, 0], preferred_element_type=F32) + bi_ref[d:d + 1, :])
        half = (-0.5 * LRU_C * LOG2E) * _softplus(-lam_ref[d:d + 1, :])
        a = jnp.exp2(half + half * tr)
        b = jnp.sqrt(1.0 - a * a) * ((0.5 * v) * (1.0 + ti))
        return a, b

    def scan_tile(a, b, h_in, reverse):
        order = range(GROUPS - 1, -1, -1) if reverse else range(GROUPS)
        hl = None
        for g in order:
            ag = a[g * SUB:(g + 1) * SUB, :]
            bg = b[g * SUB:(g + 1) * SUB, :]
            if hl is None:
                hl, p = bg, ag
            else:
                hl = ag * hl + bg
                p = ag * p
            hl_scr[g * SUB:(g + 1) * SUB, :] = hl
            p_scr[g * SUB:(g + 1) * SUB, :] = p
        blocks = range(SUB - 1, -1, -1) if reverse else range(SUB)
        carry = h_in
        cins = {}
        for s in blocks:
            cins[s] = carry
            carry = hl[s:s + 1, :] + p[s:s + 1, :] * carry
        cin = jnp.concatenate([cins[s] for s in range(SUB)], axis=0)
        hfull = hl_scr[...] + p_scr[...] * jnp.tile(cin, (GROUPS, 1))
        return hfull, carry

    zero = jnp.zeros((1, C), F32)

    def fwd_body(c, h):
        r0 = pl.multiple_of(c * TILE, TILE)
        v = conv_tile(c, r0)
        v_scr[pl.ds(r0, TILE), :] = v
        a, b = coeffs(v, 0)
        hfull, h = scan_tile(a, b, h, False)
        hf_scr[pl.ds(r0, TILE), :] = hfull
        return h

    lax.fori_loop(0, N_TILES, fwd_body, zero)

    def rev_tile(c, h):
        r0 = pl.multiple_of(c * TILE, TILE)
        a, b = coeffs(v_scr[pl.ds(r0, TILE), :], 1)
        hfull, h = scan_tile(a, b, h, True)
        gate = jax.nn.gelu(g_ref[0, pl.ds(r0, TILE), :].astype(F32))
        y_ref[0, pl.ds(r0, TILE), :] = (gate * (hf_scr[pl.ds(r0, TILE), :] + hfull)).astype(BF16)
        return h

    h_ctx = rev_tile(jnp.int32(0), zero)
    lax.fori_loop(0, N_LAT_TILES, lambda i, h: rev_tile(N_TILES - 1 - i, h), h_ctx)


def _rglru(u, g, cw, cb, wr, wi, br, bi, lam):
    n_cb = LRU_WIDTH // LRU_CH
    seq = pl.BlockSpec((1, L_TOT, LRU_CH), lambda b, c: (b, 0, c))
    vec = lambda rows: pl.BlockSpec((rows, LRU_CH), lambda b, c: (0, c))
    mat = pl.BlockSpec((2, 1, LRU_CH, LRU_CH), lambda b, c: (0, c, 0, 0))
    return pl.pallas_call(
        _lru_kernel,
        grid=(SB, n_cb),
        in_specs=[seq, seq, vec(CONV_W), vec(1), mat, mat, vec(2), vec(2), vec(2)],
        out_specs=seq,
        out_shape=jax.ShapeDtypeStruct((SB, L_TOT, LRU_WIDTH), BF16),
        scratch_shapes=[pltpu.VMEM((L_TOT, LRU_CH), F32), pltpu.VMEM((L_TOT, LRU_CH), F32),
                        pltpu.VMEM((TILE, LRU_CH), F32), pltpu.VMEM((TILE, LRU_CH), F32)],
        compiler_params=_cparams(("parallel", "parallel")),
        name="rglru",
    )(u, g, cw, cb, wr, wi, br, bi, lam)


def _route(sel, aff):
    def top2_sum(a, b, c, d):
        hi1, lo1 = jnp.maximum(a, b), jnp.minimum(a, b)
        hi2, lo2 = jnp.maximum(c, d), jnp.minimum(c, d)
        return jnp.maximum(hi1, hi2) + jnp.maximum(jnp.minimum(hi1, hi2), jnp.maximum(lo1, lo2))

    scores = [top2_sum(*sel[EXPERTS_PER_GROUP * g:EXPERTS_PER_GROUP * (g + 1)]) for g in range(N_GROUPS)]
    best = jnp.zeros_like(scores[0], dtype=jnp.int32)
    best_v = scores[0]
    for g in range(1, N_GROUPS):
        upd = scores[g] > best_v
        best = jnp.where(upd, g, best)
        best_v = jnp.where(upd, scores[g], best_v)
    chosen = []
    for e in range(N_EXPERTS):
        g = e // EXPERTS_PER_GROUP
        rank = jnp.zeros_like(best)
        for o in range(EXPERTS_PER_GROUP * g, EXPERTS_PER_GROUP * (g + 1)):
            if o == e:
                continue
            ahead = sel[o] > sel[e]
            if o < e:
                ahead = jnp.logical_or(ahead, sel[o] == sel[e])
            rank = rank + ahead.astype(jnp.int32)
        chosen.append(jnp.logical_and(best == g, rank < TOP_K))
    total = jnp.zeros_like(aff[0])
    for e in range(N_EXPERTS):
        total = total + jnp.where(chosen[e], aff[e], 0.0)
    gates = [jnp.where(chosen[e], aff[e] / total, 0.0) for e in range(N_EXPERTS)]
    return gates, [c.astype(F32) for c in chosen]


def _out_kernel(x_ref, ya_ref, yb_ref, *refs):
    mod_refs = refs[:PAIR]
    nrm_ref, w_ref, rwh_ref, rwl_ref, rb_ref, xo_ref, hp_ref, rt_ref = refs[PAIR:]
    rows = PAIR * TILE
    ya = ya_ref[...].reshape(rows, NA_WIDTH)
    yb = yb_ref[...].reshape(rows, LRU_WIDTH)
    y = jnp.dot(ya, w_ref[0:NA_WIDTH, :], preferred_element_type=F32)
    y = y + jnp.dot(yb, w_ref[NA_WIDTH:, :], preferred_element_type=F32)
    hs = []
    for k in range(PAIR):
        m = mod_refs[k][0, 0]
        x = x_ref[k] + m[2:3, :] * y[k * TILE:(k + 1) * TILE, :]
        xo_ref[k] = x
        ms = jnp.mean(x * x, axis=-1, keepdims=True)
        h = (x * lax.rsqrt(ms + EPS)) * nrm_ref[...]
        h = h * (1.0 + m[4:5, :]) + m[3:4, :]
        hp_ref[k] = _pack_rows(h)
        hs.append(h)
    h = jnp.concatenate(hs, axis=0)
    h_hi = h.astype(BF16)
    h_lo = (h - h_hi.astype(F32)).astype(BF16)
    nt = (((1,), (1,)), ((), ()))
    lg = (lax.dot_general(rwh_ref[...], h_hi, nt, preferred_element_type=F32)
          + lax.dot_general(rwh_ref[...], h_lo, nt, preferred_element_type=F32)
          + lax.dot_general(rwl_ref[...], h_hi, nt, preferred_element_type=F32))
    aff_all = jax.nn.sigmoid(lg)
    sel_all = aff_all + rb_ref[...]
    aff = [aff_all[e:e + 1, :] for e in range(N_EXPERTS)]
    sel = [sel_all[e:e + 1, :] for e in range(N_EXPERTS)]
    gates, chosen = _route(sel, aff)
    rt = jnp.concatenate(gates + chosen, axis=0)
    for k in range(PAIR):
        rt_ref[k, 0] = rt[:, k * TILE:(k + 1) * TILE]


def _out_proj(l, boff, xs, ya, yb, mod, nrm, w_out, rwh, rwl, rb):
    return pl.pallas_call(
        _out_kernel,
        grid=(SB // PAIR, N_TILES),
        in_specs=[_pair_tok_spec(D_MODEL), _pair_tok_spec(NA_WIDTH), _pair_tok_spec(LRU_WIDTH)]
        + _pair_mod_specs(l, boff)
        + [_full_spec((1, D_MODEL)), _full_spec((D_MODEL, D_MODEL)), _full_spec((N_EXPERTS, D_MODEL)),
           _full_spec((N_EXPERTS, D_MODEL)), _full_spec((N_EXPERTS, 1))],
        out_specs=[_pair_tok_spec(D_MODEL), _pair_tok_spec(PACK_W),
                   pl.BlockSpec((PAIR, 1, ROUTE_ROWS, TILE), lambda bp, i: (bp, i, 0, 0))],
        out_shape=[jax.ShapeDtypeStruct((SB, L_TOT, D_MODEL), F32),
                   jax.ShapeDtypeStruct((SB, L_TOT, PACK_W), I32),
                   jax.ShapeDtypeStruct((SB, N_TILES, ROUTE_ROWS, TILE), F32)],
        compiler_params=_cparams(("parallel", "parallel")),
        name="out_proj_router",
    )(xs, ya, yb, *([mod] * PAIR), nrm, w_out, rwh, rwl, rb)


def _pos_kernel(rt_ref, tri_ref, pos_ref, wt_ref, cnt_ref, run_scr, start_scr):
    phase = pl.program_id(0)
    i = pl.program_id(1)

    @pl.when(jnp.logical_and(phase == 0, i == 0))
    def _():
        run_scr[...] = jnp.zeros_like(run_scr)

    @pl.when(jnp.logical_and(phase == 1, i == 0))
    def _():
        tot = run_scr[...]
        cnt_ref[...] = jnp.broadcast_to(tot, (N_EXPERTS, LANES))
        acc = jnp.zeros((1, 1), F32)
        rows = []
        for e in range(N_EXPERTS):
            rows.append(acc)
            acc = acc + tot[e:e + 1, :]
        start_scr[...] = jnp.concatenate(rows, axis=0)
        run_scr[...] = jnp.zeros_like(run_scr)

    @pl.when(phase == 0)
    def _():
        tot = run_scr[...]
        for k in range(POS_TILES):
            tot = tot + jnp.sum(rt_ref[k, N_EXPERTS:, :], axis=1, keepdims=True)
        run_scr[...] = tot

    @pl.when(phase == 1)
    def _():
        base = start_scr[...] + run_scr[...]
        for k in range(POS_TILES):
            gates = rt_ref[k, 0:N_EXPERTS, :]
            chosen = rt_ref[k, N_EXPERTS:, :]
            rank = jnp.dot(chosen.astype(BF16), tri_ref[...], preferred_element_type=F32)
            posf = rank + base
            seen = jnp.zeros((1, TILE), F32)
            p0 = jnp.zeros((1, TILE), F32)
            p1 = jnp.zeros((1, TILE), F32)
            w0 = jnp.zeros((1, TILE), F32)
            w1 = jnp.zeros((1, TILE), F32)
            for e in range(N_EXPERTS):
                ch = chosen[e:e + 1, :]
                first = ch * (1.0 - seen)
                second = ch * seen
                p0 = p0 + first * posf[e:e + 1, :]
                p1 = p1 + second * posf[e:e + 1, :]
                w0 = w0 + first * gates[e:e + 1, :]
                w1 = w1 + second * gates[e:e + 1, :]
                seen = jnp.minimum(seen + ch, 1.0)
            pos_ref[k] = jnp.concatenate([p0, p1], axis=0).astype(I32)
            wpad = jnp.concatenate([w0, w1, jnp.zeros((LANES - TOP_K, TILE), F32)], axis=0)
            wt_ref[k * TILE:(k + 1) * TILE, :] = jnp.transpose(wpad)
            base = base + jnp.sum(chosen, axis=1, keepdims=True)
        run_scr[...] = base - start_scr[...]


def _positions(route, tri):
    return pl.pallas_call(
        _pos_kernel,
        grid=(2, N_TOK_TILES // POS_TILES),
        in_specs=[pl.BlockSpec((POS_TILES, ROUTE_ROWS, TILE), lambda p, i: (i, 0, 0)),
                  pl.BlockSpec((TILE, TILE), lambda p, i: (0, 0))],
        out_specs=[pl.BlockSpec((POS_TILES, TOP_K, TILE), lambda p, i: (i * p, 0, 0)),
                   pl.BlockSpec((POS_TILES * TILE, LANES), lambda p, i: (i * p, 0)),
                   pl.BlockSpec((N_EXPERTS, LANES), lambda p, i: (0, 0))],
        out_shape=[jax.ShapeDtypeStruct((N_TOK_TILES, TOP_K, TILE), I32),
                   jax.ShapeDtypeStruct((N_TOK, LANES), F32),
                   jax.ShapeDtypeStruct((N_EXPERTS, LANES), F32)],
        scratch_shapes=[pltpu.VMEM((N_EXPERTS, 1), F32), pltpu.VMEM((N_EXPERTS, 1), F32)],
        compiler_params=_cparams(("arbitrary", "arbitrary")),
        name="moe_positions",
    )(route, tri)


def _work_items(counts):
    smem = pl.BlockSpec(memory_space=pltpu.SMEM)
    item = jax.ShapeDtypeStruct((N_ITEMS,), I32)
    return pl.pallas_call(
        _items_kernel,
        in_specs=[smem],
        out_specs=[smem] * 4,
        out_shape=[item] * 4,
        name="moe_work_items",
    )(counts)


def _items_kernel(cnt_ref, tile_ref, exp_ref, lo_ref, hi_ref):
    n = jnp.int32(0)
    start = jnp.int32(0)
    last_e = jnp.int32(0)
    for e in range(N_EXPERTS):
        cnt = cnt_ref[e]
        end = start + cnt
        first = start // EXP_TILE
        n_tiles = jnp.where(cnt > 0, (end - 1) // EXP_TILE - first + 1, 0)

        def put(j, carry, e=e, n=n, start=start, end=end, first=first):
            tile = first + j
            tile_ref[n + j] = tile
            exp_ref[n + j] = jnp.int32(e)
            lo_ref[n + j] = jnp.maximum(start - tile * EXP_TILE, 0)
            hi_ref[n + j] = jnp.minimum(end - tile * EXP_TILE, EXP_TILE)
            return carry

        lax.fori_loop(0, n_tiles, put, 0)
        n = n + n_tiles
        start = end
        last_e = jnp.where(cnt > 0, e, last_e)

    def pad(j, carry):
        tile_ref[j] = jnp.int32(N_EXP_TILES - 1)
        exp_ref[j] = last_e
        lo_ref[j] = jnp.int32(0)
        hi_ref[j] = jnp.int32(0)
        return carry

    lax.fori_loop(n, N_ITEMS, pad, 0)


def _sc_worker_base():
    return (lax.axis_index("s") * SC_CORES + lax.axis_index("c")) * SC_ROWS


def _sc_dispatch_body(h_hbm, p0_hbm, p1_hbm, out_hbm, i0_v, i1_v, rows_v, sem):
    base = _sc_worker_base()

    @pl.loop(0, SC_N_CHUNKS)
    def _(j):
        off = base + j * SC_CHUNK
        pltpu.sync_copy(h_hbm.at[pl.ds(off, SC_CHUNK)], rows_v)
        pltpu.sync_copy(p0_hbm.at[pl.ds(off, SC_CHUNK)], i0_v)
        pltpu.sync_copy(p1_hbm.at[pl.ds(off, SC_CHUNK)], i1_v)
        c0 = pltpu.async_copy(rows_v, out_hbm.at[i0_v], sem)
        c1 = pltpu.async_copy(rows_v, out_hbm.at[i1_v], sem)
        c0.wait()
        c1.wait()


def _sc_combine_body(y_hbm, p0_hbm, p1_hbm, g0_hbm, g1_hbm, i_v, rows_v, sem):
    base = _sc_worker_base()

    @pl.loop(0, SC_N_CHUNKS)
    def _(j):
        off = base + j * SC_CHUNK
        for p_hbm, g_hbm in ((p0_hbm, g0_hbm), (p1_hbm, g1_hbm)):
            pltpu.sync_copy(p_hbm.at[pl.ds(off, SC_CHUNK)], i_v)
            pltpu.async_copy(y_hbm.at[i_v], rows_v, sem).wait()
            pltpu.sync_copy(rows_v, g_hbm.at[pl.ds(off, SC_CHUNK)])


def _sc_mesh():
    return plsc.VectorSubcoreMesh(core_axis_name="c", subcore_axis_name="s")


def _dispatch(hp, pos0, pos1):
    return pl.kernel(
        _sc_dispatch_body,
        out_type=jax.ShapeDtypeStruct((N_ASSIGN, PACK_W), I32),
        mesh=_sc_mesh(),
        scratch_types=[pltpu.VMEM((SC_CHUNK,), I32), pltpu.VMEM((SC_CHUNK,), I32),
                       pltpu.VMEM((SC_CHUNK, PACK_W), I32), pltpu.SemaphoreType.DMA],
        name="moe_dispatch",
    )(hp, pos0, pos1)


def _combine(ys, pos0, pos1):
    row = jax.ShapeDtypeStruct((N_TOK, PACK_W), I32)
    return pl.kernel(
        _sc_combine_body,
        out_type=[row, row],
        mesh=_sc_mesh(),
        scratch_types=[pltpu.VMEM((SC_CHUNK,), I32), pltpu.VMEM((SC_CHUNK, PACK_W), I32),
                       pltpu.SemaphoreType.DMA],
        name="moe_combine",
    )(ys, pos0, pos1)


def _expert_kernel(tile_ref, exp_ref, lo_ref, hi_ref, x_ref, w1_ref, w3_ref, w2_ref, o_ref, w13_scr, w2_scr):
    i = pl.program_id(0)
    prev = jnp.maximum(i - 1, 0)

    @pl.when(jnp.logical_or(i == 0, exp_ref[i] != exp_ref[prev]))
    def _():
        w13_scr[:, 0:EXPERT_FF] = w1_ref[0, 0].astype(BF16)
        w13_scr[:, EXPERT_FF:] = w3_ref[0, 0].astype(BF16)
        w2_scr[...] = w2_ref[0, 0].astype(BF16)

    lo_f, hi_f = _unpack_rows(x_ref[...])
    h = jnp.concatenate([lo_f.astype(BF16), hi_f.astype(BF16)], axis=1)
    ab = jnp.dot(h, w13_scr[...], preferred_element_type=F32)
    a = ab[:, 0:EXPERT_FF]
    b = ab[:, EXPERT_FF:]
    t = ((a * jax.nn.sigmoid(a)) * b).astype(BF16)
    y = _pack_rows(jnp.dot(t, w2_scr[...], preferred_element_type=F32))
    row = lax.broadcasted_iota(jnp.int32, (EXP_TILE, PACK_W), 0)
    mine = jnp.logical_and(row >= lo_ref[i], row < hi_ref[i])
    revisit = jnp.logical_and(i > 0, tile_ref[i] == tile_ref[prev])

    @pl.when(jnp.logical_not(revisit))
    def _():
        o_ref[...] = jnp.where(mine, y, 0)

    @pl.when(revisit)
    def _():
        o_ref[...] = jnp.where(mine, y, o_ref[...])


def _experts(l, xs_sorted, items, w1, w3, w2):
    tile, e, lo, hi = items
    grid_spec = pltpu.PrefetchScalarGridSpec(
        num_scalar_prefetch=4,
        grid=(N_ITEMS,),
        in_specs=[pl.BlockSpec((EXP_TILE, PACK_W), lambda i, t, e, lo, hi: (t[i], 0)),
                  pl.BlockSpec((1, 1, D_MODEL, EXPERT_FF), lambda i, t, e, lo, hi: (l, e[i], 0, 0)),
                  pl.BlockSpec((1, 1, D_MODEL, EXPERT_FF), lambda i, t, e, lo, hi: (l, e[i], 0, 0)),
                  pl.BlockSpec((1, 1, EXPERT_FF, D_MODEL), lambda i, t, e, lo, hi: (l, e[i], 0, 0))],
        out_specs=pl.BlockSpec((EXP_TILE, PACK_W), lambda i, t, e, lo, hi: (t[i], 0)),
        scratch_shapes=[pltpu.VMEM((D_MODEL, 2 * EXPERT_FF), BF16), pltpu.VMEM((EXPERT_FF, D_MODEL), BF16)],
    )
    return pl.pallas_call(
        _expert_kernel,
        grid_spec=grid_spec,
        out_shape=jax.ShapeDtypeStruct((N_ASSIGN, PACK_W), I32),
        compiler_params=_cparams(("arbitrary",)),
        name="moe_experts",
    )(tile, e, lo, hi, xs_sorted, w1, w3, w2)


def _moe(l, hp, route, tri, w1, w3, w2):
    pos, wt, cnt = _positions(route, tri)
    pos0 = pos[:, 0, :].reshape(N_TOK)
    pos1 = pos[:, 1, :].reshape(N_TOK)
    items = _work_items(cnt[:, 0].astype(I32))
    xs_sorted = _dispatch(hp.reshape(N_TOK, PACK_W), pos0, pos1)
    ys = _experts(l, xs_sorted, items, w1, w3, w2)
    g0, g1 = _combine(ys, pos0, pos1)
    shape = (SB, L_TOT, PACK_W)
    return g0.reshape(shape), g1.reshape(shape), wt.reshape(SB, L_TOT, LANES)


def _final_kernel(x_ref, g0_ref, g1_ref, wt_ref, mod_ref, o_ref):
    o_ref[0] = _moe_residual(x_ref[0], g0_ref[0], g1_ref[0], wt_ref[0], mod_ref[0, 0][5:6, :])


def _final(boff, xs, moe, mod):
    lat = lambda width: pl.BlockSpec((1, TILE, width), lambda b, j: (b, j + 1, 0))
    return pl.pallas_call(
        _final_kernel,
        grid=(SB, N_LAT_TILES),
        in_specs=[lat(D_MODEL), lat(PACK_W), lat(PACK_W), lat(LANES),
                  pl.BlockSpec((1, 1, N_MOD, D_MODEL), lambda b, j: (DEPTH - 1, boff + b, 0, 0))],
        out_specs=pl.BlockSpec((1, TILE, D_MODEL), lambda b, j: (b, j, 0)),
        out_shape=jax.ShapeDtypeStruct((SB, SEQ, D_MODEL), F32),
        compiler_params=_cparams(("parallel", "parallel")),
        name="final_residual",
    )(xs, *moe, mod)


def _to_scan_major(t):
    b, n, d = t.shape
    return t.reshape(b, n // TILE, SUB, GROUPS, d).transpose(0, 1, 3, 2, 4).reshape(b, n, d)


def _from_scan_major(t):
    b, n, d = t.shape
    return t.reshape(b, n // TILE, GROUPS, SUB, d).transpose(0, 1, 3, 2, 4).reshape(b, n, d)


def _bias_tiles(table):
    rows_q = TILE // GRID_W
    rows_k = 3 * rows_q
    a = np.arange(rows_q)[:, None]
    c = np.arange(rows_k)[None, :]
    r_idx = np.clip(c - a + 3, 0, 2 * NA_WIN_ROWS - 2)
    qc = np.arange(GRID_W)[:, None]
    kc = np.arange(GRID_W)[None, :]
    col0 = np.clip(qc - NA_WIN_COLS // 2, 0, GRID_W - NA_WIN_COLS)
    in_win = (kc >= col0) & (kc < col0 + NA_WIN_COLS)
    pad = GRID_W - NA_WIN_COLS
    tp = jnp.pad(table.astype(F32), ((0, 0), (0, 0), (pad, pad)))
    toeplitz = jnp.stack([tp[:, :, GRID_W - 1 - q:2 * GRID_W - 1 - q] for q in range(GRID_W)], axis=2)
    blk = jnp.stack([jnp.stack([toeplitz[:, int(r_idx[i, j])] for j in range(rows_k)], axis=1)
                     for i in range(rows_q)], axis=1)
    allowed = np.stack([
        np.broadcast_to(c >= rows_q, (rows_q, rows_k)),
        (c >= a) & (c <= a + NA_WIN_ROWS - 1),
        np.broadcast_to(c < 2 * rows_q, (rows_q, rows_k)),
        np.zeros((rows_q, rows_k), bool),
    ])
    mask = allowed[:, None, :, :, None, None] & in_win[None, None, None, None, :, :]
    tiles = jnp.where(mask, blk[None] * LOG2E, NEG_INF)
    half = GRID_W // GROUPS
    n_h = table.shape[0]
    tiles = tiles.reshape(4, n_h, rows_q, 3, rows_q, half, GROUPS, half, GROUPS)
    tiles = tiles.transpose(0, 1, 6, 2, 5, 3, 8, 4, 7)
    return tiles.reshape(4, n_h, TILE, 3 * TILE)


def _block_diag(w, n_chunks):
    per = LRU_BLOCKS // n_chunks
    w = w.reshape(2, n_chunks, per, LRU_BLOCK, LRU_BLOCK)
    eye = jnp.eye(per, dtype=w.dtype)
    out = jnp.einsum('dcpij,pq->dcpiqj', w, eye)
    return out.reshape(2, n_chunks, per * LRU_BLOCK, per * LRU_BLOCK)


def kernel(x, c, ctx, c_ctx, w_mod, b_mod, norm_mix, norm_ffn, w_in, w_out, q_gain, k_gain, na_bias,
           conv_w, conv_b, lru_w_r, lru_b_r, lru_w_i, lru_b_i, lru_lambda, router_w, router_b,
           exp_w1, exp_w3, exp_w2):
    cs = jnp.concatenate([c, c_ctx[None, :], jnp.zeros((MOD_ROWS - BATCH - 1, D_MODEL), F32)], axis=0)
    mod = _modulation(cs, w_mod, b_mod).reshape(DEPTH, MOD_ROWS, N_MOD, D_MODEL)

    head_of = np.arange(NA_WIDTH) // HEAD_DIM
    bd = jnp.asarray((head_of[:, None] == head_of[None, :]).astype(np.float32) / HEAD_DIM, BF16)
    tri = jnp.asarray(np.triu(np.ones((TILE, TILE), np.float32), 1), BF16)
    rwt = router_w.T
    rwh = rwt.astype(BF16)
    rwl = (rwt - rwh.astype(F32)).astype(BF16)
    rb = router_b.reshape(N_EXPERTS, 1)
    n_cb = LRU_WIDTH // LRU_CH

    bias_tiles = _bias_tiles(na_bias.reshape(DEPTH * NA_HEADS, 2 * NA_WIN_ROWS - 1, 2 * NA_WIN_COLS - 1))
    streams = []
    for sidx in range(STREAMS):
        rows = slice(sidx * SB, (sidx + 1) * SB)
        streams.append({"xs": jnp.concatenate([_to_scan_major(ctx[rows]), _to_scan_major(x[rows])], axis=1),
                        "moe": None, "boff": sidx * SB})
    for l in range(DEPTH):
        qg = jnp.tile(q_gain[l] * (ATTN_SCALE * LOG2E), NA_HEADS)[None, :]
        kg = jnp.tile(k_gain[l], NA_HEADS)[None, :]
        w_in_l = w_in[l].astype(BF16)
        w_out_l = w_out[l].astype(BF16)
        wr = (0.5 * _block_diag(lru_w_r[l], n_cb)).astype(BF16)
        wi = (0.5 * _block_diag(lru_w_i[l], n_cb)).astype(BF16)
        for st in streams:
            boff = st["boff"]
            xs, (q, k, v, u, g) = _in_proj(l, boff, st["xs"], st["moe"], mod, norm_mix[l][None, :], w_in_l, bd,
                                           qg, kg)
            ya = _attention(l, q, k, v, bias_tiles)
            yb = _rglru(u, g, conv_w[l], conv_b[l][None, :], wr, wi, 0.5 * lru_b_r[l], 0.5 * lru_b_i[l],
                        lru_lambda[l])
            xs, hp, route = _out_proj(l, boff, xs, ya, yb, mod, norm_ffn[l][None, :], w_out_l, rwh, rwl, rb)
            st["xs"] = xs
            st["moe"] = _moe(l, hp, route.reshape(N_TOK_TILES, ROUTE_ROWS, TILE), tri, exp_w1, exp_w3, exp_w2)
    out = jnp.concatenate([_final(st["boff"], st["xs"], st["moe"], mod) for st in streams], axis=0)
    return _from_scan_major(out)
```

```python
import functools

import jax
import jax.numpy as jnp
import numpy as np
from jax import lax
from jax.experimental import pallas as pl
from jax.experimental.pallas import tpu as pltpu
from jax.experimental.pallas import tpu_sc as plsc

F32 = jnp.float32
BF16 = jnp.bfloat16
I32 = jnp.int32

D_MODEL = 1024
BATCH = 4
SEQ = 8192
DEPTH = 4
GRID_W = 64
CTX_LEN = 256
HEAD_DIM = 64
NA_WIDTH = 512
NA_HEADS = 8
NA_WIN_ROWS = 8
NA_WIN_COLS = 16
LRU_WIDTH = 512
LRU_BLOCKS = 8
LRU_BLOCK = 64
CONV_W = 4
LRU_C = 8.0
IN_COLS = 3 * NA_WIDTH + 2 * LRU_WIDTH
N_EXPERTS = 16
N_GROUPS = 4
EXPERTS_PER_GROUP = 4
TOP_K = 2
EXPERT_FF = 512
N_MOD = 6
ATTN_SCALE = HEAD_DIM ** -0.5
LOG2E = 1.4426950408889634
EPS = 1e-6
NEG_INF = -1e30

TILE = 256
SUB = 8
LANES = 128
GROUPS = TILE // SUB
L_TOT = CTX_LEN + SEQ
N_TILES = L_TOT // TILE
N_LAT_TILES = SEQ // TILE
STREAMS = 2
SB = BATCH // STREAMS
N_TOK = SB * L_TOT
N_TOK_TILES = N_TOK // TILE
PAIR = 2
LRU_CH = 256
MOD_ROWS = 8
VMEM_LIMIT = 56 * 1024 * 1024

PACK_W = D_MODEL // 2
HI_MASK = -65536
N_ASSIGN = TOP_K * N_TOK
EXP_TILE = 512
N_EXP_TILES = N_ASSIGN // EXP_TILE
N_ITEMS = N_EXP_TILES + N_EXPERTS - 1
ROUTE_ROWS = 2 * N_EXPERTS
POS_TILES = 6

SC_CORES = 2
SC_SUBCORES = 16
SC_WORKERS = SC_CORES * SC_SUBCORES
SC_ROWS = N_TOK // SC_WORKERS
SC_CHUNK = 88
SC_N_CHUNKS = SC_ROWS // SC_CHUNK


def _cparams(sem):
    return pltpu.CompilerParams(dimension_semantics=sem, vmem_limit_bytes=VMEM_LIMIT)


def _pack_rows(v):
    lo = pltpu.bitcast(v[:, :PACK_W].astype(BF16).astype(F32), I32)
    hi = pltpu.bitcast(v[:, PACK_W:].astype(BF16).astype(F32), I32)
    return ((lo >> 16) & 0xFFFF) | (hi & HI_MASK)


def _unpack_rows(p):
    return pltpu.bitcast(p << 16, F32), pltpu.bitcast(p & HI_MASK, F32)


def _mod_kernel(c_ref, w_ref, b_ref, o_ref):
    c = c_ref[...]
    s = c * jax.nn.sigmoid(c)
    o_ref[0] = jnp.dot(s.astype(BF16), w_ref[0].astype(BF16), preferred_element_type=F32) + b_ref[0]


def _modulation(cs, w_mod, b_mod):
    return pl.pallas_call(
        _mod_kernel,
        grid=(DEPTH, N_MOD),
        in_specs=[
            pl.BlockSpec((MOD_ROWS, D_MODEL), lambda l, n: (0, 0)),
            pl.BlockSpec((1, D_MODEL, D_MODEL), lambda l, n: (l, 0, n)),
            pl.BlockSpec((1, 1, D_MODEL), lambda l, n: (l, 0, n)),
        ],
        out_specs=pl.BlockSpec((1, MOD_ROWS, D_MODEL), lambda l, n: (l, 0, n)),
        out_shape=jax.ShapeDtypeStruct((DEPTH, MOD_ROWS, N_MOD * D_MODEL), F32),
        compiler_params=_cparams(("arbitrary", "arbitrary")),
        name="modulation",
    )(cs, w_mod, b_mod.reshape(DEPTH, 1, N_MOD * D_MODEL))


def _mod_row(b, i):
    return jnp.where(i == 0, BATCH, b)


def _moe_residual(x, g0, g1, wt, gate_row):
    lo0, hi0 = _unpack_rows(g0)
    lo1, hi1 = _unpack_rows(g1)
    w0 = wt[:, 0:1]
    w1 = wt[:, 1:2]
    f = jnp.concatenate([w0 * lo0 + w1 * lo1, w0 * hi0 + w1 * hi1], axis=1)
    return x + gate_row * f


def _pair_tok_spec(width):
    return pl.BlockSpec((PAIR, TILE, width), lambda bp, i: (bp, i, 0))


def _pair_mod_specs(layer, boff):
    return [pl.BlockSpec((1, 1, N_MOD, D_MODEL),
                         lambda bp, i, k=k: (layer, _mod_row(boff + PAIR * bp + k, i), 0, 0))
            for k in range(PAIR)]


def _full_spec(shape):
    return pl.BlockSpec(shape, lambda bp, i: tuple(0 for _ in shape))


def _to_scan_major(src_ref, scr):
    n_slab = D_MODEL // LANES
    for s in range(SUB):
        for j in range(n_slab):
            scr[j, pl.ds(s, GROUPS, stride=SUB), :] = src_ref[s * GROUPS:(s + 1) * GROUPS, j * LANES:(j + 1) * LANES]
    return jnp.concatenate([scr[j] for j in range(n_slab)], axis=1)


def _from_scan_major(val, scr):
    n_slab = D_MODEL // LANES
    for j in range(n_slab):
        scr[j] = val[:, j * LANES:(j + 1) * LANES]
    blocks = [jnp.concatenate([scr[j, pl.ds(s, GROUPS, stride=SUB), :] for j in range(n_slab)], axis=1)
              for s in range(SUB)]
    return jnp.concatenate(blocks, axis=0)


def _in_kernel(has_prev, *refs):
    if has_prev:
        x_ref, g0_ref, g1_ref, wt_ref = refs[:4]
        mprev_refs = refs[4:4 + PAIR]
        refs = (x_ref,) + refs[4 + PAIR:]
    else:
        x_ref, ctx_ref = refs[:2]
        refs = (x_ref,) + refs[2:]
    mod_refs = refs[1:1 + PAIR]
    nrm_ref, w_ref, bd_ref, qg_ref, kg_ref = refs[1 + PAIR:6 + PAIR]
    outs = refs[6 + PAIR:]
    xo_ref, q_ref, k_ref, v_ref, u_ref, g_ref = outs[:6]
    x_ref = refs[0]
    hs = []
    for k in range(PAIR):
        if has_prev:
            x = _moe_residual(x_ref[k], g0_ref[k], g1_ref[k], wt_ref[k], mprev_refs[k][0, 0][5:6, :])
            xo_ref[k] = x
        else:
            perm_scr = outs[6]

            @pl.when(pl.program_id(1) == 0)
            def _():
                xo_ref[k] = _to_scan_major(ctx_ref.at[k], perm_scr)

            @pl.when(pl.program_id(1) > 0)
            def _():
                xo_ref[k] = _to_scan_major(x_ref.at[k], perm_scr)

            x = xo_ref[k]
        m = mod_refs[k][0, 0]
        ms = jnp.mean(x * x, axis=-1, keepdims=True)
        h = (x * lax.rsqrt(ms + EPS)) * nrm_ref[...]
        hs.append((h * (1.0 + m[1:2, :]) + m[0:1, :]).astype(BF16))
    acc = jnp.dot(jnp.concatenate(hs, axis=0), w_ref[...], preferred_element_type=F32)
    bd = bd_ref[...]

    def head_norm(t, gain):
        ss = jnp.dot((t * t).astype(BF16), bd, preferred_element_type=F32)
        return (t * lax.rsqrt(ss + EPS)) * gain

    def put(ref, val):
        for k in range(PAIR):
            ref[k] = val[k * TILE:(k + 1) * TILE, :].astype(BF16)

    put(q_ref, head_norm(acc[:, 0:NA_WIDTH], qg_ref[...]))
    put(k_ref, head_norm(acc[:, NA_WIDTH:2 * NA_WIDTH], kg_ref[...]))
    put(v_ref, acc[:, 2 * NA_WIDTH:3 * NA_WIDTH])
    put(u_ref, acc[:, 3 * NA_WIDTH:3 * NA_WIDTH + LRU_WIDTH])
    put(g_ref, acc[:, 3 * NA_WIDTH + LRU_WIDTH:])


def _in_proj(l, boff, xs, moe_prev, mod, nrm, w_in, bd, qg, kg):
    has_prev = moe_prev is not None
    if has_prev:
        in_specs = [_pair_tok_spec(D_MODEL)]
        args = [xs]
        in_specs += [_pair_tok_spec(PACK_W), _pair_tok_spec(PACK_W), _pair_tok_spec(LANES)]
        in_specs += _pair_mod_specs(l - 1, boff)
        args += list(moe_prev) + [mod] * PAIR
    else:
        pair0 = boff // PAIR
        in_specs = [pl.BlockSpec((PAIR, TILE, D_MODEL), lambda bp, i: (pair0 + bp, jnp.maximum(i - 1, 0), 0)),
                    pl.BlockSpec((PAIR, TILE, D_MODEL), lambda bp, i: (pair0 + bp, 0, 0))]
        args = list(xs)
    in_specs += _pair_mod_specs(l, boff) + [_full_spec((1, D_MODEL)), _full_spec((D_MODEL, IN_COLS)),
                                            _full_spec((NA_WIDTH, NA_WIDTH)), _full_spec((1, NA_WIDTH)),
                                            _full_spec((1, NA_WIDTH))]
    args += [mod] * PAIR + [nrm, w_in, bd, qg, kg]
    half = jax.ShapeDtypeStruct((SB, L_TOT, NA_WIDTH), BF16)
    out_shape = [jax.ShapeDtypeStruct((SB, L_TOT, D_MODEL), F32)] + [half] * 5
    out_specs = [_pair_tok_spec(D_MODEL)] + [_pair_tok_spec(NA_WIDTH)] * 5
    scratch = [] if has_prev else [pltpu.VMEM((D_MODEL // LANES, TILE, LANES), F32)]
    outs = pl.pallas_call(
        functools.partial(_in_kernel, has_prev),
        grid=(SB // PAIR, N_TILES),
        in_specs=in_specs,
        out_specs=out_specs,
        out_shape=out_shape,
        scratch_shapes=scratch,
        compiler_params=_cparams(("parallel", "arbitrary")),
        name="in_proj",
    )(*args)
    return outs[0], outs[1:]


def _attn_kernel(q_ref, kp_ref, kc_ref, kn_ref, kx_ref, vp_ref, vc_ref, vn_ref, vx_ref, bias_ref, o_ref):
    pair_w = 2 * HEAD_DIM
    lane = lax.broadcasted_iota(jnp.int32, (TILE, pair_w), 1)
    low = lane < HEAD_DIM
    k_refs = (kp_ref, kc_ref, kn_ref, kx_ref)
    v_refs = (vp_ref, vc_ref, vn_ref, vx_ref)
    nt = (((1,), (1,)), ((), ()))
    for hp in range(NA_HEADS // 2):
        cols = slice(hp * pair_w, (hp + 1) * pair_w)
        q = q_ref[0, :, cols]
        kb = [r[0, :, cols] for r in k_refs]
        vb = [r[0, :, cols] for r in v_refs]
        outs = []
        for hh in range(2):
            own = low if hh == 0 else jnp.logical_not(low)
            qh = jnp.where(own, q, jnp.zeros_like(q))
            s = [lax.dot_general(qh, kb[i], nt, preferred_element_type=F32) for i in range(4)]
            for i in range(3):
                s[i] = s[i] + bias_ref[0, 2 * hp + hh, :, i * TILE:(i + 1) * TILE]
            m = jnp.max(jnp.maximum(jnp.maximum(s[0], s[1]), jnp.maximum(s[2], s[3])), axis=-1, keepdims=True)
            o = None
            for i in range(4):
                p = jnp.exp2((s[i] - m).astype(BF16))
                va = jnp.where(own, vb[i], jnp.ones_like(vb[i]))
                part = jnp.dot(p, va, preferred_element_type=F32)
                o = part if o is None else o + part
            outs.append(o / pltpu.roll(o, HEAD_DIM, 1))
        o_ref[0, :, cols] = jnp.where(low, outs[0], outs[1]).astype(BF16)


def _attention(l, q, k, v, bias_tiles):
    last = N_LAT_TILES - 1
    blk = (1, TILE, NA_WIDTH)
    prev_map = lambda b, j: (b, 1 + jnp.clip(j - 2, 0, last), 0)
    cur_map = lambda b, j: (b, jnp.maximum(j, 1), 0)
    next_map = lambda b, j: (b, 1 + jnp.clip(j, 0, last), 0)
    ctx_map = lambda b, j: (b, 0, 0)
    var_map = lambda b, j: (jnp.where(j == 0, 3, jnp.where(j == 1, 0, jnp.where(j == N_TILES - 1, 2, 1))), l, 0, 0)
    kv_specs = [pl.BlockSpec(blk, prev_map), pl.BlockSpec(blk, cur_map), pl.BlockSpec(blk, next_map),
                pl.BlockSpec(blk, ctx_map)]
    return pl.pallas_call(
        _attn_kernel,
        grid=(SB, N_TILES),
        in_specs=[pl.BlockSpec(blk, lambda b, j: (b, j, 0))] + kv_specs + kv_specs
        + [pl.BlockSpec((1, NA_HEADS, TILE, 3 * TILE), var_map)],
        out_specs=pl.BlockSpec(blk, lambda b, j: (b, j, 0)),
        out_shape=jax.ShapeDtypeStruct((SB, L_TOT, NA_WIDTH), BF16),
        compiler_params=_cparams(("parallel", "arbitrary")),
        name="na_attention",
    )(q, k, k, k, k, v, v, v, v, bias_tiles)


def _softplus(x):
    return jnp.maximum(x, 0.0) + jnp.log1p(jnp.exp(-jnp.abs(x)))


def _lru_kernel(u_ref, g_ref, cw_ref, cb_ref, wr_ref, wi_ref, br_ref, bi_ref, lam_ref, y_ref,
                hf_scr, v_scr, hl_scr, p_scr):
    C = LRU_CH
    cw = cw_ref[...]
    cb = cb_ref[...]
    sub = lax.broadcasted_iota(jnp.int32, (SUB, C), 0)

    def conv_tile(c, r0):
        U = u_ref[0, pl.ds(r0, TILE), :].astype(F32)
        pstart = pl.multiple_of(jnp.maximum(r0 - 16, 0), 16)
        nstart = pl.multiple_of(jnp.minimum(r0 + TILE, L_TOT - 16), 16)
        prev16 = u_ref[0, pl.ds(pstart, 16), :].astype(F32)
        next16 = u_ref[0, pl.ds(nstart, 16), :].astype(F32)
        has_prev = jnp.where(c >= 2, 1.0, 0.0).astype(F32)
        has_next = jnp.where(jnp.logical_and(c >= 1, c <= N_TILES - 2), 1.0, 0.0).astype(F32)
        prow = prev16[15:16, :] * has_prev
        n0 = next16[0:1, :] * has_next
        n8 = next16[8:9, :] * has_next
        first8 = jnp.where(sub == 0, prow, pltpu.roll(U[TILE - SUB:TILE, :], 1, 0))
        last_a = jnp.where(sub == SUB - 1, n0, pltpu.roll(U[0:SUB, :], SUB - 1, 0))
        last_b = jnp.where(sub == SUB - 1, n8, pltpu.roll(U[SUB:2 * SUB, :], SUB - 1, 0))
        um1 = jnp.concatenate([first8, U[0:TILE - SUB, :]], axis=0)
        up1 = jnp.concatenate([U[SUB:TILE, :], last_a], axis=0)
        up2 = jnp.concatenate([U[2 * SUB:TILE, :], last_a, last_b], axis=0)
        return cw[0:1, :] * um1 + cw[1:2, :] * U + cw[2:3, :] * up1 + cw[3:4, :] * up2 + cb

    def coeffs(v, d):
        vb = v.astype(BF16)
        tr = jnp.tanh(jnp.dot(vb, wr_ref[d, 0], preferred_element_type=F32) + br_ref[d:d + 1, :])
        ti = jnp.tanh(jnp.dot(vb, wi_ref[d, 0], preferred_element_type=F32) + bi_ref[d:d + 1, :])
        half = (-0.5 * LRU_C * LOG2E) * _softplus(-lam_ref[d:d + 1, :])
        a = jnp.exp2(half + half * tr)
        b = jnp.sqrt(1.0 - a * a) * ((0.5 * v) * (1.0 + ti))
        return a, b

    def scan_tile(a, b, h_in, reverse):
        order = range(GROUPS - 1, -1, -1) if reverse else range(GROUPS)
        hl = None
        for g in order:
            ag = a[g * SUB:(g + 1) * SUB, :]
            bg = b[g * SUB:(g + 1) * SUB, :]
            if hl is None:
                hl, p = bg, ag
            else:
                hl = ag * hl + bg
                p = ag * p
            hl_scr[g * SUB:(g + 1) * SUB, :] = hl
            p_scr[g * SUB:(g + 1) * SUB, :] = p
        blocks = range(SUB - 1, -1, -1) if reverse else range(SUB)
        carry = h_in
        cins = {}
        for s in blocks:
            cins[s] = carry
            carry = hl[s:s + 1, :] + p[s:s + 1, :] * carry
        cin = jnp.concatenate([cins[s] for s in range(SUB)], axis=0)
        hfull = hl_scr[...] + p_scr[...] * jnp.tile(cin, (GROUPS, 1))
        return hfull, carry

    zero = jnp.zeros((1, C), F32)

    def fwd_body(c, h):
        r0 = pl.multiple_of(c * TILE, TILE)
        v = conv_tile(c, r0)
        v_scr[pl.ds(r0, TILE), :] = v
        a, b = coeffs(v, 0)
        hfull, h = scan_tile(a, b, h, False)
        hf_scr[pl.ds(r0, TILE), :] = hfull
        return h

    lax.fori_loop(0, N_TILES, fwd_body, zero)

    def rev_tile(c, h):
        r0 = pl.multiple_of(c * TILE, TILE)
        a, b = coeffs(v_scr[pl.ds(r0, TILE), :], 1)
        hfull, h = scan_tile(a, b, h, True)
        gate = jax.nn.gelu(g_ref[0, pl.ds(r0, TILE), :].astype(F32))
        y_ref[0, pl.ds(r0, TILE), :] = (gate * (hf_scr[pl.ds(r0, TILE), :] + hfull)).astype(BF16)
        return h

    h_ctx = rev_tile(jnp.int32(0), zero)
    lax.fori_loop(0, N_LAT_TILES, lambda i, h: rev_tile(N_TILES - 1 - i, h), h_ctx)


def _rglru(u, g, cw, cb, wr, wi, br, bi, lam):
    n_cb = LRU_WIDTH // LRU_CH
    seq = pl.BlockSpec((1, L_TOT, LRU_CH), lambda b, c: (b, 0, c))
    vec = lambda rows: pl.BlockSpec((rows, LRU_CH), lambda b, c: (0, c))
    mat = pl.BlockSpec((2, 1, LRU_CH, LRU_CH), lambda b, c: (0, c, 0, 0))
    return pl.pallas_call(
        _lru_kernel,
        grid=(SB, n_cb),
        in_specs=[seq, seq, vec(CONV_W), vec(1), mat, mat, vec(2), vec(2), vec(2)],
        out_specs=seq,
        out_shape=jax.ShapeDtypeStruct((SB, L_TOT, LRU_WIDTH), BF16),
        scratch_shapes=[pltpu.VMEM((L_TOT, LRU_CH), F32), pltpu.VMEM((L_TOT, LRU_CH), F32),
                        pltpu.VMEM((TILE, LRU_CH), F32), pltpu.VMEM((TILE, LRU_CH), F32)],
        compiler_params=_cparams(("parallel", "parallel")),
        name="rglru",
    )(u, g, cw, cb, wr, wi, br, bi, lam)


def _route(sel, aff):
    def top2_sum(a, b, c, d):
        hi1, lo1 = jnp.maximum(a, b), jnp.minimum(a, b)
        hi2, lo2 = jnp.maximum(c, d), jnp.minimum(c, d)
        return jnp.maximum(hi1, hi2) + jnp.maximum(jnp.minimum(hi1, hi2), jnp.maximum(lo1, lo2))

    scores = [top2_sum(*sel[EXPERTS_PER_GROUP * g:EXPERTS_PER_GROUP * (g + 1)]) for g in range(N_GROUPS)]
    best = jnp.zeros_like(scores[0], dtype=jnp.int32)
    best_v = scores[0]
    for g in range(1, N_GROUPS):
        upd = scores[g] > best_v
        best = jnp.where(upd, g, best)
        best_v = jnp.where(upd, scores[g], best_v)
    chosen = []
    for e in range(N_EXPERTS):
        g = e // EXPERTS_PER_GROUP
        rank = jnp.zeros_like(best)
        for o in range(EXPERTS_PER_GROUP * g, EXPERTS_PER_GROUP * (g + 1)):
            if o == e:
                continue
            ahead = sel[o] > sel[e]
            if o < e:
                ahead = jnp.logical_or(ahead, sel[o] == sel[e])
            rank = rank + ahead.astype(jnp.int32)
        chosen.append(jnp.logical_and(best == g, rank < TOP_K))
    total = jnp.zeros_like(aff[0])
    for e in range(N_EXPERTS):
        total = total + jnp.where(chosen[e], aff[e], 0.0)
    gates = [jnp.where(chosen[e], aff[e] / total, 0.0) for e in range(N_EXPERTS)]
    return gates, [c.astype(F32) for c in chosen]


def _out_kernel(x_ref, ya_ref, yb_ref, *refs):
    mod_refs = refs[:PAIR]
    nrm_ref, w_ref, rwh_ref, rwl_ref, rb_ref, xo_ref, hp_ref, rt_ref = refs[PAIR:]
    rows = PAIR * TILE
    ya = ya_ref[...].reshape(rows, NA_WIDTH)
    yb = yb_ref[...].reshape(rows, LRU_WIDTH)
    y = jnp.dot(ya, w_ref[0:NA_WIDTH, :], preferred_element_type=F32)
    y = y + jnp.dot(yb, w_ref[NA_WIDTH:, :], preferred_element_type=F32)
    hs = []
    for k in range(PAIR):
        m = mod_refs[k][0, 0]
        x = x_ref[k] + m[2:3, :] * y[k * TILE:(k + 1) * TILE, :]
        xo_ref[k] = x
        ms = jnp.mean(x * x, axis=-1, keepdims=True)
        h = (x * lax.rsqrt(ms + EPS)) * nrm_ref[...]
        h = h * (1.0 + m[4:5, :]) + m[3:4, :]
        hp_ref[k] = _pack_rows(h)
        hs.append(h)
    h = jnp.concatenate(hs, axis=0)
    h_hi = h.astype(BF16)
    h_lo = (h - h_hi.astype(F32)).astype(BF16)
    nt = (((1,), (1,)), ((), ()))
    lg = (lax.dot_general(rwh_ref[...], h_hi, nt, preferred_element_type=F32)
          + lax.dot_general(rwh_ref[...], h_lo, nt, preferred_element_type=F32)
          + lax.dot_general(rwl_ref[...], h_hi, nt, preferred_element_type=F32))
    aff_all = jax.nn.sigmoid(lg)
    sel_all = aff_all + rb_ref[...]
    aff = [aff_all[e:e + 1, :] for e in range(N_EXPERTS)]
    sel = [sel_all[e:e + 1, :] for e in range(N_EXPERTS)]
    gates, chosen = _route(sel, aff)
    rt = jnp.concatenate(gates + chosen, axis=0)
    for k in range(PAIR):
        rt_ref[k, 0] = rt[:, k * TILE:(k + 1) * TILE]


def _out_proj(l, boff, xs, ya, yb, mod, nrm, w_out, rwh, rwl, rb):
    return pl.pallas_call(
        _out_kernel,
        grid=(SB // PAIR, N_TILES),
        in_specs=[_pair_tok_spec(D_MODEL), _pair_tok_spec(NA_WIDTH), _pair_tok_spec(LRU_WIDTH)]
        + _pair_mod_specs(l, boff)
        + [_full_spec((1, D_MODEL)), _full_spec((D_MODEL, D_MODEL)), _full_spec((N_EXPERTS, D_MODEL)),
           _full_spec((N_EXPERTS, D_MODEL)), _full_spec((N_EXPERTS, 1))],
        out_specs=[_pair_tok_spec(D_MODEL), _pair_tok_spec(PACK_W),
                   pl.BlockSpec((PAIR, 1, ROUTE_ROWS, TILE), lambda bp, i: (bp, i, 0, 0))],
        out_shape=[jax.ShapeDtypeStruct((SB, L_TOT, D_MODEL), F32),
                   jax.ShapeDtypeStruct((SB, L_TOT, PACK_W), I32),
                   jax.ShapeDtypeStruct((SB, N_TILES, ROUTE_ROWS, TILE), F32)],
        compiler_params=_cparams(("parallel", "parallel")),
        name="out_proj_router",
    )(xs, ya, yb, *([mod] * PAIR), nrm, w_out, rwh, rwl, rb)


def _pos_kernel(rt_ref, tri_ref, pos_ref, wt_ref, cnt_ref, run_scr, start_scr):
    phase = pl.program_id(0)
    i = pl.program_id(1)

    @pl.when(jnp.logical_and(phase == 0, i == 0))
    def _():
        run_scr[...] = jnp.zeros_like(run_scr)

    @pl.when(jnp.logical_and(phase == 1, i == 0))
    def _():
        tot = run_scr[...]
        cnt_ref[...] = jnp.broadcast_to(tot, (N_EXPERTS, LANES))
        acc = jnp.zeros((1, 1), F32)
        rows = []
        for e in range(N_EXPERTS):
            rows.append(acc)
            acc = acc + tot[e:e + 1, :]
        start_scr[...] = jnp.concatenate(rows, axis=0)
        run_scr[...] = jnp.zeros_like(run_scr)

    @pl.when(phase == 0)
    def _():
        tot = run_scr[...]
        for k in range(POS_TILES):
            tot = tot + jnp.sum(rt_ref[k, N_EXPERTS:, :], axis=1, keepdims=True)
        run_scr[...] = tot

    @pl.when(phase == 1)
    def _():
        base = start_scr[...] + run_scr[...]
        for k in range(POS_TILES):
            gates = rt_ref[k, 0:N_EXPERTS, :]
            chosen = rt_ref[k, N_EXPERTS:, :]
            rank = jnp.dot(chosen.astype(BF16), tri_ref[...], preferred_element_type=F32)
            posf = rank + base
            seen = jnp.zeros((1, TILE), F32)
            p0 = jnp.zeros((1, TILE), F32)
            p1 = jnp.zeros((1, TILE), F32)
            w0 = jnp.zeros((1, TILE), F32)
            w1 = jnp.zeros((1, TILE), F32)
            for e in range(N_EXPERTS):
                ch = chosen[e:e + 1, :]
                first = ch * (1.0 - seen)
                second = ch * seen
                p0 = p0 + first * posf[e:e + 1, :]
                p1 = p1 + second * posf[e:e + 1, :]
                w0 = w0 + first * gates[e:e + 1, :]
                w1 = w1 + second * gates[e:e + 1, :]
                seen = jnp.minimum(seen + ch, 1.0)
            pos_ref[k] = jnp.concatenate([p0, p1], axis=0).astype(I32)
            wpad = jnp.concatenate([w0, w1, jnp.zeros((LANES - TOP_K, TILE), F32)], axis=0)
            wt_ref[k * TILE:(k + 1) * TILE, :] = jnp.transpose(wpad)
            base = base + jnp.sum(chosen, axis=1, keepdims=True)
        run_scr[...] = base - start_scr[...]


def _positions(route, tri):
    return pl.pallas_call(
        _pos_kernel,
        grid=(2, N_TOK_TILES // POS_TILES),
        in_specs=[pl.BlockSpec((POS_TILES, ROUTE_ROWS, TILE), lambda p, i: (i, 0, 0)),
                  pl.BlockSpec((TILE, TILE), lambda p, i: (0, 0))],
        out_specs=[pl.BlockSpec((POS_TILES, TOP_K, TILE), lambda p, i: (i * p, 0, 0)),
                   pl.BlockSpec((POS_TILES * TILE, LANES), lambda p, i: (i * p, 0)),
                   pl.BlockSpec((N_EXPERTS, LANES), lambda p, i: (0, 0))],
        out_shape=[jax.ShapeDtypeStruct((N_TOK_TILES, TOP_K, TILE), I32),
                   jax.ShapeDtypeStruct((N_TOK, LANES), F32),
                   jax.ShapeDtypeStruct((N_EXPERTS, LANES), F32)],
        scratch_shapes=[pltpu.VMEM((N_EXPERTS, 1), F32), pltpu.VMEM((N_EXPERTS, 1), F32)],
        compiler_params=_cparams(("arbitrary", "arbitrary")),
        name="moe_positions",
    )(route, tri)


def _work_items(counts):
    smem = pl.BlockSpec(memory_space=pltpu.SMEM)
    item = jax.ShapeDtypeStruct((N_ITEMS,), I32)
    return pl.pallas_call(
        _items_kernel,
        in_specs=[smem],
        out_specs=[smem] * 4,
        out_shape=[item] * 4,
        name="moe_work_items",
    )(counts)


def _items_kernel(cnt_ref, tile_ref, exp_ref, lo_ref, hi_ref):
    n = jnp.int32(0)
    start = jnp.int32(0)
    last_e = jnp.int32(0)
    for e in range(N_EXPERTS):
        cnt = cnt_ref[e]
        end = start + cnt
        first = start // EXP_TILE
        n_tiles = jnp.where(cnt > 0, (end - 1) // EXP_TILE - first + 1, 0)

        def put(j, carry, e=e, n=n, start=start, end=end, first=first):
            tile = first + j
            tile_ref[n + j] = tile
            exp_ref[n + j] = jnp.int32(e)
            lo_ref[n + j] = jnp.maximum(start - tile * EXP_TILE, 0)
            hi_ref[n + j] = jnp.minimum(end - tile * EXP_TILE, EXP_TILE)
            return carry

        lax.fori_loop(0, n_tiles, put, 0)
        n = n + n_tiles
        start = end
        last_e = jnp.where(cnt > 0, e, last_e)

    def pad(j, carry):
        tile_ref[j] = jnp.int32(N_EXP_TILES - 1)
        exp_ref[j] = last_e
        lo_ref[j] = jnp.int32(0)
        hi_ref[j] = jnp.int32(0)
        return carry

    lax.fori_loop(n, N_ITEMS, pad, 0)


def _sc_worker_base():
    return (lax.axis_index("s") * SC_CORES + lax.axis_index("c")) * SC_ROWS


def _sc_dispatch_body(h_hbm, p0_hbm, p1_hbm, out_hbm, i0_v, i1_v, rows_v, sem):
    base = _sc_worker_base()

    @pl.loop(0, SC_N_CHUNKS)
    def _(j):
        off = base + j * SC_CHUNK
        pltpu.sync_copy(h_hbm.at[pl.ds(off, SC_CHUNK)], rows_v)
        pltpu.sync_copy(p0_hbm.at[pl.ds(off, SC_CHUNK)], i0_v)
        pltpu.sync_copy(p1_hbm.at[pl.ds(off, SC_CHUNK)], i1_v)
        c0 = pltpu.async_copy(rows_v, out_hbm.at[i0_v], sem)
        c1 = pltpu.async_copy(rows_v, out_hbm.at[i1_v], sem)
        c0.wait()
        c1.wait()


def _sc_combine_body(y_hbm, p0_hbm, p1_hbm, g0_hbm, g1_hbm, i_v, rows_v, sem):
    base = _sc_worker_base()

    @pl.loop(0, SC_N_CHUNKS)
    def _(j):
        off = base + j * SC_CHUNK
        for p_hbm, g_hbm in ((p0_hbm, g0_hbm), (p1_hbm, g1_hbm)):
            pltpu.sync_copy(p_hbm.at[pl.ds(off, SC_CHUNK)], i_v)
            pltpu.async_copy(y_hbm.at[i_v], rows_v, sem).wait()
            pltpu.sync_copy(rows_v, g_hbm.at[pl.ds(off, SC_CHUNK)])


def _sc_mesh():
    return plsc.VectorSubcoreMesh(core_axis_name="c", subcore_axis_name="s")


def _dispatch(hp, pos0, pos1):
    return pl.kernel(
        _sc_dispatch_body,
        out_type=jax.ShapeDtypeStruct((N_ASSIGN, PACK_W), I32),
        mesh=_sc_mesh(),
        scratch_types=[pltpu.VMEM((SC_CHUNK,), I32), pltpu.VMEM((SC_CHUNK,), I32),
                       pltpu.VMEM((SC_CHUNK, PACK_W), I32), pltpu.SemaphoreType.DMA],
        name="moe_dispatch",
    )(hp, pos0, pos1)


def _combine(ys, pos0, pos1):
    row = jax.ShapeDtypeStruct((N_TOK, PACK_W), I32)
    return pl.kernel(
        _sc_combine_body,
        out_type=[row, row],
        mesh=_sc_mesh(),
        scratch_types=[pltpu.VMEM((SC_CHUNK,), I32), pltpu.VMEM((SC_CHUNK, PACK_W), I32),
                       pltpu.SemaphoreType.DMA],
        name="moe_combine",
    )(ys, pos0, pos1)


def _expert_kernel(tile_ref, exp_ref, lo_ref, hi_ref, x_ref, w1_ref, w3_ref, w2_ref, o_ref, w13_scr, w2_scr):
    i = pl.program_id(0)
    prev = jnp.maximum(i - 1, 0)

    @pl.when(jnp.logical_or(i == 0, exp_ref[i] != exp_ref[prev]))
    def _():
        w13_scr[:, 0:EXPERT_FF] = w1_ref[0, 0].astype(BF16)
        w13_scr[:, EXPERT_FF:] = w3_ref[0, 0].astype(BF16)
        w2_scr[...] = w2_ref[0, 0].astype(BF16)

    lo_f, hi_f = _unpack_rows(x_ref[...])
    h = jnp.concatenate([lo_f.astype(BF16), hi_f.astype(BF16)], axis=1)
    ab = jnp.dot(h, w13_scr[...], preferred_element_type=F32)
    a = ab[:, 0:EXPERT_FF]
    b = ab[:, EXPERT_FF:]
    t = ((a * jax.nn.sigmoid(a)) * b).astype(BF16)
    y = _pack_rows(jnp.dot(t, w2_scr[...], preferred_element_type=F32))
    row = lax.broadcasted_iota(jnp.int32, (EXP_TILE, PACK_W), 0)
    mine = jnp.logical_and(row >= lo_ref[i], row < hi_ref[i])
    revisit = jnp.logical_and(i > 0, tile_ref[i] == tile_ref[prev])

    @pl.when(jnp.logical_not(revisit))
    def _():
        o_ref[...] = jnp.where(mine, y, 0)

    @pl.when(revisit)
    def _():
        o_ref[...] = jnp.where(mine, y, o_ref[...])


def _experts(l, xs_sorted, items, w1, w3, w2):
    tile, e, lo, hi = items
    grid_spec = pltpu.PrefetchScalarGridSpec(
        num_scalar_prefetch=4,
        grid=(N_ITEMS,),
        in_specs=[pl.BlockSpec((EXP_TILE, PACK_W), lambda i, t, e, lo, hi: (t[i], 0)),
                  pl.BlockSpec((1, 1, D_MODEL, EXPERT_FF), lambda i, t, e, lo, hi: (l, e[i], 0, 0)),
                  pl.BlockSpec((1, 1, D_MODEL, EXPERT_FF), lambda i, t, e, lo, hi: (l, e[i], 0, 0)),
                  pl.BlockSpec((1, 1, EXPERT_FF, D_MODEL), lambda i, t, e, lo, hi: (l, e[i], 0, 0))],
        out_specs=pl.BlockSpec((EXP_TILE, PACK_W), lambda i, t, e, lo, hi: (t[i], 0)),
        scratch_shapes=[pltpu.VMEM((D_MODEL, 2 * EXPERT_FF), BF16), pltpu.VMEM((EXPERT_FF, D_MODEL), BF16)],
    )
    return pl.pallas_call(
        _expert_kernel,
        grid_spec=grid_spec,
        out_shape=jax.ShapeDtypeStruct((N_ASSIGN, PACK_W), I32),
        compiler_params=_cparams(("arbitrary",)),
        name="moe_experts",
    )(tile, e, lo, hi, xs_sorted, w1, w3, w2)


def _moe(l, hp, route, tri, w1, w3, w2):
    pos, wt, cnt = _positions(route, tri)
    pos0 = pos[:, 0, :].reshape(N_TOK)
    pos1 = pos[:, 1, :].reshape(N_TOK)
    items = _work_items(cnt[:, 0].astype(I32))
    xs_sorted = _dispatch(hp.reshape(N_TOK, PACK_W), pos0, pos1)
    ys = _experts(l, xs_sorted, items, w1, w3, w2)
    g0, g1 = _combine(ys, pos0, pos1)
    shape = (SB, L_TOT, PACK_W)
    return g0.reshape(shape), g1.reshape(shape), wt.reshape(SB, L_TOT, LANES)


def _final_kernel(*refs):
    n_in = 4
    stream_refs = [refs[k * n_in:(k + 1) * n_in] for k in range(STREAMS)]
    mod_ref, o_ref, scr = refs[STREAMS * n_in:]
    for k in range(STREAMS):
        @pl.when(pl.program_id(0) == k)
        def _(k=k):
            x_ref, g0_ref, g1_ref, wt_ref = stream_refs[k]
            x = _moe_residual(x_ref[0], g0_ref[0], g1_ref[0], wt_ref[0], mod_ref[0, 0][5:6, :])
            o_ref[0] = _from_scan_major(x, scr)


def _final(streams, mod):
    def lat(k, width):
        park_b, park_j = (SB - 1, N_TILES - 1) if k == 0 else (0, 1)
        return pl.BlockSpec((1, TILE, width),
                            lambda s, b, j: (jnp.where(s == k, b, park_b), jnp.where(s == k, j + 1, park_j), 0))

    in_specs, args = [], []
    for k, st in enumerate(streams):
        in_specs += [lat(k, D_MODEL), lat(k, PACK_W), lat(k, PACK_W), lat(k, LANES)]
        args += [st["xs"]] + list(st["moe"])
    in_specs.append(pl.BlockSpec((1, 1, N_MOD, D_MODEL), lambda s, b, j: (DEPTH - 1, s * SB + b, 0, 0)))
    return pl.pallas_call(
        _final_kernel,
        grid=(STREAMS, SB, N_LAT_TILES),
        in_specs=in_specs,
        out_specs=pl.BlockSpec((1, TILE, D_MODEL), lambda s, b, j: (s * SB + b, j, 0)),
        out_shape=jax.ShapeDtypeStruct((BATCH, SEQ, D_MODEL), F32),
        scratch_shapes=[pltpu.VMEM((D_MODEL // LANES, TILE, LANES), F32)],
        compiler_params=_cparams(("arbitrary", "arbitrary", "arbitrary")),
        name="final_residual",
    )(*args, mod)


def _bias_kernel(t4_ref, o_ref, scr):
    rows_q = TILE // GRID_W
    lane = lax.broadcasted_iota(jnp.int32, (GROUPS, TILE), 1)
    ka = (lane >> 1) & (rows_q - 1)
    neg = jnp.full((GROUPS, TILE), NEG_INF, F32)
    for a in range(rows_q):
        for qs in range(SUB // rows_q):
            s = (SUB // rows_q) * a + qs
            for kt in range(3):
                src = t4_ref[0, rows_q * kt - a + 3, qs * GROUPS:(qs + 1) * GROUPS, :]
                c = rows_q * kt + ka
                variants = (
                    src if kt >= 1 else neg,
                    jnp.where(jnp.logical_and(c >= a, c <= a + NA_WIN_ROWS - 1), src, neg),
                    src if kt <= 1 else neg,
                )
                for v, val in enumerate(variants):
                    for half in range(TILE // LANES):
                        scr[v, 2 * kt + half, pl.ds(s, GROUPS, stride=SUB), :] = val[:, half * LANES:(half + 1) * LANES]
    for v in range(3):
        o_ref[v, 0] = jnp.concatenate([scr[v, j] for j in range(3 * TILE // LANES)], axis=1)
    o_ref[3, 0] = jnp.full((TILE, 3 * TILE), NEG_INF, F32)


def _bias_tiles(table):
    rows_q = TILE // GRID_W
    n_r0 = 3 * rows_q
    qc = np.arange(GRID_W)[:, None]
    kc = np.arange(GRID_W)[None, :]
    col0 = np.clip(qc - NA_WIN_COLS // 2, 0, GRID_W - NA_WIN_COLS)
    in_win = (kc >= col0) & (kc < col0 + NA_WIN_COLS)
    pad = GRID_W - NA_WIN_COLS
    tp = jnp.pad(table.astype(F32) * LOG2E, ((0, 0), (0, 0), (pad, pad)))
    toeplitz = jnp.stack([tp[:, :, GRID_W - 1 - q:2 * GRID_W - 1 - q] for q in range(GRID_W)], axis=2)
    toeplitz = jnp.where(in_win[None, None], toeplitz, NEG_INF)
    n_h = table.shape[0]
    half = GRID_W // GROUPS
    t4 = jnp.stack([toeplitz[:, k:k + n_r0] for k in range(rows_q)], axis=-1)
    t4 = t4.reshape(n_h, n_r0, GRID_W, half, GROUPS, rows_q).transpose(0, 1, 2, 4, 5, 3)
    t4 = t4.reshape(n_h, n_r0, GRID_W, TILE)
    return pl.pallas_call(
        _bias_kernel,
        grid=(n_h,),
        in_specs=[pl.BlockSpec((1, n_r0, GRID_W, TILE), lambda h: (h, 0, 0, 0))],
        out_specs=pl.BlockSpec((4, 1, TILE, 3 * TILE), lambda h: (0, h, 0, 0)),
        out_shape=jax.ShapeDtypeStruct((4, n_h, TILE, 3 * TILE), F32),
        scratch_shapes=[pltpu.VMEM((3, 3 * TILE // LANES, TILE, LANES), F32)],
        compiler_params=_cparams(("parallel",)),
        name="bias_tiles",
    )(t4)


def _block_diag(w, n_chunks):
    per = LRU_BLOCKS // n_chunks
    w = w.reshape(2, n_chunks, per, LRU_BLOCK, LRU_BLOCK)
    eye = jnp.eye(per, dtype=w.dtype)
    out = jnp.einsum('dcpij,pq->dcpiqj', w, eye)
    return out.reshape(2, n_chunks, per * LRU_BLOCK, per * LRU_BLOCK)


def kernel(x, c, ctx, c_ctx, w_mod, b_mod, norm_mix, norm_ffn, w_in, w_out, q_gain, k_gain, na_bias,
           conv_w, conv_b, lru_w_r, lru_b_r, lru_w_i, lru_b_i, lru_lambda, router_w, router_b,
           exp_w1, exp_w3, exp_w2):
    cs = jnp.concatenate([c, c_ctx[None, :], jnp.zeros((MOD_ROWS - BATCH - 1, D_MODEL), F32)], axis=0)
    mod = _modulation(cs, w_mod, b_mod).reshape(DEPTH, MOD_ROWS, N_MOD, D_MODEL)

    head_of = np.arange(NA_WIDTH) // HEAD_DIM
    bd = jnp.asarray((head_of[:, None] == head_of[None, :]).astype(np.float32) / HEAD_DIM, BF16)
    tri = jnp.asarray(np.triu(np.ones((TILE, TILE), np.float32), 1), BF16)
    rwt = router_w.T
    rwh = rwt.astype(BF16)
    rwl = (rwt - rwh.astype(F32)).astype(BF16)
    rb = router_b.reshape(N_EXPERTS, 1)
    n_cb = LRU_WIDTH // LRU_CH

    bias_tiles = _bias_tiles(na_bias.reshape(DEPTH * NA_HEADS, 2 * NA_WIN_ROWS - 1, 2 * NA_WIN_COLS - 1))
    streams = [{"xs": (x, ctx), "moe": None, "boff": sidx * SB} for sidx in range(STREAMS)]
    for l in range(DEPTH):
        qg = jnp.tile(q_gain[l] * (ATTN_SCALE * LOG2E), NA_HEADS)[None, :]
        kg = jnp.tile(k_gain[l], NA_HEADS)[None, :]
        w_in_l = w_in[l].astype(BF16)
        w_out_l = w_out[l].astype(BF16)
        wr = (0.5 * _block_diag(lru_w_r[l], n_cb)).astype(BF16)
        wi = (0.5 * _block_diag(lru_w_i[l], n_cb)).astype(BF16)
        for st in streams:
            boff = st["boff"]
            xs, (q, k, v, u, g) = _in_proj(l, boff, st["xs"], st["moe"], mod, norm_mix[l][None, :], w_in_l, bd,
                                           qg, kg)
            ya = _attention(l, q, k, v, bias_tiles)
            yb = _rglru(u, g, conv_w[l], conv_b[l][None, :], wr, wi, 0.5 * lru_b_r[l], 0.5 * lru_b_i[l],
                        lru_lambda[l])
            xs, hp, route = _out_proj(l, boff, xs, ya, yb, mod, norm_ffn[l][None, :], w_out_l, rwh, rwl, rb)
            st["xs"] = xs
            st["moe"] = _moe(l, hp, route.reshape(N_TOK_TILES, ROUTE_ROWS, TILE), tri, exp_w1, exp_w3, exp_w2)
    return _final(streams, mod)
```

```python
import functools

import jax
import jax.numpy as jnp
import numpy as np
from jax import lax
from jax.experimental import pallas as pl
from jax.experimental.pallas import tpu as pltpu
from jax.experimental.pallas import tpu_sc as plsc

F32 = jnp.float32
BF16 = jnp.bfloat16
I32 = jnp.int32

D_MODEL = 1024
BATCH = 4
SEQ = 8192
DEPTH = 4
GRID_W = 64
CTX_LEN = 256
HEAD_DIM = 64
NA_WIDTH = 512
NA_HEADS = 8
NA_WIN_ROWS = 8
NA_WIN_COLS = 16
LRU_WIDTH = 512
LRU_BLOCKS = 8
LRU_BLOCK = 64
CONV_W = 4
LRU_C = 8.0
IN_COLS = 3 * NA_WIDTH + 2 * LRU_WIDTH
N_EXPERTS = 16
N_GROUPS = 4
EXPERTS_PER_GROUP = 4
TOP_K = 2
EXPERT_FF = 512
N_MOD = 6
ATTN_SCALE = HEAD_DIM ** -0.5
LOG2E = 1.4426950408889634
EPS = 1e-6
NEG_INF = -1e30
TINY = 1e-30
GELU_C = 0.7978845608028654

TILE = 256
SUB = 8
LANES = 128
GROUPS = TILE // SUB
L_TOT = CTX_LEN + SEQ
N_TILES = L_TOT // TILE
N_LAT_TILES = SEQ // TILE
STREAMS = 2
SB = BATCH // STREAMS
N_TOK = SB * L_TOT
N_TOK_TILES = N_TOK // TILE
PAIR = 2
LRU_CH = 256
MOD_ROWS = 8
VMEM_LIMIT = 56 * 1024 * 1024

PACK_W = D_MODEL // 2
HI_MASK = -65536
N_ASSIGN = TOP_K * N_TOK
EXP_TILE = 512
EXP_HALF = 256
N_EXP_TILES = N_ASSIGN // EXP_TILE
N_ITEMS = N_EXP_TILES + N_EXPERTS - 1
ROUTE_ROWS = 2 * N_EXPERTS
POS_TILES = 6

SC_CORES = 2
SC_SUBCORES = 16
SC_WORKERS = SC_CORES * SC_SUBCORES
SC_ROWS = N_TOK // SC_WORKERS
SC_CHUNK = 88
SC_N_CHUNKS = SC_ROWS // SC_CHUNK


def _cparams(sem):
    return pltpu.CompilerParams(dimension_semantics=sem, vmem_limit_bytes=VMEM_LIMIT)


def _pack_rows(v):
    lo = pltpu.bitcast(v[:, :PACK_W].astype(BF16).astype(F32), I32)
    hi = pltpu.bitcast(v[:, PACK_W:].astype(BF16).astype(F32), I32)
    return ((lo >> 16) & 0xFFFF) | (hi & HI_MASK)


def _unpack_rows(p):
    return pltpu.bitcast(p << 16, F32), pltpu.bitcast(p & HI_MASK, F32)


def _mod_kernel(c_ref, w_ref, b_ref, o_ref):
    c = c_ref[...]
    s = c * jax.nn.sigmoid(c)
    o_ref[0] = jnp.dot(s.astype(BF16), w_ref[0].astype(BF16), preferred_element_type=F32) + b_ref[0]


def _modulation(cs, w_mod, b_mod):
    return pl.pallas_call(
        _mod_kernel,
        grid=(DEPTH, N_MOD),
        in_specs=[
            pl.BlockSpec((MOD_ROWS, D_MODEL), lambda l, n: (0, 0)),
            pl.BlockSpec((1, D_MODEL, D_MODEL), lambda l, n: (l, 0, n)),
            pl.BlockSpec((1, 1, D_MODEL), lambda l, n: (l, 0, n)),
        ],
        out_specs=pl.BlockSpec((1, MOD_ROWS, D_MODEL), lambda l, n: (l, 0, n)),
        out_shape=jax.ShapeDtypeStruct((DEPTH, MOD_ROWS, N_MOD * D_MODEL), F32),
        compiler_params=_cparams(("arbitrary", "arbitrary")),
        name="modulation",
    )(cs, w_mod, b_mod.reshape(DEPTH, 1, N_MOD * D_MODEL))


def _mod_row(b, i):
    return jnp.where(i == 0, BATCH, b)


def _moe_residual(x, g0, g1, wt, gate_row):
    lo0, hi0 = _unpack_rows(g0)
    lo1, hi1 = _unpack_rows(g1)
    w0 = wt[:, 0:1]
    w1 = wt[:, 1:2]
    f = jnp.concatenate([w0 * lo0 + w1 * lo1, w0 * hi0 + w1 * hi1], axis=1)
    return x + gate_row * f


def _pair_tok_spec(width):
    return pl.BlockSpec((PAIR, TILE, width), lambda bp, i: (bp, i, 0))


def _pair_mod_specs(layer, boff):
    return [pl.BlockSpec((1, 1, N_MOD, D_MODEL),
                         lambda bp, i, k=k: (layer, _mod_row(boff + PAIR * bp + k, i), 0, 0))
            for k in range(PAIR)]


def _full_spec(shape):
    return pl.BlockSpec(shape, lambda bp, i: tuple(0 for _ in shape))


def _to_scan_major(src_ref, scr):
    n_slab = D_MODEL // LANES
    for s in range(SUB):
        for j in range(n_slab):
            scr[j, pl.ds(s, GROUPS, stride=SUB), :] = src_ref[s * GROUPS:(s + 1) * GROUPS, j * LANES:(j + 1) * LANES]
    return jnp.concatenate([scr[j] for j in range(n_slab)], axis=1)


def _from_scan_major(val, scr):
    n_slab = D_MODEL // LANES
    for j in range(n_slab):
        scr[j] = val[:, j * LANES:(j + 1) * LANES]
    blocks = [jnp.concatenate([scr[j, pl.ds(s, GROUPS, stride=SUB), :] for j in range(n_slab)], axis=1)
              for s in range(SUB)]
    return jnp.concatenate(blocks, axis=0)


def _in_kernel(has_prev, *refs):
    if has_prev:
        x_ref, g0_ref, g1_ref, wt_ref = refs[:4]
        mprev_refs = refs[4:4 + PAIR]
        refs = (x_ref,) + refs[4 + PAIR:]
    else:
        x_ref, ctx_ref = refs[:2]
        refs = (x_ref,) + refs[2:]
    mod_refs = refs[1:1 + PAIR]
    nrm_ref, w_ref, bd_ref, qg_ref, kg_ref = refs[1 + PAIR:6 + PAIR]
    outs = refs[6 + PAIR:]
    xo_ref, q_ref, k_ref, v_ref, u_ref, g_ref = outs[:6]
    x_ref = refs[0]
    hs = []
    for k in range(PAIR):
        if has_prev:
            x = _moe_residual(x_ref[k], g0_ref[k], g1_ref[k], wt_ref[k], mprev_refs[k][0, 0][5:6, :])
            xo_ref[k] = x
        else:
            perm_scr = outs[6]

            @pl.when(pl.program_id(1) == 0)
            def _():
                xo_ref[k] = _to_scan_major(ctx_ref.at[k], perm_scr)

            @pl.when(pl.program_id(1) > 0)
            def _():
                xo_ref[k] = _to_scan_major(x_ref.at[k], perm_scr)

            x = xo_ref[k]
        m = mod_refs[k][0, 0]
        ms = jnp.mean(x * x, axis=-1, keepdims=True)
        h = (x * lax.rsqrt(ms + EPS)) * nrm_ref[...]
        hs.append((h * (1.0 + m[1:2, :]) + m[0:1, :]).astype(BF16))
    acc = jnp.dot(jnp.concatenate(hs, axis=0), w_ref[...], preferred_element_type=F32)
    bd = bd_ref[...]

    def head_norm(t, gain):
        ss = jnp.dot((t * t).astype(BF16), bd, preferred_element_type=F32)
        return (t * lax.rsqrt(ss + EPS)) * gain

    def put(ref, val):
        for k in range(PAIR):
            ref[k] = val[k * TILE:(k + 1) * TILE, :].astype(BF16)

    put(q_ref, head_norm(acc[:, 0:NA_WIDTH], qg_ref[...]))
    put(k_ref, head_norm(acc[:, NA_WIDTH:2 * NA_WIDTH], kg_ref[...]))
    put(v_ref, acc[:, 2 * NA_WIDTH:3 * NA_WIDTH])
    put(u_ref, acc[:, 3 * NA_WIDTH:3 * NA_WIDTH + LRU_WIDTH])
    put(g_ref, acc[:, 3 * NA_WIDTH + LRU_WIDTH:])


def _in_proj(l, boff, xs, moe_prev, mod, nrm, w_in, bd, qg, kg):
    has_prev = moe_prev is not None
    if has_prev:
        in_specs = [_pair_tok_spec(D_MODEL)]
        args = [xs]
        in_specs += [_pair_tok_spec(PACK_W), _pair_tok_spec(PACK_W), _pair_tok_spec(LANES)]
        in_specs += _pair_mod_specs(l - 1, boff)
        args += list(moe_prev) + [mod] * PAIR
    else:
        pair0 = boff // PAIR
        in_specs = [pl.BlockSpec((PAIR, TILE, D_MODEL), lambda bp, i: (pair0 + bp, jnp.maximum(i - 1, 0), 0)),
                    pl.BlockSpec((PAIR, TILE, D_MODEL), lambda bp, i: (pair0 + bp, 0, 0))]
        args = list(xs)
    in_specs += _pair_mod_specs(l, boff) + [_full_spec((1, D_MODEL)), _full_spec((D_MODEL, IN_COLS)),
                                            _full_spec((NA_WIDTH, NA_WIDTH)), _full_spec((1, NA_WIDTH)),
                                            _full_spec((1, NA_WIDTH))]
    args += [mod] * PAIR + [nrm, w_in, bd, qg, kg]
    half = jax.ShapeDtypeStruct((SB, L_TOT, NA_WIDTH), BF16)
    out_shape = [jax.ShapeDtypeStruct((SB, L_TOT, D_MODEL), F32)] + [half] * 5
    out_specs = [_pair_tok_spec(D_MODEL)] + [_pair_tok_spec(NA_WIDTH)] * 5
    scratch = [] if has_prev else [pltpu.VMEM((D_MODEL // LANES, TILE, LANES), F32)]
    outs = pl.pallas_call(
        functools.partial(_in_kernel, has_prev),
        grid=(SB // PAIR, N_TILES),
        in_specs=in_specs,
        out_specs=out_specs,
        out_shape=out_shape,
        scratch_shapes=scratch,
        compiler_params=_cparams(("parallel", "arbitrary")),
        name="in_proj",
    )(*args)
    return outs[0], outs[1:]


def _attn_kernel(q_ref, kp_ref, kc_ref, kn_ref, kx_ref, vp_ref, vc_ref, vn_ref, vx_ref, bias_ref, o_ref):
    pair_w = 2 * HEAD_DIM
    lane = lax.broadcasted_iota(jnp.int32, (TILE, pair_w), 1)
    low = lane < HEAD_DIM
    k_refs = (kp_ref, kc_ref, kn_ref, kx_ref)
    v_refs = (vp_ref, vc_ref, vn_ref, vx_ref)
    nt = (((1,), (1,)), ((), ()))
    n_win = 3 * TILE
    low_kv = lax.broadcasted_iota(jnp.int32, (4 * TILE, pair_w), 1) < HEAD_DIM
    for hp in range(NA_HEADS // 2):
        cols = slice(hp * pair_w, (hp + 1) * pair_w)
        q = q_ref[0, :, cols]
        kb = jnp.concatenate([r[0, :, cols] for r in k_refs], axis=0)
        vb = jnp.concatenate([r[0, :, cols] for r in v_refs], axis=0)
        outs = []
        for hh in range(2):
            own = low if hh == 0 else jnp.logical_not(low)
            own_kv = low_kv if hh == 0 else jnp.logical_not(low_kv)
            qh = jnp.where(own, q, jnp.zeros_like(q))
            s = lax.dot_general(qh, kb, nt, preferred_element_type=F32)
            s_win = s[:, 0:n_win] + bias_ref[0, 2 * hp + hh]
            s_ctx = s[:, n_win:]
            m = jnp.maximum(jnp.max(s_win, axis=-1, keepdims=True), jnp.max(s_ctx, axis=-1, keepdims=True))
            p = jnp.concatenate([jnp.exp2((s_win - m).astype(BF16)), jnp.exp2((s_ctx - m).astype(BF16))], axis=1)
            va = jnp.where(own_kv, vb, jnp.ones_like(vb))
            o = jnp.dot(p, va, preferred_element_type=F32)
            outs.append(o / pltpu.roll(o, HEAD_DIM, 1))
        o_ref[0, :, cols] = jnp.where(low, outs[0], outs[1]).astype(BF16)


def _attention(l, q, k, v, bias_tiles):
    last = N_LAT_TILES - 1
    blk = (1, TILE, NA_WIDTH)
    prev_map = lambda b, j: (b, 1 + jnp.clip(j - 2, 0, last), 0)
    cur_map = lambda b, j: (b, jnp.maximum(j, 1), 0)
    next_map = lambda b, j: (b, 1 + jnp.clip(j, 0, last), 0)
    ctx_map = lambda b, j: (b, 0, 0)
    var_map = lambda b, j: (jnp.where(j == 0, 3, jnp.where(j == 1, 0, jnp.where(j == N_TILES - 1, 2, 1))), l, 0, 0)
    kv_specs = [pl.BlockSpec(blk, prev_map), pl.BlockSpec(blk, cur_map), pl.BlockSpec(blk, next_map),
                pl.BlockSpec(blk, ctx_map)]
    return pl.pallas_call(
        _attn_kernel,
        grid=(SB, N_TILES),
        in_specs=[pl.BlockSpec(blk, lambda b, j: (b, j, 0))] + kv_specs + kv_specs
        + [pl.BlockSpec((1, NA_HEADS, TILE, 3 * TILE), var_map)],
        out_specs=pl.BlockSpec(blk, lambda b, j: (b, j, 0)),
        out_shape=jax.ShapeDtypeStruct((SB, L_TOT, NA_WIDTH), BF16),
        compiler_params=_cparams(("parallel", "arbitrary")),
        name="na_attention",
    )(q, k, k, k, k, v, v, v, v, bias_tiles)


def _softplus(x):
    return jnp.maximum(x, 0.0) + jnp.log1p(jnp.exp(-jnp.abs(x)))


def _lru_kernel(u_ref, g_ref, cw_ref, cb_ref, wr_ref, wi_ref, br_ref, bi_ref, lam_ref, y_ref,
                hf_scr, v_scr, hl_scr, p_scr):
    C = LRU_CH
    cw = cw_ref[...]
    cb = cb_ref[...]
    sub = lax.broadcasted_iota(jnp.int32, (SUB, C), 0)

    def conv_tile(c, r0):
        U = u_ref[0, pl.ds(r0, TILE), :].astype(F32)
        pstart = pl.multiple_of(jnp.maximum(r0 - 16, 0), 16)
        nstart = pl.multiple_of(jnp.minimum(r0 + TILE, L_TOT - 16), 16)
        prev16 = u_ref[0, pl.ds(pstart, 16), :].astype(F32)
        next16 = u_ref[0, pl.ds(nstart, 16), :].astype(F32)
        has_prev = jnp.where(c >= 2, 1.0, 0.0).astype(F32)
        has_next = jnp.where(jnp.logical_and(c >= 1, c <= N_TILES - 2), 1.0, 0.0).astype(F32)
        prow = prev16[15:16, :] * has_prev
        n0 = next16[0:1, :] * has_next
        n8 = next16[8:9, :] * has_next
        first8 = jnp.where(sub == 0, prow, pltpu.roll(U[TILE - SUB:TILE, :], 1, 0))
        last_a = jnp.where(sub == SUB - 1, n0, pltpu.roll(U[0:SUB, :], SUB - 1, 0))
        last_b = jnp.where(sub == SUB - 1, n8, pltpu.roll(U[SUB:2 * SUB, :], SUB - 1, 0))
        um1 = jnp.concatenate([first8, U[0:TILE - SUB, :]], axis=0)
        up1 = jnp.concatenate([U[SUB:TILE, :], last_a], axis=0)
        up2 = jnp.concatenate([U[2 * SUB:TILE, :], last_a, last_b], axis=0)
        return cw[0:1, :] * um1 + cw[1:2, :] * U + cw[2:3, :] * up1 + cw[3:4, :] * up2 + cb

    def coeffs(v, d):
        vb = v.astype(BF16)
        tr = jnp.tanh(jnp.dot(vb, wr_ref[d, 0], preferred_element_type=F32) + br_ref[d:d + 1, :])
        ti = jnp.tanh(jnp.dot(vb, wi_ref[d, 0], preferred_element_type=F32) + bi_ref[d:d + 1, :])
        half = (-0.5 * LRU_C * LOG2E) * _softplus(-lam_ref[d:d + 1, :])
        a = jnp.exp2(half + half * tr)
        om = 1.0 - a * a
        b = (om * lax.rsqrt(jnp.maximum(om, TINY))) * ((0.5 * v) * (1.0 + ti))
        return a, b

    def scan_tile(a, b, h_in, reverse):
        order = range(GROUPS - 1, -1, -1) if reverse else range(GROUPS)
        hl = None
        for g in order:
            ag = a[g * SUB:(g + 1) * SUB, :]
            bg = b[g * SUB:(g + 1) * SUB, :]
            if hl is None:
                hl, p = bg, ag
            else:
                hl = ag * hl + bg
                p = ag * p
            hl_scr[g * SUB:(g + 1) * SUB, :] = hl
            p_scr[g * SUB:(g + 1) * SUB, :] = p
        blocks = range(SUB - 1, -1, -1) if reverse else range(SUB)
        carry = h_in
        cins = {}
        for s in blocks:
            cins[s] = carry
            carry = hl[s:s + 1, :] + p[s:s + 1, :] * carry
        cin = jnp.concatenate([cins[s] for s in range(SUB)], axis=0)
        hfull = hl_scr[...] + p_scr[...] * jnp.tile(cin, (GROUPS, 1))
        return hfull, carry

    zero = jnp.zeros((1, C), F32)

    def fwd_body(c, h):
        r0 = pl.multiple_of(c * TILE, TILE)
        v = conv_tile(c, r0)
        v_scr[pl.ds(r0, TILE), :] = v
        a, b = coeffs(v, 0)
        hfull, h = scan_tile(a, b, h, False)
        hf_scr[pl.ds(r0, TILE), :] = hfull
        return h

    lax.fori_loop(0, N_TILES, fwd_body, zero)

    def rev_tile(c, h):
        r0 = pl.multiple_of(c * TILE, TILE)
        a, b = coeffs(v_scr[pl.ds(r0, TILE), :], 1)
        hfull, h = scan_tile(a, b, h, True)
        gx = g_ref[0, pl.ds(r0, TILE), :].astype(F32)
        gate = (0.5 * gx) * (1.0 + jnp.tanh(gx * (GELU_C + (GELU_C * 0.044715) * (gx * gx))))
        y_ref[0, pl.ds(r0, TILE), :] = (gate * (hf_scr[pl.ds(r0, TILE), :] + hfull)).astype(BF16)
        return h

    h_ctx = rev_tile(jnp.int32(0), zero)
    lax.fori_loop(0, N_LAT_TILES, lambda i, h: rev_tile(N_TILES - 1 - i, h), h_ctx)


def _rglru(u, g, cw, cb, wr, wi, br, bi, lam):
    n_cb = LRU_WIDTH // LRU_CH
    seq = pl.BlockSpec((1, L_TOT, LRU_CH), lambda b, c: (b, 0, c))
    vec = lambda rows: pl.BlockSpec((rows, LRU_CH), lambda b, c: (0, c))
    mat = pl.BlockSpec((2, 1, LRU_CH, LRU_CH), lambda b, c: (0, c, 0, 0))
    return pl.pallas_call(
        _lru_kernel,
        grid=(SB, n_cb),
        in_specs=[seq, seq, vec(CONV_W), vec(1), mat, mat, vec(2), vec(2), vec(2)],
        out_specs=seq,
        out_shape=jax.ShapeDtypeStruct((SB, L_TOT, LRU_WIDTH), BF16),
        scratch_shapes=[pltpu.VMEM((L_TOT, LRU_CH), F32), pltpu.VMEM((L_TOT, LRU_CH), F32),
                        pltpu.VMEM((TILE, LRU_CH), F32), pltpu.VMEM((TILE, LRU_CH), F32)],
        compiler_params=_cparams(("parallel", "parallel")),
        name="rglru",
    )(u, g, cw, cb, wr, wi, br, bi, lam)


def _route(sel, aff):
    def top2_sum(a, b, c, d):
        hi1, lo1 = jnp.maximum(a, b), jnp.minimum(a, b)
        hi2, lo2 = jnp.maximum(c, d), jnp.minimum(c, d)
        return jnp.maximum(hi1, hi2) + jnp.maximum(jnp.minimum(hi1, hi2), jnp.maximum(lo1, lo2))

    scores = [top2_sum(*sel[EXPERTS_PER_GROUP * g:EXPERTS_PER_GROUP * (g + 1)]) for g in range(N_GROUPS)]
    best = jnp.zeros_like(scores[0], dtype=jnp.int32)
    best_v = scores[0]
    for g in range(1, N_GROUPS):
        upd = scores[g] > best_v
        best = jnp.where(upd, g, best)
        best_v = jnp.where(upd, scores[g], best_v)
    chosen = []
    for e in range(N_EXPERTS):
        g = e // EXPERTS_PER_GROUP
        rank = jnp.zeros_like(best)
        for o in range(EXPERTS_PER_GROUP * g, EXPERTS_PER_GROUP * (g + 1)):
            if o == e:
                continue
            ahead = sel[o] > sel[e]
            if o < e:
                ahead = jnp.logical_or(ahead, sel[o] == sel[e])
            rank = rank + ahead.astype(jnp.int32)
        chosen.append(jnp.logical_and(best == g, rank < TOP_K))
    total = jnp.zeros_like(aff[0])
    for e in range(N_EXPERTS):
        total = total + jnp.where(chosen[e], aff[e], 0.0)
    gates = [jnp.where(chosen[e], aff[e] / total, 0.0) for e in range(N_EXPERTS)]
    return gates, [c.astype(F32) for c in chosen]


def _out_kernel(x_ref, ya_ref, yb_ref, *refs):
    mod_refs = refs[:PAIR]
    nrm_ref, w_ref, rwc_ref, rb_ref, xo_ref, hp_ref, rt_ref = refs[PAIR:]
    rows = PAIR * TILE
    ya = ya_ref[...].reshape(rows, NA_WIDTH)
    yb = yb_ref[...].reshape(rows, LRU_WIDTH)
    y = jnp.dot(ya, w_ref[0:NA_WIDTH, :], preferred_element_type=F32)
    y = y + jnp.dot(yb, w_ref[NA_WIDTH:, :], preferred_element_type=F32)
    hs = []
    for k in range(PAIR):
        m = mod_refs[k][0, 0]
        x = x_ref[k] + m[2:3, :] * y[k * TILE:(k + 1) * TILE, :]
        xo_ref[k] = x
        ms = jnp.mean(x * x, axis=-1, keepdims=True)
        h = (x * lax.rsqrt(ms + EPS)) * nrm_ref[...]
        h = h * (1.0 + m[4:5, :]) + m[3:4, :]
        hp_ref[k] = _pack_rows(h)
        hs.append(h)
    h = jnp.concatenate(hs, axis=0)
    h_hi = h.astype(BF16)
    h_lo = (h - h_hi.astype(F32)).astype(BF16)
    nt = (((1,), (1,)), ((), ()))
    rwc = rwc_ref[...]
    both = lax.dot_general(rwc, h_hi, nt, preferred_element_type=F32)
    lg = (both[0:N_EXPERTS, :] + both[N_EXPERTS:, :]
          + lax.dot_general(rwc[0:N_EXPERTS, :], h_lo, nt, preferred_element_type=F32))
    aff_all = jax.nn.sigmoid(lg)
    sel_all = aff_all + rb_ref[...]
    aff = [aff_all[e:e + 1, :] for e in range(N_EXPERTS)]
    sel = [sel_all[e:e + 1, :] for e in range(N_EXPERTS)]
    gates, chosen = _route(sel, aff)
    rt = jnp.concatenate(gates + chosen, axis=0)
    for k in range(PAIR):
        rt_ref[k, 0] = rt[:, k * TILE:(k + 1) * TILE]


def _out_proj(l, boff, xs, ya, yb, mod, nrm, w_out, rwc, rb):
    return pl.pallas_call(
        _out_kernel,
        grid=(SB // PAIR, N_TILES),
        in_specs=[_pair_tok_spec(D_MODEL), _pair_tok_spec(NA_WIDTH), _pair_tok_spec(LRU_WIDTH)]
        + _pair_mod_specs(l, boff)
        + [_full_spec((1, D_MODEL)), _full_spec((D_MODEL, D_MODEL)), _full_spec((2 * N_EXPERTS, D_MODEL)),
           _full_spec((N_EXPERTS, 1))],
        out_specs=[_pair_tok_spec(D_MODEL), _pair_tok_spec(PACK_W),
                   pl.BlockSpec((PAIR, 1, ROUTE_ROWS, TILE), lambda bp, i: (bp, i, 0, 0))],
        out_shape=[jax.ShapeDtypeStruct((SB, L_TOT, D_MODEL), F32),
                   jax.ShapeDtypeStruct((SB, L_TOT, PACK_W), I32),
                   jax.ShapeDtypeStruct((SB, N_TILES, ROUTE_ROWS, TILE), F32)],
        compiler_params=_cparams(("parallel", "parallel")),
        name="out_proj_router",
    )(xs, ya, yb, *([mod] * PAIR), nrm, w_out, rwc, rb)


def _pos_kernel(rt_ref, tri_ref, pos_ref, wt_ref, cnt_ref, run_scr, start_scr):
    phase = pl.program_id(0)
    i = pl.program_id(1)

    @pl.when(jnp.logical_and(phase == 0, i == 0))
    def _():
        run_scr[...] = jnp.zeros_like(run_scr)

    @pl.when(jnp.logical_and(phase == 1, i == 0))
    def _():
        tot = run_scr[...]
        cnt_ref[...] = jnp.broadcast_to(tot, (N_EXPERTS, LANES))
        acc = jnp.zeros((1, 1), F32)
        rows = []
        for e in range(N_EXPERTS):
            rows.append(acc)
            acc = acc + tot[e:e + 1, :]
        start_scr[...] = jnp.concatenate(rows, axis=0)
        run_scr[...] = jnp.zeros_like(run_scr)

    @pl.when(phase == 0)
    def _():
        tot = run_scr[...]
        for k in range(POS_TILES):
            tot = tot + jnp.sum(rt_ref[k, N_EXPERTS:, :], axis=1, keepdims=True)
        run_scr[...] = tot

    @pl.when(phase == 1)
    def _():
        base = start_scr[...] + run_scr[...]
        for k in range(POS_TILES):
            gates = rt_ref[k, 0:N_EXPERTS, :]
            chosen = rt_ref[k, N_EXPERTS:, :]
            rank = jnp.dot(chosen.astype(BF16), tri_ref[...], preferred_element_type=F32)
            posf = rank + base
            seen = jnp.zeros((1, TILE), F32)
            p0 = jnp.zeros((1, TILE), F32)
            p1 = jnp.zeros((1, TILE), F32)
            w0 = jnp.zeros((1, TILE), F32)
            w1 = jnp.zeros((1, TILE), F32)
            for e in range(N_EXPERTS):
                ch = chosen[e:e + 1, :]
                first = ch * (1.0 - seen)
                second = ch * seen
                p0 = p0 + first * posf[e:e + 1, :]
                p1 = p1 + second * posf[e:e + 1, :]
                w0 = w0 + first * gates[e:e + 1, :]
                w1 = w1 + second * gates[e:e + 1, :]
                seen = jnp.minimum(seen + ch, 1.0)
            pos_ref[k] = jnp.concatenate([p0, p1], axis=0).astype(I32)
            wpad = jnp.concatenate([w0, w1, jnp.zeros((LANES - TOP_K, TILE), F32)], axis=0)
            wt_ref[k * TILE:(k + 1) * TILE, :] = jnp.transpose(wpad)
            base = base + jnp.sum(chosen, axis=1, keepdims=True)
        run_scr[...] = base - start_scr[...]


def _positions(route, tri):
    return pl.pallas_call(
        _pos_kernel,
        grid=(2, N_TOK_TILES // POS_TILES),
        in_specs=[pl.BlockSpec((POS_TILES, ROUTE_ROWS, TILE), lambda p, i: (i, 0, 0)),
                  pl.BlockSpec((TILE, TILE), lambda p, i: (0, 0))],
        out_specs=[pl.BlockSpec((POS_TILES, TOP_K, TILE), lambda p, i: (i * p, 0, 0)),
                   pl.BlockSpec((POS_TILES * TILE, LANES), lambda p, i: (i * p, 0)),
                   pl.BlockSpec((N_EXPERTS, LANES), lambda p, i: (0, 0))],
        out_shape=[jax.ShapeDtypeStruct((N_TOK_TILES, TOP_K, TILE), I32),
                   jax.ShapeDtypeStruct((N_TOK, LANES), F32),
                   jax.ShapeDtypeStruct((N_EXPERTS, LANES), F32)],
        scratch_shapes=[pltpu.VMEM((N_EXPERTS, 1), F32), pltpu.VMEM((N_EXPERTS, 1), F32)],
        compiler_params=_cparams(("arbitrary", "arbitrary")),
        name="moe_positions",
    )(route, tri)


def _work_items(counts):
    smem = pl.BlockSpec(memory_space=pltpu.SMEM)
    item = jax.ShapeDtypeStruct((N_ITEMS,), I32)
    return pl.pallas_call(
        _items_kernel,
        in_specs=[smem],
        out_specs=[smem] * 4,
        out_shape=[item] * 4,
        name="moe_work_items",
    )(counts)


def _items_kernel(cnt_ref, tile_ref, exp_ref, lo_ref, hi_ref):
    n = jnp.int32(0)
    start = jnp.int32(0)
    last_e = jnp.int32(0)
    for e in range(N_EXPERTS):
        cnt = cnt_ref[e]
        end = start + cnt
        first = start // EXP_TILE
        n_tiles = jnp.where(cnt > 0, (end - 1) // EXP_TILE - first + 1, 0)

        def put(j, carry, e=e, n=n, start=start, end=end, first=first):
            tile = first + j
            tile_ref[n + j] = tile
            exp_ref[n + j] = jnp.int32(e)
            lo_ref[n + j] = jnp.maximum(start - tile * EXP_TILE, 0)
            hi_ref[n + j] = jnp.minimum(end - tile * EXP_TILE, EXP_TILE)
            return carry

        lax.fori_loop(0, n_tiles, put, 0)
        n = n + n_tiles
        start = end
        last_e = jnp.where(cnt > 0, e, last_e)

    def pad(j, carry):
        tile_ref[j] = jnp.int32(N_EXP_TILES - 1)
        exp_ref[j] = last_e
        lo_ref[j] = jnp.int32(0)
        hi_ref[j] = jnp.int32(0)
        return carry

    lax.fori_loop(n, N_ITEMS, pad, 0)


def _sc_worker_base():
    return (lax.axis_index("s") * SC_CORES + lax.axis_index("c")) * SC_ROWS


def _sc_dispatch_body(h_hbm, p0_hbm, p1_hbm, out_hbm, i0_v, i1_v, rows_v, sem):
    base = _sc_worker_base()

    @pl.loop(0, SC_N_CHUNKS)
    def _(j):
        off = base + j * SC_CHUNK
        pltpu.sync_copy(h_hbm.at[pl.ds(off, SC_CHUNK)], rows_v)
        pltpu.sync_copy(p0_hbm.at[pl.ds(off, SC_CHUNK)], i0_v)
        pltpu.sync_copy(p1_hbm.at[pl.ds(off, SC_CHUNK)], i1_v)
        c0 = pltpu.async_copy(rows_v, out_hbm.at[i0_v], sem)
        c1 = pltpu.async_copy(rows_v, out_hbm.at[i1_v], sem)
        c0.wait()
        c1.wait()


def _sc_combine_body(y_hbm, p0_hbm, p1_hbm, g0_hbm, g1_hbm, i_v, rows_v, sem):
    base = _sc_worker_base()

    @pl.loop(0, SC_N_CHUNKS)
    def _(j):
        off = base + j * SC_CHUNK
        for p_hbm, g_hbm in ((p0_hbm, g0_hbm), (p1_hbm, g1_hbm)):
            pltpu.sync_copy(p_hbm.at[pl.ds(off, SC_CHUNK)], i_v)
            pltpu.async_copy(y_hbm.at[i_v], rows_v, sem).wait()
            pltpu.sync_copy(rows_v, g_hbm.at[pl.ds(off, SC_CHUNK)])


def _sc_mesh():
    return plsc.VectorSubcoreMesh(core_axis_name="c", subcore_axis_name="s")


def _dispatch(hp, pos0, pos1):
    return pl.kernel(
        _sc_dispatch_body,
        out_type=jax.ShapeDtypeStruct((N_ASSIGN, PACK_W), I32),
        mesh=_sc_mesh(),
        scratch_types=[pltpu.VMEM((SC_CHUNK,), I32), pltpu.VMEM((SC_CHUNK,), I32),
                       pltpu.VMEM((SC_CHUNK, PACK_W), I32), pltpu.SemaphoreType.DMA],
        name="moe_dispatch",
    )(hp, pos0, pos1)


def _combine(ys, pos0, pos1):
    row = jax.ShapeDtypeStruct((N_TOK, PACK_W), I32)
    return pl.kernel(
        _sc_combine_body,
        out_type=[row, row],
        mesh=_sc_mesh(),
        scratch_types=[pltpu.VMEM((SC_CHUNK,), I32), pltpu.VMEM((SC_CHUNK, PACK_W), I32),
                       pltpu.SemaphoreType.DMA],
        name="moe_combine",
    )(ys, pos0, pos1)


def _expert_kernel(tile_ref, exp_ref, lo_ref, hi_ref, x_ref, w1_ref, w3_ref, w2_ref, o_ref, w13_scr, w2_scr):
    i = pl.program_id(0)
    prev = jnp.maximum(i - 1, 0)

    @pl.when(jnp.logical_or(i == 0, exp_ref[i] != exp_ref[prev]))
    def _():
        w13_scr[:, 0:EXPERT_FF] = w1_ref[0, 0].astype(BF16)
        w13_scr[:, EXPERT_FF:] = w3_ref[0, 0].astype(BF16)
        w2_scr[...] = w2_ref[0, 0].astype(BF16)

    lo = lo_ref[i]
    hi = hi_ref[i]
    revisit = jnp.logical_and(i > 0, tile_ref[i] == tile_ref[prev])
    for r0 in range(0, EXP_TILE, EXP_HALF):
        rows = slice(r0, r0 + EXP_HALF)
        has_rows = jnp.logical_and(lo < r0 + EXP_HALF, hi > r0)

        @pl.when(has_rows)
        def _(r0=r0, rows=rows):
            lo_f, hi_f = _unpack_rows(x_ref[rows, :])
            h = jnp.concatenate([lo_f.astype(BF16), hi_f.astype(BF16)], axis=1)
            ab = jnp.dot(h, w13_scr[...], preferred_element_type=F32)
            a = ab[:, 0:EXPERT_FF]
            b = ab[:, EXPERT_FF:]
            t = ((a * jax.nn.sigmoid(a)) * b).astype(BF16)
            y = _pack_rows(jnp.dot(t, w2_scr[...], preferred_element_type=F32))
            row = r0 + lax.broadcasted_iota(jnp.int32, (EXP_HALF, PACK_W), 0)
            mine = jnp.logical_and(row >= lo, row < hi)
            @pl.when(revisit)
            def _():
                o_ref[rows, :] = jnp.where(mine, y, o_ref[rows, :])

            @pl.when(jnp.logical_not(revisit))
            def _():
                o_ref[rows, :] = jnp.where(mine, y, 0)

        @pl.when(jnp.logical_and(jnp.logical_not(has_rows), jnp.logical_not(revisit)))
        def _(rows=rows):
            o_ref[rows, :] = jnp.zeros((EXP_HALF, PACK_W), I32)


def _experts(l, xs_sorted, items, w1, w3, w2):
    tile, e, lo, hi = items
    grid_spec = pltpu.PrefetchScalarGridSpec(
        num_scalar_prefetch=4,
        grid=(N_ITEMS,),
        in_specs=[pl.BlockSpec((EXP_TILE, PACK_W), lambda i, t, e, lo, hi: (t[i], 0)),
                  pl.BlockSpec((1, 1, D_MODEL, EXPERT_FF), lambda i, t, e, lo, hi: (l, e[i], 0, 0)),
                  pl.BlockSpec((1, 1, D_MODEL, EXPERT_FF), lambda i, t, e, lo, hi: (l, e[i], 0, 0)),
                  pl.BlockSpec((1, 1, EXPERT_FF, D_MODEL), lambda i, t, e, lo, hi: (l, e[i], 0, 0))],
        out_specs=pl.BlockSpec((EXP_TILE, PACK_W), lambda i, t, e, lo, hi: (t[i], 0)),
        scratch_shapes=[pltpu.VMEM((D_MODEL, 2 * EXPERT_FF), BF16), pltpu.VMEM((EXPERT_FF, D_MODEL), BF16)],
    )
    return pl.pallas_call(
        _expert_kernel,
        grid_spec=grid_spec,
        out_shape=jax.ShapeDtypeStruct((N_ASSIGN, PACK_W), I32),
        compiler_params=_cparams(("arbitrary",)),
        name="moe_experts",
    )(tile, e, lo, hi, xs_sorted, w1, w3, w2)


def _moe(l, hp, route, tri, w1, w3, w2):
    pos, wt, cnt = _positions(route, tri)
    pos0 = pos[:, 0, :].reshape(N_TOK)
    pos1 = pos[:, 1, :].reshape(N_TOK)
    items = _work_items(cnt[:, 0].astype(I32))
    xs_sorted = _dispatch(hp.reshape(N_TOK, PACK_W), pos0, pos1)
    ys = _experts(l, xs_sorted, items, w1, w3, w2)
    g0, g1 = _combine(ys, pos0, pos1)
    shape = (SB, L_TOT, PACK_W)
    return g0.reshape(shape), g1.reshape(shape), wt.reshape(SB, L_TOT, LANES)


def _final_kernel(*refs):
    n_in = 4
    stream_refs = [refs[k * n_in:(k + 1) * n_in] for k in range(STREAMS)]
    mod_ref, o_ref, scr = refs[STREAMS * n_in:]
    for k in range(STREAMS):
        @pl.when(pl.program_id(0) == k)
        def _(k=k):
            x_ref, g0_ref, g1_ref, wt_ref = stream_refs[k]
            x = _moe_residual(x_ref[0], g0_ref[0], g1_ref[0], wt_ref[0], mod_ref[0, 0][5:6, :])
            o_ref[0] = _from_scan_major(x, scr)


def _final(streams, mod):
    def lat(k, width):
        park_b, park_j = (SB - 1, N_TILES - 1) if k == 0 else (0, 1)
        return pl.BlockSpec((1, TILE, width),
                            lambda s, b, j: (jnp.where(s == k, b, park_b), jnp.where(s == k, j + 1, park_j), 0))

    in_specs, args = [], []
    for k, st in enumerate(streams):
        in_specs += [lat(k, D_MODEL), lat(k, PACK_W), lat(k, PACK_W), lat(k, LANES)]
        args += [st["xs"]] + list(st["moe"])
    in_specs.append(pl.BlockSpec((1, 1, N_MOD, D_MODEL), lambda s, b, j: (DEPTH - 1, s * SB + b, 0, 0)))
    return pl.pallas_call(
        _final_kernel,
        grid=(STREAMS, SB, N_LAT_TILES),
        in_specs=in_specs,
        out_specs=pl.BlockSpec((1, TILE, D_MODEL), lambda s, b, j: (s * SB + b, j, 0)),
        out_shape=jax.ShapeDtypeStruct((BATCH, SEQ, D_MODEL), F32),
        scratch_shapes=[pltpu.VMEM((D_MODEL // LANES, TILE, LANES), F32)],
        compiler_params=_cparams(("arbitrary", "arbitrary", "arbitrary")),
        name="final_residual",
    )(*args, mod)


def _bias_kernel(t4_ref, o_ref, scr):
    rows_q = TILE // GRID_W
    lane = lax.broadcasted_iota(jnp.int32, (GROUPS, TILE), 1)
    ka = (lane >> 1) & (rows_q - 1)
    neg = jnp.full((GROUPS, TILE), NEG_INF, F32)
    for a in range(rows_q):
        for qs in range(SUB // rows_q):
            s = (SUB // rows_q) * a + qs
            for kt in range(3):
                src = t4_ref[0, rows_q * kt - a + 3, qs * GROUPS:(qs + 1) * GROUPS, :]
                c = rows_q * kt + ka
                variants = (
                    src if kt >= 1 else neg,
                    jnp.where(jnp.logical_and(c >= a, c <= a + NA_WIN_ROWS - 1), src, neg),
                    src if kt <= 1 else neg,
                )
                for v, val in enumerate(variants):
                    for half in range(TILE // LANES):
                        scr[v, 2 * kt + half, pl.ds(s, GROUPS, stride=SUB), :] = val[:, half * LANES:(half + 1) * LANES]
    for v in range(3):
        o_ref[v, 0] = jnp.concatenate([scr[v, j] for j in range(3 * TILE // LANES)], axis=1)
    o_ref[3, 0] = jnp.full((TILE, 3 * TILE), NEG_INF, F32)


def _bias_tiles(table):
    rows_q = TILE // GRID_W
    n_r0 = 3 * rows_q
    qc = np.arange(GRID_W)[:, None]
    kc = np.arange(GRID_W)[None, :]
    col0 = np.clip(qc - NA_WIN_COLS // 2, 0, GRID_W - NA_WIN_COLS)
    in_win = (kc >= col0) & (kc < col0 + NA_WIN_COLS)
    pad = GRID_W - NA_WIN_COLS
    tp = jnp.pad(table.astype(F32) * LOG2E, ((0, 0), (0, 0), (pad, pad)))
    toeplitz = jnp.stack([tp[:, :, GRID_W - 1 - q:2 * GRID_W - 1 - q] for q in range(GRID_W)], axis=2)
    toeplitz = jnp.where(in_win[None, None], toeplitz, NEG_INF)
    n_h = table.shape[0]
    half = GRID_W // GROUPS
    t4 = jnp.stack([toeplitz[:, k:k + n_r0] for k in range(rows_q)], axis=-1)
    t4 = t4.reshape(n_h, n_r0, GRID_W, half, GROUPS, rows_q).transpose(0, 1, 2, 4, 5, 3)
    t4 = t4.reshape(n_h, n_r0, GRID_W, TILE)
    return pl.pallas_call(
        _bias_kernel,
        grid=(n_h,),
        in_specs=[pl.BlockSpec((1, n_r0, GRID_W, TILE), lambda h: (h, 0, 0, 0))],
        out_specs=pl.BlockSpec((4, 1, TILE, 3 * TILE), lambda h: (0, h, 0, 0)),
        out_shape=jax.ShapeDtypeStruct((4, n_h, TILE, 3 * TILE), F32),
        scratch_shapes=[pltpu.VMEM((3, 3 * TILE // LANES, TILE, LANES), F32)],
        compiler_params=_cparams(("parallel",)),
        name="bias_tiles",
    )(t4)


def _block_diag(w, n_chunks):
    per = LRU_BLOCKS // n_chunks
    w = w.reshape(2, n_chunks, per, LRU_BLOCK, LRU_BLOCK)
    eye = jnp.eye(per, dtype=w.dtype)
    out = jnp.einsum('dcpij,pq->dcpiqj', w, eye)
    return out.reshape(2, n_chunks, per * LRU_BLOCK, per * LRU_BLOCK)


def kernel(x, c, ctx, c_ctx, w_mod, b_mod, norm_mix, norm_ffn, w_in, w_out, q_gain, k_gain, na_bias,
           conv_w, conv_b, lru_w_r, lru_b_r, lru_w_i, lru_b_i, lru_lambda, router_w, router_b,
           exp_w1, exp_w3, exp_w2):
    cs = jnp.concatenate([c, c_ctx[None, :], jnp.zeros((MOD_ROWS - BATCH - 1, D_MODEL), F32)], axis=0)
    mod = _modulation(cs, w_mod, b_mod).reshape(DEPTH, MOD_ROWS, N_MOD, D_MODEL)

    head_of = np.arange(NA_WIDTH) // HEAD_DIM
    bd = jnp.asarray((head_of[:, None] == head_of[None, :]).astype(np.float32) / HEAD_DIM, BF16)
    tri = jnp.asarray(np.triu(np.ones((TILE, TILE), np.float32), 1), BF16)
    rwt = router_w.T
    rwh = rwt.astype(BF16)
    rwc = jnp.concatenate([rwh, (rwt - rwh.astype(F32)).astype(BF16)], axis=0)
    rb = router_b.reshape(N_EXPERTS, 1)
    n_cb = LRU_WIDTH // LRU_CH

    bias_tiles = _bias_tiles(na_bias.reshape(DEPTH * NA_HEADS, 2 * NA_WIN_ROWS - 1, 2 * NA_WIN_COLS - 1))
    streams = [{"xs": (x, ctx), "moe": None, "boff": sidx * SB} for sidx in range(STREAMS)]
    for l in range(DEPTH):
        qg = jnp.tile(q_gain[l] * (ATTN_SCALE * LOG2E), NA_HEADS)[None, :]
        kg = jnp.tile(k_gain[l], NA_HEADS)[None, :]
        w_in_l = w_in[l].astype(BF16)
        w_out_l = w_out[l].astype(BF16)
        wr = (0.5 * _block_diag(lru_w_r[l], n_cb)).astype(BF16)
        wi = (0.5 * _block_diag(lru_w_i[l], n_cb)).astype(BF16)
        for st in streams:
            boff = st["boff"]
            xs, (q, k, v, u, g) = _in_proj(l, boff, st["xs"], st["moe"], mod, norm_mix[l][None, :], w_in_l, bd,
                                           qg, kg)
            ya = _attention(l, q, k, v, bias_tiles)
            yb = _rglru(u, g, conv_w[l], conv_b[l][None, :], wr, wi, 0.5 * lru_b_r[l], 0.5 * lru_b_i[l],
                        lru_lambda[l])
            xs, hp, route = _out_proj(l, boff, xs, ya, yb, mod, norm_ffn[l][None, :], w_out_l, rwc, rb)
            st["xs"] = xs
            st["moe"] = _moe(l, hp, route.reshape(N_TOK_TILES, ROUTE_ROWS, TILE), tri, exp_w1, exp_w3, exp_w2)
    return _final(streams, mod)
```

```python
import functools

import jax
import jax.numpy as jnp
import numpy as np
from jax import lax
from jax.experimental import pallas as pl
from jax.experimental.pallas import tpu as pltpu
from jax.experimental.pallas import tpu_sc as plsc

F32 = jnp.float32
BF16 = jnp.bfloat16
I32 = jnp.int32

D_MODEL = 1024
BATCH = 4
SEQ = 8192
DEPTH = 4
GRID_W = 64
CTX_LEN = 256
HEAD_DIM = 64
NA_WIDTH = 512
NA_HEADS = 8
NA_WIN_ROWS = 8
NA_WIN_COLS = 16
LRU_WIDTH = 512
LRU_BLOCKS = 8
LRU_BLOCK = 64
CONV_W = 4
LRU_C = 8.0
IN_COLS = 3 * NA_WIDTH + 2 * LRU_WIDTH
N_EXPERTS = 16
N_GROUPS = 4
EXPERTS_PER_GROUP = 4
TOP_K = 2
EXPERT_FF = 512
N_MOD = 6
ATTN_SCALE = HEAD_DIM ** -0.5
LOG2E = 1.4426950408889634
EPS = 1e-6
NEG_INF = -1e30
TINY = 1e-30
GELU_C = 0.7978845608028654

TILE = 256
SUB = 8
LANES = 128
GROUPS = TILE // SUB
L_TOT = CTX_LEN + SEQ
N_TILES = L_TOT // TILE
N_LAT_TILES = SEQ // TILE
STREAMS = 2
SB = BATCH // STREAMS
N_TOK = SB * L_TOT
N_TOK_TILES = N_TOK // TILE
PAIR = 2
LRU_CH = 256
MOD_ROWS = 8
VMEM_LIMIT = 56 * 1024 * 1024

PACK_W = D_MODEL // 2
HI_MASK = -65536
N_ASSIGN = TOP_K * N_TOK
EXP_TILE = 512
N_EXP_TILES = N_ASSIGN // EXP_TILE
N_ITEMS = N_EXP_TILES + N_EXPERTS - 1
ROUTE_ROWS = 2 * N_EXPERTS
POS_TILES = 6

SC_CORES = 2
SC_SUBCORES = 16
SC_WORKERS = SC_CORES * SC_SUBCORES
SC_ROWS = N_TOK // SC_WORKERS
SC_CHUNK = 88
SC_N_CHUNKS = SC_ROWS // SC_CHUNK


def _cparams(sem):
    return pltpu.CompilerParams(dimension_semantics=sem, vmem_limit_bytes=VMEM_LIMIT)


def _pack_rows(v):
    lo = pltpu.bitcast(v[:, :PACK_W].astype(BF16).astype(F32), I32)
    hi = pltpu.bitcast(v[:, PACK_W:].astype(BF16).astype(F32), I32)
    return ((lo >> 16) & 0xFFFF) | (hi & HI_MASK)


def _unpack_rows(p):
    return pltpu.bitcast(p << 16, F32), pltpu.bitcast(p & HI_MASK, F32)


def _mod_kernel(c_ref, w_ref, b_ref, o_ref):
    c = c_ref[...]
    s = c * jax.nn.sigmoid(c)
    o_ref[0] = jnp.dot(s.astype(BF16), w_ref[0].astype(BF16), preferred_element_type=F32) + b_ref[0]


def _modulation(cs, w_mod, b_mod):
    return pl.pallas_call(
        _mod_kernel,
        grid=(DEPTH, N_MOD),
        in_specs=[
            pl.BlockSpec((MOD_ROWS, D_MODEL), lambda l, n: (0, 0)),
            pl.BlockSpec((1, D_MODEL, D_MODEL), lambda l, n: (l, 0, n)),
            pl.BlockSpec((1, 1, D_MODEL), lambda l, n: (l, 0, n)),
        ],
        out_specs=pl.BlockSpec((1, MOD_ROWS, D_MODEL), lambda l, n: (l, 0, n)),
        out_shape=jax.ShapeDtypeStruct((DEPTH, MOD_ROWS, N_MOD * D_MODEL), F32),
        compiler_params=_cparams(("arbitrary", "arbitrary")),
        name="modulation",
    )(cs, w_mod, b_mod.reshape(DEPTH, 1, N_MOD * D_MODEL))


def _mod_row(b, i):
    return jnp.where(i == 0, BATCH, b)


def _moe_residual(x, g0, g1, wt, gate_row):
    lo0, hi0 = _unpack_rows(g0)
    lo1, hi1 = _unpack_rows(g1)
    w0 = wt[:, 0:1]
    w1 = wt[:, 1:2]
    f = jnp.concatenate([w0 * lo0 + w1 * lo1, w0 * hi0 + w1 * hi1], axis=1)
    return x + gate_row * f


def _pair_tok_spec(width):
    return pl.BlockSpec((PAIR, TILE, width), lambda bp, i: (bp, i, 0))


def _pair_mod_specs(layer, boff):
    return [pl.BlockSpec((1, 1, N_MOD, D_MODEL),
                         lambda bp, i, k=k: (layer, _mod_row(boff + PAIR * bp + k, i), 0, 0))
            for k in range(PAIR)]


def _full_spec(shape):
    return pl.BlockSpec(shape, lambda bp, i: tuple(0 for _ in shape))


def _to_scan_major(src_ref, scr):
    n_slab = D_MODEL // LANES
    for s in range(SUB):
        for j in range(n_slab):
            scr[j, pl.ds(s, GROUPS, stride=SUB), :] = src_ref[s * GROUPS:(s + 1) * GROUPS, j * LANES:(j + 1) * LANES]
    return jnp.concatenate([scr[j] for j in range(n_slab)], axis=1)


def _from_scan_major(val, scr):
    n_slab = D_MODEL // LANES
    for j in range(n_slab):
        scr[j] = val[:, j * LANES:(j + 1) * LANES]
    blocks = [jnp.concatenate([scr[j, pl.ds(s, GROUPS, stride=SUB), :] for j in range(n_slab)], axis=1)
              for s in range(SUB)]
    return jnp.concatenate(blocks, axis=0)


def _in_kernel(has_prev, *refs):
    if has_prev:
        x_ref, g0_ref, g1_ref, wt_ref = refs[:4]
        mprev_refs = refs[4:4 + PAIR]
        refs = (x_ref,) + refs[4 + PAIR:]
    else:
        x_ref, ctx_ref = refs[:2]
        refs = (x_ref,) + refs[2:]
    mod_refs = refs[1:1 + PAIR]
    nrm_ref, w_ref, bd_ref, qg_ref, kg_ref = refs[1 + PAIR:6 + PAIR]
    outs = refs[6 + PAIR:]
    xo_ref, q_ref, k_ref, v_ref, u_ref, g_ref = outs[:6]
    x_ref = refs[0]
    hs = []
    for k in range(PAIR):
        if has_prev:
            x = _moe_residual(x_ref[k], g0_ref[k], g1_ref[k], wt_ref[k], mprev_refs[k][0, 0][5:6, :])
            xo_ref[k] = x
        else:
            perm_scr = outs[6]

            @pl.when(pl.program_id(1) == 0)
            def _():
                xo_ref[k] = _to_scan_major(ctx_ref.at[k], perm_scr)

            @pl.when(pl.program_id(1) > 0)
            def _():
                xo_ref[k] = _to_scan_major(x_ref.at[k], perm_scr)

            x = xo_ref[k]
        m = mod_refs[k][0, 0]
        ms = jnp.mean(x * x, axis=-1, keepdims=True)
        h = (x * lax.rsqrt(ms + EPS)) * nrm_ref[...]
        hs.append((h * (1.0 + m[1:2, :]) + m[0:1, :]).astype(BF16))
    acc = jnp.dot(jnp.concatenate(hs, axis=0), w_ref[...], preferred_element_type=F32)
    bd = bd_ref[...]

    def head_norm(t, gain):
        ss = jnp.dot((t * t).astype(BF16), bd, preferred_element_type=F32)
        return (t * lax.rsqrt(ss + EPS)) * gain

    def put(ref, val):
        for k in range(PAIR):
            ref[k] = val[k * TILE:(k + 1) * TILE, :].astype(BF16)

    put(q_ref, head_norm(acc[:, 0:NA_WIDTH], qg_ref[...]))
    put(k_ref, head_norm(acc[:, NA_WIDTH:2 * NA_WIDTH], kg_ref[...]))
    put(v_ref, acc[:, 2 * NA_WIDTH:3 * NA_WIDTH])
    put(u_ref, acc[:, 3 * NA_WIDTH:3 * NA_WIDTH + LRU_WIDTH])
    put(g_ref, acc[:, 3 * NA_WIDTH + LRU_WIDTH:])


def _in_proj(l, boff, xs, moe_prev, mod, nrm, w_in, bd, qg, kg):
    has_prev = moe_prev is not None
    if has_prev:
        in_specs = [_pair_tok_spec(D_MODEL)]
        args = [xs]
        in_specs += [_pair_tok_spec(PACK_W), _pair_tok_spec(PACK_W), _pair_tok_spec(LANES)]
        in_specs += _pair_mod_specs(l - 1, boff)
        args += list(moe_prev) + [mod] * PAIR
    else:
        pair0 = boff // PAIR
        in_specs = [pl.BlockSpec((PAIR, TILE, D_MODEL), lambda bp, i: (pair0 + bp, jnp.maximum(i - 1, 0), 0)),
                    pl.BlockSpec((PAIR, TILE, D_MODEL), lambda bp, i: (pair0 + bp, 0, 0))]
        args = list(xs)
    in_specs += _pair_mod_specs(l, boff) + [_full_spec((1, D_MODEL)), _full_spec((D_MODEL, IN_COLS)),
                                            _full_spec((NA_WIDTH, NA_WIDTH)), _full_spec((1, NA_WIDTH)),
                                            _full_spec((1, NA_WIDTH))]
    args += [mod] * PAIR + [nrm, w_in, bd, qg, kg]
    half = jax.ShapeDtypeStruct((SB, L_TOT, NA_WIDTH), BF16)
    out_shape = [jax.ShapeDtypeStruct((SB, L_TOT, D_MODEL), F32)] + [half] * 5
    out_specs = [_pair_tok_spec(D_MODEL)] + [_pair_tok_spec(NA_WIDTH)] * 5
    scratch = [] if has_prev else [pltpu.VMEM((D_MODEL // LANES, TILE, LANES), F32)]
    outs = pl.pallas_call(
        functools.partial(_in_kernel, has_prev),
        grid=(SB // PAIR, N_TILES),
        in_specs=in_specs,
        out_specs=out_specs,
        out_shape=out_shape,
        scratch_shapes=scratch,
        compiler_params=_cparams(("parallel", "arbitrary")),
        name="in_proj",
    )(*args)
    return outs[0], outs[1:]


def _attn_kernel(q_ref, kp_ref, kc_ref, kn_ref, kx_ref, vp_ref, vc_ref, vn_ref, vx_ref, bias_ref,
                 u_ref, up_ref, un_ref, cw_ref, cb_ref, wr_ref, wi_ref, br_ref, bi_ref, lam_ref,
                 o_ref, hf_ref, cv_ref, h_scr, hl_scr, p_scr):
    j = pl.program_id(1)

    @pl.when(j == 0)
    def _():
        h_scr[...] = jnp.zeros_like(h_scr)

    has_prev = jnp.where(j >= 2, 1.0, 0.0).astype(F32)
    has_next = jnp.where(jnp.logical_and(j >= 1, j <= N_TILES - 2), 1.0, 0.0).astype(F32)
    cv = _lru_conv(u_ref[0].astype(F32), up_ref[0].astype(F32), un_ref[0].astype(F32), has_prev, has_next,
                   cw_ref[...], cb_ref[...])
    cv_ref[0] = cv
    a_f, b_f = _lru_gates(cv, 0, wr_ref, wi_ref, br_ref, bi_ref, lam_ref)
    hfull, h_last = _lru_scan(a_f, b_f, h_scr[...], False, hl_scr, p_scr)
    hf_ref[0] = hfull
    h_scr[...] = h_last

    pair_w = 2 * HEAD_DIM
    lane = lax.broadcasted_iota(jnp.int32, (TILE, pair_w), 1)
    low = lane < HEAD_DIM
    k_refs = (kp_ref, kc_ref, kn_ref, kx_ref)
    v_refs = (vp_ref, vc_ref, vn_ref, vx_ref)
    nt = (((1,), (1,)), ((), ()))
    n_win = 3 * TILE
    low_kv = lax.broadcasted_iota(jnp.int32, (4 * TILE, pair_w), 1) < HEAD_DIM
    for hp in range(NA_HEADS // 2):
        cols = slice(hp * pair_w, (hp + 1) * pair_w)
        q = q_ref[0, :, cols]
        kb = jnp.concatenate([r[0, :, cols] for r in k_refs], axis=0)
        vb = jnp.concatenate([r[0, :, cols] for r in v_refs], axis=0)
        outs = []
        for hh in range(2):
            own = low if hh == 0 else jnp.logical_not(low)
            own_kv = low_kv if hh == 0 else jnp.logical_not(low_kv)
            qh = jnp.where(own, q, jnp.zeros_like(q))
            s = lax.dot_general(qh, kb, nt, preferred_element_type=F32)
            s_win = s[:, 0:n_win] + bias_ref[0, 2 * hp + hh]
            s_ctx = s[:, n_win:]
            m = jnp.maximum(jnp.max(s_win, axis=-1, keepdims=True), jnp.max(s_ctx, axis=-1, keepdims=True))
            p = jnp.concatenate([jnp.exp2((s_win - m).astype(BF16)), jnp.exp2((s_ctx - m).astype(BF16))], axis=1)
            va = jnp.where(own_kv, vb, jnp.ones_like(vb))
            o = jnp.dot(p, va, preferred_element_type=F32)
            outs.append(o / pltpu.roll(o, HEAD_DIM, 1))
        o_ref[0, :, cols] = jnp.where(low, outs[0], outs[1]).astype(BF16)


def _attention(l, q, k, v, bias_tiles, u, lru):
    last = N_LAT_TILES - 1
    blk = (1, TILE, NA_WIDTH)
    prev_map = lambda b, j: (b, 1 + jnp.clip(j - 2, 0, last), 0)
    cur_map = lambda b, j: (b, jnp.maximum(j, 1), 0)
    next_map = lambda b, j: (b, 1 + jnp.clip(j, 0, last), 0)
    ctx_map = lambda b, j: (b, 0, 0)
    var_map = lambda b, j: (jnp.where(j == 0, 3, jnp.where(j == 1, 0, jnp.where(j == N_TILES - 1, 2, 1))), l, 0, 0)
    kv_specs = [pl.BlockSpec(blk, prev_map), pl.BlockSpec(blk, cur_map), pl.BlockSpec(blk, next_map),
                pl.BlockSpec(blk, ctx_map)]
    tok = pl.BlockSpec(blk, lambda b, j: (b, j, 0))
    halo = TILE // 16
    n_halo = L_TOT // 16
    prev16 = pl.BlockSpec((1, 16, LRU_WIDTH), lambda b, j: (b, jnp.maximum(j * halo - 1, 0), 0))
    next16 = pl.BlockSpec((1, 16, LRU_WIDTH), lambda b, j: (b, jnp.minimum((j + 1) * halo, n_halo - 1), 0))
    whole = lambda shape: pl.BlockSpec(shape, lambda b, j: tuple(0 for _ in shape))
    n_cb = LRU_WIDTH // LRU_CH
    lru_specs = [whole((CONV_W, LRU_WIDTH)), whole((1, LRU_WIDTH)), whole((2, n_cb, LRU_CH, LRU_CH)),
                 whole((2, n_cb, LRU_CH, LRU_CH)), whole((2, LRU_WIDTH)), whole((2, LRU_WIDTH)), whole((2, LRU_WIDTH))]
    state = jax.ShapeDtypeStruct((SB, L_TOT, LRU_WIDTH), F32)
    return pl.pallas_call(
        _attn_kernel,
        grid=(SB, N_TILES),
        in_specs=[tok] + kv_specs + kv_specs + [pl.BlockSpec((1, NA_HEADS, TILE, 3 * TILE), var_map)]
        + [tok, prev16, next16] + lru_specs,
        out_specs=[tok, tok, tok],
        out_shape=[jax.ShapeDtypeStruct((SB, L_TOT, NA_WIDTH), BF16), state, state],
        scratch_shapes=[pltpu.VMEM((1, LRU_WIDTH), F32), pltpu.VMEM((TILE, LRU_WIDTH), F32),
                        pltpu.VMEM((TILE, LRU_WIDTH), F32)],
        compiler_params=_cparams(("parallel", "arbitrary")),
        name="na_attention",
    )(q, k, k, k, k, v, v, v, v, bias_tiles, u, u, u, *lru)


def _softplus(x):
    return jnp.maximum(x, 0.0) + jnp.log1p(jnp.exp(-jnp.abs(x)))


def _lru_conv(u, prev16, next16, has_prev, has_next, cw, cb):
    sub = lax.broadcasted_iota(jnp.int32, (SUB, u.shape[1]), 0)
    prow = prev16[15:16, :] * has_prev
    n0 = next16[0:1, :] * has_next
    n8 = next16[8:9, :] * has_next
    first8 = jnp.where(sub == 0, prow, pltpu.roll(u[TILE - SUB:TILE, :], 1, 0))
    last_a = jnp.where(sub == SUB - 1, n0, pltpu.roll(u[0:SUB, :], SUB - 1, 0))
    last_b = jnp.where(sub == SUB - 1, n8, pltpu.roll(u[SUB:2 * SUB, :], SUB - 1, 0))
    um1 = jnp.concatenate([first8, u[0:TILE - SUB, :]], axis=0)
    up1 = jnp.concatenate([u[SUB:TILE, :], last_a], axis=0)
    up2 = jnp.concatenate([u[2 * SUB:TILE, :], last_a, last_b], axis=0)
    return cw[0:1, :] * um1 + cw[1:2, :] * u + cw[2:3, :] * up1 + cw[3:4, :] * up2 + cb


def _lru_gates(v, d, wr_ref, wi_ref, br_ref, bi_ref, lam_ref):
    vb = v.astype(BF16)
    n_cb = LRU_WIDTH // LRU_CH

    def gate(w_ref, b_ref):
        z = [jnp.dot(vb[:, c * LRU_CH:(c + 1) * LRU_CH], w_ref[d, c], preferred_element_type=F32) for c in range(n_cb)]
        return jnp.tanh(jnp.concatenate(z, axis=1) + b_ref[d:d + 1, :])

    tr = gate(wr_ref, br_ref)
    ti = gate(wi_ref, bi_ref)
    half = (-0.5 * LRU_C * LOG2E) * _softplus(-lam_ref[d:d + 1, :])
    a = jnp.exp2(half + half * tr)
    om = 1.0 - a * a
    b = (om * lax.rsqrt(jnp.maximum(om, TINY))) * ((0.5 * v) * (1.0 + ti))
    return a, b


def _lru_scan(a, b, h_in, reverse, hl_scr, p_scr):
    order = range(GROUPS - 1, -1, -1) if reverse else range(GROUPS)
    hl = None
    for g in order:
        ag = a[g * SUB:(g + 1) * SUB, :]
        bg = b[g * SUB:(g + 1) * SUB, :]
        if hl is None:
            hl, p = bg, ag
        else:
            hl = ag * hl + bg
            p = ag * p
        hl_scr[g * SUB:(g + 1) * SUB, :] = hl
        p_scr[g * SUB:(g + 1) * SUB, :] = p
    blocks = range(SUB - 1, -1, -1) if reverse else range(SUB)
    carry = h_in
    cins = {}
    for s in blocks:
        cins[s] = carry
        carry = hl[s:s + 1, :] + p[s:s + 1, :] * carry
    cin = jnp.concatenate([cins[s] for s in range(SUB)], axis=0)
    hfull = hl_scr[...] + p_scr[...] * jnp.tile(cin, (GROUPS, 1))
    return hfull, carry


def _route(sel, aff):
    def top2_sum(a, b, c, d):
        hi1, lo1 = jnp.maximum(a, b), jnp.minimum(a, b)
        hi2, lo2 = jnp.maximum(c, d), jnp.minimum(c, d)
        return jnp.maximum(hi1, hi2) + jnp.maximum(jnp.minimum(hi1, hi2), jnp.maximum(lo1, lo2))

    scores = [top2_sum(*sel[EXPERTS_PER_GROUP * g:EXPERTS_PER_GROUP * (g + 1)]) for g in range(N_GROUPS)]
    best = jnp.zeros_like(scores[0], dtype=jnp.int32)
    best_v = scores[0]
    for g in range(1, N_GROUPS):
        upd = scores[g] > best_v
        best = jnp.where(upd, g, best)
        best_v = jnp.where(upd, scores[g], best_v)
    chosen = []
    for e in range(N_EXPERTS):
        g = e // EXPERTS_PER_GROUP
        rank = jnp.zeros_like(best)
        for o in range(EXPERTS_PER_GROUP * g, EXPERTS_PER_GROUP * (g + 1)):
            if o == e:
                continue
            ahead = sel[o] > sel[e]
            if o < e:
                ahead = jnp.logical_or(ahead, sel[o] == sel[e])
            rank = rank + ahead.astype(jnp.int32)
        chosen.append(jnp.logical_and(best == g, rank < TOP_K))
    total = jnp.zeros_like(aff[0])
    for e in range(N_EXPERTS):
        total = total + jnp.where(chosen[e], aff[e], 0.0)
    gates = [jnp.where(chosen[e], aff[e] / total, 0.0) for e in range(N_EXPERTS)]
    return gates, [c.astype(F32) for c in chosen]


def _out_kernel(x_ref, ya_ref, hf_ref, cv_ref, g_ref, *refs):
    mod_refs = refs[:PAIR]
    (wr_ref, wi_ref, br_ref, bi_ref, lam_ref, nrm_ref, w_ref, rwc_ref, rb_ref,
     xo_ref, hp_ref, rt_ref, h_scr, hl_scr, p_scr) = refs[PAIR:]
    rows = PAIR * TILE

    @pl.when(pl.program_id(1) == 0)
    def _():
        h_scr[...] = jnp.zeros_like(h_scr)

    ybs = []
    for k in range(PAIR):
        a_r, b_r = _lru_gates(cv_ref[k], 1, wr_ref, wi_ref, br_ref, bi_ref, lam_ref)
        hrev, h_last = _lru_scan(a_r, b_r, h_scr[k], True, hl_scr, p_scr)
        h_scr[k] = h_last
        gx = g_ref[k].astype(F32)
        gate = (0.5 * gx) * (1.0 + jnp.tanh(gx * (GELU_C + (GELU_C * 0.044715) * (gx * gx))))
        ybs.append((gate * (hf_ref[k] + hrev)).astype(BF16))
    ya = ya_ref[...].reshape(rows, NA_WIDTH)
    yb = jnp.concatenate(ybs, axis=0)
    y = jnp.dot(ya, w_ref[0:NA_WIDTH, :], preferred_element_type=F32)
    y = y + jnp.dot(yb, w_ref[NA_WIDTH:, :], preferred_element_type=F32)
    hs = []
    for k in range(PAIR):
        m = mod_refs[k][0, 0]
        x = x_ref[k] + m[2:3, :] * y[k * TILE:(k + 1) * TILE, :]
        xo_ref[k] = x
        ms = jnp.mean(x * x, axis=-1, keepdims=True)
        h = (x * lax.rsqrt(ms + EPS)) * nrm_ref[...]
        h = h * (1.0 + m[4:5, :]) + m[3:4, :]
        hp_ref[k] = _pack_rows(h)
        hs.append(h)
    h = jnp.concatenate(hs, axis=0)
    h_hi = h.astype(BF16)
    h_lo = (h - h_hi.astype(F32)).astype(BF16)
    nt = (((1,), (1,)), ((), ()))
    rwc = rwc_ref[...]
    both = lax.dot_general(rwc, h_hi, nt, preferred_element_type=F32)
    lg = (both[0:N_EXPERTS, :] + both[N_EXPERTS:, :]
          + lax.dot_general(rwc[0:N_EXPERTS, :], h_lo, nt, preferred_element_type=F32))
    aff_all = jax.nn.sigmoid(lg)
    sel_all = aff_all + rb_ref[...]
    aff = [aff_all[e:e + 1, :] for e in range(N_EXPERTS)]
    sel = [sel_all[e:e + 1, :] for e in range(N_EXPERTS)]
    gates, chosen = _route(sel, aff)
    rt = jnp.concatenate(gates + chosen, axis=0)
    for k in range(PAIR):
        rt_ref[k, 0] = rt[:, k * TILE:(k + 1) * TILE]


def _out_proj(l, boff, xs, ya, hf, cv, g, lru, mod, nrm, w_out, rwc, rb):
    rev = lambda i: jnp.where(i == 0, 0, N_TILES - i)
    tok = lambda width: pl.BlockSpec((PAIR, TILE, width), lambda bp, i: (bp, rev(i), 0))
    mod_specs = [pl.BlockSpec((1, 1, N_MOD, D_MODEL),
                              lambda bp, i, k=k: (l, _mod_row(boff + PAIR * bp + k, i), 0, 0)) for k in range(PAIR)]
    n_cb = LRU_WIDTH // LRU_CH
    lru_specs = [_full_spec((2, n_cb, LRU_CH, LRU_CH)), _full_spec((2, n_cb, LRU_CH, LRU_CH)),
                 _full_spec((2, LRU_WIDTH)), _full_spec((2, LRU_WIDTH)), _full_spec((2, LRU_WIDTH))]
    return pl.pallas_call(
        _out_kernel,
        grid=(SB // PAIR, N_TILES),
        in_specs=[tok(D_MODEL), tok(NA_WIDTH), tok(LRU_WIDTH), tok(LRU_WIDTH), tok(LRU_WIDTH)]
        + mod_specs + lru_specs
        + [_full_spec((1, D_MODEL)), _full_spec((D_MODEL, D_MODEL)), _full_spec((2 * N_EXPERTS, D_MODEL)),
           _full_spec((N_EXPERTS, 1))],
        out_specs=[tok(D_MODEL), tok(PACK_W),
                   pl.BlockSpec((PAIR, 1, ROUTE_ROWS, TILE), lambda bp, i: (bp, rev(i), 0, 0))],
        out_shape=[jax.ShapeDtypeStruct((SB, L_TOT, D_MODEL), F32),
                   jax.ShapeDtypeStruct((SB, L_TOT, PACK_W), I32),
                   jax.ShapeDtypeStruct((SB, N_TILES, ROUTE_ROWS, TILE), F32)],
        scratch_shapes=[pltpu.VMEM((PAIR, 1, LRU_WIDTH), F32), pltpu.VMEM((TILE, LRU_WIDTH), F32),
                        pltpu.VMEM((TILE, LRU_WIDTH), F32)],
        compiler_params=_cparams(("parallel", "arbitrary")),
        name="out_proj_router",
    )(xs, ya, hf, cv, g, *([mod] * PAIR), *lru[2:], nrm, w_out, rwc, rb)


def _pos_kernel(rt_ref, tri_ref, pos_ref, wt_ref, cnt_ref, run_scr, start_scr):
    phase = pl.program_id(0)
    i = pl.program_id(1)

    @pl.when(jnp.logical_and(phase == 0, i == 0))
    def _():
        run_scr[...] = jnp.zeros_like(run_scr)

    @pl.when(jnp.logical_and(phase == 1, i == 0))
    def _():
        tot = run_scr[...]
        cnt_ref[...] = jnp.broadcast_to(tot, (N_EXPERTS, LANES))
        acc = jnp.zeros((1, 1), F32)
        rows = []
        for e in range(N_EXPERTS):
            rows.append(acc)
            acc = acc + tot[e:e + 1, :]
        start_scr[...] = jnp.concatenate(rows, axis=0)
        run_scr[...] = jnp.zeros_like(run_scr)

    @pl.when(phase == 0)
    def _():
        tot = run_scr[...]
        for k in range(POS_TILES):
            tot = tot + jnp.sum(rt_ref[k, N_EXPERTS:, :], axis=1, keepdims=True)
        run_scr[...] = tot

    @pl.when(phase == 1)
    def _():
        base = start_scr[...] + run_scr[...]
        for k in range(POS_TILES):
            gates = rt_ref[k, 0:N_EXPERTS, :]
            chosen = rt_ref[k, N_EXPERTS:, :]
            rank = jnp.dot(chosen.astype(BF16), tri_ref[...], preferred_element_type=F32)
            posf = rank + base
            seen = jnp.zeros((1, TILE), F32)
            p0 = jnp.zeros((1, TILE), F32)
            p1 = jnp.zeros((1, TILE), F32)
            w0 = jnp.zeros((1, TILE), F32)
            w1 = jnp.zeros((1, TILE), F32)
            for e in range(N_EXPERTS):
                ch = chosen[e:e + 1, :]
                first = ch * (1.0 - seen)
                second = ch * seen
                p0 = p0 + first * posf[e:e + 1, :]
                p1 = p1 + second * posf[e:e + 1, :]
                w0 = w0 + first * gates[e:e + 1, :]
                w1 = w1 + second * gates[e:e + 1, :]
                seen = jnp.minimum(seen + ch, 1.0)
            pos_ref[k] = jnp.concatenate([p0, p1], axis=0).astype(I32)
            wpad = jnp.concatenate([w0, w1, jnp.zeros((LANES - TOP_K, TILE), F32)], axis=0)
            wt_ref[k * TILE:(k + 1) * TILE, :] = jnp.transpose(wpad)
            base = base + jnp.sum(chosen, axis=1, keepdims=True)
        run_scr[...] = base - start_scr[...]


def _positions(route, tri):
    return pl.pallas_call(
        _pos_kernel,
        grid=(2, N_TOK_TILES // POS_TILES),
        in_specs=[pl.BlockSpec((POS_TILES, ROUTE_ROWS, TILE), lambda p, i: (i, 0, 0)),
                  pl.BlockSpec((TILE, TILE), lambda p, i: (0, 0))],
        out_specs=[pl.BlockSpec((POS_TILES, TOP_K, TILE), lambda p, i: (i * p, 0, 0)),
                   pl.BlockSpec((POS_TILES * TILE, LANES), lambda p, i: (i * p, 0)),
                   pl.BlockSpec((N_EXPERTS, LANES), lambda p, i: (0, 0))],
        out_shape=[jax.ShapeDtypeStruct((N_TOK_TILES, TOP_K, TILE), I32),
                   jax.ShapeDtypeStruct((N_TOK, LANES), F32),
                   jax.ShapeDtypeStruct((N_EXPERTS, LANES), F32)],
        scratch_shapes=[pltpu.VMEM((N_EXPERTS, 1), F32), pltpu.VMEM((N_EXPERTS, 1), F32)],
        compiler_params=_cparams(("arbitrary", "arbitrary")),
        name="moe_positions",
    )(route, tri)


def _work_items(counts):
    smem = pl.BlockSpec(memory_space=pltpu.SMEM)
    item = jax.ShapeDtypeStruct((N_ITEMS,), I32)
    return pl.pallas_call(
        _items_kernel,
        in_specs=[smem],
        out_specs=[smem] * 4,
        out_shape=[item] * 4,
        name="moe_work_items",
    )(counts)


def _items_kernel(cnt_ref, tile_ref, exp_ref, lo_ref, hi_ref):
    n = jnp.int32(0)
    start = jnp.int32(0)
    last_e = jnp.int32(0)
    for e in range(N_EXPERTS):
        cnt = cnt_ref[e]
        end = start + cnt
        first = start // EXP_TILE
        n_tiles = jnp.where(cnt > 0, (end - 1) // EXP_TILE - first + 1, 0)

        def put(j, carry, e=e, n=n, start=start, end=end, first=first):
            tile = first + j
            tile_ref[n + j] = tile
            exp_ref[n + j] = jnp.int32(e)
            lo_ref[n + j] = jnp.maximum(start - tile * EXP_TILE, 0)
            hi_ref[n + j] = jnp.minimum(end - tile * EXP_TILE, EXP_TILE)
            return carry

        lax.fori_loop(0, n_tiles, put, 0)
        n = n + n_tiles
        start = end
        last_e = jnp.where(cnt > 0, e, last_e)

    def pad(j, carry):
        tile_ref[j] = jnp.int32(N_EXP_TILES - 1)
        exp_ref[j] = last_e
        lo_ref[j] = jnp.int32(0)
        hi_ref[j] = jnp.int32(0)
        return carry

    lax.fori_loop(n, N_ITEMS, pad, 0)


def _sc_worker_base():
    return (lax.axis_index("s") * SC_CORES + lax.axis_index("c")) * SC_ROWS


def _sc_dispatch_body(h_hbm, p0_hbm, p1_hbm, out_hbm, i0_v, i1_v, rows_v, sem):
    base = _sc_worker_base()

    @pl.loop(0, SC_N_CHUNKS)
    def _(j):
        off = base + j * SC_CHUNK
        pltpu.sync_copy(h_hbm.at[pl.ds(off, SC_CHUNK)], rows_v)
        pltpu.sync_copy(p0_hbm.at[pl.ds(off, SC_CHUNK)], i0_v)
        pltpu.sync_copy(p1_hbm.at[pl.ds(off, SC_CHUNK)], i1_v)
        c0 = pltpu.async_copy(rows_v, out_hbm.at[i0_v], sem)
        c1 = pltpu.async_copy(rows_v, out_hbm.at[i1_v], sem)
        c0.wait()
        c1.wait()


def _sc_combine_body(y_hbm, p0_hbm, p1_hbm, g0_hbm, g1_hbm, i_v, rows_v, sem):
    base = _sc_worker_base()

    @pl.loop(0, SC_N_CHUNKS)
    def _(j):
        off = base + j * SC_CHUNK
        for p_hbm, g_hbm in ((p0_hbm, g0_hbm), (p1_hbm, g1_hbm)):
            pltpu.sync_copy(p_hbm.at[pl.ds(off, SC_CHUNK)], i_v)
            pltpu.async_copy(y_hbm.at[i_v], rows_v, sem).wait()
            pltpu.sync_copy(rows_v, g_hbm.at[pl.ds(off, SC_CHUNK)])


def _sc_mesh():
    return plsc.VectorSubcoreMesh(core_axis_name="c", subcore_axis_name="s")


def _dispatch(hp, pos0, pos1):
    return pl.kernel(
        _sc_dispatch_body,
        out_type=jax.ShapeDtypeStruct((N_ASSIGN, PACK_W), I32),
        mesh=_sc_mesh(),
        scratch_types=[pltpu.VMEM((SC_CHUNK,), I32), pltpu.VMEM((SC_CHUNK,), I32),
                       pltpu.VMEM((SC_CHUNK, PACK_W), I32), pltpu.SemaphoreType.DMA],
        name="moe_dispatch",
    )(hp, pos0, pos1)


def _combine(ys, pos0, pos1):
    row = jax.ShapeDtypeStruct((N_TOK, PACK_W), I32)
    return pl.kernel(
        _sc_combine_body,
        out_type=[row, row],
        mesh=_sc_mesh(),
        scratch_types=[pltpu.VMEM((SC_CHUNK,), I32), pltpu.VMEM((SC_CHUNK, PACK_W), I32),
                       pltpu.SemaphoreType.DMA],
        name="moe_combine",
    )(ys, pos0, pos1)


def _expert_kernel(tile_ref, exp_ref, lo_ref, hi_ref, x_ref, w1_ref, w3_ref, w2_ref, o_ref, w13_scr, w2_scr):
    i = pl.program_id(0)
    prev = jnp.maximum(i - 1, 0)

    @pl.when(jnp.logical_or(i == 0, exp_ref[i] != exp_ref[prev]))
    def _():
        w13_scr[:, 0:EXPERT_FF] = w1_ref[0, 0].astype(BF16)
        w13_scr[:, EXPERT_FF:] = w3_ref[0, 0].astype(BF16)
        w2_scr[...] = w2_ref[0, 0].astype(BF16)

    lo_f, hi_f = _unpack_rows(x_ref[...])
    h = jnp.concatenate([lo_f.astype(BF16), hi_f.astype(BF16)], axis=1)
    ab = jnp.dot(h, w13_scr[...], preferred_element_type=F32)
    a = ab[:, 0:EXPERT_FF]
    b = ab[:, EXPERT_FF:]
    t = ((a * jax.nn.sigmoid(a)) * b).astype(BF16)
    y = _pack_rows(jnp.dot(t, w2_scr[...], preferred_element_type=F32))
    row = lax.broadcasted_iota(jnp.int32, (EXP_TILE, PACK_W), 0)
    mine = jnp.logical_and(row >= lo_ref[i], row < hi_ref[i])
    revisit = jnp.logical_and(i > 0, tile_ref[i] == tile_ref[prev])

    @pl.when(jnp.logical_not(revisit))
    def _():
        o_ref[...] = jnp.where(mine, y, 0)

    @pl.when(revisit)
    def _():
        o_ref[...] = jnp.where(mine, y, o_ref[...])


def _experts(l, xs_sorted, items, w1, w3, w2):
    tile, e, lo, hi = items
    grid_spec = pltpu.PrefetchScalarGridSpec(
        num_scalar_prefetch=4,
        grid=(N_ITEMS,),
        in_specs=[pl.BlockSpec((EXP_TILE, PACK_W), lambda i, t, e, lo, hi: (t[i], 0)),
                  pl.BlockSpec((1, 1, D_MODEL, EXPERT_FF), lambda i, t, e, lo, hi: (l, e[i], 0, 0)),
                  pl.BlockSpec((1, 1, D_MODEL, EXPERT_FF), lambda i, t, e, lo, hi: (l, e[i], 0, 0)),
                  pl.BlockSpec((1, 1, EXPERT_FF, D_MODEL), lambda i, t, e, lo, hi: (l, e[i], 0, 0))],
        out_specs=pl.BlockSpec((EXP_TILE, PACK_W), lambda i, t, e, lo, hi: (t[i], 0)),
        scratch_shapes=[pltpu.VMEM((D_MODEL, 2 * EXPERT_FF), BF16), pltpu.VMEM((EXPERT_FF, D_MODEL), BF16)],
    )
    return pl.pallas_call(
        _expert_kernel,
        grid_spec=grid_spec,
        out_shape=jax.ShapeDtypeStruct((N_ASSIGN, PACK_W), I32),
        compiler_params=_cparams(("arbitrary",)),
        name="moe_experts",
    )(tile, e, lo, hi, xs_sorted, w1, w3, w2)


def _moe(l, hp, route, tri, w1, w3, w2):
    pos, wt, cnt = _positions(route, tri)
    pos0 = pos[:, 0, :].reshape(N_TOK)
    pos1 = pos[:, 1, :].reshape(N_TOK)
    items = _work_items(cnt[:, 0].astype(I32))
    xs_sorted = _dispatch(hp.reshape(N_TOK, PACK_W), pos0, pos1)
    ys = _experts(l, xs_sorted, items, w1, w3, w2)
    g0, g1 = _combine(ys, pos0, pos1)
    shape = (SB, L_TOT, PACK_W)
    return g0.reshape(shape), g1.reshape(shape), wt.reshape(SB, L_TOT, LANES)


def _final_kernel(*refs):
    n_in = 4
    stream_refs = [refs[k * n_in:(k + 1) * n_in] for k in range(STREAMS)]
    mod_ref, o_ref, scr = refs[STREAMS * n_in:]
    for k in range(STREAMS):
        @pl.when(pl.program_id(0) == k)
        def _(k=k):
            x_ref, g0_ref, g1_ref, wt_ref = stream_refs[k]
            x = _moe_residual(x_ref[0], g0_ref[0], g1_ref[0], wt_ref[0], mod_ref[0, 0][5:6, :])
            o_ref[0] = _from_scan_major(x, scr)


def _final(streams, mod):
    def lat(k, width):
        park_b, park_j = (SB - 1, N_TILES - 1) if k == 0 else (0, 1)
        return pl.BlockSpec((1, TILE, width),
                            lambda s, b, j: (jnp.where(s == k, b, park_b), jnp.where(s == k, j + 1, park_j), 0))

    in_specs, args = [], []
    for k, st in enumerate(streams):
        in_specs += [lat(k, D_MODEL), lat(k, PACK_W), lat(k, PACK_W), lat(k, LANES)]
        args += [st["xs"]] + list(st["moe"])
    in_specs.append(pl.BlockSpec((1, 1, N_MOD, D_MODEL), lambda s, b, j: (DEPTH - 1, s * SB + b, 0, 0)))
    return pl.pallas_call(
        _final_kernel,
        grid=(STREAMS, SB, N_LAT_TILES),
        in_specs=in_specs,
        out_specs=pl.BlockSpec((1, TILE, D_MODEL), lambda s, b, j: (s * SB + b, j, 0)),
        out_shape=jax.ShapeDtypeStruct((BATCH, SEQ, D_MODEL), F32),
        scratch_shapes=[pltpu.VMEM((D_MODEL // LANES, TILE, LANES), F32)],
        compiler_params=_cparams(("arbitrary", "arbitrary", "arbitrary")),
        name="final_residual",
    )(*args, mod)


def _bias_kernel(t4_ref, o_ref, scr):
    rows_q = TILE // GRID_W
    lane = lax.broadcasted_iota(jnp.int32, (GROUPS, TILE), 1)
    ka = (lane >> 1) & (rows_q - 1)
    neg = jnp.full((GROUPS, TILE), NEG_INF, F32)
    for a in range(rows_q):
        for qs in range(SUB // rows_q):
            s = (SUB // rows_q) * a + qs
            for kt in range(3):
                src = t4_ref[0, rows_q * kt - a + 3, qs * GROUPS:(qs + 1) * GROUPS, :]
                c = rows_q * kt + ka
                variants = (
                    src if kt >= 1 else neg,
                    jnp.where(jnp.logical_and(c >= a, c <= a + NA_WIN_ROWS - 1), src, neg),
                    src if kt <= 1 else neg,
                )
                for v, val in enumerate(variants):
                    for half in range(TILE // LANES):
                        scr[v, 2 * kt + half, pl.ds(s, GROUPS, stride=SUB), :] = val[:, half * LANES:(half + 1) * LANES]
    for v in range(3):
        o_ref[v, 0] = jnp.concatenate([scr[v, j] for j in range(3 * TILE // LANES)], axis=1)
    o_ref[3, 0] = jnp.full((TILE, 3 * TILE), NEG_INF, F32)


def _bias_tiles(table):
    rows_q = TILE // GRID_W
    n_r0 = 3 * rows_q
    qc = np.arange(GRID_W)[:, None]
    kc = np.arange(GRID_W)[None, :]
    col0 = np.clip(qc - NA_WIN_COLS // 2, 0, GRID_W - NA_WIN_COLS)
    in_win = (kc >= col0) & (kc < col0 + NA_WIN_COLS)
    pad = GRID_W - NA_WIN_COLS
    tp = jnp.pad(table.astype(F32) * LOG2E, ((0, 0), (0, 0), (pad, pad)))
    toeplitz = jnp.stack([tp[:, :, GRID_W - 1 - q:2 * GRID_W - 1 - q] for q in range(GRID_W)], axis=2)
    toeplitz = jnp.where(in_win[None, None], toeplitz, NEG_INF)
    n_h = table.shape[0]
    half = GRID_W // GROUPS
    t4 = jnp.stack([toeplitz[:, k:k + n_r0] for k in range(rows_q)], axis=-1)
    t4 = t4.reshape(n_h, n_r0, GRID_W, half, GROUPS, rows_q).transpose(0, 1, 2, 4, 5, 3)
    t4 = t4.reshape(n_h, n_r0, GRID_W, TILE)
    return pl.pallas_call(
        _bias_kernel,
        grid=(n_h,),
        in_specs=[pl.BlockSpec((1, n_r0, GRID_W, TILE), lambda h: (h, 0, 0, 0))],
        out_specs=pl.BlockSpec((4, 1, TILE, 3 * TILE), lambda h: (0, h, 0, 0)),
        out_shape=jax.ShapeDtypeStruct((4, n_h, TILE, 3 * TILE), F32),
        scratch_shapes=[pltpu.VMEM((3, 3 * TILE // LANES, TILE, LANES), F32)],
        compiler_params=_cparams(("parallel",)),
        name="bias_tiles",
    )(t4)


def _block_diag(w, n_chunks):
    per = LRU_BLOCKS // n_chunks
    w = w.reshape(2, n_chunks, per, LRU_BLOCK, LRU_BLOCK)
    eye = jnp.eye(per, dtype=w.dtype)
    out = jnp.einsum('dcpij,pq->dcpiqj', w, eye)
    return out.reshape(2, n_chunks, per * LRU_BLOCK, per * LRU_BLOCK)


def kernel(x, c, ctx, c_ctx, w_mod, b_mod, norm_mix, norm_ffn, w_in, w_out, q_gain, k_gain, na_bias,
           conv_w, conv_b, lru_w_r, lru_b_r, lru_w_i, lru_b_i, lru_lambda, router_w, router_b,
           exp_w1, exp_w3, exp_w2):
    cs = jnp.concatenate([c, c_ctx[None, :], jnp.zeros((MOD_ROWS - BATCH - 1, D_MODEL), F32)], axis=0)
    mod = _modulation(cs, w_mod, b_mod).reshape(DEPTH, MOD_ROWS, N_MOD, D_MODEL)

    head_of = np.arange(NA_WIDTH) // HEAD_DIM
    bd = jnp.asarray((head_of[:, None] == head_of[None, :]).astype(np.float32) / HEAD_DIM, BF16)
    tri = jnp.asarray(np.triu(np.ones((TILE, TILE), np.float32), 1), BF16)
    rwt = router_w.T
    rwh = rwt.astype(BF16)
    rwc = jnp.concatenate([rwh, (rwt - rwh.astype(F32)).astype(BF16)], axis=0)
    rb = router_b.reshape(N_EXPERTS, 1)
    n_cb = LRU_WIDTH // LRU_CH

    bias_tiles = _bias_tiles(na_bias.reshape(DEPTH * NA_HEADS, 2 * NA_WIN_ROWS - 1, 2 * NA_WIN_COLS - 1))
    streams = [{"xs": (x, ctx), "moe": None, "boff": sidx * SB} for sidx in range(STREAMS)]
    for l in range(DEPTH):
        qg = jnp.tile(q_gain[l] * (ATTN_SCALE * LOG2E), NA_HEADS)[None, :]
        kg = jnp.tile(k_gain[l], NA_HEADS)[None, :]
        w_in_l = w_in[l].astype(BF16)
        w_out_l = w_out[l].astype(BF16)
        wr = (0.5 * _block_diag(lru_w_r[l], n_cb)).astype(BF16)
        wi = (0.5 * _block_diag(lru_w_i[l], n_cb)).astype(BF16)
        lru = (conv_w[l], conv_b[l][None, :], wr, wi, 0.5 * lru_b_r[l], 0.5 * lru_b_i[l], lru_lambda[l])
        for st in streams:
            boff = st["boff"]
            xs, (q, k, v, u, g) = _in_proj(l, boff, st["xs"], st["moe"], mod, norm_mix[l][None, :], w_in_l, bd,
                                           qg, kg)
            ya, hf, cv = _attention(l, q, k, v, bias_tiles, u, lru)
            xs, hp, route = _out_proj(l, boff, xs, ya, hf, cv, g, lru, mod, norm_ffn[l][None, :], w_out_l, rwc, rb)
            st["xs"] = xs
            st["moe"] = _moe(l, hp, route.reshape(N_TOK_TILES, ROUTE_ROWS, TILE), tri, exp_w1, exp_w3, exp_w2)
    return _final(streams, mod)
```

```python
import functools

import jax
import jax.numpy as jnp
import numpy as np
from jax import lax
from jax.experimental import pallas as pl
from jax.experimental.pallas import tpu as pltpu
from jax.experimental.pallas import tpu_sc as plsc

F32 = jnp.float32
BF16 = jnp.bfloat16
I32 = jnp.int32

D_MODEL = 1024
BATCH = 4
SEQ = 8192
DEPTH = 4
GRID_W = 64
CTX_LEN = 256
HEAD_DIM = 64
NA_WIDTH = 512
NA_HEADS = 8
NA_WIN_ROWS = 8
NA_WIN_COLS = 16
LRU_WIDTH = 512
LRU_BLOCKS = 8
LRU_BLOCK = 64
CONV_W = 4
LRU_C = 8.0
IN_COLS = 3 * NA_WIDTH + 2 * LRU_WIDTH
COL_KV, COL_Q, COL_U, COL_G = 0, 2, 3, 4
N_EXPERTS = 16
N_GROUPS = 4
EXPERTS_PER_GROUP = 4
TOP_K = 2
EXPERT_FF = 512
N_MOD = 6
ATTN_SCALE = HEAD_DIM ** -0.5
LOG2E = 1.4426950408889634
EPS = 1e-6
NEG_INF = -1e30
TINY = 1e-30
GELU_C = 0.7978845608028654

TILE = 256
SUB = 8
LANES = 128
GROUPS = TILE // SUB
L_TOT = CTX_LEN + SEQ
N_TILES = L_TOT // TILE
N_LAT_TILES = SEQ // TILE
STREAMS = 2
SB = BATCH // STREAMS
N_TOK = SB * L_TOT
N_TOK_TILES = N_TOK // TILE
PAIR = 2
LRU_CH = 256
MOD_ROWS = 8
VMEM_LIMIT = 56 * 1024 * 1024

PACK_W = D_MODEL // 2
HI_MASK = -65536
N_ASSIGN = TOP_K * N_TOK
EXP_TILE = 512
N_EXP_TILES = N_ASSIGN // EXP_TILE
N_ITEMS = N_EXP_TILES + N_EXPERTS - 1
ROUTE_ROWS = 2 * N_EXPERTS
POS_TILES = 6

SC_CORES = 2
SC_SUBCORES = 16
SC_WORKERS = SC_CORES * SC_SUBCORES
SC_ROWS = N_TOK // SC_WORKERS
SC_CHUNK = 88
SC_N_CHUNKS = SC_ROWS // SC_CHUNK


def _cparams(sem):
    return pltpu.CompilerParams(dimension_semantics=sem, vmem_limit_bytes=VMEM_LIMIT)


def _pack_rows(v):
    lo = pltpu.bitcast(v[:, :PACK_W].astype(BF16).astype(F32), I32)
    hi = pltpu.bitcast(v[:, PACK_W:].astype(BF16).astype(F32), I32)
    return ((lo >> 16) & 0xFFFF) | (hi & HI_MASK)


def _unpack_rows(p):
    return pltpu.bitcast(p << 16, F32), pltpu.bitcast(p & HI_MASK, F32)


def _mod_kernel(c_ref, w_ref, b_ref, o_ref):
    c = c_ref[...]
    s = c * jax.nn.sigmoid(c)
    o_ref[0] = jnp.dot(s.astype(BF16), w_ref[0].astype(BF16), preferred_element_type=F32) + b_ref[0]


def _modulation(cs, w_mod, b_mod):
    return pl.pallas_call(
        _mod_kernel,
        grid=(DEPTH, N_MOD),
        in_specs=[
            pl.BlockSpec((MOD_ROWS, D_MODEL), lambda l, n: (0, 0)),
            pl.BlockSpec((1, D_MODEL, D_MODEL), lambda l, n: (l, 0, n)),
            pl.BlockSpec((1, 1, D_MODEL), lambda l, n: (l, 0, n)),
        ],
        out_specs=pl.BlockSpec((1, MOD_ROWS, D_MODEL), lambda l, n: (l, 0, n)),
        out_shape=jax.ShapeDtypeStruct((DEPTH, MOD_ROWS, N_MOD * D_MODEL), F32),
        compiler_params=_cparams(("arbitrary", "arbitrary")),
        name="modulation",
    )(cs, w_mod, b_mod.reshape(DEPTH, 1, N_MOD * D_MODEL))


def _mod_row(b, i):
    return jnp.where(i == 0, BATCH, b)


def _moe_residual(x, g0, g1, wt, gate_row):
    lo0, hi0 = _unpack_rows(g0)
    lo1, hi1 = _unpack_rows(g1)
    w0 = wt[:, 0:1]
    w1 = wt[:, 1:2]
    f = jnp.concatenate([w0 * lo0 + w1 * lo1, w0 * hi0 + w1 * hi1], axis=1)
    return x + gate_row * f


def _pair_tok_spec(width):
    return pl.BlockSpec((PAIR, TILE, width), lambda bp, i: (bp, i, 0))


def _pair_mod_specs(layer, boff):
    return [pl.BlockSpec((1, 1, N_MOD, D_MODEL),
                         lambda bp, i, k=k: (layer, _mod_row(boff + PAIR * bp + k, i), 0, 0))
            for k in range(PAIR)]


def _full_spec(shape):
    return pl.BlockSpec(shape, lambda bp, i: tuple(0 for _ in shape))


def _to_scan_major(src_ref, scr):
    n_slab = D_MODEL // LANES
    for s in range(SUB):
        for j in range(n_slab):
            scr[j, pl.ds(s, GROUPS, stride=SUB), :] = src_ref[s * GROUPS:(s + 1) * GROUPS, j * LANES:(j + 1) * LANES]
    return jnp.concatenate([scr[j] for j in range(n_slab)], axis=1)


def _from_scan_major(val, scr):
    n_slab = D_MODEL // LANES
    for j in range(n_slab):
        scr[j] = val[:, j * LANES:(j + 1) * LANES]
    blocks = [jnp.concatenate([scr[j, pl.ds(s, GROUPS, stride=SUB), :] for j in range(n_slab)], axis=1)
              for s in range(SUB)]
    return jnp.concatenate(blocks, axis=0)


def _in_kernel(has_prev, *refs):
    if has_prev:
        x_ref, g0_ref, g1_ref, wt_ref = refs[:4]
        mprev_refs = refs[4:4 + PAIR]
        refs = (x_ref,) + refs[4 + PAIR:]
    else:
        x_ref, ctx_ref = refs[:2]
        refs = (x_ref,) + refs[2:]
    mod_refs = refs[1:1 + PAIR]
    nrm_ref, w_ref, bd_ref, qg_ref, kg_ref = refs[1 + PAIR:6 + PAIR]
    outs = refs[6 + PAIR:]
    xo_ref, proj_ref = outs[:2]
    x_ref = refs[0]
    hs = []
    for k in range(PAIR):
        if has_prev:
            x = _moe_residual(x_ref[k], g0_ref[k], g1_ref[k], wt_ref[k], mprev_refs[k][0, 0][5:6, :])
            xo_ref[k] = x
        else:
            perm_scr = outs[2]

            @pl.when(pl.program_id(1) == 0)
            def _():
                xo_ref[k] = _to_scan_major(ctx_ref.at[k], perm_scr)

            @pl.when(pl.program_id(1) > 0)
            def _():
                xo_ref[k] = _to_scan_major(x_ref.at[k], perm_scr)

            x = xo_ref[k]
        m = mod_refs[k][0, 0]
        ms = jnp.mean(x * x, axis=-1, keepdims=True)
        h = (x * lax.rsqrt(ms + EPS)) * nrm_ref[...]
        hs.append((h * (1.0 + m[1:2, :]) + m[0:1, :]).astype(BF16))
    acc = jnp.dot(jnp.concatenate(hs, axis=0), w_ref[...], preferred_element_type=F32)
    bd = bd_ref[...]

    def head_norm(t, gain):
        ss = jnp.dot((t * t).astype(BF16), bd, preferred_element_type=F32)
        return (t * lax.rsqrt(ss + EPS)) * gain

    def put(col_block, val):
        for k in range(PAIR):
            proj_ref[k, :, col_block * NA_WIDTH:(col_block + 1) * NA_WIDTH] = val[k * TILE:(k + 1) * TILE, :].astype(BF16)

    put(COL_Q, head_norm(acc[:, 0:NA_WIDTH], qg_ref[...]))
    put(COL_KV, head_norm(acc[:, NA_WIDTH:2 * NA_WIDTH], kg_ref[...]))
    put(COL_KV + 1, acc[:, 2 * NA_WIDTH:3 * NA_WIDTH])
    put(COL_U, acc[:, 3 * NA_WIDTH:3 * NA_WIDTH + LRU_WIDTH])
    put(COL_G, acc[:, 3 * NA_WIDTH + LRU_WIDTH:])


def _in_proj(l, boff, xs, moe_prev, mod, nrm, w_in, bd, qg, kg):
    has_prev = moe_prev is not None
    if has_prev:
        in_specs = [_pair_tok_spec(D_MODEL)]
        args = [xs]
        in_specs += [_pair_tok_spec(PACK_W), _pair_tok_spec(PACK_W), _pair_tok_spec(LANES)]
        in_specs += _pair_mod_specs(l - 1, boff)
        args += list(moe_prev) + [mod] * PAIR
    else:
        pair0 = boff // PAIR
        in_specs = [pl.BlockSpec((PAIR, TILE, D_MODEL), lambda bp, i: (pair0 + bp, jnp.maximum(i - 1, 0), 0)),
                    pl.BlockSpec((PAIR, TILE, D_MODEL), lambda bp, i: (pair0 + bp, 0, 0))]
        args = list(xs)
    in_specs += _pair_mod_specs(l, boff) + [_full_spec((1, D_MODEL)), _full_spec((D_MODEL, IN_COLS)),
                                            _full_spec((NA_WIDTH, NA_WIDTH)), _full_spec((1, NA_WIDTH)),
                                            _full_spec((1, NA_WIDTH))]
    args += [mod] * PAIR + [nrm, w_in, bd, qg, kg]
    out_shape = [jax.ShapeDtypeStruct((SB, L_TOT, D_MODEL), F32), jax.ShapeDtypeStruct((SB, L_TOT, IN_COLS), BF16)]
    out_specs = [_pair_tok_spec(D_MODEL), _pair_tok_spec(IN_COLS)]
    scratch = [] if has_prev else [pltpu.VMEM((D_MODEL // LANES, TILE, LANES), F32)]
    outs = pl.pallas_call(
        functools.partial(_in_kernel, has_prev),
        grid=(SB // PAIR, N_TILES),
        in_specs=in_specs,
        out_specs=out_specs,
        out_shape=out_shape,
        scratch_shapes=scratch,
        compiler_params=_cparams(("parallel", "arbitrary")),
        name="in_proj",
    )(*args)
    return outs[0], outs[1]


def _attn_kernel(q_ref, kvp_ref, kvc_ref, kvn_ref, kvx_ref, bias_ref,
                 u_ref, up_ref, un_ref, cw_ref, cb_ref, wr_ref, wi_ref, br_ref, bi_ref, lam_ref,
                 o_ref, hc_ref, h_scr, hl_scr, p_scr):
    j = pl.program_id(1)

    @pl.when(j == 0)
    def _():
        h_scr[...] = jnp.zeros_like(h_scr)

    has_prev = jnp.where(j >= 2, 1.0, 0.0).astype(F32)
    has_next = jnp.where(jnp.logical_and(j >= 1, j <= N_TILES - 2), 1.0, 0.0).astype(F32)
    cv = _lru_conv(u_ref[0].astype(F32), up_ref[0].astype(F32), un_ref[0].astype(F32), has_prev, has_next,
                   cw_ref[...], cb_ref[...])
    hc_ref[0, :, LRU_WIDTH:] = cv
    a_f, b_f = _lru_gates(cv, 0, wr_ref, wi_ref, br_ref, bi_ref, lam_ref)
    hfull, h_last = _lru_scan(a_f, b_f, h_scr[...], False, hl_scr, p_scr)
    hc_ref[0, :, 0:LRU_WIDTH] = hfull
    h_scr[...] = h_last

    pair_w = 2 * HEAD_DIM
    lane = lax.broadcasted_iota(jnp.int32, (TILE, pair_w), 1)
    low = lane < HEAD_DIM
    kv_refs = (kvp_ref, kvc_ref, kvn_ref, kvx_ref)
    nt = (((1,), (1,)), ((), ()))
    n_win = 3 * TILE
    low_kv = lax.broadcasted_iota(jnp.int32, (4 * TILE, pair_w), 1) < HEAD_DIM
    for hp in range(NA_HEADS // 2):
        cols = slice(hp * pair_w, (hp + 1) * pair_w)
        q = q_ref[0, :, cols]
        vcols = slice(NA_WIDTH + hp * pair_w, NA_WIDTH + (hp + 1) * pair_w)
        kb = jnp.concatenate([r[0, :, cols] for r in kv_refs], axis=0)
        vb = jnp.concatenate([r[0, :, vcols] for r in kv_refs], axis=0)
        outs = []
        for hh in range(2):
            own = low if hh == 0 else jnp.logical_not(low)
            own_kv = low_kv if hh == 0 else jnp.logical_not(low_kv)
            qh = jnp.where(own, q, jnp.zeros_like(q))
            s = lax.dot_general(qh, kb, nt, preferred_element_type=F32)
            s_win = s[:, 0:n_win] + bias_ref[0, 2 * hp + hh]
            s_ctx = s[:, n_win:]
            m = jnp.maximum(jnp.max(s_win, axis=-1, keepdims=True), jnp.max(s_ctx, axis=-1, keepdims=True))
            p = jnp.concatenate([jnp.exp2((s_win - m).astype(BF16)), jnp.exp2((s_ctx - m).astype(BF16))], axis=1)
            va = jnp.where(own_kv, vb, jnp.ones_like(vb))
            o = jnp.dot(p, va, preferred_element_type=F32)
            outs.append(o / pltpu.roll(o, HEAD_DIM, 1))
        o_ref[0, :, cols] = jnp.where(low, outs[0], outs[1]).astype(BF16)


def _attention(l, proj, bias_tiles, lru):
    last = N_LAT_TILES - 1
    blk = (1, TILE, 2 * NA_WIDTH)
    prev_map = lambda b, j: (b, 1 + jnp.clip(j - 2, 0, last), COL_KV // 2)
    cur_map = lambda b, j: (b, jnp.maximum(j, 1), COL_KV // 2)
    next_map = lambda b, j: (b, 1 + jnp.clip(j, 0, last), COL_KV // 2)
    ctx_map = lambda b, j: (b, 0, COL_KV // 2)
    var_map = lambda b, j: (jnp.where(j == 0, 3, jnp.where(j == 1, 0, jnp.where(j == N_TILES - 1, 2, 1))), l, 0, 0)
    kv_specs = [pl.BlockSpec(blk, prev_map), pl.BlockSpec(blk, cur_map), pl.BlockSpec(blk, next_map),
                pl.BlockSpec(blk, ctx_map)]
    tok = lambda col: pl.BlockSpec((1, TILE, NA_WIDTH), lambda b, j: (b, j, col))
    halo = TILE // 16
    n_halo = L_TOT // 16
    prev16 = pl.BlockSpec((1, 16, LRU_WIDTH), lambda b, j: (b, jnp.maximum(j * halo - 1, 0), COL_U))
    next16 = pl.BlockSpec((1, 16, LRU_WIDTH), lambda b, j: (b, jnp.minimum((j + 1) * halo, n_halo - 1), COL_U))
    whole = lambda shape: pl.BlockSpec(shape, lambda b, j: tuple(0 for _ in shape))
    n_cb = LRU_WIDTH // LRU_CH
    lru_specs = [whole((CONV_W, LRU_WIDTH)), whole((1, LRU_WIDTH)), whole((2, n_cb, LRU_CH, LRU_CH)),
                 whole((2, n_cb, LRU_CH, LRU_CH)), whole((2, LRU_WIDTH)), whole((2, LRU_WIDTH)), whole((2, LRU_WIDTH))]
    return pl.pallas_call(
        _attn_kernel,
        grid=(SB, N_TILES),
        in_specs=[tok(COL_Q)] + kv_specs + [pl.BlockSpec((1, NA_HEADS, TILE, 3 * TILE), var_map)]
        + [tok(COL_U), prev16, next16] + lru_specs,
        out_specs=[tok(0), pl.BlockSpec((1, TILE, 2 * LRU_WIDTH), lambda b, j: (b, j, 0))],
        out_shape=[jax.ShapeDtypeStruct((SB, L_TOT, NA_WIDTH), BF16),
                   jax.ShapeDtypeStruct((SB, L_TOT, 2 * LRU_WIDTH), F32)],
        scratch_shapes=[pltpu.VMEM((1, LRU_WIDTH), F32), pltpu.VMEM((TILE, LRU_WIDTH), F32),
                        pltpu.VMEM((TILE, LRU_WIDTH), F32)],
        compiler_params=_cparams(("parallel", "arbitrary")),
        name="na_attention",
    )(proj, proj, proj, proj, proj, bias_tiles, proj, proj, proj, *lru)


def _softplus(x):
    return jnp.maximum(x, 0.0) + jnp.log1p(jnp.exp(-jnp.abs(x)))


def _lru_conv(u, prev16, next16, has_prev, has_next, cw, cb):
    sub = lax.broadcasted_iota(jnp.int32, (SUB, u.shape[1]), 0)
    prow = prev16[15:16, :] * has_prev
    n0 = next16[0:1, :] * has_next
    n8 = next16[8:9, :] * has_next
    first8 = jnp.where(sub == 0, prow, pltpu.roll(u[TILE - SUB:TILE, :], 1, 0))
    last_a = jnp.where(sub == SUB - 1, n0, pltpu.roll(u[0:SUB, :], SUB - 1, 0))
    last_b = jnp.where(sub == SUB - 1, n8, pltpu.roll(u[SUB:2 * SUB, :], SUB - 1, 0))
    um1 = jnp.concatenate([first8, u[0:TILE - SUB, :]], axis=0)
    up1 = jnp.concatenate([u[SUB:TILE, :], last_a], axis=0)
    up2 = jnp.concatenate([u[2 * SUB:TILE, :], last_a, last_b], axis=0)
    return cw[0:1, :] * um1 + cw[1:2, :] * u + cw[2:3, :] * up1 + cw[3:4, :] * up2 + cb


def _lru_gates(v, d, wr_ref, wi_ref, br_ref, bi_ref, lam_ref):
    vb = v.astype(BF16)
    n_cb = LRU_WIDTH // LRU_CH

    def gate(w_ref, b_ref):
        z = [jnp.dot(vb[:, c * LRU_CH:(c + 1) * LRU_CH], w_ref[d, c], preferred_element_type=F32) for c in range(n_cb)]
        return jnp.tanh(jnp.concatenate(z, axis=1) + b_ref[d:d + 1, :])

    tr = gate(wr_ref, br_ref)
    ti = gate(wi_ref, bi_ref)
    half = (-0.5 * LRU_C * LOG2E) * _softplus(-lam_ref[d:d + 1, :])
    a = jnp.exp2(half + half * tr)
    om = 1.0 - a * a
    b = (om * lax.rsqrt(jnp.maximum(om, TINY))) * ((0.5 * v) * (1.0 + ti))
    return a, b


def _lru_scan(a, b, h_in, reverse, hl_scr, p_scr):
    order = range(GROUPS - 1, -1, -1) if reverse else range(GROUPS)
    hl = None
    for g in order:
        ag = a[g * SUB:(g + 1) * SUB, :]
        bg = b[g * SUB:(g + 1) * SUB, :]
        if hl is None:
            hl, p = bg, ag
        else:
            hl = ag * hl + bg
            p = ag * p
        hl_scr[g * SUB:(g + 1) * SUB, :] = hl
        p_scr[g * SUB:(g + 1) * SUB, :] = p
    blocks = range(SUB - 1, -1, -1) if reverse else range(SUB)
    carry = h_in
    cins = {}
    for s in blocks:
        cins[s] = carry
        carry = hl[s:s + 1, :] + p[s:s + 1, :] * carry
    cin = jnp.concatenate([cins[s] for s in range(SUB)], axis=0)
    hfull = hl_scr[...] + p_scr[...] * jnp.tile(cin, (GROUPS, 1))
    return hfull, carry


def _route(sel, aff):
    def top2_sum(a, b, c, d):
        hi1, lo1 = jnp.maximum(a, b), jnp.minimum(a, b)
        hi2, lo2 = jnp.maximum(c, d), jnp.minimum(c, d)
        return jnp.maximum(hi1, hi2) + jnp.maximum(jnp.minimum(hi1, hi2), jnp.maximum(lo1, lo2))

    scores = [top2_sum(*sel[EXPERTS_PER_GROUP * g:EXPERTS_PER_GROUP * (g + 1)]) for g in range(N_GROUPS)]
    best = jnp.zeros_like(scores[0], dtype=jnp.int32)
    best_v = scores[0]
    for g in range(1, N_GROUPS):
        upd = scores[g] > best_v
        best = jnp.where(upd, g, best)
        best_v = jnp.where(upd, scores[g], best_v)
    chosen = []
    for e in range(N_EXPERTS):
        g = e // EXPERTS_PER_GROUP
        rank = jnp.zeros_like(best)
        for o in range(EXPERTS_PER_GROUP * g, EXPERTS_PER_GROUP * (g + 1)):
            if o == e:
                continue
            ahead = sel[o] > sel[e]
            if o < e:
                ahead = jnp.logical_or(ahead, sel[o] == sel[e])
            rank = rank + ahead.astype(jnp.int32)
        chosen.append(jnp.logical_and(best == g, rank < TOP_K))
    total = jnp.zeros_like(aff[0])
    for e in range(N_EXPERTS):
        total = total + jnp.where(chosen[e], aff[e], 0.0)
    gates = [jnp.where(chosen[e], aff[e] / total, 0.0) for e in range(N_EXPERTS)]
    return gates, [c.astype(F32) for c in chosen]


def _out_kernel(x_ref, ya_ref, hc_ref, g_ref, *refs):
    mod_refs = refs[:PAIR]
    (wr_ref, wi_ref, br_ref, bi_ref, lam_ref, nrm_ref, w_ref, rwc_ref, rb_ref,
     xo_ref, hp_ref, rt_ref, h_scr, hl_scr, p_scr) = refs[PAIR:]
    rows = PAIR * TILE

    @pl.when(pl.program_id(1) == 0)
    def _():
        h_scr[...] = jnp.zeros_like(h_scr)

    ybs = []
    for k in range(PAIR):
        a_r, b_r = _lru_gates(hc_ref[k, :, LRU_WIDTH:], 1, wr_ref, wi_ref, br_ref, bi_ref, lam_ref)
        hrev, h_last = _lru_scan(a_r, b_r, h_scr[k], True, hl_scr, p_scr)
        h_scr[k] = h_last
        gx = g_ref[k].astype(F32)
        gate = (0.5 * gx) * (1.0 + jnp.tanh(gx * (GELU_C + (GELU_C * 0.044715) * (gx * gx))))
        ybs.append((gate * (hc_ref[k, :, 0:LRU_WIDTH] + hrev)).astype(BF16))
    ya = ya_ref[...].reshape(rows, NA_WIDTH)
    yb = jnp.concatenate(ybs, axis=0)
    y = jnp.dot(ya, w_ref[0:NA_WIDTH, :], preferred_element_type=F32)
    y = y + jnp.dot(yb, w_ref[NA_WIDTH:, :], preferred_element_type=F32)
    hs = []
    for k in range(PAIR):
        m = mod_refs[k][0, 0]
        x = x_ref[k] + m[2:3, :] * y[k * TILE:(k + 1) * TILE, :]
        xo_ref[k] = x
        ms = jnp.mean(x * x, axis=-1, keepdims=True)
        h = (x * lax.rsqrt(ms + EPS)) * nrm_ref[...]
        h = h * (1.0 + m[4:5, :]) + m[3:4, :]
        hp_ref[k] = _pack_rows(h)
        hs.append(h)
    h = jnp.concatenate(hs, axis=0)
    h_hi = h.astype(BF16)
    h_lo = (h - h_hi.astype(F32)).astype(BF16)
    nt = (((1,), (1,)), ((), ()))
    rwc = rwc_ref[...]
    both = lax.dot_general(rwc, h_hi, nt, preferred_element_type=F32)
    lg = (both[0:N_EXPERTS, :] + both[N_EXPERTS:, :]
          + lax.dot_general(rwc[0:N_EXPERTS, :], h_lo, nt, preferred_element_type=F32))
    aff_all = jax.nn.sigmoid(lg)
    sel_all = aff_all + rb_ref[...]
    aff = [aff_all[e:e + 1, :] for e in range(N_EXPERTS)]
    sel = [sel_all[e:e + 1, :] for e in range(N_EXPERTS)]
    gates, chosen = _route(sel, aff)
    rt = jnp.concatenate(gates + chosen, axis=0)
    for k in range(PAIR):
        rt_ref[k, 0] = rt[:, k * TILE:(k + 1) * TILE]


def _out_proj(l, boff, xs, ya, hc, proj, lru, mod, nrm, w_out, rwc, rb):
    rev = lambda i: jnp.where(i == 0, 0, N_TILES - i)
    tok = lambda width, col=0: pl.BlockSpec((PAIR, TILE, width), lambda bp, i: (bp, rev(i), col))
    mod_specs = [pl.BlockSpec((1, 1, N_MOD, D_MODEL),
                              lambda bp, i, k=k: (l, _mod_row(boff + PAIR * bp + k, i), 0, 0)) for k in range(PAIR)]
    n_cb = LRU_WIDTH // LRU_CH
    lru_specs = [_full_spec((2, n_cb, LRU_CH, LRU_CH)), _full_spec((2, n_cb, LRU_CH, LRU_CH)),
                 _full_spec((2, LRU_WIDTH)), _full_spec((2, LRU_WIDTH)), _full_spec((2, LRU_WIDTH))]
    return pl.pallas_call(
        _out_kernel,
        grid=(SB // PAIR, N_TILES),
        in_specs=[tok(D_MODEL), tok(NA_WIDTH), tok(2 * LRU_WIDTH), tok(LRU_WIDTH, COL_G)]
        + mod_specs + lru_specs
        + [_full_spec((1, D_MODEL)), _full_spec((D_MODEL, D_MODEL)), _full_spec((2 * N_EXPERTS, D_MODEL)),
           _full_spec((N_EXPERTS, 1))],
        out_specs=[tok(D_MODEL), tok(PACK_W),
                   pl.BlockSpec((PAIR, 1, ROUTE_ROWS, TILE), lambda bp, i: (bp, rev(i), 0, 0))],
        out_shape=[jax.ShapeDtypeStruct((SB, L_TOT, D_MODEL), F32),
                   jax.ShapeDtypeStruct((SB, L_TOT, PACK_W), I32),
                   jax.ShapeDtypeStruct((SB, N_TILES, ROUTE_ROWS, TILE), F32)],
        scratch_shapes=[pltpu.VMEM((PAIR, 1, LRU_WIDTH), F32), pltpu.VMEM((TILE, LRU_WIDTH), F32),
                        pltpu.VMEM((TILE, LRU_WIDTH), F32)],
        compiler_params=_cparams(("parallel", "arbitrary")),
        name="out_proj_router",
    )(xs, ya, hc, proj, *([mod] * PAIR), *lru[2:], nrm, w_out, rwc, rb)


def _pos_kernel(rt_ref, tri_ref, start_ref, pos_ref, wt_ref, run_scr):
    @pl.when(pl.program_id(0) == 0)
    def _():
        run_scr[...] = jnp.zeros_like(run_scr)

    base = start_ref[...] + run_scr[...]
    for k in range(POS_TILES):
        gates = rt_ref[k, 0:N_EXPERTS, :]
        chosen = rt_ref[k, N_EXPERTS:, :]
        rank = jnp.dot(chosen.astype(BF16), tri_ref[...], preferred_element_type=F32)
        posf = rank + base
        seen = jnp.zeros((1, TILE), F32)
        p0 = jnp.zeros((1, TILE), F32)
        p1 = jnp.zeros((1, TILE), F32)
        w0 = jnp.zeros((1, TILE), F32)
        w1 = jnp.zeros((1, TILE), F32)
        for e in range(N_EXPERTS):
            ch = chosen[e:e + 1, :]
            first = ch * (1.0 - seen)
            second = ch * seen
            p0 = p0 + first * posf[e:e + 1, :]
            p1 = p1 + second * posf[e:e + 1, :]
            w0 = w0 + first * gates[e:e + 1, :]
            w1 = w1 + second * gates[e:e + 1, :]
            seen = jnp.minimum(seen + ch, 1.0)
        pos_ref[k] = jnp.concatenate([p0, p1], axis=0).astype(I32)
        wpad = jnp.concatenate([w0, w1, jnp.zeros((LANES - TOP_K, TILE), F32)], axis=0)
        wt_ref[k * TILE:(k + 1) * TILE, :] = jnp.transpose(wpad)
        base = base + jnp.sum(chosen, axis=1, keepdims=True)
    run_scr[...] = base - start_ref[...]


def _positions(route, tri, start):
    return pl.pallas_call(
        _pos_kernel,
        grid=(N_TOK_TILES // POS_TILES,),
        in_specs=[pl.BlockSpec((POS_TILES, ROUTE_ROWS, TILE), lambda i: (i, 0, 0)),
                  pl.BlockSpec((TILE, TILE), lambda i: (0, 0)),
                  pl.BlockSpec((N_EXPERTS, 1), lambda i: (0, 0))],
        out_specs=[pl.BlockSpec((POS_TILES, TOP_K, TILE), lambda i: (i, 0, 0)),
                   pl.BlockSpec((POS_TILES * TILE, LANES), lambda i: (i, 0))],
        out_shape=[jax.ShapeDtypeStruct((N_TOK_TILES, TOP_K, TILE), I32),
                   jax.ShapeDtypeStruct((N_TOK, LANES), F32)],
        scratch_shapes=[pltpu.VMEM((N_EXPERTS, 1), F32)],
        compiler_params=_cparams(("arbitrary",)),
        name="moe_positions",
    )(route, tri, start)


def _work_items(counts):
    smem = pl.BlockSpec(memory_space=pltpu.SMEM)
    item = jax.ShapeDtypeStruct((N_ITEMS,), I32)
    return pl.pallas_call(
        _items_kernel,
        in_specs=[smem],
        out_specs=[smem] * 4,
        out_shape=[item] * 4,
        name="moe_work_items",
    )(counts)


def _items_kernel(cnt_ref, tile_ref, exp_ref, lo_ref, hi_ref):
    n = jnp.int32(0)
    start = jnp.int32(0)
    last_e = jnp.int32(0)
    for e in range(N_EXPERTS):
        cnt = cnt_ref[e]
        end = start + cnt
        first = start // EXP_TILE
        n_tiles = jnp.where(cnt > 0, (end - 1) // EXP_TILE - first + 1, 0)

        def put(j, carry, e=e, n=n, start=start, end=end, first=first):
            tile = first + j
            tile_ref[n + j] = tile
            exp_ref[n + j] = jnp.int32(e)
            lo_ref[n + j] = jnp.maximum(start - tile * EXP_TILE, 0)
            hi_ref[n + j] = jnp.minimum(end - tile * EXP_TILE, EXP_TILE)
            return carry

        lax.fori_loop(0, n_tiles, put, 0)
        n = n + n_tiles
        start = end
        last_e = jnp.where(cnt > 0, e, last_e)

    def pad(j, carry):
        tile_ref[j] = jnp.int32(N_EXP_TILES - 1)
        exp_ref[j] = last_e
        lo_ref[j] = jnp.int32(0)
        hi_ref[j] = jnp.int32(0)
        return carry

    lax.fori_loop(n, N_ITEMS, pad, 0)


def _sc_worker_base():
    return (lax.axis_index("s") * SC_CORES + lax.axis_index("c")) * SC_ROWS


def _sc_dispatch_body(h_hbm, p0_hbm, p1_hbm, out_hbm, i0_v, i1_v, rows_v, sem):
    base = _sc_worker_base()

    @pl.loop(0, SC_N_CHUNKS)
    def _(j):
        off = base + j * SC_CHUNK
        pltpu.sync_copy(h_hbm.at[pl.ds(off, SC_CHUNK)], rows_v)
        pltpu.sync_copy(p0_hbm.at[pl.ds(off, SC_CHUNK)], i0_v)
        pltpu.sync_copy(p1_hbm.at[pl.ds(off, SC_CHUNK)], i1_v)
        c0 = pltpu.async_copy(rows_v, out_hbm.at[i0_v], sem)
        c1 = pltpu.async_copy(rows_v, out_hbm.at[i1_v], sem)
        c0.wait()
        c1.wait()


def _sc_combine_body(y_hbm, p0_hbm, p1_hbm, g0_hbm, g1_hbm, i_v, rows_v, sem):
    base = _sc_worker_base()

    @pl.loop(0, SC_N_CHUNKS)
    def _(j):
        off = base + j * SC_CHUNK
        for p_hbm, g_hbm in ((p0_hbm, g0_hbm), (p1_hbm, g1_hbm)):
            pltpu.sync_copy(p_hbm.at[pl.ds(off, SC_CHUNK)], i_v)
            pltpu.async_copy(y_hbm.at[i_v], rows_v, sem).wait()
            pltpu.sync_copy(rows_v, g_hbm.at[pl.ds(off, SC_CHUNK)])


def _sc_mesh():
    return plsc.VectorSubcoreMesh(core_axis_name="c", subcore_axis_name="s")


def _dispatch(hp, pos0, pos1):
    return pl.kernel(
        _sc_dispatch_body,
        out_type=jax.ShapeDtypeStruct((N_ASSIGN, PACK_W), I32),
        mesh=_sc_mesh(),
        scratch_types=[pltpu.VMEM((SC_CHUNK,), I32), pltpu.VMEM((SC_CHUNK,), I32),
                       pltpu.VMEM((SC_CHUNK, PACK_W), I32), pltpu.SemaphoreType.DMA],
        name="moe_dispatch",
    )(hp, pos0, pos1)


def _combine(ys, pos0, pos1):
    row = jax.ShapeDtypeStruct((N_TOK, PACK_W), I32)
    return pl.kernel(
        _sc_combine_body,
        out_type=[row, row],
        mesh=_sc_mesh(),
        scratch_types=[pltpu.VMEM((SC_CHUNK,), I32), pltpu.VMEM((SC_CHUNK, PACK_W), I32),
                       pltpu.SemaphoreType.DMA],
        name="moe_combine",
    )(ys, pos0, pos1)


def _expert_kernel(tile_ref, exp_ref, lo_ref, hi_ref, x_ref, w1_ref, w3_ref, w2_ref, o_ref, w13_scr, w2_scr):
    i = pl.program_id(0)
    prev = jnp.maximum(i - 1, 0)

    @pl.when(jnp.logical_or(i == 0, exp_ref[i] != exp_ref[prev]))
    def _():
        w13_scr[:, 0:EXPERT_FF] = w1_ref[0, 0].astype(BF16)
        w13_scr[:, EXPERT_FF:] = w3_ref[0, 0].astype(BF16)
        w2_scr[...] = w2_ref[0, 0].astype(BF16)

    lo_f, hi_f = _unpack_rows(x_ref[...])
    h = jnp.concatenate([lo_f.astype(BF16), hi_f.astype(BF16)], axis=1)
    ab = jnp.dot(h, w13_scr[...], preferred_element_type=F32)
    a = ab[:, 0:EXPERT_FF]
    b = ab[:, EXPERT_FF:]
    t = ((a * jax.nn.sigmoid(a)) * b).astype(BF16)
    y = _pack_rows(jnp.dot(t, w2_scr[...], preferred_element_type=F32))
    row = lax.broadcasted_iota(jnp.int32, (EXP_TILE, PACK_W), 0)
    mine = jnp.logical_and(row >= lo_ref[i], row < hi_ref[i])
    revisit = jnp.logical_and(i > 0, tile_ref[i] == tile_ref[prev])

    @pl.when(jnp.logical_not(revisit))
    def _():
        o_ref[...] = jnp.where(mine, y, 0)

    @pl.when(revisit)
    def _():
        o_ref[...] = jnp.where(mine, y, o_ref[...])


def _experts(l, xs_sorted, items, w1, w3, w2):
    tile, e, lo, hi = items
    grid_spec = pltpu.PrefetchScalarGridSpec(
        num_scalar_prefetch=4,
        grid=(N_ITEMS,),
        in_specs=[pl.BlockSpec((EXP_TILE, PACK_W), lambda i, t, e, lo, hi: (t[i], 0)),
                  pl.BlockSpec((1, 1, D_MODEL, EXPERT_FF), lambda i, t, e, lo, hi: (l, e[i], 0, 0)),
                  pl.BlockSpec((1, 1, D_MODEL, EXPERT_FF), lambda i, t, e, lo, hi: (l, e[i], 0, 0)),
                  pl.BlockSpec((1, 1, EXPERT_FF, D_MODEL), lambda i, t, e, lo, hi: (l, e[i], 0, 0))],
        out_specs=pl.BlockSpec((EXP_TILE, PACK_W), lambda i, t, e, lo, hi: (t[i], 0)),
        scratch_shapes=[pltpu.VMEM((D_MODEL, 2 * EXPERT_FF), BF16), pltpu.VMEM((EXPERT_FF, D_MODEL), BF16)],
    )
    return pl.pallas_call(
        _expert_kernel,
        grid_spec=grid_spec,
        out_shape=jax.ShapeDtypeStruct((N_ASSIGN, PACK_W), I32),
        compiler_params=_cparams(("arbitrary",)),
        name="moe_experts",
    )(tile, e, lo, hi, xs_sorted, w1, w3, w2)


def _moe(l, hp, route, tri, w1, w3, w2):
    counts = jnp.sum(route[:, N_EXPERTS:, :], axis=(0, 2))
    start = (jnp.cumsum(counts) - counts).reshape(N_EXPERTS, 1)
    pos, wt = _positions(route, tri, start)
    pos0 = pos[:, 0, :].reshape(N_TOK)
    pos1 = pos[:, 1, :].reshape(N_TOK)
    items = _work_items(counts.astype(I32))
    xs_sorted = _dispatch(hp.reshape(N_TOK, PACK_W), pos0, pos1)
    ys = _experts(l, xs_sorted, items, w1, w3, w2)
    g0, g1 = _combine(ys, pos0, pos1)
    shape = (SB, L_TOT, PACK_W)
    return g0.reshape(shape), g1.reshape(shape), wt.reshape(SB, L_TOT, LANES)


def _final_kernel(*refs):
    n_in = 4
    stream_refs = [refs[k * n_in:(k + 1) * n_in] for k in range(STREAMS)]
    mod_ref, o_ref, scr = refs[STREAMS * n_in:]
    for k in range(STREAMS):
        @pl.when(pl.program_id(0) == k)
        def _(k=k):
            x_ref, g0_ref, g1_ref, wt_ref = stream_refs[k]
            x = _moe_residual(x_ref[0], g0_ref[0], g1_ref[0], wt_ref[0], mod_ref[0, 0][5:6, :])
            o_ref[0] = _from_scan_major(x, scr)


def _final(streams, mod):
    def lat(k, width):
        park_b, park_j = (SB - 1, N_TILES - 1) if k == 0 else (0, 1)
        return pl.BlockSpec((1, TILE, width),
                            lambda s, b, j: (jnp.where(s == k, b, park_b), jnp.where(s == k, j + 1, park_j), 0))

    in_specs, args = [], []
    for k, st in enumerate(streams):
        in_specs += [lat(k, D_MODEL), lat(k, PACK_W), lat(k, PACK_W), lat(k, LANES)]
        args += [st["xs"]] + list(st["moe"])
    in_specs.append(pl.BlockSpec((1, 1, N_MOD, D_MODEL), lambda s, b, j: (DEPTH - 1, s * SB + b, 0, 0)))
    return pl.pallas_call(
        _final_kernel,
        grid=(STREAMS, SB, N_LAT_TILES),
        in_specs=in_specs,
        out_specs=pl.BlockSpec((1, TILE, D_MODEL), lambda s, b, j: (s * SB + b, j, 0)),
        out_shape=jax.ShapeDtypeStruct((BATCH, SEQ, D_MODEL), F32),
        scratch_shapes=[pltpu.VMEM((D_MODEL // LANES, TILE, LANES), F32)],
        compiler_params=_cparams(("arbitrary", "arbitrary", "arbitrary")),
        name="final_residual",
    )(*args, mod)


def _bias_kernel(t4_ref, o_ref, scr):
    rows_q = TILE // GRID_W
    lane = lax.broadcasted_iota(jnp.int32, (GROUPS, TILE), 1)
    ka = (lane >> 1) & (rows_q - 1)
    neg = jnp.full((GROUPS, TILE), NEG_INF, F32)
    for a in range(rows_q):
        for qs in range(SUB // rows_q):
            s = (SUB // rows_q) * a + qs
            for kt in range(3):
                src = t4_ref[0, rows_q * kt - a + 3, qs * GROUPS:(qs + 1) * GROUPS, :]
                c = rows_q * kt + ka
                variants = (
                    src if kt >= 1 else neg,
                    jnp.where(jnp.logical_and(c >= a, c <= a + NA_WIN_ROWS - 1), src, neg),
                    src if kt <= 1 else neg,
                )
                for v, val in enumerate(variants):
                    for half in range(TILE // LANES):
                        scr[v, 2 * kt + half, pl.ds(s, GROUPS, stride=SUB), :] = val[:, half * LANES:(half + 1) * LANES]
    for v in range(3):
        o_ref[v, 0] = jnp.concatenate([scr[v, j] for j in range(3 * TILE // LANES)], axis=1)
    o_ref[3, 0] = jnp.full((TILE, 3 * TILE), NEG_INF, F32)


def _bias_tiles(table):
    rows_q = TILE // GRID_W
    n_r0 = 3 * rows_q
    qc = np.arange(GRID_W)[:, None]
    kc = np.arange(GRID_W)[None, :]
    col0 = np.clip(qc - NA_WIN_COLS // 2, 0, GRID_W - NA_WIN_COLS)
    in_win = (kc >= col0) & (kc < col0 + NA_WIN_COLS)
    pad = GRID_W - NA_WIN_COLS
    tp = jnp.pad(table.astype(F32) * LOG2E, ((0, 0), (0, 0), (pad, pad)))
    toeplitz = jnp.stack([tp[:, :, GRID_W - 1 - q:2 * GRID_W - 1 - q] for q in range(GRID_W)], axis=2)
    toeplitz = jnp.where(in_win[None, None], toeplitz, NEG_INF)
    n_h = table.shape[0]
    half = GRID_W // GROUPS
    t4 = jnp.stack([toeplitz[:, k:k + n_r0] for k in range(rows_q)], axis=-1)
    t4 = t4.reshape(n_h, n_r0, GRID_W, half, GROUPS, rows_q).transpose(0, 1, 2, 4, 5, 3)
    t4 = t4.reshape(n_h, n_r0, GRID_W, TILE)
    return pl.pallas_call(
        _bias_kernel,
        grid=(n_h,),
        in_specs=[pl.BlockSpec((1, n_r0, GRID_W, TILE), lambda h: (h, 0, 0, 0))],
        out_specs=pl.BlockSpec((4, 1, TILE, 3 * TILE), lambda h: (0, h, 0, 0)),
        out_shape=jax.ShapeDtypeStruct((4, n_h, TILE, 3 * TILE), F32),
        scratch_shapes=[pltpu.VMEM((3, 3 * TILE // LANES, TILE, LANES), F32)],
        compiler_params=_cparams(("parallel",)),
        name="bias_tiles",
    )(t4)


def _block_diag(w, n_chunks):
    per = LRU_BLOCKS // n_chunks
    w = w.reshape(2, n_chunks, per, LRU_BLOCK, LRU_BLOCK)
    eye = jnp.eye(per, dtype=w.dtype)
    out = jnp.einsum('dcpij,pq->dcpiqj', w, eye)
    return out.reshape(2, n_chunks, per * LRU_BLOCK, per * LRU_BLOCK)


def kernel(x, c, ctx, c_ctx, w_mod, b_mod, norm_mix, norm_ffn, w_in, w_out, q_gain, k_gain, na_bias,
           conv_w, conv_b, lru_w_r, lru_b_r, lru_w_i, lru_b_i, lru_lambda, router_w, router_b,
           exp_w1, exp_w3, exp_w2):
    cs = jnp.concatenate([c, c_ctx[None, :], jnp.zeros((MOD_ROWS - BATCH - 1, D_MODEL), F32)], axis=0)
    mod = _modulation(cs, w_mod, b_mod).reshape(DEPTH, MOD_ROWS, N_MOD, D_MODEL)

    head_of = np.arange(NA_WIDTH) // HEAD_DIM
    bd = jnp.asarray((head_of[:, None] == head_of[None, :]).astype(np.float32) / HEAD_DIM, BF16)
    tri = jnp.asarray(np.triu(np.ones((TILE, TILE), np.float32), 1), BF16)
    rwt = router_w.T
    rwh = rwt.astype(BF16)
    rwc = jnp.concatenate([rwh, (rwt - rwh.astype(F32)).astype(BF16)], axis=0)
    rb = router_b.reshape(N_EXPERTS, 1)
    n_cb = LRU_WIDTH // LRU_CH

    bias_tiles = _bias_tiles(na_bias.reshape(DEPTH * NA_HEADS, 2 * NA_WIN_ROWS - 1, 2 * NA_WIN_COLS - 1))
    streams = [{"xs": (x, ctx), "moe": None, "boff": sidx * SB} for sidx in range(STREAMS)]
    for l in range(DEPTH):
        qg = jnp.tile(q_gain[l] * (ATTN_SCALE * LOG2E), NA_HEADS)[None, :]
        kg = jnp.tile(k_gain[l], NA_HEADS)[None, :]
        w_in_l = w_in[l].astype(BF16)
        w_out_l = w_out[l].astype(BF16)
        wr = (0.5 * _block_diag(lru_w_r[l], n_cb)).astype(BF16)
        wi = (0.5 * _block_diag(lru_w_i[l], n_cb)).astype(BF16)
        lru = (conv_w[l], conv_b[l][None, :], wr, wi, 0.5 * lru_b_r[l], 0.5 * lru_b_i[l], lru_lambda[l])
        for st in streams:
            boff = st["boff"]
            xs, proj = _in_proj(l, boff, st["xs"], st["moe"], mod, norm_mix[l][None, :], w_in_l, bd, qg, kg)
            ya, hc = _attention(l, proj, bias_tiles, lru)
            xs, hp, route = _out_proj(l, boff, xs, ya, hc, proj, lru, mod, norm_ffn[l][None, :], w_out_l, rwc, rb)
            st["xs"] = xs
            st["moe"] = _moe(l, hp, route.reshape(N_TOK_TILES, ROUTE_ROWS, TILE), tri, exp_w1, exp_w3, exp_w2)
    return _final(streams, mod)
```

```python
import functools

import jax
import jax.numpy as jnp
import numpy as np
from jax import lax
from jax.experimental import pallas as pl
from jax.experimental.pallas import tpu as pltpu
from jax.experimental.pallas import tpu_sc as plsc

F32 = jnp.float32
BF16 = jnp.bfloat16
I32 = jnp.int32

D_MODEL = 1024
BATCH = 4
SEQ = 8192
DEPTH = 4
GRID_W = 64
CTX_LEN = 256
HEAD_DIM = 64
NA_WIDTH = 512
NA_HEADS = 8
NA_WIN_ROWS = 8
NA_WIN_COLS = 16
LRU_WIDTH = 512
LRU_BLOCKS = 8
LRU_BLOCK = 64
CONV_W = 4
LRU_C = 8.0
IN_COLS = 3 * NA_WIDTH + 2 * LRU_WIDTH
COL_KV, COL_Q, COL_U, COL_G = 0, 2, 3, 4
N_EXPERTS = 16
N_GROUPS = 4
EXPERTS_PER_GROUP = 4
TOP_K = 2
EXPERT_FF = 512
N_MOD = 6
ATTN_SCALE = HEAD_DIM ** -0.5
LOG2E = 1.4426950408889634
EPS = 1e-6
NEG_INF = -1e30
TINY = 1e-30
GELU_C = 0.7978845608028654

TILE = 256
SUB = 8
LANES = 128
GROUPS = TILE // SUB
L_TOT = CTX_LEN + SEQ
N_TILES = L_TOT // TILE
N_LAT_TILES = SEQ // TILE
STREAMS = 2
SB = BATCH // STREAMS
N_TOK = SB * L_TOT
N_TOK_TILES = N_TOK // TILE
PAIR = 2
LRU_CH = 256
MOD_ROWS = 8
VMEM_LIMIT = 56 * 1024 * 1024

PACK_W = D_MODEL // 2
HI_MASK = -65536
N_ASSIGN = TOP_K * N_TOK
EXP_TILE = 512
N_EXP_TILES = N_ASSIGN // EXP_TILE
N_ITEMS = N_EXP_TILES + N_EXPERTS - 1
ROUTE_ROWS = 2 * N_EXPERTS
POS_TILES = 6

SC_CORES = 2
SC_SUBCORES = 16
SC_WORKERS = SC_CORES * SC_SUBCORES
SC_ROWS = N_TOK // SC_WORKERS
SC_CHUNK = 88
SC_N_CHUNKS = SC_ROWS // SC_CHUNK


def _cparams(sem):
    return pltpu.CompilerParams(dimension_semantics=sem, vmem_limit_bytes=VMEM_LIMIT)


def _pack_rows(v):
    lo = pltpu.bitcast(v[:, :PACK_W].astype(BF16).astype(F32), I32)
    hi = pltpu.bitcast(v[:, PACK_W:].astype(BF16).astype(F32), I32)
    return ((lo >> 16) & 0xFFFF) | (hi & HI_MASK)


def _unpack_rows(p):
    return pltpu.bitcast(p << 16, F32), pltpu.bitcast(p & HI_MASK, F32)


def _mod_kernel(c_ref, w_ref, b_ref, o_ref):
    c = c_ref[...]
    s = c * jax.nn.sigmoid(c)
    o_ref[0] = jnp.dot(s.astype(BF16), w_ref[0].astype(BF16), preferred_element_type=F32) + b_ref[0]


def _modulation(cs, w_mod, b_mod):
    return pl.pallas_call(
        _mod_kernel,
        grid=(DEPTH, N_MOD),
        in_specs=[
            pl.BlockSpec((MOD_ROWS, D_MODEL), lambda l, n: (0, 0)),
            pl.BlockSpec((1, D_MODEL, D_MODEL), lambda l, n: (l, 0, n)),
            pl.BlockSpec((1, 1, D_MODEL), lambda l, n: (l, 0, n)),
        ],
        out_specs=pl.BlockSpec((1, MOD_ROWS, D_MODEL), lambda l, n: (l, 0, n)),
        out_shape=jax.ShapeDtypeStruct((DEPTH, MOD_ROWS, N_MOD * D_MODEL), F32),
        compiler_params=_cparams(("arbitrary", "arbitrary")),
        name="modulation",
    )(cs, w_mod, b_mod.reshape(DEPTH, 1, N_MOD * D_MODEL))


def _mod_row(b, i):
    return jnp.where(i == 0, BATCH, b)


def _moe_residual(x, g0, g1, wt, gate_row):
    lo0, hi0 = _unpack_rows(g0)
    lo1, hi1 = _unpack_rows(g1)
    w0 = wt[:, 0:1]
    w1 = wt[:, 1:2]
    f = jnp.concatenate([w0 * lo0 + w1 * lo1, w0 * hi0 + w1 * hi1], axis=1)
    return x + gate_row * f


def _pair_tok_spec(width):
    return pl.BlockSpec((PAIR, TILE, width), lambda bp, i: (bp, i, 0))


def _pair_mod_specs(layer, boff):
    return [pl.BlockSpec((1, 1, N_MOD, D_MODEL),
                         lambda bp, i, k=k: (layer, _mod_row(boff + PAIR * bp + k, i), 0, 0))
            for k in range(PAIR)]


def _full_spec(shape):
    return pl.BlockSpec(shape, lambda bp, i: tuple(0 for _ in shape))


def _to_scan_major(src_ref, scr):
    n_slab = D_MODEL // LANES
    for s in range(SUB):
        for j in range(n_slab):
            scr[j, pl.ds(s, GROUPS, stride=SUB), :] = src_ref[s * GROUPS:(s + 1) * GROUPS, j * LANES:(j + 1) * LANES]
    return jnp.concatenate([scr[j] for j in range(n_slab)], axis=1)


def _from_scan_major(val, scr):
    n_slab = D_MODEL // LANES
    for j in range(n_slab):
        scr[j] = val[:, j * LANES:(j + 1) * LANES]
    blocks = [jnp.concatenate([scr[j, pl.ds(s, GROUPS, stride=SUB), :] for j in range(n_slab)], axis=1)
              for s in range(SUB)]
    return jnp.concatenate(blocks, axis=0)


def _in_kernel(has_prev, *refs):
    if has_prev:
        x_ref, g0_ref, g1_ref, wt_ref = refs[:4]
        mprev_refs = refs[4:4 + PAIR]
        refs = (x_ref,) + refs[4 + PAIR:]
    else:
        x_ref, ctx_ref = refs[:2]
        refs = (x_ref,) + refs[2:]
    mod_refs = refs[1:1 + PAIR]
    nrm_ref, w_ref, bd_ref, qg_ref, kg_ref = refs[1 + PAIR:6 + PAIR]
    outs = refs[6 + PAIR:]
    xo_ref, proj_ref = outs[:2]
    x_ref = refs[0]
    hs = []
    for k in range(PAIR):
        if has_prev:
            x = _moe_residual(x_ref[k], g0_ref[k], g1_ref[k], wt_ref[k], mprev_refs[k][0, 0][5:6, :])
            xo_ref[k] = x
        else:
            perm_scr = outs[2]

            @pl.when(pl.program_id(1) == 0)
            def _():
                xo_ref[k] = _to_scan_major(ctx_ref.at[k], perm_scr)

            @pl.when(pl.program_id(1) > 0)
            def _():
                xo_ref[k] = _to_scan_major(x_ref.at[k], perm_scr)

            x = xo_ref[k]
        m = mod_refs[k][0, 0]
        ms = jnp.mean(x * x, axis=-1, keepdims=True)
        h = (x * lax.rsqrt(ms + EPS)) * nrm_ref[...]
        hs.append((h * (1.0 + m[1:2, :]) + m[0:1, :]).astype(BF16))
    acc = jnp.dot(jnp.concatenate(hs, axis=0), w_ref[...], preferred_element_type=F32)
    bd = bd_ref[...]

    def head_norm(t, gain):
        ss = jnp.dot((t * t).astype(BF16), bd, preferred_element_type=F32)
        return (t * lax.rsqrt(ss + EPS)) * gain

    def put(col_block, val):
        for k in range(PAIR):
            proj_ref[k, :, col_block * NA_WIDTH:(col_block + 1) * NA_WIDTH] = val[k * TILE:(k + 1) * TILE, :].astype(BF16)

    put(COL_Q, head_norm(acc[:, 0:NA_WIDTH], qg_ref[...]))
    put(COL_KV, head_norm(acc[:, NA_WIDTH:2 * NA_WIDTH], kg_ref[...]))
    put(COL_KV + 1, acc[:, 2 * NA_WIDTH:3 * NA_WIDTH])
    put(COL_U, acc[:, 3 * NA_WIDTH:3 * NA_WIDTH + LRU_WIDTH])
    put(COL_G, acc[:, 3 * NA_WIDTH + LRU_WIDTH:])


def _in_proj(l, boff, xs, moe_prev, mod, nrm, w_in, bd, qg, kg):
    has_prev = moe_prev is not None
    if has_prev:
        in_specs = [_pair_tok_spec(D_MODEL)]
        args = [xs]
        in_specs += [_pair_tok_spec(PACK_W), _pair_tok_spec(PACK_W), _pair_tok_spec(LANES)]
        in_specs += _pair_mod_specs(l - 1, boff)
        args += list(moe_prev) + [mod] * PAIR
    else:
        pair0 = boff // PAIR
        in_specs = [pl.BlockSpec((PAIR, TILE, D_MODEL), lambda bp, i: (pair0 + bp, jnp.maximum(i - 1, 0), 0)),
                    pl.BlockSpec((PAIR, TILE, D_MODEL), lambda bp, i: (pair0 + bp, 0, 0))]
        args = list(xs)
    in_specs += _pair_mod_specs(l, boff) + [_full_spec((1, D_MODEL)), _full_spec((D_MODEL, IN_COLS)),
                                            _full_spec((NA_WIDTH, NA_WIDTH)), _full_spec((1, NA_WIDTH)),
                                            _full_spec((1, NA_WIDTH))]
    args += [mod] * PAIR + [nrm, w_in, bd, qg, kg]
    out_shape = [jax.ShapeDtypeStruct((SB, L_TOT, D_MODEL), F32), jax.ShapeDtypeStruct((SB, L_TOT, IN_COLS), BF16)]
    out_specs = [_pair_tok_spec(D_MODEL), _pair_tok_spec(IN_COLS)]
    scratch = [] if has_prev else [pltpu.VMEM((D_MODEL // LANES, TILE, LANES), F32)]
    outs = pl.pallas_call(
        functools.partial(_in_kernel, has_prev),
        grid=(SB // PAIR, N_TILES),
        in_specs=in_specs,
        out_specs=out_specs,
        out_shape=out_shape,
        scratch_shapes=scratch,
        compiler_params=_cparams(("parallel", "arbitrary")),
        name="in_proj",
    )(*args)
    return outs[0], outs[1]


def _attn_kernel(q_ref, kvp_ref, kvc_ref, kvn_ref, kvx_ref, bias_ref,
                 u_ref, up_ref, un_ref, cw_ref, cb_ref, wr_ref, wi_ref, br_ref, bi_ref, lam_ref,
                 o_ref, hc_ref, h_scr, hl_scr, p_scr):
    j = pl.program_id(1)

    @pl.when(j == 0)
    def _():
        h_scr[...] = jnp.zeros_like(h_scr)

    has_prev = jnp.where(j >= 2, 1.0, 0.0).astype(F32)
    has_next = jnp.where(jnp.logical_and(j >= 1, j <= N_TILES - 2), 1.0, 0.0).astype(F32)
    for b in range(PAIR):
        cv = _lru_conv(u_ref[b].astype(F32), up_ref[b].astype(F32), un_ref[b].astype(F32), has_prev, has_next,
                       cw_ref[...], cb_ref[...])
        hc_ref[b, :, LRU_WIDTH:] = cv
        a_f, b_f = _lru_gates(cv, 0, wr_ref, wi_ref, br_ref, bi_ref, lam_ref)
        hfull, h_last = _lru_scan(a_f, b_f, h_scr[b], False, hl_scr, p_scr)
        hc_ref[b, :, 0:LRU_WIDTH] = hfull
        h_scr[b] = h_last

    pair_w = 2 * HEAD_DIM
    lane = lax.broadcasted_iota(jnp.int32, (TILE, pair_w), 1)
    low = lane < HEAD_DIM
    kv_refs = (kvp_ref, kvc_ref, kvn_ref, kvx_ref)
    nt = (((1,), (1,)), ((), ()))
    n_win = 3 * TILE
    low_kv = lax.broadcasted_iota(jnp.int32, (4 * TILE, pair_w), 1) < HEAD_DIM
    for b, hp in [(b, hp) for b in range(PAIR) for hp in range(NA_HEADS // 2)]:
        cols = slice(hp * pair_w, (hp + 1) * pair_w)
        q = q_ref[b, :, cols]
        vcols = slice(NA_WIDTH + hp * pair_w, NA_WIDTH + (hp + 1) * pair_w)
        kb = jnp.concatenate([r[b, :, cols] for r in kv_refs], axis=0)
        vb = jnp.concatenate([r[b, :, vcols] for r in kv_refs], axis=0)
        outs = []
        for hh in range(2):
            own = low if hh == 0 else jnp.logical_not(low)
            own_kv = low_kv if hh == 0 else jnp.logical_not(low_kv)
            qh = jnp.where(own, q, jnp.zeros_like(q))
            s = lax.dot_general(qh, kb, nt, preferred_element_type=F32)
            s_win = s[:, 0:n_win] + bias_ref[0, 2 * hp + hh]
            s_ctx = s[:, n_win:]
            m = jnp.maximum(jnp.max(s_win, axis=-1, keepdims=True), jnp.max(s_ctx, axis=-1, keepdims=True))
            p = jnp.concatenate([jnp.exp2((s_win - m).astype(BF16)), jnp.exp2((s_ctx - m).astype(BF16))], axis=1)
            va = jnp.where(own_kv, vb, jnp.ones_like(vb))
            o = jnp.dot(p, va, preferred_element_type=F32)
            outs.append(o / pltpu.roll(o, HEAD_DIM, 1))
        o_ref[b, :, cols] = jnp.where(low, outs[0], outs[1]).astype(BF16)


def _attention(l, proj, bias_tiles, lru):
    last = N_LAT_TILES - 1
    blk = (PAIR, TILE, 2 * NA_WIDTH)
    prev_map = lambda b, j: (b, 1 + jnp.clip(j - 2, 0, last), COL_KV // 2)
    cur_map = lambda b, j: (b, jnp.maximum(j, 1), COL_KV // 2)
    next_map = lambda b, j: (b, 1 + jnp.clip(j, 0, last), COL_KV // 2)
    ctx_map = lambda b, j: (b, 0, COL_KV // 2)
    var_map = lambda b, j: (jnp.where(j == 0, 3, jnp.where(j == 1, 0, jnp.where(j == N_TILES - 1, 2, 1))), l, 0, 0)
    kv_specs = [pl.BlockSpec(blk, prev_map), pl.BlockSpec(blk, cur_map), pl.BlockSpec(blk, next_map),
                pl.BlockSpec(blk, ctx_map)]
    tok = lambda col: pl.BlockSpec((PAIR, TILE, NA_WIDTH), lambda b, j: (b, j, col))
    halo = TILE // 16
    n_halo = L_TOT // 16
    prev16 = pl.BlockSpec((PAIR, 16, LRU_WIDTH), lambda b, j: (b, jnp.maximum(j * halo - 1, 0), COL_U))
    next16 = pl.BlockSpec((PAIR, 16, LRU_WIDTH), lambda b, j: (b, jnp.minimum((j + 1) * halo, n_halo - 1), COL_U))
    whole = lambda shape: pl.BlockSpec(shape, lambda b, j: tuple(0 for _ in shape))
    n_cb = LRU_WIDTH // LRU_CH
    lru_specs = [whole((CONV_W, LRU_WIDTH)), whole((1, LRU_WIDTH)), whole((2, n_cb, LRU_CH, LRU_CH)),
                 whole((2, n_cb, LRU_CH, LRU_CH)), whole((2, LRU_WIDTH)), whole((2, LRU_WIDTH)), whole((2, LRU_WIDTH))]
    return pl.pallas_call(
        _attn_kernel,
        grid=(SB // PAIR, N_TILES),
        in_specs=[tok(COL_Q)] + kv_specs + [pl.BlockSpec((1, NA_HEADS, TILE, 3 * TILE), var_map)]
        + [tok(COL_U), prev16, next16] + lru_specs,
        out_specs=[tok(0), pl.BlockSpec((PAIR, TILE, 2 * LRU_WIDTH), lambda b, j: (b, j, 0))],
        out_shape=[jax.ShapeDtypeStruct((SB, L_TOT, NA_WIDTH), BF16),
                   jax.ShapeDtypeStruct((SB, L_TOT, 2 * LRU_WIDTH), F32)],
        scratch_shapes=[pltpu.VMEM((PAIR, 1, LRU_WIDTH), F32), pltpu.VMEM((TILE, LRU_WIDTH), F32),
                        pltpu.VMEM((TILE, LRU_WIDTH), F32)],
        compiler_params=_cparams(("parallel", "arbitrary")),
        name="na_attention",
    )(proj, proj, proj, proj, proj, bias_tiles, proj, proj, proj, *lru)


def _softplus(x):
    return jnp.maximum(x, 0.0) + jnp.log1p(jnp.exp(-jnp.abs(x)))


def _lru_conv(u, prev16, next16, has_prev, has_next, cw, cb):
    sub = lax.broadcasted_iota(jnp.int32, (SUB, u.shape[1]), 0)
    prow = prev16[15:16, :] * has_prev
    n0 = next16[0:1, :] * has_next
    n8 = next16[8:9, :] * has_next
    first8 = jnp.where(sub == 0, prow, pltpu.roll(u[TILE - SUB:TILE, :], 1, 0))
    last_a = jnp.where(sub == SUB - 1, n0, pltpu.roll(u[0:SUB, :], SUB - 1, 0))
    last_b = jnp.where(sub == SUB - 1, n8, pltpu.roll(u[SUB:2 * SUB, :], SUB - 1, 0))
    um1 = jnp.concatenate([first8, u[0:TILE - SUB, :]], axis=0)
    up1 = jnp.concatenate([u[SUB:TILE, :], last_a], axis=0)
    up2 = jnp.concatenate([u[2 * SUB:TILE, :], last_a, last_b], axis=0)
    return cw[0:1, :] * um1 + cw[1:2, :] * u + cw[2:3, :] * up1 + cw[3:4, :] * up2 + cb


def _lru_gates(v, d, wr_ref, wi_ref, br_ref, bi_ref, lam_ref):
    vb = v.astype(BF16)
    n_cb = LRU_WIDTH // LRU_CH

    def gate(w_ref, b_ref):
        z = [jnp.dot(vb[:, c * LRU_CH:(c + 1) * LRU_CH], w_ref[d, c], preferred_element_type=F32) for c in range(n_cb)]
        return jnp.tanh(jnp.concatenate(z, axis=1) + b_ref[d:d + 1, :])

    tr = gate(wr_ref, br_ref)
    ti = gate(wi_ref, bi_ref)
    half = (-0.5 * LRU_C * LOG2E) * _softplus(-lam_ref[d:d + 1, :])
    a = jnp.exp2(half + half * tr)
    om = 1.0 - a * a
    b = (om * lax.rsqrt(jnp.maximum(om, TINY))) * ((0.5 * v) * (1.0 + ti))
    return a, b


def _lru_scan(a, b, h_in, reverse, hl_scr, p_scr):
    order = range(GROUPS - 1, -1, -1) if reverse else range(GROUPS)
    hl = None
    for g in order:
        ag = a[g * SUB:(g + 1) * SUB, :]
        bg = b[g * SUB:(g + 1) * SUB, :]
        if hl is None:
            hl, p = bg, ag
        else:
            hl = ag * hl + bg
            p = ag * p
        hl_scr[g * SUB:(g + 1) * SUB, :] = hl
        p_scr[g * SUB:(g + 1) * SUB, :] = p
    blocks = range(SUB - 1, -1, -1) if reverse else range(SUB)
    carry = h_in
    cins = {}
    for s in blocks:
        cins[s] = carry
        carry = hl[s:s + 1, :] + p[s:s + 1, :] * carry
    cin = jnp.concatenate([cins[s] for s in range(SUB)], axis=0)
    hfull = hl_scr[...] + p_scr[...] * jnp.tile(cin, (GROUPS, 1))
    return hfull, carry


def _route(sel, aff):
    def top2_sum(a, b, c, d):
        hi1, lo1 = jnp.maximum(a, b), jnp.minimum(a, b)
        hi2, lo2 = jnp.maximum(c, d), jnp.minimum(c, d)
        return jnp.maximum(hi1, hi2) + jnp.maximum(jnp.minimum(hi1, hi2), jnp.maximum(lo1, lo2))

    scores = [top2_sum(*sel[EXPERTS_PER_GROUP * g:EXPERTS_PER_GROUP * (g + 1)]) for g in range(N_GROUPS)]
    best = jnp.zeros_like(scores[0], dtype=jnp.int32)
    best_v = scores[0]
    for g in range(1, N_GROUPS):
        upd = scores[g] > best_v
        best = jnp.where(upd, g, best)
        best_v = jnp.where(upd, scores[g], best_v)
    chosen = []
    for e in range(N_EXPERTS):
        g = e // EXPERTS_PER_GROUP
        rank = jnp.zeros_like(best)
        for o in range(EXPERTS_PER_GROUP * g, EXPERTS_PER_GROUP * (g + 1)):
            if o == e:
                continue
            ahead = sel[o] > sel[e]
            if o < e:
                ahead = jnp.logical_or(ahead, sel[o] == sel[e])
            rank = rank + ahead.astype(jnp.int32)
        chosen.append(jnp.logical_and(best == g, rank < TOP_K))
    total = jnp.zeros_like(aff[0])
    for e in range(N_EXPERTS):
        total = total + jnp.where(chosen[e], aff[e], 0.0)
    gates = [jnp.where(chosen[e], aff[e] / total, 0.0) for e in range(N_EXPERTS)]
    return gates, [c.astype(F32) for c in chosen]


def _out_kernel(x_ref, ya_ref, hc_ref, g_ref, *refs):
    mod_refs = refs[:PAIR]
    (wr_ref, wi_ref, br_ref, bi_ref, lam_ref, nrm_ref, w_ref, rwc_ref, rb_ref,
     xo_ref, hp_ref, rt_ref, h_scr, hl_scr, p_scr) = refs[PAIR:]
    rows = PAIR * TILE

    @pl.when(pl.program_id(1) == 0)
    def _():
        h_scr[...] = jnp.zeros_like(h_scr)

    ybs = []
    for k in range(PAIR):
        a_r, b_r = _lru_gates(hc_ref[k, :, LRU_WIDTH:], 1, wr_ref, wi_ref, br_ref, bi_ref, lam_ref)
        hrev, h_last = _lru_scan(a_r, b_r, h_scr[k], True, hl_scr, p_scr)
        h_scr[k] = h_last
        gx = g_ref[k].astype(F32)
        gate = (0.5 * gx) * (1.0 + jnp.tanh(gx * (GELU_C + (GELU_C * 0.044715) * (gx * gx))))
        ybs.append((gate * (hc_ref[k, :, 0:LRU_WIDTH] + hrev)).astype(BF16))
    ya = ya_ref[...].reshape(rows, NA_WIDTH)
    yb = jnp.concatenate(ybs, axis=0)
    y = jnp.dot(ya, w_ref[0:NA_WIDTH, :], preferred_element_type=F32)
    y = y + jnp.dot(yb, w_ref[NA_WIDTH:, :], preferred_element_type=F32)
    hs = []
    for k in range(PAIR):
        m = mod_refs[k][0, 0]
        x = x_ref[k] + m[2:3, :] * y[k * TILE:(k + 1) * TILE, :]
        xo_ref[k] = x
        ms = jnp.mean(x * x, axis=-1, keepdims=True)
        h = (x * lax.rsqrt(ms + EPS)) * nrm_ref[...]
        h = h * (1.0 + m[4:5, :]) + m[3:4, :]
        hp_ref[k] = _pack_rows(h)
        hs.append(h)
    h = jnp.concatenate(hs, axis=0)
    h_hi = h.astype(BF16)
    h_lo = (h - h_hi.astype(F32)).astype(BF16)
    nt = (((1,), (1,)), ((), ()))
    rwc = rwc_ref[...]
    both = lax.dot_general(rwc, h_hi, nt, preferred_element_type=F32)
    lg = (both[0:N_EXPERTS, :] + both[N_EXPERTS:, :]
          + lax.dot_general(rwc[0:N_EXPERTS, :], h_lo, nt, preferred_element_type=F32))
    aff_all = jax.nn.sigmoid(lg)
    sel_all = aff_all + rb_ref[...]
    aff = [aff_all[e:e + 1, :] for e in range(N_EXPERTS)]
    sel = [sel_all[e:e + 1, :] for e in range(N_EXPERTS)]
    gates, chosen = _route(sel, aff)
    rt = jnp.concatenate(gates + chosen, axis=0)
    for k in range(PAIR):
        rt_ref[k, 0] = rt[:, k * TILE:(k + 1) * TILE]


def _out_proj(l, boff, xs, ya, hc, proj, lru, mod, nrm, w_out, rwc, rb):
    rev = lambda i: jnp.where(i == 0, 0, N_TILES - i)
    tok = lambda width, col=0: pl.BlockSpec((PAIR, TILE, width), lambda bp, i: (bp, rev(i), col))
    mod_specs = [pl.BlockSpec((1, 1, N_MOD, D_MODEL),
                              lambda bp, i, k=k: (l, _mod_row(boff + PAIR * bp + k, i), 0, 0)) for k in range(PAIR)]
    n_cb = LRU_WIDTH // LRU_CH
    lru_specs = [_full_spec((2, n_cb, LRU_CH, LRU_CH)), _full_spec((2, n_cb, LRU_CH, LRU_CH)),
                 _full_spec((2, LRU_WIDTH)), _full_spec((2, LRU_WIDTH)), _full_spec((2, LRU_WIDTH))]
    return pl.pallas_call(
        _out_kernel,
        grid=(SB // PAIR, N_TILES),
        in_specs=[tok(D_MODEL), tok(NA_WIDTH), tok(2 * LRU_WIDTH), tok(LRU_WIDTH, COL_G)]
        + mod_specs + lru_specs
        + [_full_spec((1, D_MODEL)), _full_spec((D_MODEL, D_MODEL)), _full_spec((2 * N_EXPERTS, D_MODEL)),
           _full_spec((N_EXPERTS, 1))],
        out_specs=[tok(D_MODEL), tok(PACK_W),
                   pl.BlockSpec((PAIR, 1, ROUTE_ROWS, TILE), lambda bp, i: (bp, rev(i), 0, 0))],
        out_shape=[jax.ShapeDtypeStruct((SB, L_TOT, D_MODEL), F32),
                   jax.ShapeDtypeStruct((SB, L_TOT, PACK_W), I32),
                   jax.ShapeDtypeStruct((SB, N_TILES, ROUTE_ROWS, TILE), F32)],
        scratch_shapes=[pltpu.VMEM((PAIR, 1, LRU_WIDTH), F32), pltpu.VMEM((TILE, LRU_WIDTH), F32),
                        pltpu.VMEM((TILE, LRU_WIDTH), F32)],
        compiler_params=_cparams(("parallel", "arbitrary")),
        name="out_proj_router",
    )(xs, ya, hc, proj, *([mod] * PAIR), *lru[2:], nrm, w_out, rwc, rb)


def _pos_kernel(rt_ref, tri_ref, start_ref, pos_ref, wt_ref, run_scr):
    @pl.when(pl.program_id(0) == 0)
    def _():
        run_scr[...] = jnp.zeros_like(run_scr)

    base = start_ref[...] + run_scr[...]
    for k in range(POS_TILES):
        gates = rt_ref[k, 0:N_EXPERTS, :]
        chosen = rt_ref[k, N_EXPERTS:, :]
        rank = jnp.dot(chosen.astype(BF16), tri_ref[...], preferred_element_type=F32)
        posf = rank + base
        seen = jnp.zeros((1, TILE), F32)
        p0 = jnp.zeros((1, TILE), F32)
        p1 = jnp.zeros((1, TILE), F32)
        w0 = jnp.zeros((1, TILE), F32)
        w1 = jnp.zeros((1, TILE), F32)
        for e in range(N_EXPERTS):
            ch = chosen[e:e + 1, :]
            first = ch * (1.0 - seen)
            second = ch * seen
            p0 = p0 + first * posf[e:e + 1, :]
            p1 = p1 + second * posf[e:e + 1, :]
            w0 = w0 + first * gates[e:e + 1, :]
            w1 = w1 + second * gates[e:e + 1, :]
            seen = jnp.minimum(seen + ch, 1.0)
        pos_ref[k] = jnp.concatenate([p0, p1], axis=0).astype(I32)
        wpad = jnp.concatenate([w0, w1, jnp.zeros((LANES - TOP_K, TILE), F32)], axis=0)
        wt_ref[k * TILE:(k + 1) * TILE, :] = jnp.transpose(wpad)
        base = base + jnp.sum(chosen, axis=1, keepdims=True)
    run_scr[...] = base - start_ref[...]


def _positions(route, tri, start):
    return pl.pallas_call(
        _pos_kernel,
        grid=(N_TOK_TILES // POS_TILES,),
        in_specs=[pl.BlockSpec((POS_TILES, ROUTE_ROWS, TILE), lambda i: (i, 0, 0)),
                  pl.BlockSpec((TILE, TILE), lambda i: (0, 0)),
                  pl.BlockSpec((N_EXPERTS, 1), lambda i: (0, 0))],
        out_specs=[pl.BlockSpec((POS_TILES, TOP_K, TILE), lambda i: (i, 0, 0)),
                   pl.BlockSpec((POS_TILES * TILE, LANES), lambda i: (i, 0))],
        out_shape=[jax.ShapeDtypeStruct((N_TOK_TILES, TOP_K, TILE), I32),
                   jax.ShapeDtypeStruct((N_TOK, LANES), F32)],
        scratch_shapes=[pltpu.VMEM((N_EXPERTS, 1), F32)],
        compiler_params=_cparams(("arbitrary",)),
        name="moe_positions",
    )(route, tri, start)


def _work_items(counts):
    smem = pl.BlockSpec(memory_space=pltpu.SMEM)
    item = jax.ShapeDtypeStruct((N_ITEMS,), I32)
    return pl.pallas_call(
        _items_kernel,
        in_specs=[smem],
        out_specs=[smem] * 4,
        out_shape=[item] * 4,
        name="moe_work_items",
    )(counts)


def _items_kernel(cnt_ref, tile_ref, exp_ref, lo_ref, hi_ref):
    n = jnp.int32(0)
    start = jnp.int32(0)
    last_e = jnp.int32(0)
    for e in range(N_EXPERTS):
        cnt = cnt_ref[e]
        end = start + cnt
        first = start // EXP_TILE
        n_tiles = jnp.where(cnt > 0, (end - 1) // EXP_TILE - first + 1, 0)

        def put(j, carry, e=e, n=n, start=start, end=end, first=first):
            tile = first + j
            tile_ref[n + j] = tile
            exp_ref[n + j] = jnp.int32(e)
            lo_ref[n + j] = jnp.maximum(start - tile * EXP_TILE, 0)
            hi_ref[n + j] = jnp.minimum(end - tile * EXP_TILE, EXP_TILE)
            return carry

        lax.fori_loop(0, n_tiles, put, 0)
        n = n + n_tiles
        start = end
        last_e = jnp.where(cnt > 0, e, last_e)

    def pad(j, carry):
        tile_ref[j] = jnp.int32(N_EXP_TILES - 1)
        exp_ref[j] = last_e
        lo_ref[j] = jnp.int32(0)
        hi_ref[j] = jnp.int32(0)
        return carry

    lax.fori_loop(n, N_ITEMS, pad, 0)


def _sc_worker_base():
    return (lax.axis_index("s") * SC_CORES + lax.axis_index("c")) * SC_ROWS


def _sc_dispatch_body(h_hbm, p0_hbm, p1_hbm, out_hbm, i0_v, i1_v, rows_v, sem):
    base = _sc_worker_base()

    @pl.loop(0, SC_N_CHUNKS)
    def _(j):
        off = base + j * SC_CHUNK
        pltpu.sync_copy(h_hbm.at[pl.ds(off, SC_CHUNK)], rows_v)
        pltpu.sync_copy(p0_hbm.at[pl.ds(off, SC_CHUNK)], i0_v)
        pltpu.sync_copy(p1_hbm.at[pl.ds(off, SC_CHUNK)], i1_v)
        c0 = pltpu.async_copy(rows_v, out_hbm.at[i0_v], sem)
        c1 = pltpu.async_copy(rows_v, out_hbm.at[i1_v], sem)
        c0.wait()
        c1.wait()


def _sc_combine_body(y_hbm, p0_hbm, p1_hbm, g0_hbm, g1_hbm, i_v, rows_v, sem):
    base = _sc_worker_base()

    @pl.loop(0, SC_N_CHUNKS)
    def _(j):
        off = base + j * SC_CHUNK
        for p_hbm, g_hbm in ((p0_hbm, g0_hbm), (p1_hbm, g1_hbm)):
            pltpu.sync_copy(p_hbm.at[pl.ds(off, SC_CHUNK)], i_v)
            pltpu.async_copy(y_hbm.at[i_v], rows_v, sem).wait()
            pltpu.sync_copy(rows_v, g_hbm.at[pl.ds(off, SC_CHUNK)])


def _sc_mesh():
    return plsc.VectorSubcoreMesh(core_axis_name="c", subcore_axis_name="s")


def _dispatch(hp, pos0, pos1):
    return pl.kernel(
        _sc_dispatch_body,
        out_type=jax.ShapeDtypeStruct((N_ASSIGN, PACK_W), I32),
        mesh=_sc_mesh(),
        scratch_types=[pltpu.VMEM((SC_CHUNK,), I32), pltpu.VMEM((SC_CHUNK,), I32),
                       pltpu.VMEM((SC_CHUNK, PACK_W), I32), pltpu.SemaphoreType.DMA],
        name="moe_dispatch",
    )(hp, pos0, pos1)


def _combine(ys, pos0, pos1):
    row = jax.ShapeDtypeStruct((N_TOK, PACK_W), I32)
    return pl.kernel(
        _sc_combine_body,
        out_type=[row, row],
        mesh=_sc_mesh(),
        scratch_types=[pltpu.VMEM((SC_CHUNK,), I32), pltpu.VMEM((SC_CHUNK, PACK_W), I32),
                       pltpu.SemaphoreType.DMA],
        name="moe_combine",
    )(ys, pos0, pos1)


def _expert_kernel(tile_ref, exp_ref, lo_ref, hi_ref, x_ref, w1_ref, w3_ref, w2_ref, o_ref, w13_scr, w2_scr):
    i = pl.program_id(0)
    prev = jnp.maximum(i - 1, 0)

    @pl.when(jnp.logical_or(i == 0, exp_ref[i] != exp_ref[prev]))
    def _():
        w13_scr[:, 0:EXPERT_FF] = w1_ref[0, 0].astype(BF16)
        w13_scr[:, EXPERT_FF:] = w3_ref[0, 0].astype(BF16)
        w2_scr[...] = w2_ref[0, 0].astype(BF16)

    lo_f, hi_f = _unpack_rows(x_ref[...])
    h = jnp.concatenate([lo_f.astype(BF16), hi_f.astype(BF16)], axis=1)
    ab = jnp.dot(h, w13_scr[...], preferred_element_type=F32)
    a = ab[:, 0:EXPERT_FF]
    b = ab[:, EXPERT_FF:]
    t = ((a * jax.nn.sigmoid(a)) * b).astype(BF16)
    y = _pack_rows(jnp.dot(t, w2_scr[...], preferred_element_type=F32))
    row = lax.broadcasted_iota(jnp.int32, (EXP_TILE, PACK_W), 0)
    mine = jnp.logical_and(row >= lo_ref[i], row < hi_ref[i])
    revisit = jnp.logical_and(i > 0, tile_ref[i] == tile_ref[prev])

    @pl.when(jnp.logical_not(revisit))
    def _():
        o_ref[...] = jnp.where(mine, y, 0)

    @pl.when(revisit)
    def _():
        o_ref[...] = jnp.where(mine, y, o_ref[...])


def _experts(l, xs_sorted, items, w1, w3, w2):
    tile, e, lo, hi = items
    grid_spec = pltpu.PrefetchScalarGridSpec(
        num_scalar_prefetch=4,
        grid=(N_ITEMS,),
        in_specs=[pl.BlockSpec((EXP_TILE, PACK_W), lambda i, t, e, lo, hi: (t[i], 0)),
                  pl.BlockSpec((1, 1, D_MODEL, EXPERT_FF), lambda i, t, e, lo, hi: (l, e[i], 0, 0)),
                  pl.BlockSpec((1, 1, D_MODEL, EXPERT_FF), lambda i, t, e, lo, hi: (l, e[i], 0, 0)),
                  pl.BlockSpec((1, 1, EXPERT_FF, D_MODEL), lambda i, t, e, lo, hi: (l, e[i], 0, 0))],
        out_specs=pl.BlockSpec((EXP_TILE, PACK_W), lambda i, t, e, lo, hi: (t[i], 0)),
        scratch_shapes=[pltpu.VMEM((D_MODEL, 2 * EXPERT_FF), BF16), pltpu.VMEM((EXPERT_FF, D_MODEL), BF16)],
    )
    return pl.pallas_call(
        _expert_kernel,
        grid_spec=grid_spec,
        out_shape=jax.ShapeDtypeStruct((N_ASSIGN, PACK_W), I32),
        compiler_params=_cparams(("arbitrary",)),
        name="moe_experts",
    )(tile, e, lo, hi, xs_sorted, w1, w3, w2)


def _moe(l, hp, route, tri, w1, w3, w2):
    counts = jnp.sum(route[:, N_EXPERTS:, :], axis=(0, 2))
    start = (jnp.cumsum(counts) - counts).reshape(N_EXPERTS, 1)
    pos, wt = _positions(route, tri, start)
    pos0 = pos[:, 0, :].reshape(N_TOK)
    pos1 = pos[:, 1, :].reshape(N_TOK)
    items = _work_items(counts.astype(I32))
    xs_sorted = _dispatch(hp.reshape(N_TOK, PACK_W), pos0, pos1)
    ys = _experts(l, xs_sorted, items, w1, w3, w2)
    g0, g1 = _combine(ys, pos0, pos1)
    shape = (SB, L_TOT, PACK_W)
    return g0.reshape(shape), g1.reshape(shape), wt.reshape(SB, L_TOT, LANES)


def _final_kernel(*refs):
    n_in = 4
    stream_refs = [refs[k * n_in:(k + 1) * n_in] for k in range(STREAMS)]
    mod_ref, o_ref, scr = refs[STREAMS * n_in:]
    for k in range(STREAMS):
        @pl.when(pl.program_id(0) == k)
        def _(k=k):
            x_ref, g0_ref, g1_ref, wt_ref = stream_refs[k]
            for b in range(SB):
                x = _moe_residual(x_ref[b], g0_ref[b], g1_ref[b], wt_ref[b], mod_ref[0, b][5:6, :])
                o_ref[b] = _from_scan_major(x, scr)


def _final(streams, mod):
    def lat(k, width):
        park_j = N_TILES - 1 if k == 0 else 1
        return pl.BlockSpec((SB, TILE, width), lambda s, j: (0, jnp.where(s == k, j + 1, park_j), 0))

    in_specs, args = [], []
    for k, st in enumerate(streams):
        in_specs += [lat(k, D_MODEL), lat(k, PACK_W), lat(k, PACK_W), lat(k, LANES)]
        args += [st["xs"]] + list(st["moe"])
    in_specs.append(pl.BlockSpec((1, SB, N_MOD, D_MODEL), lambda s, j: (DEPTH - 1, s, 0, 0)))
    return pl.pallas_call(
        _final_kernel,
        grid=(STREAMS, N_LAT_TILES),
        in_specs=in_specs,
        out_specs=pl.BlockSpec((SB, TILE, D_MODEL), lambda s, j: (s, j, 0)),
        out_shape=jax.ShapeDtypeStruct((BATCH, SEQ, D_MODEL), F32),
        scratch_shapes=[pltpu.VMEM((D_MODEL // LANES, TILE, LANES), F32)],
        compiler_params=_cparams(("arbitrary", "arbitrary")),
        name="final_residual",
    )(*args, mod)


def _bias_kernel(t4_ref, o_ref, scr):
    rows_q = TILE // GRID_W
    lane = lax.broadcasted_iota(jnp.int32, (GROUPS, TILE), 1)
    ka = (lane >> 1) & (rows_q - 1)
    neg = jnp.full((GROUPS, TILE), NEG_INF, F32)
    for a in range(rows_q):
        for qs in range(SUB // rows_q):
            s = (SUB // rows_q) * a + qs
            for kt in range(3):
                src = t4_ref[0, rows_q * kt - a + 3, qs * GROUPS:(qs + 1) * GROUPS, :]
                c = rows_q * kt + ka
                variants = (
                    src if kt >= 1 else neg,
                    jnp.where(jnp.logical_and(c >= a, c <= a + NA_WIN_ROWS - 1), src, neg),
                    src if kt <= 1 else neg,
                )
                for v, val in enumerate(variants):
                    for half in range(TILE // LANES):
                        scr[v, 2 * kt + half, pl.ds(s, GROUPS, stride=SUB), :] = val[:, half * LANES:(half + 1) * LANES]
    for v in range(3):
        o_ref[v, 0] = jnp.concatenate([scr[v, j] for j in range(3 * TILE // LANES)], axis=1)
    o_ref[3, 0] = jnp.full((TILE, 3 * TILE), NEG_INF, F32)


def _bias_tiles(table):
    rows_q = TILE // GRID_W
    n_r0 = 3 * rows_q
    qc = np.arange(GRID_W)[:, None]
    kc = np.arange(GRID_W)[None, :]
    col0 = np.clip(qc - NA_WIN_COLS // 2, 0, GRID_W - NA_WIN_COLS)
    in_win = (kc >= col0) & (kc < col0 + NA_WIN_COLS)
    pad = GRID_W - NA_WIN_COLS
    tp = jnp.pad(table.astype(F32) * LOG2E, ((0, 0), (0, 0), (pad, pad)))
    toeplitz = jnp.stack([tp[:, :, GRID_W - 1 - q:2 * GRID_W - 1 - q] for q in range(GRID_W)], axis=2)
    toeplitz = jnp.where(in_win[None, None], toeplitz, NEG_INF)
    n_h = table.shape[0]
    half = GRID_W // GROUPS
    t4 = jnp.stack([toeplitz[:, k:k + n_r0] for k in range(rows_q)], axis=-1)
    t4 = t4.reshape(n_h, n_r0, GRID_W, half, GROUPS, rows_q).transpose(0, 1, 2, 4, 5, 3)
    t4 = t4.reshape(n_h, n_r0, GRID_W, TILE)
    return pl.pallas_call(
        _bias_kernel,
        grid=(n_h,),
        in_specs=[pl.BlockSpec((1, n_r0, GRID_W, TILE), lambda h: (h, 0, 0, 0))],
        out_specs=pl.BlockSpec((4, 1, TILE, 3 * TILE), lambda h: (0, h, 0, 0)),
        out_shape=jax.ShapeDtypeStruct((4, n_h, TILE, 3 * TILE), F32),
        scratch_shapes=[pltpu.VMEM((3, 3 * TILE // LANES, TILE, LANES), F32)],
        compiler_params=_cparams(("parallel",)),
        name="bias_tiles",
    )(t4)


def _block_diag(w, n_chunks):
    per = LRU_BLOCKS // n_chunks
    w = w.reshape(2, n_chunks, per, LRU_BLOCK, LRU_BLOCK)
    eye = jnp.eye(per, dtype=w.dtype)
    out = jnp.einsum('dcpij,pq->dcpiqj', w, eye)
    return out.reshape(2, n_chunks, per * LRU_BLOCK, per * LRU_BLOCK)


def kernel(x, c, ctx, c_ctx, w_mod, b_mod, norm_mix, norm_ffn, w_in, w_out, q_gain, k_gain, na_bias,
           conv_w, conv_b, lru_w_r, lru_b_r, lru_w_i, lru_b_i, lru_lambda, router_w, router_b,
           exp_w1, exp_w3, exp_w2):
    cs = jnp.concatenate([c, c_ctx[None, :], jnp.zeros((MOD_ROWS - BATCH - 1, D_MODEL), F32)], axis=0)
    mod = _modulation(cs, w_mod, b_mod).reshape(DEPTH, MOD_ROWS, N_MOD, D_MODEL)

    head_of = np.arange(NA_WIDTH) // HEAD_DIM
    bd = jnp.asarray((head_of[:, None] == head_of[None, :]).astype(np.float32) / HEAD_DIM, BF16)
    tri = jnp.asarray(np.triu(np.ones((TILE, TILE), np.float32), 1), BF16)
    rwt = router_w.T
    rwh = rwt.astype(BF16)
    rwc = jnp.concatenate([rwh, (rwt - rwh.astype(F32)).astype(BF16)], axis=0)
    rb = router_b.reshape(N_EXPERTS, 1)
    n_cb = LRU_WIDTH // LRU_CH

    bias_tiles = _bias_tiles(na_bias.reshape(DEPTH * NA_HEADS, 2 * NA_WIN_ROWS - 1, 2 * NA_WIN_COLS - 1))
    streams = [{"xs": (x, ctx), "moe": None, "boff": sidx * SB} for sidx in range(STREAMS)]
    for l in range(DEPTH):
        qg = jnp.tile(q_gain[l] * (ATTN_SCALE * LOG2E), NA_HEADS)[None, :]
        kg = jnp.tile(k_gain[l], NA_HEADS)[None, :]
        w_in_l = w_in[l].astype(BF16)
        w_out_l = w_out[l].astype(BF16)
        wr = (0.5 * _block_diag(lru_w_r[l], n_cb)).astype(BF16)
        wi = (0.5 * _block_diag(lru_w_i[l], n_cb)).astype(BF16)
        lru = (conv_w[l], conv_b[l][None, :], wr, wi, 0.5 * lru_b_r[l], 0.5 * lru_b_i[l], lru_lambda[l])
        for st in streams:
            boff = st["boff"]
            xs, proj = _in_proj(l, boff, st["xs"], st["moe"], mod, norm_mix[l][None, :], w_in_l, bd, qg, kg)
            ya, hc = _attention(l, proj, bias_tiles, lru)
            xs, hp, route = _out_proj(l, boff, xs, ya, hc, proj, lru, mod, norm_ffn[l][None, :], w_out_l, rwc, rb)
            st["xs"] = xs
            st["moe"] = _moe(l, hp, route.reshape(N_TOK_TILES, ROUTE_ROWS, TILE), tri, exp_w1, exp_w3, exp_w2)
    return _final(streams, mod)
```

```python
import functools

import jax
import jax.numpy as jnp
import numpy as np
from jax import lax
from jax.experimental import pallas as pl
from jax.experimental.pallas import tpu as pltpu
from jax.experimental.pallas import tpu_sc as plsc

F32 = jnp.float32
BF16 = jnp.bfloat16
I32 = jnp.int32

D_MODEL = 1024
BATCH = 4
SEQ = 8192
DEPTH = 4
GRID_W = 64
CTX_LEN = 256
HEAD_DIM = 64
NA_WIDTH = 512
NA_HEADS = 8
NA_WIN_ROWS = 8
NA_WIN_COLS = 16
LRU_WIDTH = 512
LRU_BLOCKS = 8
LRU_BLOCK = 64
CONV_W = 4
LRU_C = 8.0
IN_COLS = 3 * NA_WIDTH + 2 * LRU_WIDTH
COL_KV, COL_Q, COL_U, COL_G = 0, 2, 3, 4
N_EXPERTS = 16
N_GROUPS = 4
EXPERTS_PER_GROUP = 4
TOP_K = 2
EXPERT_FF = 512
N_MOD = 6
ATTN_SCALE = HEAD_DIM ** -0.5
LOG2E = 1.4426950408889634
EPS = 1e-6
NEG_INF = -1e30
TINY = 1e-30
GELU_C = 0.7978845608028654

TILE = 256
SUB = 8
LANES = 128
GROUPS = TILE // SUB
L_TOT = CTX_LEN + SEQ
N_TILES = L_TOT // TILE
N_LAT_TILES = SEQ // TILE
STREAMS = 2
SB = BATCH // STREAMS
N_TOK = SB * L_TOT
N_TOK_TILES = N_TOK // TILE
PAIR = 2
LRU_CH = 256
MOD_ROWS = 8
VMEM_LIMIT = 56 * 1024 * 1024

PACK_W = D_MODEL // 2
HI_MASK = -65536
N_ASSIGN = TOP_K * N_TOK
EXP_TILE = 512
N_EXP_TILES = N_ASSIGN // EXP_TILE
N_ITEMS = N_EXP_TILES + N_EXPERTS - 1
ROUTE_ROWS = 2 * N_EXPERTS
POS_TILES = 22

SC_CORES = 2
SC_SUBCORES = 16
SC_WORKERS = SC_CORES * SC_SUBCORES
SC_ROWS = N_TOK // SC_WORKERS
SC_CHUNK = 88
SC_N_CHUNKS = SC_ROWS // SC_CHUNK


def _cparams(sem):
    return pltpu.CompilerParams(dimension_semantics=sem, vmem_limit_bytes=VMEM_LIMIT)


def _pack_rows(v):
    lo = pltpu.bitcast(v[:, :PACK_W].astype(BF16).astype(F32), I32)
    hi = pltpu.bitcast(v[:, PACK_W:].astype(BF16).astype(F32), I32)
    return ((lo >> 16) & 0xFFFF) | (hi & HI_MASK)


def _unpack_rows(p):
    return pltpu.bitcast(p << 16, F32), pltpu.bitcast(p & HI_MASK, F32)


def _mod_kernel(c_ref, w_ref, b_ref, o_ref):
    c = c_ref[...]
    s = c * jax.nn.sigmoid(c)
    o_ref[0] = jnp.dot(s.astype(BF16), w_ref[0].astype(BF16), preferred_element_type=F32) + b_ref[0]


def _modulation(cs, w_mod, b_mod):
    return pl.pallas_call(
        _mod_kernel,
        grid=(DEPTH, N_MOD),
        in_specs=[
            pl.BlockSpec((MOD_ROWS, D_MODEL), lambda l, n: (0, 0)),
            pl.BlockSpec((1, D_MODEL, D_MODEL), lambda l, n: (l, 0, n)),
            pl.BlockSpec((1, 1, D_MODEL), lambda l, n: (l, 0, n)),
        ],
        out_specs=pl.BlockSpec((1, MOD_ROWS, D_MODEL), lambda l, n: (l, 0, n)),
        out_shape=jax.ShapeDtypeStruct((DEPTH, MOD_ROWS, N_MOD * D_MODEL), F32),
        compiler_params=_cparams(("arbitrary", "arbitrary")),
        name="modulation",
    )(cs, w_mod, b_mod.reshape(DEPTH, 1, N_MOD * D_MODEL))


def _mod_row(b, i):
    return jnp.where(i == 0, BATCH, b)


def _moe_residual(x, g0, g1, wt, gate_row):
    lo0, hi0 = _unpack_rows(g0)
    lo1, hi1 = _unpack_rows(g1)
    w0 = wt[:, 0:1]
    w1 = wt[:, 1:2]
    f = jnp.concatenate([w0 * lo0 + w1 * lo1, w0 * hi0 + w1 * hi1], axis=1)
    return x + gate_row * f


def _pair_tok_spec(width):
    return pl.BlockSpec((PAIR, TILE, width), lambda bp, i: (bp, i, 0))


def _pair_mod_specs(layer, boff):
    return [pl.BlockSpec((1, 1, N_MOD, D_MODEL),
                         lambda bp, i, k=k: (layer, _mod_row(boff + PAIR * bp + k, i), 0, 0))
            for k in range(PAIR)]


def _full_spec(shape):
    return pl.BlockSpec(shape, lambda bp, i: tuple(0 for _ in shape))


def _to_scan_major(src_ref, scr):
    n_slab = D_MODEL // LANES
    for s in range(SUB):
        for j in range(n_slab):
            scr[j, pl.ds(s, GROUPS, stride=SUB), :] = src_ref[s * GROUPS:(s + 1) * GROUPS, j * LANES:(j + 1) * LANES]
    return jnp.concatenate([scr[j] for j in range(n_slab)], axis=1)


def _from_scan_major(val, scr):
    n_slab = D_MODEL // LANES
    for j in range(n_slab):
        scr[j] = val[:, j * LANES:(j + 1) * LANES]
    blocks = [jnp.concatenate([scr[j, pl.ds(s, GROUPS, stride=SUB), :] for j in range(n_slab)], axis=1)
              for s in range(SUB)]
    return jnp.concatenate(blocks, axis=0)


def _in_kernel(has_prev, *refs):
    if has_prev:
        x_ref, g0_ref, g1_ref, wt_ref = refs[:4]
        mprev_refs = refs[4:4 + PAIR]
        refs = (x_ref,) + refs[4 + PAIR:]
    else:
        x_ref, ctx_ref = refs[:2]
        refs = (x_ref,) + refs[2:]
    mod_refs = refs[1:1 + PAIR]
    nrm_ref, w_ref, bd_ref, qg_ref, kg_ref = refs[1 + PAIR:6 + PAIR]
    outs = refs[6 + PAIR:]
    xo_ref, proj_ref = outs[:2]
    x_ref = refs[0]
    hs = []
    for k in range(PAIR):
        if has_prev:
            x = _moe_residual(x_ref[k], g0_ref[k], g1_ref[k], wt_ref[k], mprev_refs[k][0, 0][5:6, :])
            xo_ref[k] = x
        else:
            perm_scr = outs[2]

            @pl.when(pl.program_id(1) == 0)
            def _():
                xo_ref[k] = _to_scan_major(ctx_ref.at[k], perm_scr)

            @pl.when(pl.program_id(1) > 0)
            def _():
                xo_ref[k] = _to_scan_major(x_ref.at[k], perm_scr)

            x = xo_ref[k]
        m = mod_refs[k][0, 0]
        ms = jnp.mean(x * x, axis=-1, keepdims=True)
        h = (x * lax.rsqrt(ms + EPS)) * nrm_ref[...]
        hs.append((h * (1.0 + m[1:2, :]) + m[0:1, :]).astype(BF16))
    acc = jnp.dot(jnp.concatenate(hs, axis=0), w_ref[...], preferred_element_type=F32)
    bd = bd_ref[...]

    def head_norm(t, gain):
        ss = jnp.dot((t * t).astype(BF16), bd, preferred_element_type=F32)
        return (t * lax.rsqrt(ss + EPS)) * gain

    def put(col_block, val):
        for k in range(PAIR):
            proj_ref[k, :, col_block * NA_WIDTH:(col_block + 1) * NA_WIDTH] = val[k * TILE:(k + 1) * TILE, :].astype(BF16)

    put(COL_Q, head_norm(acc[:, 0:NA_WIDTH], qg_ref[...]))
    put(COL_KV, head_norm(acc[:, NA_WIDTH:2 * NA_WIDTH], kg_ref[...]))
    put(COL_KV + 1, acc[:, 2 * NA_WIDTH:3 * NA_WIDTH])
    put(COL_U, acc[:, 3 * NA_WIDTH:3 * NA_WIDTH + LRU_WIDTH])
    put(COL_G, acc[:, 3 * NA_WIDTH + LRU_WIDTH:])


def _in_proj(l, boff, xs, moe_prev, mod, nrm, w_in, bd, qg, kg):
    has_prev = moe_prev is not None
    if has_prev:
        in_specs = [_pair_tok_spec(D_MODEL)]
        args = [xs]
        in_specs += [_pair_tok_spec(PACK_W), _pair_tok_spec(PACK_W), _pair_tok_spec(LANES)]
        in_specs += _pair_mod_specs(l - 1, boff)
        args += list(moe_prev) + [mod] * PAIR
    else:
        pair0 = boff // PAIR
        in_specs = [pl.BlockSpec((PAIR, TILE, D_MODEL), lambda bp, i: (pair0 + bp, jnp.maximum(i - 1, 0), 0)),
                    pl.BlockSpec((PAIR, TILE, D_MODEL), lambda bp, i: (pair0 + bp, 0, 0))]
        args = list(xs)
    in_specs += _pair_mod_specs(l, boff) + [_full_spec((1, D_MODEL)), _full_spec((D_MODEL, IN_COLS)),
                                            _full_spec((NA_WIDTH, NA_WIDTH)), _full_spec((1, NA_WIDTH)),
                                            _full_spec((1, NA_WIDTH))]
    args += [mod] * PAIR + [nrm, w_in, bd, qg, kg]
    out_shape = [jax.ShapeDtypeStruct((SB, L_TOT, D_MODEL), F32), jax.ShapeDtypeStruct((SB, L_TOT, IN_COLS), BF16)]
    out_specs = [_pair_tok_spec(D_MODEL), _pair_tok_spec(IN_COLS)]
    scratch = [] if has_prev else [pltpu.VMEM((D_MODEL // LANES, TILE, LANES), F32)]
    outs = pl.pallas_call(
        functools.partial(_in_kernel, has_prev),
        grid=(SB // PAIR, N_TILES),
        in_specs=in_specs,
        out_specs=out_specs,
        out_shape=out_shape,
        scratch_shapes=scratch,
        compiler_params=_cparams(("parallel", "arbitrary")),
        name="in_proj",
    )(*args)
    return outs[0], outs[1]


def _attn_kernel(q_ref, kvp_ref, kvc_ref, kvn_ref, kvx_ref, bias_ref,
                 u_ref, up_ref, un_ref, cw_ref, cb_ref, wr_ref, wi_ref, br_ref, bi_ref, lam_ref,
                 o_ref, hc_ref, h_scr, hl_scr, p_scr):
    j = pl.program_id(1)

    @pl.when(j == 0)
    def _():
        h_scr[...] = jnp.zeros_like(h_scr)

    has_prev = jnp.where(j >= 2, 1.0, 0.0).astype(F32)
    has_next = jnp.where(jnp.logical_and(j >= 1, j <= N_TILES - 2), 1.0, 0.0).astype(F32)
    for b in range(PAIR):
        cv = _lru_conv(u_ref[b].astype(F32), up_ref[b].astype(F32), un_ref[b].astype(F32), has_prev, has_next,
                       cw_ref[...], cb_ref[...])
        hc_ref[b, :, LRU_WIDTH:] = cv.astype(BF16)
        a_f, b_f = _lru_gates(cv, 0, wr_ref, wi_ref, br_ref, bi_ref, lam_ref)
        hfull, h_last = _lru_scan(a_f, b_f, h_scr[b], False, hl_scr, p_scr)
        hc_ref[b, :, 0:LRU_WIDTH] = hfull.astype(BF16)
        h_scr[b] = h_last

    pair_w = 2 * HEAD_DIM
    lane = lax.broadcasted_iota(jnp.int32, (TILE, pair_w), 1)
    low = lane < HEAD_DIM
    kv_refs = (kvp_ref, kvc_ref, kvn_ref, kvx_ref)
    nt = (((1,), (1,)), ((), ()))
    n_win = 3 * TILE
    low_kv = lax.broadcasted_iota(jnp.int32, (4 * TILE, pair_w), 1) < HEAD_DIM
    for b, hp in [(b, hp) for b in range(PAIR) for hp in range(NA_HEADS // 2)]:
        cols = slice(hp * pair_w, (hp + 1) * pair_w)
        q = q_ref[b, :, cols]
        vcols = slice(NA_WIDTH + hp * pair_w, NA_WIDTH + (hp + 1) * pair_w)
        kb = jnp.concatenate([r[b, :, cols] for r in kv_refs], axis=0)
        vb = jnp.concatenate([r[b, :, vcols] for r in kv_refs], axis=0)
        outs = []
        for hh in range(2):
            own = low if hh == 0 else jnp.logical_not(low)
            own_kv = low_kv if hh == 0 else jnp.logical_not(low_kv)
            qh = jnp.where(own, q, jnp.zeros_like(q))
            s = lax.dot_general(qh, kb, nt, preferred_element_type=F32)
            s_win = s[:, 0:n_win] + bias_ref[0, 2 * hp + hh]
            s_ctx = s[:, n_win:]
            m = jnp.maximum(jnp.max(s_win, axis=-1, keepdims=True), jnp.max(s_ctx, axis=-1, keepdims=True))
            p = jnp.concatenate([jnp.exp2((s_win - m).astype(BF16)), jnp.exp2((s_ctx - m).astype(BF16))], axis=1)
            va = jnp.where(own_kv, vb, jnp.ones_like(vb))
            o = jnp.dot(p, va, preferred_element_type=F32)
            outs.append(o / pltpu.roll(o, HEAD_DIM, 1))
        o_ref[b, :, cols] = jnp.where(low, outs[0], outs[1]).astype(BF16)


def _attention(l, proj, bias_tiles, lru):
    last = N_LAT_TILES - 1
    blk = (PAIR, TILE, 2 * NA_WIDTH)
    prev_map = lambda b, j: (b, 1 + jnp.clip(j - 2, 0, last), COL_KV // 2)
    cur_map = lambda b, j: (b, jnp.maximum(j, 1), COL_KV // 2)
    next_map = lambda b, j: (b, 1 + jnp.clip(j, 0, last), COL_KV // 2)
    ctx_map = lambda b, j: (b, 0, COL_KV // 2)
    var_map = lambda b, j: (jnp.where(j == 0, 3, jnp.where(j == 1, 0, jnp.where(j == N_TILES - 1, 2, 1))), l, 0, 0)
    kv_specs = [pl.BlockSpec(blk, prev_map), pl.BlockSpec(blk, cur_map), pl.BlockSpec(blk, next_map),
                pl.BlockSpec(blk, ctx_map)]
    tok = lambda col: pl.BlockSpec((PAIR, TILE, NA_WIDTH), lambda b, j: (b, j, col))
    halo = TILE // 16
    n_halo = L_TOT // 16
    prev16 = pl.BlockSpec((PAIR, 16, LRU_WIDTH), lambda b, j: (b, jnp.maximum(j * halo - 1, 0), COL_U))
    next16 = pl.BlockSpec((PAIR, 16, LRU_WIDTH), lambda b, j: (b, jnp.minimum((j + 1) * halo, n_halo - 1), COL_U))
    whole = lambda shape: pl.BlockSpec(shape, lambda b, j: tuple(0 for _ in shape))
    n_cb = LRU_WIDTH // LRU_CH
    lru_specs = [whole((CONV_W, LRU_WIDTH)), whole((1, LRU_WIDTH)), whole((2, n_cb, LRU_CH, LRU_CH)),
                 whole((2, n_cb, LRU_CH, LRU_CH)), whole((2, LRU_WIDTH)), whole((2, LRU_WIDTH)), whole((2, LRU_WIDTH))]
    return pl.pallas_call(
        _attn_kernel,
        grid=(SB // PAIR, N_TILES),
        in_specs=[tok(COL_Q)] + kv_specs + [pl.BlockSpec((1, NA_HEADS, TILE, 3 * TILE), var_map)]
        + [tok(COL_U), prev16, next16] + lru_specs,
        out_specs=[tok(0), pl.BlockSpec((PAIR, TILE, 2 * LRU_WIDTH), lambda b, j: (b, j, 0))],
        out_shape=[jax.ShapeDtypeStruct((SB, L_TOT, NA_WIDTH), BF16),
                   jax.ShapeDtypeStruct((SB, L_TOT, 2 * LRU_WIDTH), BF16)],
        scratch_shapes=[pltpu.VMEM((PAIR, 1, LRU_WIDTH), F32), pltpu.VMEM((TILE, LRU_WIDTH), F32),
                        pltpu.VMEM((TILE, LRU_WIDTH), F32)],
        compiler_params=_cparams(("parallel", "arbitrary")),
        name="na_attention",
    )(proj, proj, proj, proj, proj, bias_tiles, proj, proj, proj, *lru)


def _softplus(x):
    return jnp.maximum(x, 0.0) + jnp.log1p(jnp.exp(-jnp.abs(x)))


def _lru_conv(u, prev16, next16, has_prev, has_next, cw, cb):
    sub = lax.broadcasted_iota(jnp.int32, (SUB, u.shape[1]), 0)
    prow = prev16[15:16, :] * has_prev
    n0 = next16[0:1, :] * has_next
    n8 = next16[8:9, :] * has_next
    first8 = jnp.where(sub == 0, prow, pltpu.roll(u[TILE - SUB:TILE, :], 1, 0))
    last_a = jnp.where(sub == SUB - 1, n0, pltpu.roll(u[0:SUB, :], SUB - 1, 0))
    last_b = jnp.where(sub == SUB - 1, n8, pltpu.roll(u[SUB:2 * SUB, :], SUB - 1, 0))
    um1 = jnp.concatenate([first8, u[0:TILE - SUB, :]], axis=0)
    up1 = jnp.concatenate([u[SUB:TILE, :], last_a], axis=0)
    up2 = jnp.concatenate([u[2 * SUB:TILE, :], last_a, last_b], axis=0)
    return cw[0:1, :] * um1 + cw[1:2, :] * u + cw[2:3, :] * up1 + cw[3:4, :] * up2 + cb


def _lru_gates(v, d, wr_ref, wi_ref, br_ref, bi_ref, lam_ref):
    vb = v.astype(BF16)
    n_cb = LRU_WIDTH // LRU_CH

    def gate(w_ref, b_ref):
        z = [jnp.dot(vb[:, c * LRU_CH:(c + 1) * LRU_CH], w_ref[d, c], preferred_element_type=F32) for c in range(n_cb)]
        return jnp.tanh(jnp.concatenate(z, axis=1) + b_ref[d:d + 1, :])

    tr = gate(wr_ref, br_ref)
    ti = gate(wi_ref, bi_ref)
    half = (-0.5 * LRU_C * LOG2E) * _softplus(-lam_ref[d:d + 1, :])
    a = jnp.exp2(half + half * tr)
    om = 1.0 - a * a
    b = (om * lax.rsqrt(jnp.maximum(om, TINY))) * ((0.5 * v) * (1.0 + ti))
    return a, b


def _lru_scan(a, b, h_in, reverse, hl_scr, p_scr):
    order = range(GROUPS - 1, -1, -1) if reverse else range(GROUPS)
    hl = None
    for g in order:
        ag = a[g * SUB:(g + 1) * SUB, :]
        bg = b[g * SUB:(g + 1) * SUB, :]
        if hl is None:
            hl, p = bg, ag
        else:
            hl = ag * hl + bg
            p = ag * p
        hl_scr[g * SUB:(g + 1) * SUB, :] = hl
        p_scr[g * SUB:(g + 1) * SUB, :] = p
    blocks = range(SUB - 1, -1, -1) if reverse else range(SUB)
    carry = h_in
    cins = {}
    for s in blocks:
        cins[s] = carry
        carry = hl[s:s + 1, :] + p[s:s + 1, :] * carry
    cin = jnp.concatenate([cins[s] for s in range(SUB)], axis=0)
    hfull = hl_scr[...] + p_scr[...] * jnp.tile(cin, (GROUPS, 1))
    return hfull, carry


def _route(sel, aff):
    def top2_sum(a, b, c, d):
        hi1, lo1 = jnp.maximum(a, b), jnp.minimum(a, b)
        hi2, lo2 = jnp.maximum(c, d), jnp.minimum(c, d)
        return jnp.maximum(hi1, hi2) + jnp.maximum(jnp.minimum(hi1, hi2), jnp.maximum(lo1, lo2))

    scores = [top2_sum(*sel[EXPERTS_PER_GROUP * g:EXPERTS_PER_GROUP * (g + 1)]) for g in range(N_GROUPS)]
    best = jnp.zeros_like(scores[0], dtype=jnp.int32)
    best_v = scores[0]
    for g in range(1, N_GROUPS):
        upd = scores[g] > best_v
        best = jnp.where(upd, g, best)
        best_v = jnp.where(upd, scores[g], best_v)
    chosen = []
    for e in range(N_EXPERTS):
        g = e // EXPERTS_PER_GROUP
        rank = jnp.zeros_like(best)
        for o in range(EXPERTS_PER_GROUP * g, EXPERTS_PER_GROUP * (g + 1)):
            if o == e:
                continue
            ahead = sel[o] > sel[e]
            if o < e:
                ahead = jnp.logical_or(ahead, sel[o] == sel[e])
            rank = rank + ahead.astype(jnp.int32)
        chosen.append(jnp.logical_and(best == g, rank < TOP_K))
    total = jnp.zeros_like(aff[0])
    for e in range(N_EXPERTS):
        total = total + jnp.where(chosen[e], aff[e], 0.0)
    gates = [jnp.where(chosen[e], aff[e] / total, 0.0) for e in range(N_EXPERTS)]
    return gates, [c.astype(F32) for c in chosen]


def _out_kernel(x_ref, ya_ref, hc_ref, g_ref, *refs):
    mod_refs = refs[:PAIR]
    (wr_ref, wi_ref, br_ref, bi_ref, lam_ref, nrm_ref, w_ref, rwc_ref, rb_ref,
     xo_ref, hp_ref, rt_ref, h_scr, hl_scr, p_scr) = refs[PAIR:]
    rows = PAIR * TILE

    @pl.when(pl.program_id(1) == 0)
    def _():
        h_scr[...] = jnp.zeros_like(h_scr)

    ybs = []
    for k in range(PAIR):
        a_r, b_r = _lru_gates(hc_ref[k, :, LRU_WIDTH:].astype(F32), 1, wr_ref, wi_ref, br_ref, bi_ref, lam_ref)
        hrev, h_last = _lru_scan(a_r, b_r, h_scr[k], True, hl_scr, p_scr)
        h_scr[k] = h_last
        gx = g_ref[k].astype(F32)
        gate = (0.5 * gx) * (1.0 + jnp.tanh(gx * (GELU_C + (GELU_C * 0.044715) * (gx * gx))))
        ybs.append((gate * (hc_ref[k, :, 0:LRU_WIDTH].astype(F32) + hrev)).astype(BF16))
    ya = ya_ref[...].reshape(rows, NA_WIDTH)
    yb = jnp.concatenate(ybs, axis=0)
    y = jnp.dot(ya, w_ref[0:NA_WIDTH, :], preferred_element_type=F32)
    y = y + jnp.dot(yb, w_ref[NA_WIDTH:, :], preferred_element_type=F32)
    hs = []
    for k in range(PAIR):
        m = mod_refs[k][0, 0]
        x = x_ref[k] + m[2:3, :] * y[k * TILE:(k + 1) * TILE, :]
        xo_ref[k] = x
        ms = jnp.mean(x * x, axis=-1, keepdims=True)
        h = (x * lax.rsqrt(ms + EPS)) * nrm_ref[...]
        h = h * (1.0 + m[4:5, :]) + m[3:4, :]
        hp_ref[k] = _pack_rows(h)
        hs.append(h)
    h = jnp.concatenate(hs, axis=0)
    h_hi = h.astype(BF16)
    h_lo = (h - h_hi.astype(F32)).astype(BF16)
    nt = (((1,), (1,)), ((), ()))
    rwc = rwc_ref[...]
    both = lax.dot_general(rwc, h_hi, nt, preferred_element_type=F32)
    lg = (both[0:N_EXPERTS, :] + both[N_EXPERTS:, :]
          + lax.dot_general(rwc[0:N_EXPERTS, :], h_lo, nt, preferred_element_type=F32))
    aff_all = jax.nn.sigmoid(lg)
    sel_all = aff_all + rb_ref[...]
    aff = [aff_all[e:e + 1, :] for e in range(N_EXPERTS)]
    sel = [sel_all[e:e + 1, :] for e in range(N_EXPERTS)]
    gates, chosen = _route(sel, aff)
    rt = jnp.concatenate(gates + chosen, axis=0)
    for k in range(PAIR):
        rt_ref[k, 0] = rt[:, k * TILE:(k + 1) * TILE]


def _out_proj(l, boff, xs, ya, hc, proj, lru, mod, nrm, w_out, rwc, rb):
    rev = lambda i: jnp.where(i == 0, 0, N_TILES - i)
    tok = lambda width, col=0: pl.BlockSpec((PAIR, TILE, width), lambda bp, i: (bp, rev(i), col))
    mod_specs = [pl.BlockSpec((1, 1, N_MOD, D_MODEL),
                              lambda bp, i, k=k: (l, _mod_row(boff + PAIR * bp + k, i), 0, 0)) for k in range(PAIR)]
    n_cb = LRU_WIDTH // LRU_CH
    lru_specs = [_full_spec((2, n_cb, LRU_CH, LRU_CH)), _full_spec((2, n_cb, LRU_CH, LRU_CH)),
                 _full_spec((2, LRU_WIDTH)), _full_spec((2, LRU_WIDTH)), _full_spec((2, LRU_WIDTH))]
    return pl.pallas_call(
        _out_kernel,
        grid=(SB // PAIR, N_TILES),
        in_specs=[tok(D_MODEL), tok(NA_WIDTH), tok(2 * LRU_WIDTH), tok(LRU_WIDTH, COL_G)]
        + mod_specs + lru_specs
        + [_full_spec((1, D_MODEL)), _full_spec((D_MODEL, D_MODEL)), _full_spec((2 * N_EXPERTS, D_MODEL)),
           _full_spec((N_EXPERTS, 1))],
        out_specs=[tok(D_MODEL), tok(PACK_W),
                   pl.BlockSpec((PAIR, 1, ROUTE_ROWS, TILE), lambda bp, i: (bp, rev(i), 0, 0))],
        out_shape=[jax.ShapeDtypeStruct((SB, L_TOT, D_MODEL), F32),
                   jax.ShapeDtypeStruct((SB, L_TOT, PACK_W), I32),
                   jax.ShapeDtypeStruct((SB, N_TILES, ROUTE_ROWS, TILE), F32)],
        scratch_shapes=[pltpu.VMEM((PAIR, 1, LRU_WIDTH), F32), pltpu.VMEM((TILE, LRU_WIDTH), F32),
                        pltpu.VMEM((TILE, LRU_WIDTH), F32)],
        compiler_params=_cparams(("parallel", "arbitrary")),
        name="out_proj_router",
    )(xs, ya, hc, proj, *([mod] * PAIR), *lru[2:], nrm, w_out, rwc, rb)


def _pos_kernel(rt_ref, tri_ref, start_ref, pos_ref, wt_ref, run_scr):
    @pl.when(pl.program_id(0) == 0)
    def _():
        run_scr[...] = jnp.zeros_like(run_scr)

    base = start_ref[...] + run_scr[...]
    for k in range(POS_TILES):
        gates = rt_ref[k, 0:N_EXPERTS, :]
        chosen = rt_ref[k, N_EXPERTS:, :]
        rank = jnp.dot(chosen.astype(BF16), tri_ref[...], preferred_element_type=F32)
        posf = rank + base
        seen = jnp.zeros((1, TILE), F32)
        p0 = jnp.zeros((1, TILE), F32)
        p1 = jnp.zeros((1, TILE), F32)
        w0 = jnp.zeros((1, TILE), F32)
        w1 = jnp.zeros((1, TILE), F32)
        for e in range(N_EXPERTS):
            ch = chosen[e:e + 1, :]
            first = ch * (1.0 - seen)
            second = ch * seen
            p0 = p0 + first * posf[e:e + 1, :]
            p1 = p1 + second * posf[e:e + 1, :]
            w0 = w0 + first * gates[e:e + 1, :]
            w1 = w1 + second * gates[e:e + 1, :]
            seen = jnp.minimum(seen + ch, 1.0)
        pos_ref[k] = jnp.concatenate([p0, p1], axis=0).astype(I32)
        wpad = jnp.concatenate([w0, w1, jnp.zeros((LANES - TOP_K, TILE), F32)], axis=0)
        wt_ref[k * TILE:(k + 1) * TILE, :] = jnp.transpose(wpad)
        base = base + jnp.sum(chosen, axis=1, keepdims=True)
    run_scr[...] = base - start_ref[...]


def _positions(route, tri, start):
    return pl.pallas_call(
        _pos_kernel,
        grid=(N_TOK_TILES // POS_TILES,),
        in_specs=[pl.BlockSpec((POS_TILES, ROUTE_ROWS, TILE), lambda i: (i, 0, 0)),
                  pl.BlockSpec((TILE, TILE), lambda i: (0, 0)),
                  pl.BlockSpec((N_EXPERTS, 1), lambda i: (0, 0))],
        out_specs=[pl.BlockSpec((POS_TILES, TOP_K, TILE), lambda i: (i, 0, 0)),
                   pl.BlockSpec((POS_TILES * TILE, LANES), lambda i: (i, 0))],
        out_shape=[jax.ShapeDtypeStruct((N_TOK_TILES, TOP_K, TILE), I32),
                   jax.ShapeDtypeStruct((N_TOK, LANES), F32)],
        scratch_shapes=[pltpu.VMEM((N_EXPERTS, 1), F32)],
        compiler_params=_cparams(("arbitrary",)),
        name="moe_positions",
    )(route, tri, start)


def _work_items(counts):
    smem = pl.BlockSpec(memory_space=pltpu.SMEM)
    item = jax.ShapeDtypeStruct((N_ITEMS,), I32)
    return pl.pallas_call(
        _items_kernel,
        in_specs=[smem],
        out_specs=[smem] * 4,
        out_shape=[item] * 4,
        name="moe_work_items",
    )(counts)


def _items_kernel(cnt_ref, tile_ref, exp_ref, lo_ref, hi_ref):
    n = jnp.int32(0)
    start = jnp.int32(0)
    last_e = jnp.int32(0)
    for e in range(N_EXPERTS):
        cnt = cnt_ref[e]
        end = start + cnt
        first = start // EXP_TILE
        n_tiles = jnp.where(cnt > 0, (end - 1) // EXP_TILE - first + 1, 0)

        def put(j, carry, e=e, n=n, start=start, end=end, first=first):
            tile = first + j
            tile_ref[n + j] = tile
            exp_ref[n + j] = jnp.int32(e)
            lo_ref[n + j] = jnp.maximum(start - tile * EXP_TILE, 0)
            hi_ref[n + j] = jnp.minimum(end - tile * EXP_TILE, EXP_TILE)
            return carry

        lax.fori_loop(0, n_tiles, put, 0)
        n = n + n_tiles
        start = end
        last_e = jnp.where(cnt > 0, e, last_e)

    def pad(j, carry):
        tile_ref[j] = jnp.int32(N_EXP_TILES - 1)
        exp_ref[j] = last_e
        lo_ref[j] = jnp.int32(0)
        hi_ref[j] = jnp.int32(0)
        return carry

    lax.fori_loop(n, N_ITEMS, pad, 0)


def _sc_worker_base():
    return (lax.axis_index("s") * SC_CORES + lax.axis_index("c")) * SC_ROWS


def _sc_dispatch_body(h_hbm, p0_hbm, p1_hbm, out_hbm, i0_v, i1_v, rows_v, sem):
    base = _sc_worker_base()

    @pl.loop(0, SC_N_CHUNKS)
    def _(j):
        off = base + j * SC_CHUNK
        pltpu.sync_copy(h_hbm.at[pl.ds(off, SC_CHUNK)], rows_v)
        pltpu.sync_copy(p0_hbm.at[pl.ds(off, SC_CHUNK)], i0_v)
        pltpu.sync_copy(p1_hbm.at[pl.ds(off, SC_CHUNK)], i1_v)
        c0 = pltpu.async_copy(rows_v, out_hbm.at[i0_v], sem)
        c1 = pltpu.async_copy(rows_v, out_hbm.at[i1_v], sem)
        c0.wait()
        c1.wait()


def _sc_combine_body(y_hbm, p0_hbm, p1_hbm, g0_hbm, g1_hbm, i_v, rows_v, sem):
    base = _sc_worker_base()

    @pl.loop(0, SC_N_CHUNKS)
    def _(j):
        off = base + j * SC_CHUNK
        for p_hbm, g_hbm in ((p0_hbm, g0_hbm), (p1_hbm, g1_hbm)):
            pltpu.sync_copy(p_hbm.at[pl.ds(off, SC_CHUNK)], i_v)
            pltpu.async_copy(y_hbm.at[i_v], rows_v, sem).wait()
            pltpu.sync_copy(rows_v, g_hbm.at[pl.ds(off, SC_CHUNK)])


def _sc_mesh():
    return plsc.VectorSubcoreMesh(core_axis_name="c", subcore_axis_name="s")


def _dispatch(hp, pos0, pos1):
    return pl.kernel(
        _sc_dispatch_body,
        out_type=jax.ShapeDtypeStruct((N_ASSIGN, PACK_W), I32),
        mesh=_sc_mesh(),
        scratch_types=[pltpu.VMEM((SC_CHUNK,), I32), pltpu.VMEM((SC_CHUNK,), I32),
                       pltpu.VMEM((SC_CHUNK, PACK_W), I32), pltpu.SemaphoreType.DMA],
        name="moe_dispatch",
    )(hp, pos0, pos1)


def _combine(ys, pos0, pos1):
    row = jax.ShapeDtypeStruct((N_TOK, PACK_W), I32)
    return pl.kernel(
        _sc_combine_body,
        out_type=[row, row],
        mesh=_sc_mesh(),
        scratch_types=[pltpu.VMEM((SC_CHUNK,), I32), pltpu.VMEM((SC_CHUNK, PACK_W), I32),
                       pltpu.SemaphoreType.DMA],
        name="moe_combine",
    )(ys, pos0, pos1)


def _expert_kernel(tile_ref, exp_ref, lo_ref, hi_ref, x_ref, w1_ref, w3_ref, w2_ref, o_ref, w13_scr, w2_scr):
    i = pl.program_id(0)
    prev = jnp.maximum(i - 1, 0)

    @pl.when(jnp.logical_or(i == 0, exp_ref[i] != exp_ref[prev]))
    def _():
        w13_scr[:, 0:EXPERT_FF] = w1_ref[0, 0].astype(BF16)
        w13_scr[:, EXPERT_FF:] = w3_ref[0, 0].astype(BF16)
        w2_scr[...] = w2_ref[0, 0].astype(BF16)

    lo_f, hi_f = _unpack_rows(x_ref[...])
    h = jnp.concatenate([lo_f.astype(BF16), hi_f.astype(BF16)], axis=1)
    ab = jnp.dot(h, w13_scr[...], preferred_element_type=F32)
    a = ab[:, 0:EXPERT_FF]
    b = ab[:, EXPERT_FF:]
    t = ((a * jax.nn.sigmoid(a)) * b).astype(BF16)
    y = _pack_rows(jnp.dot(t, w2_scr[...], preferred_element_type=F32))
    row = lax.broadcasted_iota(jnp.int32, (EXP_TILE, PACK_W), 0)
    mine = jnp.logical_and(row >= lo_ref[i], row < hi_ref[i])
    revisit = jnp.logical_and(i > 0, tile_ref[i] == tile_ref[prev])

    @pl.when(jnp.logical_not(revisit))
    def _():
        o_ref[...] = jnp.where(mine, y, 0)

    @pl.when(revisit)
    def _():
        o_ref[...] = jnp.where(mine, y, o_ref[...])


def _experts(l, xs_sorted, items, w1, w3, w2):
    tile, e, lo, hi = items
    grid_spec = pltpu.PrefetchScalarGridSpec(
        num_scalar_prefetch=4,
        grid=(N_ITEMS,),
        in_specs=[pl.BlockSpec((EXP_TILE, PACK_W), lambda i, t, e, lo, hi: (t[i], 0)),
                  pl.BlockSpec((1, 1, D_MODEL, EXPERT_FF), lambda i, t, e, lo, hi: (l, e[i], 0, 0)),
                  pl.BlockSpec((1, 1, D_MODEL, EXPERT_FF), lambda i, t, e, lo, hi: (l, e[i], 0, 0)),
                  pl.BlockSpec((1, 1, EXPERT_FF, D_MODEL), lambda i, t, e, lo, hi: (l, e[i], 0, 0))],
        out_specs=pl.BlockSpec((EXP_TILE, PACK_W), lambda i, t, e, lo, hi: (t[i], 0)),
        scratch_shapes=[pltpu.VMEM((D_MODEL, 2 * EXPERT_FF), BF16), pltpu.VMEM((EXPERT_FF, D_MODEL), BF16)],
    )
    return pl.pallas_call(
        _expert_kernel,
        grid_spec=grid_spec,
        out_shape=jax.ShapeDtypeStruct((N_ASSIGN, PACK_W), I32),
        compiler_params=_cparams(("arbitrary",)),
        name="moe_experts",
    )(tile, e, lo, hi, xs_sorted, w1, w3, w2)


def _moe(l, hp, route, tri, w1, w3, w2):
    counts = jnp.sum(route[:, N_EXPERTS:, :], axis=(0, 2))
    start = (jnp.cumsum(counts) - counts).reshape(N_EXPERTS, 1)
    pos, wt = _positions(route, tri, start)
    pos0 = pos[:, 0, :].reshape(N_TOK)
    pos1 = pos[:, 1, :].reshape(N_TOK)
    items = _work_items(counts.astype(I32))
    xs_sorted = _dispatch(hp.reshape(N_TOK, PACK_W), pos0, pos1)
    ys = _experts(l, xs_sorted, items, w1, w3, w2)
    g0, g1 = _combine(ys, pos0, pos1)
    shape = (SB, L_TOT, PACK_W)
    return g0.reshape(shape), g1.reshape(shape), wt.reshape(SB, L_TOT, LANES)


def _final_kernel(x_ref, g0_ref, g1_ref, wt_ref, mod_ref, *refs):
    o_ref, scr = refs[-2:]
    for b in range(SB):
        x = _moe_residual(x_ref[b], g0_ref[b], g1_ref[b], wt_ref[b], mod_ref[0, b][5:6, :])
        o_ref[b] = _from_scan_major(x, scr)


def _final(streams, mod):
    lat = lambda width: pl.BlockSpec((SB, TILE, width), lambda j: (0, j + 1, 0))
    out = None
    for k, st in enumerate(streams):
        in_specs = [lat(D_MODEL), lat(PACK_W), lat(PACK_W), lat(LANES),
                    pl.BlockSpec((1, SB, N_MOD, D_MODEL), lambda j, k=k: (DEPTH - 1, k, 0, 0))]
        args = [st["xs"]] + list(st["moe"]) + [mod]
        aliases = {}
        if out is not None:
            in_specs.append(pl.BlockSpec(memory_space=pl.ANY))
            args.append(out)
            aliases = {len(args) - 1: 0}
        out = pl.pallas_call(
            _final_kernel,
            grid=(N_LAT_TILES,),
            in_specs=in_specs,
            out_specs=pl.BlockSpec((SB, TILE, D_MODEL), lambda j, k=k: (k, j, 0)),
            out_shape=jax.ShapeDtypeStruct((BATCH, SEQ, D_MODEL), F32),
            scratch_shapes=[pltpu.VMEM((D_MODEL // LANES, TILE, LANES), F32)],
            input_output_aliases=aliases,
            compiler_params=_cparams(("arbitrary",)),
            name="final_residual",
        )(*args)
    return out


def _bias_kernel(t4_ref, o_ref, scr):
    rows_q = TILE // GRID_W
    lane = lax.broadcasted_iota(jnp.int32, (GROUPS, TILE), 1)
    ka = (lane >> 1) & (rows_q - 1)
    neg = jnp.full((GROUPS, TILE), NEG_INF, F32)
    for a in range(rows_q):
        for qs in range(SUB // rows_q):
            s = (SUB // rows_q) * a + qs
            for kt in range(3):
                src = t4_ref[0, rows_q * kt - a + 3, qs * GROUPS:(qs + 1) * GROUPS, :]
                c = rows_q * kt + ka
                variants = (
                    src if kt >= 1 else neg,
                    jnp.where(jnp.logical_and(c >= a, c <= a + NA_WIN_ROWS - 1), src, neg),
                    src if kt <= 1 else neg,
                )
                for v, val in enumerate(variants):
                    for half in range(TILE // LANES):
                        scr[v, 2 * kt + half, pl.ds(s, GROUPS, stride=SUB), :] = val[:, half * LANES:(half + 1) * LANES]
    for v in range(3):
        o_ref[v, 0] = jnp.concatenate([scr[v, j] for j in range(3 * TILE // LANES)], axis=1)
    o_ref[3, 0] = jnp.full((TILE, 3 * TILE), NEG_INF, F32)


def _bias_tiles(table):
    rows_q = TILE // GRID_W
    n_r0 = 3 * rows_q
    qc = np.arange(GRID_W)[:, None]
    kc = np.arange(GRID_W)[None, :]
    col0 = np.clip(qc - NA_WIN_COLS // 2, 0, GRID_W - NA_WIN_COLS)
    in_win = (kc >= col0) & (kc < col0 + NA_WIN_COLS)
    pad = GRID_W - NA_WIN_COLS
    tp = jnp.pad(table.astype(F32) * LOG2E, ((0, 0), (0, 0), (pad, pad)))
    toeplitz = jnp.stack([tp[:, :, GRID_W - 1 - q:2 * GRID_W - 1 - q] for q in range(GRID_W)], axis=2)
    toeplitz = jnp.where(in_win[None, None], toeplitz, NEG_INF)
    n_h = table.shape[0]
    half = GRID_W // GROUPS
    t4 = jnp.stack([toeplitz[:, k:k + n_r0] for k in range(rows_q)], axis=-1)
    t4 = t4.reshape(n_h, n_r0, GRID_W, half, GROUPS, rows_q).transpose(0, 1, 2, 4, 5, 3)
    t4 = t4.reshape(n_h, n_r0, GRID_W, TILE)
    return pl.pallas_call(
        _bias_kernel,
        grid=(n_h,),
        in_specs=[pl.BlockSpec((1, n_r0, GRID_W, TILE), lambda h: (h, 0, 0, 0))],
        out_specs=pl.BlockSpec((4, 1, TILE, 3 * TILE), lambda h: (0, h, 0, 0)),
        out_shape=jax.ShapeDtypeStruct((4, n_h, TILE, 3 * TILE), F32),
        scratch_shapes=[pltpu.VMEM((3, 3 * TILE // LANES, TILE, LANES), F32)],
        compiler_params=_cparams(("parallel",)),
        name="bias_tiles",
    )(t4)


def _block_diag(w, n_chunks):
    per = LRU_BLOCKS // n_chunks
    w = w.reshape(2, n_chunks, per, LRU_BLOCK, LRU_BLOCK)
    eye = jnp.eye(per, dtype=w.dtype)
    out = jnp.einsum('dcpij,pq->dcpiqj', w, eye)
    return out.reshape(2, n_chunks, per * LRU_BLOCK, per * LRU_BLOCK)


def kernel(x, c, ctx, c_ctx, w_mod, b_mod, norm_mix, norm_ffn, w_in, w_out, q_gain, k_gain, na_bias,
           conv_w, conv_b, lru_w_r, lru_b_r, lru_w_i, lru_b_i, lru_lambda, router_w, router_b,
           exp_w1, exp_w3, exp_w2):
    cs = jnp.concatenate([c, c_ctx[None, :], jnp.zeros((MOD_ROWS - BATCH - 1, D_MODEL), F32)], axis=0)
    mod = _modulation(cs, w_mod, b_mod).reshape(DEPTH, MOD_ROWS, N_MOD, D_MODEL)

    head_of = np.arange(NA_WIDTH) // HEAD_DIM
    bd = jnp.asarray((head_of[:, None] == head_of[None, :]).astype(np.float32) / HEAD_DIM, BF16)
    tri = jnp.asarray(np.triu(np.ones((TILE, TILE), np.float32), 1), BF16)
    rwt = router_w.T
    rwh = rwt.astype(BF16)
    rwc = jnp.concatenate([rwh, (rwt - rwh.astype(F32)).astype(BF16)], axis=0)
    rb = router_b.reshape(N_EXPERTS, 1)
    n_cb = LRU_WIDTH // LRU_CH

    bias_tiles = _bias_tiles(na_bias.reshape(DEPTH * NA_HEADS, 2 * NA_WIN_ROWS - 1, 2 * NA_WIN_COLS - 1))
    streams = [{"xs": (x, ctx), "moe": None, "boff": sidx * SB} for sidx in range(STREAMS)]
    for l in range(DEPTH):
        qg = jnp.tile(q_gain[l] * (ATTN_SCALE * LOG2E), NA_HEADS)[None, :]
        kg = jnp.tile(k_gain[l], NA_HEADS)[None, :]
        w_in_l = w_in[l].astype(BF16)
        w_out_l = w_out[l].astype(BF16)
        wr = (0.5 * _block_diag(lru_w_r[l], n_cb)).astype(BF16)
        wi = (0.5 * _block_diag(lru_w_i[l], n_cb)).astype(BF16)
        lru = (conv_w[l], conv_b[l][None, :], wr, wi, 0.5 * lru_b_r[l], 0.5 * lru_b_i[l], lru_lambda[l])
        for st in streams:
            boff = st["boff"]
            xs, proj = _in_proj(l, boff, st["xs"], st["moe"], mod, norm_mix[l][None, :], w_in_l, bd, qg, kg)
            ya, hc = _attention(l, proj, bias_tiles, lru)
            xs, hp, route = _out_proj(l, boff, xs, ya, hc, proj, lru, mod, norm_ffn[l][None, :], w_out_l, rwc, rb)
            st["xs"] = xs
            st["moe"] = _moe(l, hp, route.reshape(N_TOK_TILES, ROUTE_ROWS, TILE), tri, exp_w1, exp_w3, exp_w2)
    return _final(streams, mod)
```

```python
import functools

import jax
import jax.numpy as jnp
import numpy as np
from jax import lax
from jax.experimental import pallas as pl
from jax.experimental.pallas import tpu as pltpu
from jax.experimental.pallas import tpu_sc as plsc

F32 = jnp.float32
BF16 = jnp.bfloat16
I32 = jnp.int32

D_MODEL = 1024
BATCH = 4
SEQ = 8192
DEPTH = 4
GRID_W = 64
CTX_LEN = 256
HEAD_DIM = 64
NA_WIDTH = 512
NA_HEADS = 8
NA_WIN_ROWS = 8
NA_WIN_COLS = 16
LRU_WIDTH = 512
LRU_BLOCKS = 8
LRU_BLOCK = 64
CONV_W = 4
LRU_C = 8.0
IN_COLS = 3 * NA_WIDTH + 2 * LRU_WIDTH
COL_KV, COL_Q, COL_U, COL_G = 0, 2, 3, 4
N_EXPERTS = 16
N_GROUPS = 4
EXPERTS_PER_GROUP = 4
TOP_K = 2
EXPERT_FF = 512
N_MOD = 6
ATTN_SCALE = HEAD_DIM ** -0.5
LOG2E = 1.4426950408889634
EPS = 1e-6
NEG_INF = -1e30
TINY = 1e-30
GELU_C = 0.7978845608028654

TILE = 256
SUB = 8
LANES = 128
GROUPS = TILE // SUB
L_TOT = CTX_LEN + SEQ
N_TILES = L_TOT // TILE
N_LAT_TILES = SEQ // TILE
STREAMS = 2
SB = BATCH // STREAMS
N_TOK = SB * L_TOT
N_TOK_TILES = N_TOK // TILE
PAIR = 2
LRU_CH = 256
MOD_ROWS = 8
VMEM_LIMIT = 56 * 1024 * 1024

PACK_W = D_MODEL // 2
HI_MASK = -65536
N_ASSIGN = TOP_K * N_TOK
EXP_TILE = 512
N_EXP_TILES = N_ASSIGN // EXP_TILE
N_ITEMS = N_EXP_TILES + N_EXPERTS - 1
ROUTE_ROWS = 2 * N_EXPERTS
POS_TILES = 22

SC_CORES = 2
SC_SUBCORES = 16
SC_WORKERS = SC_CORES * SC_SUBCORES
SC_ROWS = N_TOK // SC_WORKERS
SC_CHUNK = 88
SC_N_CHUNKS = SC_ROWS // SC_CHUNK


def _cparams(sem):
    return pltpu.CompilerParams(dimension_semantics=sem, vmem_limit_bytes=VMEM_LIMIT)


def _pack_rows(v):
    lo = pltpu.bitcast(v[:, :PACK_W].astype(BF16).astype(F32), I32)
    hi = pltpu.bitcast(v[:, PACK_W:].astype(BF16).astype(F32), I32)
    return ((lo >> 16) & 0xFFFF) | (hi & HI_MASK)


def _unpack_rows(p):
    return pltpu.bitcast(p << 16, F32), pltpu.bitcast(p & HI_MASK, F32)


def _mod_kernel(c_ref, w_ref, b_ref, o_ref):
    c = c_ref[...]
    s = c * jax.nn.sigmoid(c)
    o_ref[0] = jnp.dot(s.astype(BF16), w_ref[0].astype(BF16), preferred_element_type=F32) + b_ref[0]


def _modulation(cs, w_mod, b_mod):
    return pl.pallas_call(
        _mod_kernel,
        grid=(DEPTH, N_MOD),
        in_specs=[
            pl.BlockSpec((MOD_ROWS, D_MODEL), lambda l, n: (0, 0)),
            pl.BlockSpec((1, D_MODEL, D_MODEL), lambda l, n: (l, 0, n)),
            pl.BlockSpec((1, 1, D_MODEL), lambda l, n: (l, 0, n)),
        ],
        out_specs=pl.BlockSpec((1, MOD_ROWS, D_MODEL), lambda l, n: (l, 0, n)),
        out_shape=jax.ShapeDtypeStruct((DEPTH, MOD_ROWS, N_MOD * D_MODEL), F32),
        compiler_params=_cparams(("arbitrary", "arbitrary")),
        name="modulation",
    )(cs, w_mod, b_mod.reshape(DEPTH, 1, N_MOD * D_MODEL))


def _mod_row(b, i):
    return jnp.where(i == 0, BATCH, b)


def _moe_residual(x, g0, g1, wt, gate_row):
    lo0, hi0 = _unpack_rows(g0)
    lo1, hi1 = _unpack_rows(g1)
    w0 = wt[:, 0:1]
    w1 = wt[:, 1:2]
    f = jnp.concatenate([w0 * lo0 + w1 * lo1, w0 * hi0 + w1 * hi1], axis=1)
    return x + gate_row * f


def _pair_tok_spec(width):
    return pl.BlockSpec((PAIR, TILE, width), lambda bp, i: (bp, i, 0))


def _pair_mod_specs(layer, boff):
    return [pl.BlockSpec((1, 1, N_MOD, D_MODEL),
                         lambda bp, i, k=k: (layer, _mod_row(boff + PAIR * bp + k, i), 0, 0))
            for k in range(PAIR)]


def _full_spec(shape):
    return pl.BlockSpec(shape, lambda bp, i: tuple(0 for _ in shape))


def _to_scan_major(src_ref, scr):
    n_slab = D_MODEL // LANES
    for s in range(SUB):
        for j in range(n_slab):
            scr[j, pl.ds(s, GROUPS, stride=SUB), :] = src_ref[s * GROUPS:(s + 1) * GROUPS, j * LANES:(j + 1) * LANES]
    return jnp.concatenate([scr[j] for j in range(n_slab)], axis=1)


def _from_scan_major(val, scr):
    n_slab = D_MODEL // LANES
    for j in range(n_slab):
        scr[j] = val[:, j * LANES:(j + 1) * LANES]
    blocks = [jnp.concatenate([scr[j, pl.ds(s, GROUPS, stride=SUB), :] for j in range(n_slab)], axis=1)
              for s in range(SUB)]
    return jnp.concatenate(blocks, axis=0)


def _in_kernel(has_prev, *refs):
    if has_prev:
        x_ref, g0_ref, g1_ref, wt_ref = refs[:4]
        mprev_refs = refs[4:4 + PAIR]
        refs = (x_ref,) + refs[4 + PAIR:]
    else:
        x_ref, ctx_ref = refs[:2]
        refs = (x_ref,) + refs[2:]
    mod_refs = refs[1:1 + PAIR]
    nrm_ref, w_ref, bd_ref, qg_ref, kg_ref = refs[1 + PAIR:6 + PAIR]
    outs = refs[6 + PAIR:]
    xo_ref, proj_ref = outs[:2]
    x_ref = refs[0]
    hs = []
    for k in range(PAIR):
        if has_prev:
            x = _moe_residual(x_ref[k], g0_ref[k], g1_ref[k], wt_ref[k], mprev_refs[k][0, 0][5:6, :])
            xo_ref[k] = x
        else:
            perm_scr = outs[2]

            @pl.when(pl.program_id(1) == 0)
            def _():
                xo_ref[k] = _to_scan_major(ctx_ref.at[k], perm_scr)

            @pl.when(pl.program_id(1) > 0)
            def _():
                xo_ref[k] = _to_scan_major(x_ref.at[k], perm_scr)

            x = xo_ref[k]
        m = mod_refs[k][0, 0]
        ms = jnp.mean(x * x, axis=-1, keepdims=True)
        h = (x * lax.rsqrt(ms + EPS)) * nrm_ref[...]
        hs.append((h * (1.0 + m[1:2, :]) + m[0:1, :]).astype(BF16))
    acc = jnp.dot(jnp.concatenate(hs, axis=0), w_ref[...], preferred_element_type=F32)
    bd = bd_ref[...]

    def head_norm(t, gain):
        ss = jnp.dot((t * t).astype(BF16), bd, preferred_element_type=F32)
        return (t * lax.rsqrt(ss + EPS)) * gain

    def put(col_block, val):
        for k in range(PAIR):
            proj_ref[k, :, col_block * NA_WIDTH:(col_block + 1) * NA_WIDTH] = val[k * TILE:(k + 1) * TILE, :].astype(BF16)

    put(COL_Q, head_norm(acc[:, 0:NA_WIDTH], qg_ref[...]))
    put(COL_KV, head_norm(acc[:, NA_WIDTH:2 * NA_WIDTH], kg_ref[...]))
    put(COL_KV + 1, acc[:, 2 * NA_WIDTH:3 * NA_WIDTH])
    put(COL_U, acc[:, 3 * NA_WIDTH:3 * NA_WIDTH + LRU_WIDTH])
    put(COL_G, acc[:, 3 * NA_WIDTH + LRU_WIDTH:])


def _in_proj(l, boff, xs, moe_prev, mod, nrm, w_in, bd, qg, kg):
    has_prev = moe_prev is not None
    if has_prev:
        in_specs = [_pair_tok_spec(D_MODEL)]
        args = [xs]
        in_specs += [_pair_tok_spec(PACK_W), _pair_tok_spec(PACK_W), _pair_tok_spec(LANES)]
        in_specs += _pair_mod_specs(l - 1, boff)
        args += list(moe_prev) + [mod] * PAIR
    else:
        pair0 = boff // PAIR
        in_specs = [pl.BlockSpec((PAIR, TILE, D_MODEL), lambda bp, i: (pair0 + bp, jnp.maximum(i - 1, 0), 0)),
                    pl.BlockSpec((PAIR, TILE, D_MODEL), lambda bp, i: (pair0 + bp, 0, 0))]
        args = list(xs)
    in_specs += _pair_mod_specs(l, boff) + [_full_spec((1, D_MODEL)), _full_spec((D_MODEL, IN_COLS)),
                                            _full_spec((NA_WIDTH, NA_WIDTH)), _full_spec((1, NA_WIDTH)),
                                            _full_spec((1, NA_WIDTH))]
    args += [mod] * PAIR + [nrm, w_in, bd, qg, kg]
    out_shape = [jax.ShapeDtypeStruct((SB, L_TOT, D_MODEL), F32), jax.ShapeDtypeStruct((SB, L_TOT, IN_COLS), BF16)]
    out_specs = [_pair_tok_spec(D_MODEL), _pair_tok_spec(IN_COLS)]
    scratch = [] if has_prev else [pltpu.VMEM((D_MODEL // LANES, TILE, LANES), F32)]
    outs = pl.pallas_call(
        functools.partial(_in_kernel, has_prev),
        grid=(SB // PAIR, N_TILES),
        in_specs=in_specs,
        out_specs=out_specs,
        out_shape=out_shape,
        scratch_shapes=scratch,
        compiler_params=_cparams(("parallel", "arbitrary")),
        name="in_proj",
    )(*args)
    return outs[0], outs[1]


def _attn_kernel(q_ref, kvp_ref, kvc_ref, kvn_ref, kvx_ref, bias_ref,
                 u_ref, up_ref, un_ref, cw_ref, cb_ref, wr_ref, wi_ref, br_ref, bi_ref, lam_ref,
                 o_ref, hc_ref, h_scr, hl_scr, p_scr):
    j = pl.program_id(1)

    @pl.when(j == 0)
    def _():
        h_scr[...] = jnp.zeros_like(h_scr)

    has_prev = jnp.where(j >= 2, 1.0, 0.0).astype(F32)
    has_next = jnp.where(jnp.logical_and(j >= 1, j <= N_TILES - 2), 1.0, 0.0).astype(F32)
    for b in range(PAIR):
        cv = _lru_conv(u_ref[b].astype(F32), up_ref[b].astype(F32), un_ref[b].astype(F32), has_prev, has_next,
                       cw_ref[...], cb_ref[...])
        hc_ref[b, :, LRU_WIDTH:] = cv.astype(BF16)
        a_f, b_f = _lru_gates(cv, 0, wr_ref, wi_ref, br_ref, bi_ref, lam_ref)
        hfull, h_last = _lru_scan(a_f, b_f, h_scr[b], False, hl_scr, p_scr)
        hc_ref[b, :, 0:LRU_WIDTH] = hfull.astype(BF16)
        h_scr[b] = h_last

    pair_w = 2 * HEAD_DIM
    lane = lax.broadcasted_iota(jnp.int32, (TILE, pair_w), 1)
    low = lane < HEAD_DIM
    kv_refs = (kvp_ref, kvc_ref, kvn_ref, kvx_ref)
    nt = (((1,), (1,)), ((), ()))
    n_win = 3 * TILE
    low_kv = lax.broadcasted_iota(jnp.int32, (4 * TILE, pair_w), 1) < HEAD_DIM
    for b, hp in [(b, hp) for b in range(PAIR) for hp in range(NA_HEADS // 2)]:
        cols = slice(hp * pair_w, (hp + 1) * pair_w)
        q = q_ref[b, :, cols]
        vcols = slice(NA_WIDTH + hp * pair_w, NA_WIDTH + (hp + 1) * pair_w)
        kb = jnp.concatenate([r[b, :, cols] for r in kv_refs], axis=0)
        vb = jnp.concatenate([r[b, :, vcols] for r in kv_refs], axis=0)
        outs = []
        for hh in range(2):
            own = low if hh == 0 else jnp.logical_not(low)
            own_kv = low_kv if hh == 0 else jnp.logical_not(low_kv)
            qh = jnp.where(own, q, jnp.zeros_like(q))
            s = lax.dot_general(qh, kb, nt, preferred_element_type=F32)
            s_win = s[:, 0:n_win] + bias_ref[0, 2 * hp + hh]
            s_ctx = s[:, n_win:]
            m = jnp.maximum(jnp.max(s_win, axis=-1, keepdims=True), jnp.max(s_ctx, axis=-1, keepdims=True))
            p = jnp.concatenate([jnp.exp2((s_win - m).astype(BF16)), jnp.exp2((s_ctx - m).astype(BF16))], axis=1)
            va = jnp.where(own_kv, vb, jnp.ones_like(vb))
            o = jnp.dot(p, va, preferred_element_type=F32)
            outs.append(o / pltpu.roll(o, HEAD_DIM, 1))
        o_ref[b, :, cols] = jnp.where(low, outs[0], outs[1]).astype(BF16)


def _attention(l, proj, bias_tiles, lru):
    last = N_LAT_TILES - 1
    blk = (PAIR, TILE, 2 * NA_WIDTH)
    prev_map = lambda b, j: (b, 1 + jnp.clip(j - 2, 0, last), COL_KV // 2)
    cur_map = lambda b, j: (b, jnp.maximum(j, 1), COL_KV // 2)
    next_map = lambda b, j: (b, 1 + jnp.clip(j, 0, last), COL_KV // 2)
    ctx_map = lambda b, j: (b, 0, COL_KV // 2)
    var_map = lambda b, j: (jnp.where(j == 0, 3, jnp.where(j == 1, 0, jnp.where(j == N_TILES - 1, 2, 1))), l, 0, 0)
    kv_specs = [pl.BlockSpec(blk, prev_map), pl.BlockSpec(blk, cur_map), pl.BlockSpec(blk, next_map),
                pl.BlockSpec(blk, ctx_map)]
    tok = lambda col: pl.BlockSpec((PAIR, TILE, NA_WIDTH), lambda b, j: (b, j, col))
    halo = TILE // 16
    n_halo = L_TOT // 16
    prev16 = pl.BlockSpec((PAIR, 16, LRU_WIDTH), lambda b, j: (b, jnp.maximum(j * halo - 1, 0), COL_U))
    next16 = pl.BlockSpec((PAIR, 16, LRU_WIDTH), lambda b, j: (b, jnp.minimum((j + 1) * halo, n_halo - 1), COL_U))
    whole = lambda shape: pl.BlockSpec(shape, lambda b, j: tuple(0 for _ in shape))
    n_cb = LRU_WIDTH // LRU_CH
    lru_specs = [whole((CONV_W, LRU_WIDTH)), whole((1, LRU_WIDTH)), whole((2, n_cb, LRU_CH, LRU_CH)),
                 whole((2, n_cb, LRU_CH, LRU_CH)), whole((2, LRU_WIDTH)), whole((2, LRU_WIDTH)), whole((2, LRU_WIDTH))]
    return pl.pallas_call(
        _attn_kernel,
        grid=(SB // PAIR, N_TILES),
        in_specs=[tok(COL_Q)] + kv_specs + [pl.BlockSpec((1, NA_HEADS, TILE, 3 * TILE), var_map)]
        + [tok(COL_U), prev16, next16] + lru_specs,
        out_specs=[tok(0), pl.BlockSpec((PAIR, TILE, 2 * LRU_WIDTH), lambda b, j: (b, j, 0))],
        out_shape=[jax.ShapeDtypeStruct((SB, L_TOT, NA_WIDTH), BF16),
                   jax.ShapeDtypeStruct((SB, L_TOT, 2 * LRU_WIDTH), BF16)],
        scratch_shapes=[pltpu.VMEM((PAIR, 1, LRU_WIDTH), F32), pltpu.VMEM((TILE, LRU_WIDTH), F32),
                        pltpu.VMEM((TILE, LRU_WIDTH), F32)],
        compiler_params=_cparams(("parallel", "arbitrary")),
        name="na_attention",
    )(proj, proj, proj, proj, proj, bias_tiles, proj, proj, proj, *lru)


def _softplus(x):
    return jnp.maximum(x, 0.0) + jnp.log1p(jnp.exp(-jnp.abs(x)))


def _lru_conv(u, prev16, next16, has_prev, has_next, cw, cb):
    sub = lax.broadcasted_iota(jnp.int32, (SUB, u.shape[1]), 0)
    prow = prev16[15:16, :] * has_prev
    n0 = next16[0:1, :] * has_next
    n8 = next16[8:9, :] * has_next
    first8 = jnp.where(sub == 0, prow, pltpu.roll(u[TILE - SUB:TILE, :], 1, 0))
    last_a = jnp.where(sub == SUB - 1, n0, pltpu.roll(u[0:SUB, :], SUB - 1, 0))
    last_b = jnp.where(sub == SUB - 1, n8, pltpu.roll(u[SUB:2 * SUB, :], SUB - 1, 0))
    um1 = jnp.concatenate([first8, u[0:TILE - SUB, :]], axis=0)
    up1 = jnp.concatenate([u[SUB:TILE, :], last_a], axis=0)
    up2 = jnp.concatenate([u[2 * SUB:TILE, :], last_a, last_b], axis=0)
    return cw[0:1, :] * um1 + cw[1:2, :] * u + cw[2:3, :] * up1 + cw[3:4, :] * up2 + cb


def _lru_gates(v, d, wr_ref, wi_ref, br_ref, bi_ref, lam_ref):
    vb = v.astype(BF16)
    n_cb = LRU_WIDTH // LRU_CH

    def gate(w_ref, b_ref):
        z = [jnp.dot(vb[:, c * LRU_CH:(c + 1) * LRU_CH], w_ref[d, c], preferred_element_type=F32) for c in range(n_cb)]
        return jnp.tanh(jnp.concatenate(z, axis=1) + b_ref[d:d + 1, :])

    tr = gate(wr_ref, br_ref)
    ti = gate(wi_ref, bi_ref)
    half = (-0.5 * LRU_C * LOG2E) * _softplus(-lam_ref[d:d + 1, :])
    a = jnp.exp2(half + half * tr)
    om = 1.0 - a * a
    b = (om * lax.rsqrt(jnp.maximum(om, TINY))) * ((0.5 * v) * (1.0 + ti))
    return a, b


def _lru_scan(a, b, h_in, reverse, hl_scr, p_scr):
    order = range(GROUPS - 1, -1, -1) if reverse else range(GROUPS)
    hl = None
    for g in order:
        ag = a[g * SUB:(g + 1) * SUB, :]
        bg = b[g * SUB:(g + 1) * SUB, :]
        if hl is None:
            hl, p = bg, ag
        else:
            hl = ag * hl + bg
            p = ag * p
        hl_scr[g * SUB:(g + 1) * SUB, :] = hl
        p_scr[g * SUB:(g + 1) * SUB, :] = p
    blocks = range(SUB - 1, -1, -1) if reverse else range(SUB)
    carry = h_in
    cins = {}
    for s in blocks:
        cins[s] = carry
        carry = hl[s:s + 1, :] + p[s:s + 1, :] * carry
    cin = jnp.concatenate([cins[s] for s in range(SUB)], axis=0)
    hfull = hl_scr[...] + p_scr[...] * jnp.tile(cin, (GROUPS, 1))
    return hfull, carry


def _route(sel, aff):
    def top2_sum(a, b, c, d):
        hi1, lo1 = jnp.maximum(a, b), jnp.minimum(a, b)
        hi2, lo2 = jnp.maximum(c, d), jnp.minimum(c, d)
        return jnp.maximum(hi1, hi2) + jnp.maximum(jnp.minimum(hi1, hi2), jnp.maximum(lo1, lo2))

    scores = [top2_sum(*sel[EXPERTS_PER_GROUP * g:EXPERTS_PER_GROUP * (g + 1)]) for g in range(N_GROUPS)]
    best = jnp.zeros_like(scores[0], dtype=jnp.int32)
    best_v = scores[0]
    for g in range(1, N_GROUPS):
        upd = scores[g] > best_v
        best = jnp.where(upd, g, best)
        best_v = jnp.where(upd, scores[g], best_v)
    chosen = []
    for e in range(N_EXPERTS):
        g = e // EXPERTS_PER_GROUP
        rank = jnp.zeros_like(best)
        for o in range(EXPERTS_PER_GROUP * g, EXPERTS_PER_GROUP * (g + 1)):
            if o == e:
                continue
            ahead = sel[o] > sel[e]
            if o < e:
                ahead = jnp.logical_or(ahead, sel[o] == sel[e])
            rank = rank + ahead.astype(jnp.int32)
        chosen.append(jnp.logical_and(best == g, rank < TOP_K))
    total = jnp.zeros_like(aff[0])
    for e in range(N_EXPERTS):
        total = total + jnp.where(chosen[e], aff[e], 0.0)
    gates = [jnp.where(chosen[e], aff[e] / total, 0.0) for e in range(N_EXPERTS)]
    return gates, [c.astype(F32) for c in chosen]


def _out_kernel(x_ref, ya_ref, hc_ref, g_ref, *refs):
    mod_refs = refs[:PAIR]
    (wr_ref, wi_ref, br_ref, bi_ref, lam_ref, nrm_ref, w_ref, rwc_ref, rb_ref,
     xo_ref, hp_ref, rt_ref, h_scr, hl_scr, p_scr) = refs[PAIR:]
    rows = PAIR * TILE

    @pl.when(pl.program_id(1) == 0)
    def _():
        h_scr[...] = jnp.zeros_like(h_scr)

    ybs = []
    for k in range(PAIR):
        a_r, b_r = _lru_gates(hc_ref[k, :, LRU_WIDTH:].astype(F32), 1, wr_ref, wi_ref, br_ref, bi_ref, lam_ref)
        hrev, h_last = _lru_scan(a_r, b_r, h_scr[k], True, hl_scr, p_scr)
        h_scr[k] = h_last
        gx = g_ref[k].astype(F32)
        gate = (0.5 * gx) * (1.0 + jnp.tanh(gx * (GELU_C + (GELU_C * 0.044715) * (gx * gx))))
        ybs.append((gate * (hc_ref[k, :, 0:LRU_WIDTH].astype(F32) + hrev)).astype(BF16))
    ya = ya_ref[...].reshape(rows, NA_WIDTH)
    yb = jnp.concatenate(ybs, axis=0)
    y = jnp.dot(ya, w_ref[0:NA_WIDTH, :], preferred_element_type=F32)
    y = y + jnp.dot(yb, w_ref[NA_WIDTH:, :], preferred_element_type=F32)
    hs = []
    for k in range(PAIR):
        m = mod_refs[k][0, 0]
        x = x_ref[k] + m[2:3, :] * y[k * TILE:(k + 1) * TILE, :]
        xo_ref[k] = x
        ms = jnp.mean(x * x, axis=-1, keepdims=True)
        h = (x * lax.rsqrt(ms + EPS)) * nrm_ref[...]
        h = h * (1.0 + m[4:5, :]) + m[3:4, :]
        hp_ref[k] = _pack_rows(h)
        hs.append(h)
    h = jnp.concatenate(hs, axis=0)
    h_hi = h.astype(BF16)
    h_lo = (h - h_hi.astype(F32)).astype(BF16)
    nt = (((1,), (1,)), ((), ()))
    rwc = rwc_ref[...]
    both = lax.dot_general(rwc, h_hi, nt, preferred_element_type=F32)
    lg = (both[0:N_EXPERTS, :] + both[N_EXPERTS:, :]
          + lax.dot_general(rwc[0:N_EXPERTS, :], h_lo, nt, preferred_element_type=F32))
    aff_all = jax.nn.sigmoid(lg)
    sel_all = aff_all + rb_ref[...]
    aff = [aff_all[e:e + 1, :] for e in range(N_EXPERTS)]
    sel = [sel_all[e:e + 1, :] for e in range(N_EXPERTS)]
    gates, chosen = _route(sel, aff)
    rt = jnp.concatenate(gates + chosen, axis=0)
    for k in range(PAIR):
        rt_ref[k, 0] = rt[:, k * TILE:(k + 1) * TILE]


def _out_proj(l, boff, xs, ya, hc, proj, lru, mod, nrm, w_out, rwc, rb):
    rev = lambda i: jnp.where(i == 0, 0, N_TILES - i)
    tok = lambda width, col=0: pl.BlockSpec((PAIR, TILE, width), lambda bp, i: (bp, rev(i), col))
    mod_specs = [pl.BlockSpec((1, 1, N_MOD, D_MODEL),
                              lambda bp, i, k=k: (l, _mod_row(boff + PAIR * bp + k, i), 0, 0)) for k in range(PAIR)]
    n_cb = LRU_WIDTH // LRU_CH
    lru_specs = [_full_spec((2, n_cb, LRU_CH, LRU_CH)), _full_spec((2, n_cb, LRU_CH, LRU_CH)),
                 _full_spec((2, LRU_WIDTH)), _full_spec((2, LRU_WIDTH)), _full_spec((2, LRU_WIDTH))]
    return pl.pallas_call(
        _out_kernel,
        grid=(SB // PAIR, N_TILES),
        in_specs=[tok(D_MODEL), tok(NA_WIDTH), tok(2 * LRU_WIDTH), tok(LRU_WIDTH, COL_G)]
        + mod_specs + lru_specs
        + [_full_spec((1, D_MODEL)), _full_spec((D_MODEL, D_MODEL)), _full_spec((2 * N_EXPERTS, D_MODEL)),
           _full_spec((N_EXPERTS, 1))],
        out_specs=[tok(D_MODEL), tok(PACK_W),
                   pl.BlockSpec((PAIR, 1, ROUTE_ROWS, TILE), lambda bp, i: (bp, rev(i), 0, 0))],
        out_shape=[jax.ShapeDtypeStruct((SB, L_TOT, D_MODEL), F32),
                   jax.ShapeDtypeStruct((SB, L_TOT, PACK_W), I32),
                   jax.ShapeDtypeStruct((SB, N_TILES, ROUTE_ROWS, TILE), F32)],
        scratch_shapes=[pltpu.VMEM((PAIR, 1, LRU_WIDTH), F32), pltpu.VMEM((TILE, LRU_WIDTH), F32),
                        pltpu.VMEM((TILE, LRU_WIDTH), F32)],
        compiler_params=_cparams(("parallel", "arbitrary")),
        name="out_proj_router",
    )(xs, ya, hc, proj, *([mod] * PAIR), *lru[2:], nrm, w_out, rwc, rb)


def _pos_kernel(rt_ref, tri_ref, start_ref, pos_ref, wt_ref, run_scr):
    @pl.when(pl.program_id(0) == 0)
    def _():
        run_scr[...] = jnp.zeros_like(run_scr)

    base = start_ref[...] + run_scr[...]
    for k in range(POS_TILES):
        gates = rt_ref[k, 0:N_EXPERTS, :]
        chosen = rt_ref[k, N_EXPERTS:, :]
        rank = jnp.dot(chosen.astype(BF16), tri_ref[...], preferred_element_type=F32)
        posf = rank + base
        seen = jnp.zeros((1, TILE), F32)
        p0 = jnp.zeros((1, TILE), F32)
        p1 = jnp.zeros((1, TILE), F32)
        w0 = jnp.zeros((1, TILE), F32)
        w1 = jnp.zeros((1, TILE), F32)
        for e in range(N_EXPERTS):
            ch = chosen[e:e + 1, :]
            first = ch * (1.0 - seen)
            second = ch * seen
            p0 = p0 + first * posf[e:e + 1, :]
            p1 = p1 + second * posf[e:e + 1, :]
            w0 = w0 + first * gates[e:e + 1, :]
            w1 = w1 + second * gates[e:e + 1, :]
            seen = jnp.minimum(seen + ch, 1.0)
        pos_ref[k] = jnp.concatenate([p0, p1], axis=0).astype(I32)
        wpad = jnp.concatenate([w0, w1, jnp.zeros((LANES - TOP_K, TILE), F32)], axis=0)
        wt_ref[k * TILE:(k + 1) * TILE, :] = jnp.transpose(wpad)
        base = base + jnp.sum(chosen, axis=1, keepdims=True)
    run_scr[...] = base - start_ref[...]


def _positions(route, tri, start):
    return pl.pallas_call(
        _pos_kernel,
        grid=(N_TOK_TILES // POS_TILES,),
        in_specs=[pl.BlockSpec((POS_TILES, ROUTE_ROWS, TILE), lambda i: (i, 0, 0)),
                  pl.BlockSpec((TILE, TILE), lambda i: (0, 0)),
                  pl.BlockSpec((N_EXPERTS, 1), lambda i: (0, 0))],
        out_specs=[pl.BlockSpec((POS_TILES, TOP_K, TILE), lambda i: (i, 0, 0)),
                   pl.BlockSpec((POS_TILES * TILE, LANES), lambda i: (i, 0))],
        out_shape=[jax.ShapeDtypeStruct((N_TOK_TILES, TOP_K, TILE), I32),
                   jax.ShapeDtypeStruct((N_TOK, LANES), F32)],
        scratch_shapes=[pltpu.VMEM((N_EXPERTS, 1), F32)],
        compiler_params=_cparams(("arbitrary",)),
        name="moe_positions",
    )(route, tri, start)


def _work_items(counts):
    smem = pl.BlockSpec(memory_space=pltpu.SMEM)
    item = jax.ShapeDtypeStruct((N_ITEMS,), I32)
    return pl.pallas_call(
        _items_kernel,
        in_specs=[smem],
        out_specs=[smem] * 4,
        out_shape=[item] * 4,
        name="moe_work_items",
    )(counts)


def _items_kernel(cnt_ref, tile_ref, exp_ref, lo_ref, hi_ref):
    n = jnp.int32(0)
    start = jnp.int32(0)
    last_e = jnp.int32(0)
    for e in range(N_EXPERTS):
        cnt = cnt_ref[e]
        end = start + cnt
        first = start // EXP_TILE
        n_tiles = jnp.where(cnt > 0, (end - 1) // EXP_TILE - first + 1, 0)

        def put(j, carry, e=e, n=n, start=start, end=end, first=first):
            tile = first + j
            tile_ref[n + j] = tile
            exp_ref[n + j] = jnp.int32(e)
            lo_ref[n + j] = jnp.maximum(start - tile * EXP_TILE, 0)
            hi_ref[n + j] = jnp.minimum(end - tile * EXP_TILE, EXP_TILE)
            return carry

        lax.fori_loop(0, n_tiles, put, 0)
        n = n + n_tiles
        start = end
        last_e = jnp.where(cnt > 0, e, last_e)

    def pad(j, carry):
        tile_ref[j] = jnp.int32(N_EXP_TILES - 1)
        exp_ref[j] = last_e
        lo_ref[j] = jnp.int32(0)
        hi_ref[j] = jnp.int32(0)
        return carry

    lax.fori_loop(n, N_ITEMS, pad, 0)


def _sc_worker_base():
    return (lax.axis_index("s") * SC_CORES + lax.axis_index("c")) * SC_ROWS


def _sc_dispatch_body(h_hbm, p0_hbm, p1_hbm, out_hbm, i0_v, i1_v, rows_v, sem):
    base = _sc_worker_base()

    @pl.loop(0, SC_N_CHUNKS)
    def _(j):
        off = base + j * SC_CHUNK
        pltpu.sync_copy(h_hbm.at[pl.ds(off, SC_CHUNK)], rows_v)
        pltpu.sync_copy(p0_hbm.at[pl.ds(off, SC_CHUNK)], i0_v)
        pltpu.sync_copy(p1_hbm.at[pl.ds(off, SC_CHUNK)], i1_v)
        c0 = pltpu.async_copy(rows_v, out_hbm.at[i0_v], sem)
        c1 = pltpu.async_copy(rows_v, out_hbm.at[i1_v], sem)
        c0.wait()
        c1.wait()


def _sc_combine_body(y_hbm, p0_hbm, p1_hbm, g0_hbm, g1_hbm, i_v, rows_v, sem):
    base = _sc_worker_base()

    @pl.loop(0, SC_N_CHUNKS)
    def _(j):
        off = base + j * SC_CHUNK
        for p_hbm, g_hbm in ((p0_hbm, g0_hbm), (p1_hbm, g1_hbm)):
            pltpu.sync_copy(p_hbm.at[pl.ds(off, SC_CHUNK)], i_v)
            pltpu.async_copy(y_hbm.at[i_v], rows_v, sem).wait()
            pltpu.sync_copy(rows_v, g_hbm.at[pl.ds(off, SC_CHUNK)])


def _sc_mesh():
    return plsc.VectorSubcoreMesh(core_axis_name="c", subcore_axis_name="s")


def _dispatch(hp, pos0, pos1):
    return pl.kernel(
        _sc_dispatch_body,
        out_type=jax.ShapeDtypeStruct((N_ASSIGN, PACK_W), I32),
        mesh=_sc_mesh(),
        scratch_types=[pltpu.VMEM((SC_CHUNK,), I32), pltpu.VMEM((SC_CHUNK,), I32),
                       pltpu.VMEM((SC_CHUNK, PACK_W), I32), pltpu.SemaphoreType.DMA],
        name="moe_dispatch",
    )(hp, pos0, pos1)


def _combine(ys, pos0, pos1):
    row = jax.ShapeDtypeStruct((N_TOK, PACK_W), I32)
    return pl.kernel(
        _sc_combine_body,
        out_type=[row, row],
        mesh=_sc_mesh(),
        scratch_types=[pltpu.VMEM((SC_CHUNK,), I32), pltpu.VMEM((SC_CHUNK, PACK_W), I32),
                       pltpu.SemaphoreType.DMA],
        name="moe_combine",
    )(ys, pos0, pos1)


def _expert_kernel(tile_ref, exp_ref, lo_ref, hi_ref, x_ref, w1_ref, w3_ref, w2_ref, o_ref, w13_scr, w2_scr):
    i = pl.program_id(0)
    prev = jnp.maximum(i - 1, 0)

    @pl.when(jnp.logical_or(i == 0, exp_ref[i] != exp_ref[prev]))
    def _():
        w13_scr[:, 0:EXPERT_FF] = w1_ref[0, 0].astype(BF16)
        w13_scr[:, EXPERT_FF:] = w3_ref[0, 0].astype(BF16)
        w2_scr[...] = w2_ref[0, 0].astype(BF16)

    lo_f, hi_f = _unpack_rows(x_ref[...])
    h = jnp.concatenate([lo_f.astype(BF16), hi_f.astype(BF16)], axis=1)
    ab = jnp.dot(h, w13_scr[...], preferred_element_type=F32)
    a = ab[:, 0:EXPERT_FF]
    b = ab[:, EXPERT_FF:]
    t = ((a * jax.nn.sigmoid(a)) * b).astype(BF16)
    y = _pack_rows(jnp.dot(t, w2_scr[...], preferred_element_type=F32))
    row = lax.broadcasted_iota(jnp.int32, (EXP_TILE, PACK_W), 0)
    mine = jnp.logical_and(row >= lo_ref[i], row < hi_ref[i])
    revisit = jnp.logical_and(i > 0, tile_ref[i] == tile_ref[prev])

    @pl.when(jnp.logical_not(revisit))
    def _():
        o_ref[...] = jnp.where(mine, y, 0)

    @pl.when(revisit)
    def _():
        o_ref[...] = jnp.where(mine, y, o_ref[...])


def _experts(l, xs_sorted, items, w1, w3, w2):
    tile, e, lo, hi = items
    grid_spec = pltpu.PrefetchScalarGridSpec(
        num_scalar_prefetch=4,
        grid=(N_ITEMS,),
        in_specs=[pl.BlockSpec((EXP_TILE, PACK_W), lambda i, t, e, lo, hi: (t[i], 0)),
                  pl.BlockSpec((1, 1, D_MODEL, EXPERT_FF), lambda i, t, e, lo, hi: (l, e[i], 0, 0)),
                  pl.BlockSpec((1, 1, D_MODEL, EXPERT_FF), lambda i, t, e, lo, hi: (l, e[i], 0, 0)),
                  pl.BlockSpec((1, 1, EXPERT_FF, D_MODEL), lambda i, t, e, lo, hi: (l, e[i], 0, 0))],
        out_specs=pl.BlockSpec((EXP_TILE, PACK_W), lambda i, t, e, lo, hi: (t[i], 0)),
        scratch_shapes=[pltpu.VMEM((D_MODEL, 2 * EXPERT_FF), BF16), pltpu.VMEM((EXPERT_FF, D_MODEL), BF16)],
    )
    return pl.pallas_call(
        _expert_kernel,
        grid_spec=grid_spec,
        out_shape=jax.ShapeDtypeStruct((N_ASSIGN, PACK_W), I32),
        compiler_params=_cparams(("arbitrary",)),
        name="moe_experts",
    )(tile, e, lo, hi, xs_sorted, w1, w3, w2)


def _moe(l, hp, route, tri, w1, w3, w2):
    counts = jnp.sum(route[:, N_EXPERTS:, :], axis=(0, 2))
    start = (jnp.cumsum(counts) - counts).reshape(N_EXPERTS, 1)
    pos, wt = _positions(route, tri, start)
    pos0 = pos[:, 0, :].reshape(N_TOK)
    pos1 = pos[:, 1, :].reshape(N_TOK)
    items = _work_items(counts.astype(I32))
    xs_sorted = _dispatch(hp.reshape(N_TOK, PACK_W), pos0, pos1)
    ys = _experts(l, xs_sorted, items, w1, w3, w2)
    g0, g1 = _combine(ys, pos0, pos1)
    shape = (SB, L_TOT, PACK_W)
    return g0.reshape(shape), g1.reshape(shape), wt.reshape(SB, L_TOT, LANES)


def _final_kernel(x_ref, g0_ref, g1_ref, wt_ref, mod_ref, *refs):
    o_ref, scr = refs[-2:]
    for b in range(SB):
        x = _moe_residual(x_ref[b], g0_ref[b], g1_ref[b], wt_ref[b], mod_ref[0, b][5:6, :])
        o_ref[b] = _from_scan_major(x, scr)


def _final(streams, mod):
    lat = lambda width: pl.BlockSpec((SB, TILE, width), lambda j: (0, j + 1, 0))
    out = None
    for k, st in enumerate(streams):
        in_specs = [lat(D_MODEL), lat(PACK_W), lat(PACK_W), lat(LANES),
                    pl.BlockSpec((1, SB, N_MOD, D_MODEL), lambda j, k=k: (DEPTH - 1, k, 0, 0))]
        args = [st["xs"]] + list(st["moe"]) + [mod]
        aliases = {}
        if out is not None:
            in_specs.append(pl.BlockSpec(memory_space=pl.ANY))
            args.append(out)
            aliases = {len(args) - 1: 0}
        out = pl.pallas_call(
            _final_kernel,
            grid=(N_LAT_TILES,),
            in_specs=in_specs,
            out_specs=pl.BlockSpec((SB, TILE, D_MODEL), lambda j, k=k: (k, j, 0)),
            out_shape=jax.ShapeDtypeStruct((BATCH, SEQ, D_MODEL), F32),
            scratch_shapes=[pltpu.VMEM((D_MODEL // LANES, TILE, LANES), F32)],
            input_output_aliases=aliases,
            compiler_params=_cparams(("arbitrary",)),
            name="final_residual",
        )(*args)
    return out


def _bias_kernel(toe_ref, perm_ref, o_ref, scr):
    rows_q = TILE // GRID_W
    lane = lax.broadcasted_iota(jnp.int32, (GROUPS, TILE), 1)
    ka = (lane >> 1) & (rows_q - 1)
    neg = jnp.full((GROUPS, TILE), NEG_INF, F32)

    def interleaved(r0):
        acc = None
        for k in range(rows_q):
            val = toe_ref[0, r0 + k]
            t1 = val.astype(BF16)
            r1 = val - t1.astype(F32)
            t2 = r1.astype(BF16)
            t3 = (r1 - t2.astype(F32)).astype(BF16)
            for term in (t1, t2, t3):
                moved = jnp.dot(term, perm_ref[k], preferred_element_type=F32)
                acc = moved if acc is None else acc + moved
        return acc

    t4 = [interleaved(r0) for r0 in range(3 * rows_q)]
    for a in range(rows_q):
        for qs in range(SUB // rows_q):
            s = (SUB // rows_q) * a + qs
            for kt in range(3):
                src = t4[rows_q * kt - a + 3][qs * GROUPS:(qs + 1) * GROUPS, :]
                c = rows_q * kt + ka
                variants = (
                    src if kt >= 1 else neg,
                    jnp.where(jnp.logical_and(c >= a, c <= a + NA_WIN_ROWS - 1), src, neg),
                    src if kt <= 1 else neg,
                )
                for v, val in enumerate(variants):
                    for half in range(TILE // LANES):
                        scr[v, 2 * kt + half, pl.ds(s, GROUPS, stride=SUB), :] = val[:, half * LANES:(half + 1) * LANES]
    for v in range(3):
        o_ref[v, 0] = jnp.concatenate([scr[v, j] for j in range(3 * TILE // LANES)], axis=1)
    o_ref[3, 0] = jnp.full((TILE, 3 * TILE), NEG_INF, F32)


def _bias_tiles(table):
    rows_q = TILE // GRID_W
    n_r = 2 * NA_WIN_ROWS - 1
    qc = np.arange(GRID_W)[:, None]
    kc = np.arange(GRID_W)[None, :]
    col0 = np.clip(qc - NA_WIN_COLS // 2, 0, GRID_W - NA_WIN_COLS)
    in_win = (kc >= col0) & (kc < col0 + NA_WIN_COLS)
    pad = GRID_W - NA_WIN_COLS
    tp = jnp.pad(table.astype(F32) * LOG2E, ((0, 0), (0, 0), (pad, pad)))
    toeplitz = jnp.stack([tp[:, :, GRID_W - 1 - q:2 * GRID_W - 1 - q] for q in range(GRID_W)], axis=2)
    toeplitz = jnp.where(in_win[None, None], toeplitz, NEG_INF)
    n_h = table.shape[0]
    half = GRID_W // GROUPS
    perm = np.zeros((rows_q, GRID_W, TILE), np.float32)
    for k in range(rows_q):
        for ks in range(half):
            for kg in range(GROUPS):
                perm[k, ks * GROUPS + kg, kg * SUB + k * half + ks] = 1.0
    return pl.pallas_call(
        _bias_kernel,
        grid=(n_h,),
        in_specs=[pl.BlockSpec((1, n_r, GRID_W, GRID_W), lambda h: (h, 0, 0, 0)),
                  pl.BlockSpec((rows_q, GRID_W, TILE), lambda h: (0, 0, 0))],
        out_specs=pl.BlockSpec((4, 1, TILE, 3 * TILE), lambda h: (0, h, 0, 0)),
        out_shape=jax.ShapeDtypeStruct((4, n_h, TILE, 3 * TILE), F32),
        scratch_shapes=[pltpu.VMEM((3, 3 * TILE // LANES, TILE, LANES), F32)],
        compiler_params=_cparams(("parallel",)),
        name="bias_tiles",
    )(toeplitz, jnp.asarray(perm, BF16))


def _block_diag(w, n_chunks):
    per = LRU_BLOCKS // n_chunks
    w = w.reshape(2, n_chunks, per, LRU_BLOCK, LRU_BLOCK)
    eye = jnp.eye(per, dtype=w.dtype)
    out = jnp.einsum('dcpij,pq->dcpiqj', w, eye)
    return out.reshape(2, n_chunks, per * LRU_BLOCK, per * LRU_BLOCK)


def kernel(x, c, ctx, c_ctx, w_mod, b_mod, norm_mix, norm_ffn, w_in, w_out, q_gain, k_gain, na_bias,
           conv_w, conv_b, lru_w_r, lru_b_r, lru_w_i, lru_b_i, lru_lambda, router_w, router_b,
           exp_w1, exp_w3, exp_w2):
    cs = jnp.concatenate([c, c_ctx[None, :], jnp.zeros((MOD_ROWS - BATCH - 1, D_MODEL), F32)], axis=0)
    mod = _modulation(cs, w_mod, b_mod).reshape(DEPTH, MOD_ROWS, N_MOD, D_MODEL)

    head_of = np.arange(NA_WIDTH) // HEAD_DIM
    bd = jnp.asarray((head_of[:, None] == head_of[None, :]).astype(np.float32) / HEAD_DIM, BF16)
    tri = jnp.asarray(np.triu(np.ones((TILE, TILE), np.float32), 1), BF16)
    rwt = router_w.T
    rwh = rwt.astype(BF16)
    rwc = jnp.concatenate([rwh, (rwt - rwh.astype(F32)).astype(BF16)], axis=0)
    rb = router_b.reshape(N_EXPERTS, 1)
    n_cb = LRU_WIDTH // LRU_CH

    bias_tiles = _bias_tiles(na_bias.reshape(DEPTH * NA_HEADS, 2 * NA_WIN_ROWS - 1, 2 * NA_WIN_COLS - 1))
    streams = [{"xs": (x, ctx), "moe": None, "boff": sidx * SB} for sidx in range(STREAMS)]
    for l in range(DEPTH):
        qg = jnp.tile(q_gain[l] * (ATTN_SCALE * LOG2E), NA_HEADS)[None, :]
        kg = jnp.tile(k_gain[l], NA_HEADS)[None, :]
        w_in_l = w_in[l].astype(BF16)
        w_out_l = w_out[l].astype(BF16)
        wr = (0.5 * _block_diag(lru_w_r[l], n_cb)).astype(BF16)
        wi = (0.5 * _block_diag(lru_w_i[l], n_cb)).astype(BF16)
        lru = (conv_w[l], conv_b[l][None, :], wr, wi, 0.5 * lru_b_r[l], 0.5 * lru_b_i[l], lru_lambda[l])
        for st in streams:
            boff = st["boff"]
            xs, proj = _in_proj(l, boff, st["xs"], st["moe"], mod, norm_mix[l][None, :], w_in_l, bd, qg, kg)
            ya, hc = _attention(l, proj, bias_tiles, lru)
            xs, hp, route = _out_proj(l, boff, xs, ya, hc, proj, lru, mod, norm_ffn[l][None, :], w_out_l, rwc, rb)
            st["xs"] = xs
            st["moe"] = _moe(l, hp, route.reshape(N_TOK_TILES, ROUTE_ROWS, TILE), tri, exp_w1, exp_w3, exp_w2)
    return _final(streams, mod)
```

```python
import functools

import jax
import jax.numpy as jnp
import numpy as np
from jax import lax
from jax.experimental import pallas as pl
from jax.experimental.pallas import tpu as pltpu
from jax.experimental.pallas import tpu_sc as plsc

F32 = jnp.float32
BF16 = jnp.bfloat16
I32 = jnp.int32

D_MODEL = 1024
BATCH = 4
SEQ = 8192
DEPTH = 4
GRID_W = 64
CTX_LEN = 256
HEAD_DIM = 64
NA_WIDTH = 512
NA_HEADS = 8
NA_WIN_ROWS = 8
NA_WIN_COLS = 16
LRU_WIDTH = 512
LRU_BLOCKS = 8
LRU_BLOCK = 64
CONV_W = 4
LRU_C = 8.0
IN_COLS = 3 * NA_WIDTH + 2 * LRU_WIDTH
COL_KV, COL_Q, COL_U, COL_G = 0, 2, 3, 4
N_EXPERTS = 16
N_GROUPS = 4
EXPERTS_PER_GROUP = 4
TOP_K = 2
EXPERT_FF = 512
N_MOD = 6
ATTN_SCALE = HEAD_DIM ** -0.5
LOG2E = 1.4426950408889634
EPS = 1e-6
NEG_INF = -1e30
TINY = 1e-30
GELU_C = 0.7978845608028654

TILE = 256
SUB = 8
LANES = 128
GROUPS = TILE // SUB
L_TOT = CTX_LEN + SEQ
N_TILES = L_TOT // TILE
N_LAT_TILES = SEQ // TILE
STREAMS = 2
SB = BATCH // STREAMS
N_TOK = SB * L_TOT
N_TOK_TILES = N_TOK // TILE
PAIR = 2
LRU_CH = 256
MOD_ROWS = 8
VMEM_LIMIT = 56 * 1024 * 1024

PACK_W = D_MODEL // 2
HI_MASK = -65536
N_ASSIGN = TOP_K * N_TOK
EXP_TILE = 512
N_EXP_TILES = N_ASSIGN // EXP_TILE
N_ITEMS = N_EXP_TILES + N_EXPERTS - 1
BIAS_PAD = GRID_W - NA_WIN_COLS
ROUTE_ROWS = 2 * N_EXPERTS
POS_TILES = 22

SC_CORES = 2
SC_SUBCORES = 16
SC_WORKERS = SC_CORES * SC_SUBCORES
SC_ROWS = N_TOK // SC_WORKERS
SC_CHUNK = 88
SC_N_CHUNKS = SC_ROWS // SC_CHUNK


def _cparams(sem):
    return pltpu.CompilerParams(dimension_semantics=sem, vmem_limit_bytes=VMEM_LIMIT)


def _pack_rows(v):
    lo = pltpu.bitcast(v[:, :PACK_W].astype(BF16).astype(F32), I32)
    hi = pltpu.bitcast(v[:, PACK_W:].astype(BF16).astype(F32), I32)
    return ((lo >> 16) & 0xFFFF) | (hi & HI_MASK)


def _unpack_rows(p):
    return pltpu.bitcast(p << 16, F32), pltpu.bitcast(p & HI_MASK, F32)


def _mod_kernel(c_ref, w_ref, b_ref, o_ref):
    c = c_ref[...]
    s = c * jax.nn.sigmoid(c)
    o_ref[0] = jnp.dot(s.astype(BF16), w_ref[0].astype(BF16), preferred_element_type=F32) + b_ref[0]


def _modulation(cs, w_mod, b_mod):
    return pl.pallas_call(
        _mod_kernel,
        grid=(DEPTH, N_MOD),
        in_specs=[
            pl.BlockSpec((MOD_ROWS, D_MODEL), lambda l, n: (0, 0)),
            pl.BlockSpec((1, D_MODEL, D_MODEL), lambda l, n: (l, 0, n)),
            pl.BlockSpec((1, 1, D_MODEL), lambda l, n: (l, 0, n)),
        ],
        out_specs=pl.BlockSpec((1, MOD_ROWS, D_MODEL), lambda l, n: (l, 0, n)),
        out_shape=jax.ShapeDtypeStruct((DEPTH, MOD_ROWS, N_MOD * D_MODEL), F32),
        compiler_params=_cparams(("arbitrary", "arbitrary")),
        name="modulation",
    )(cs, w_mod, b_mod.reshape(DEPTH, 1, N_MOD * D_MODEL))


def _mod_row(b, i):
    return jnp.where(i == 0, BATCH, b)


def _moe_residual(x, g0, g1, wt, gate_row):
    lo0, hi0 = _unpack_rows(g0)
    lo1, hi1 = _unpack_rows(g1)
    w0 = wt[:, 0:1]
    w1 = wt[:, 1:2]
    f = jnp.concatenate([w0 * lo0 + w1 * lo1, w0 * hi0 + w1 * hi1], axis=1)
    return x + gate_row * f


def _pair_tok_spec(width):
    return pl.BlockSpec((PAIR, TILE, width), lambda bp, i: (bp, i, 0))


def _pair_mod_specs(layer, boff):
    return [pl.BlockSpec((1, 1, N_MOD, D_MODEL),
                         lambda bp, i, k=k: (layer, _mod_row(boff + PAIR * bp + k, i), 0, 0))
            for k in range(PAIR)]


def _full_spec(shape):
    return pl.BlockSpec(shape, lambda bp, i: tuple(0 for _ in shape))


def _to_scan_major(src_ref, scr):
    n_slab = D_MODEL // LANES
    for s in range(SUB):
        for j in range(n_slab):
            scr[j, pl.ds(s, GROUPS, stride=SUB), :] = src_ref[s * GROUPS:(s + 1) * GROUPS, j * LANES:(j + 1) * LANES]
    return jnp.concatenate([scr[j] for j in range(n_slab)], axis=1)


def _from_scan_major(val, scr):
    n_slab = D_MODEL // LANES
    for j in range(n_slab):
        scr[j] = val[:, j * LANES:(j + 1) * LANES]
    blocks = [jnp.concatenate([scr[j, pl.ds(s, GROUPS, stride=SUB), :] for j in range(n_slab)], axis=1)
              for s in range(SUB)]
    return jnp.concatenate(blocks, axis=0)


def _in_kernel(has_prev, *refs):
    if has_prev:
        x_ref, g0_ref, g1_ref, wt_ref = refs[:4]
        mprev_refs = refs[4:4 + PAIR]
        refs = (x_ref,) + refs[4 + PAIR:]
    else:
        x_ref, ctx_ref = refs[:2]
        refs = (x_ref,) + refs[2:]
    mod_refs = refs[1:1 + PAIR]
    nrm_ref, w_ref, bd_ref, qg_ref, kg_ref = refs[1 + PAIR:6 + PAIR]
    outs = refs[6 + PAIR:]
    xo_ref, proj_ref = outs[:2]
    x_ref = refs[0]
    hs = []
    for k in range(PAIR):
        if has_prev:
            x = _moe_residual(x_ref[k], g0_ref[k], g1_ref[k], wt_ref[k], mprev_refs[k][0, 0][5:6, :])
            xo_ref[k] = x
        else:
            perm_scr = outs[2]

            @pl.when(pl.program_id(1) == 0)
            def _():
                xo_ref[k] = _to_scan_major(ctx_ref.at[k], perm_scr)

            @pl.when(pl.program_id(1) > 0)
            def _():
                xo_ref[k] = _to_scan_major(x_ref.at[k], perm_scr)

            x = xo_ref[k]
        m = mod_refs[k][0, 0]
        ms = jnp.mean(x * x, axis=-1, keepdims=True)
        h = (x * lax.rsqrt(ms + EPS)) * nrm_ref[...]
        hs.append((h * (1.0 + m[1:2, :]) + m[0:1, :]).astype(BF16))
    acc = jnp.dot(jnp.concatenate(hs, axis=0), w_ref[...], preferred_element_type=F32)
    bd = bd_ref[...]

    def head_norm(t, gain):
        ss = jnp.dot((t * t).astype(BF16), bd, preferred_element_type=F32)
        return (t * lax.rsqrt(ss + EPS)) * gain

    def put(col_block, val):
        for k in range(PAIR):
            proj_ref[k, :, col_block * NA_WIDTH:(col_block + 1) * NA_WIDTH] = val[k * TILE:(k + 1) * TILE, :].astype(BF16)

    put(COL_Q, head_norm(acc[:, 0:NA_WIDTH], qg_ref[...]))
    put(COL_KV, head_norm(acc[:, NA_WIDTH:2 * NA_WIDTH], kg_ref[...]))
    put(COL_KV + 1, acc[:, 2 * NA_WIDTH:3 * NA_WIDTH])
    put(COL_U, acc[:, 3 * NA_WIDTH:3 * NA_WIDTH + LRU_WIDTH])
    put(COL_G, acc[:, 3 * NA_WIDTH + LRU_WIDTH:])


def _in_proj(l, boff, xs, moe_prev, mod, nrm, w_in, bd, qg, kg):
    has_prev = moe_prev is not None
    if has_prev:
        in_specs = [_pair_tok_spec(D_MODEL)]
        args = [xs]
        in_specs += [_pair_tok_spec(PACK_W), _pair_tok_spec(PACK_W), _pair_tok_spec(LANES)]
        in_specs += _pair_mod_specs(l - 1, boff)
        args += list(moe_prev) + [mod] * PAIR
    else:
        pair0 = boff // PAIR
        in_specs = [pl.BlockSpec((PAIR, TILE, D_MODEL), lambda bp, i: (pair0 + bp, jnp.maximum(i - 1, 0), 0)),
                    pl.BlockSpec((PAIR, TILE, D_MODEL), lambda bp, i: (pair0 + bp, 0, 0))]
        args = list(xs)
    in_specs += _pair_mod_specs(l, boff) + [_full_spec((1, D_MODEL)), _full_spec((D_MODEL, IN_COLS)),
                                            _full_spec((NA_WIDTH, NA_WIDTH)), _full_spec((1, NA_WIDTH)),
                                            _full_spec((1, NA_WIDTH))]
    args += [mod] * PAIR + [nrm, w_in, bd, qg, kg]
    out_shape = [jax.ShapeDtypeStruct((SB, L_TOT, D_MODEL), F32), jax.ShapeDtypeStruct((SB, L_TOT, IN_COLS), BF16)]
    out_specs = [_pair_tok_spec(D_MODEL), _pair_tok_spec(IN_COLS)]
    scratch = [] if has_prev else [pltpu.VMEM((D_MODEL // LANES, TILE, LANES), F32)]
    outs = pl.pallas_call(
        functools.partial(_in_kernel, has_prev),
        grid=(SB // PAIR, N_TILES),
        in_specs=in_specs,
        out_specs=out_specs,
        out_shape=out_shape,
        scratch_shapes=scratch,
        compiler_params=_cparams(("parallel", "arbitrary")),
        name="in_proj",
    )(*args)
    return outs[0], outs[1]


def _attn_kernel(q_ref, kvp_ref, kvc_ref, kvn_ref, kvx_ref, bias_ref,
                 u_ref, up_ref, un_ref, cw_ref, cb_ref, wr_ref, wi_ref, br_ref, bi_ref, lam_ref,
                 o_ref, hc_ref, h_scr, hl_scr, p_scr):
    j = pl.program_id(1)

    @pl.when(j == 0)
    def _():
        h_scr[...] = jnp.zeros_like(h_scr)

    has_prev = jnp.where(j >= 2, 1.0, 0.0).astype(F32)
    has_next = jnp.where(jnp.logical_and(j >= 1, j <= N_TILES - 2), 1.0, 0.0).astype(F32)
    for b in range(PAIR):
        cv = _lru_conv(u_ref[b].astype(F32), up_ref[b].astype(F32), un_ref[b].astype(F32), has_prev, has_next,
                       cw_ref[...], cb_ref[...])
        hc_ref[b, :, LRU_WIDTH:] = cv.astype(BF16)
        a_f, b_f = _lru_gates(cv, 0, wr_ref, wi_ref, br_ref, bi_ref, lam_ref)
        hfull, h_last = _lru_scan(a_f, b_f, h_scr[b], False, hl_scr, p_scr)
        hc_ref[b, :, 0:LRU_WIDTH] = hfull.astype(BF16)
        h_scr[b] = h_last

    pair_w = 2 * HEAD_DIM
    lane = lax.broadcasted_iota(jnp.int32, (TILE, pair_w), 1)
    low = lane < HEAD_DIM
    kv_refs = (kvp_ref, kvc_ref, kvn_ref, kvx_ref)
    nt = (((1,), (1,)), ((), ()))
    n_win = 3 * TILE
    low_kv = lax.broadcasted_iota(jnp.int32, (4 * TILE, pair_w), 1) < HEAD_DIM
    for b, hp in [(b, hp) for b in range(PAIR) for hp in range(NA_HEADS // 2)]:
        cols = slice(hp * pair_w, (hp + 1) * pair_w)
        q = q_ref[b, :, cols]
        vcols = slice(NA_WIDTH + hp * pair_w, NA_WIDTH + (hp + 1) * pair_w)
        kb = jnp.concatenate([r[b, :, cols] for r in kv_refs], axis=0)
        vb = jnp.concatenate([r[b, :, vcols] for r in kv_refs], axis=0)
        outs = []
        for hh in range(2):
            own = low if hh == 0 else jnp.logical_not(low)
            own_kv = low_kv if hh == 0 else jnp.logical_not(low_kv)
            qh = jnp.where(own, q, jnp.zeros_like(q))
            s = lax.dot_general(qh, kb, nt, preferred_element_type=F32)
            s_win = s[:, 0:n_win] + bias_ref[0, 2 * hp + hh]
            s_ctx = s[:, n_win:]
            m = jnp.maximum(jnp.max(s_win, axis=-1, keepdims=True), jnp.max(s_ctx, axis=-1, keepdims=True))
            p = jnp.concatenate([jnp.exp2((s_win - m).astype(BF16)), jnp.exp2((s_ctx - m).astype(BF16))], axis=1)
            va = jnp.where(own_kv, vb, jnp.ones_like(vb))
            o = jnp.dot(p, va, preferred_element_type=F32)
            outs.append(o / pltpu.roll(o, HEAD_DIM, 1))
        o_ref[b, :, cols] = jnp.where(low, outs[0], outs[1]).astype(BF16)


def _attention(l, proj, bias_tiles, lru):
    last = N_LAT_TILES - 1
    blk = (PAIR, TILE, 2 * NA_WIDTH)
    prev_map = lambda b, j: (b, 1 + jnp.clip(j - 2, 0, last), COL_KV // 2)
    cur_map = lambda b, j: (b, jnp.maximum(j, 1), COL_KV // 2)
    next_map = lambda b, j: (b, 1 + jnp.clip(j, 0, last), COL_KV // 2)
    ctx_map = lambda b, j: (b, 0, COL_KV // 2)
    var_map = lambda b, j: (jnp.where(j == 0, 3, jnp.where(j == 1, 0, jnp.where(j == N_TILES - 1, 2, 1))), l, 0, 0)
    kv_specs = [pl.BlockSpec(blk, prev_map), pl.BlockSpec(blk, cur_map), pl.BlockSpec(blk, next_map),
                pl.BlockSpec(blk, ctx_map)]
    tok = lambda col: pl.BlockSpec((PAIR, TILE, NA_WIDTH), lambda b, j: (b, j, col))
    halo = TILE // 16
    n_halo = L_TOT // 16
    prev16 = pl.BlockSpec((PAIR, 16, LRU_WIDTH), lambda b, j: (b, jnp.maximum(j * halo - 1, 0), COL_U))
    next16 = pl.BlockSpec((PAIR, 16, LRU_WIDTH), lambda b, j: (b, jnp.minimum((j + 1) * halo, n_halo - 1), COL_U))
    whole = lambda shape: pl.BlockSpec(shape, lambda b, j: tuple(0 for _ in shape))
    n_cb = LRU_WIDTH // LRU_CH
    lru_specs = [whole((CONV_W, LRU_WIDTH)), whole((1, LRU_WIDTH)), whole((2, n_cb, LRU_CH, LRU_CH)),
                 whole((2, n_cb, LRU_CH, LRU_CH)), whole((2, LRU_WIDTH)), whole((2, LRU_WIDTH)), whole((2, LRU_WIDTH))]
    return pl.pallas_call(
        _attn_kernel,
        grid=(SB // PAIR, N_TILES),
        in_specs=[tok(COL_Q)] + kv_specs + [pl.BlockSpec((1, NA_HEADS, TILE, 3 * TILE), var_map)]
        + [tok(COL_U), prev16, next16] + lru_specs,
        out_specs=[tok(0), pl.BlockSpec((PAIR, TILE, 2 * LRU_WIDTH), lambda b, j: (b, j, 0))],
        out_shape=[jax.ShapeDtypeStruct((SB, L_TOT, NA_WIDTH), BF16),
                   jax.ShapeDtypeStruct((SB, L_TOT, 2 * LRU_WIDTH), BF16)],
        scratch_shapes=[pltpu.VMEM((PAIR, 1, LRU_WIDTH), F32), pltpu.VMEM((TILE, LRU_WIDTH), F32),
                        pltpu.VMEM((TILE, LRU_WIDTH), F32)],
        compiler_params=_cparams(("parallel", "arbitrary")),
        name="na_attention",
    )(proj, proj, proj, proj, proj, bias_tiles, proj, proj, proj, *lru)


def _softplus(x):
    return jnp.maximum(x, 0.0) + jnp.log1p(jnp.exp(-jnp.abs(x)))


def _lru_conv(u, prev16, next16, has_prev, has_next, cw, cb):
    sub = lax.broadcasted_iota(jnp.int32, (SUB, u.shape[1]), 0)
    prow = prev16[15:16, :] * has_prev
    n0 = next16[0:1, :] * has_next
    n8 = next16[8:9, :] * has_next
    first8 = jnp.where(sub == 0, prow, pltpu.roll(u[TILE - SUB:TILE, :], 1, 0))
    last_a = jnp.where(sub == SUB - 1, n0, pltpu.roll(u[0:SUB, :], SUB - 1, 0))
    last_b = jnp.where(sub == SUB - 1, n8, pltpu.roll(u[SUB:2 * SUB, :], SUB - 1, 0))
    um1 = jnp.concatenate([first8, u[0:TILE - SUB, :]], axis=0)
    up1 = jnp.concatenate([u[SUB:TILE, :], last_a], axis=0)
    up2 = jnp.concatenate([u[2 * SUB:TILE, :], last_a, last_b], axis=0)
    return cw[0:1, :] * um1 + cw[1:2, :] * u + cw[2:3, :] * up1 + cw[3:4, :] * up2 + cb


def _lru_gates(v, d, wr_ref, wi_ref, br_ref, bi_ref, lam_ref):
    vb = v.astype(BF16)
    n_cb = LRU_WIDTH // LRU_CH

    def gate(w_ref, b_ref):
        z = [jnp.dot(vb[:, c * LRU_CH:(c + 1) * LRU_CH], w_ref[d, c], preferred_element_type=F32) for c in range(n_cb)]
        return jnp.tanh(jnp.concatenate(z, axis=1) + b_ref[d:d + 1, :])

    tr = gate(wr_ref, br_ref)
    ti = gate(wi_ref, bi_ref)
    half = (-0.5 * LRU_C * LOG2E) * _softplus(-lam_ref[d:d + 1, :])
    a = jnp.exp2(half + half * tr)
    om = 1.0 - a * a
    b = (om * lax.rsqrt(jnp.maximum(om, TINY))) * ((0.5 * v) * (1.0 + ti))
    return a, b


def _lru_scan(a, b, h_in, reverse, hl_scr, p_scr):
    order = range(GROUPS - 1, -1, -1) if reverse else range(GROUPS)
    hl = None
    for g in order:
        ag = a[g * SUB:(g + 1) * SUB, :]
        bg = b[g * SUB:(g + 1) * SUB, :]
        if hl is None:
            hl, p = bg, ag
        else:
            hl = ag * hl + bg
            p = ag * p
        hl_scr[g * SUB:(g + 1) * SUB, :] = hl
        p_scr[g * SUB:(g + 1) * SUB, :] = p
    blocks = range(SUB - 1, -1, -1) if reverse else range(SUB)
    carry = h_in
    cins = {}
    for s in blocks:
        cins[s] = carry
        carry = hl[s:s + 1, :] + p[s:s + 1, :] * carry
    cin = jnp.concatenate([cins[s] for s in range(SUB)], axis=0)
    hfull = hl_scr[...] + p_scr[...] * jnp.tile(cin, (GROUPS, 1))
    return hfull, carry


def _route(sel, aff):
    def top2_sum(a, b, c, d):
        hi1, lo1 = jnp.maximum(a, b), jnp.minimum(a, b)
        hi2, lo2 = jnp.maximum(c, d), jnp.minimum(c, d)
        return jnp.maximum(hi1, hi2) + jnp.maximum(jnp.minimum(hi1, hi2), jnp.maximum(lo1, lo2))

    scores = [top2_sum(*sel[EXPERTS_PER_GROUP * g:EXPERTS_PER_GROUP * (g + 1)]) for g in range(N_GROUPS)]
    best = jnp.zeros_like(scores[0], dtype=jnp.int32)
    best_v = scores[0]
    for g in range(1, N_GROUPS):
        upd = scores[g] > best_v
        best = jnp.where(upd, g, best)
        best_v = jnp.where(upd, scores[g], best_v)
    chosen = []
    for e in range(N_EXPERTS):
        g = e // EXPERTS_PER_GROUP
        rank = jnp.zeros_like(best)
        for o in range(EXPERTS_PER_GROUP * g, EXPERTS_PER_GROUP * (g + 1)):
            if o == e:
                continue
            ahead = sel[o] > sel[e]
            if o < e:
                ahead = jnp.logical_or(ahead, sel[o] == sel[e])
            rank = rank + ahead.astype(jnp.int32)
        chosen.append(jnp.logical_and(best == g, rank < TOP_K))
    total = jnp.zeros_like(aff[0])
    for e in range(N_EXPERTS):
        total = total + jnp.where(chosen[e], aff[e], 0.0)
    gates = [jnp.where(chosen[e], aff[e] / total, 0.0) for e in range(N_EXPERTS)]
    return gates, [c.astype(F32) for c in chosen]


def _out_kernel(x_ref, ya_ref, hc_ref, g_ref, *refs):
    mod_refs = refs[:PAIR]
    (wr_ref, wi_ref, br_ref, bi_ref, lam_ref, nrm_ref, w_ref, rwc_ref, rb_ref,
     xo_ref, hp_ref, rt_ref, h_scr, hl_scr, p_scr) = refs[PAIR:]
    rows = PAIR * TILE

    @pl.when(pl.program_id(1) == 0)
    def _():
        h_scr[...] = jnp.zeros_like(h_scr)

    ybs = []
    for k in range(PAIR):
        a_r, b_r = _lru_gates(hc_ref[k, :, LRU_WIDTH:].astype(F32), 1, wr_ref, wi_ref, br_ref, bi_ref, lam_ref)
        hrev, h_last = _lru_scan(a_r, b_r, h_scr[k], True, hl_scr, p_scr)
        h_scr[k] = h_last
        gx = g_ref[k].astype(F32)
        gate = (0.5 * gx) * (1.0 + jnp.tanh(gx * (GELU_C + (GELU_C * 0.044715) * (gx * gx))))
        ybs.append((gate * (hc_ref[k, :, 0:LRU_WIDTH].astype(F32) + hrev)).astype(BF16))
    ya = ya_ref[...].reshape(rows, NA_WIDTH)
    yb = jnp.concatenate(ybs, axis=0)
    y = jnp.dot(ya, w_ref[0:NA_WIDTH, :], preferred_element_type=F32)
    y = y + jnp.dot(yb, w_ref[NA_WIDTH:, :], preferred_element_type=F32)
    hs = []
    for k in range(PAIR):
        m = mod_refs[k][0, 0]
        x = x_ref[k] + m[2:3, :] * y[k * TILE:(k + 1) * TILE, :]
        xo_ref[k] = x
        ms = jnp.mean(x * x, axis=-1, keepdims=True)
        h = (x * lax.rsqrt(ms + EPS)) * nrm_ref[...]
        h = h * (1.0 + m[4:5, :]) + m[3:4, :]
        hp_ref[k] = _pack_rows(h)
        hs.append(h)
    h = jnp.concatenate(hs, axis=0)
    h_hi = h.astype(BF16)
    h_lo = (h - h_hi.astype(F32)).astype(BF16)
    nt = (((1,), (1,)), ((), ()))
    rwc = rwc_ref[...]
    both = lax.dot_general(rwc, h_hi, nt, preferred_element_type=F32)
    lg = (both[0:N_EXPERTS, :] + both[N_EXPERTS:, :]
          + lax.dot_general(rwc[0:N_EXPERTS, :], h_lo, nt, preferred_element_type=F32))
    aff_all = jax.nn.sigmoid(lg)
    sel_all = aff_all + rb_ref[...]
    aff = [aff_all[e:e + 1, :] for e in range(N_EXPERTS)]
    sel = [sel_all[e:e + 1, :] for e in range(N_EXPERTS)]
    gates, chosen = _route(sel, aff)
    rt = jnp.concatenate(gates + chosen, axis=0)
    for k in range(PAIR):
        rt_ref[k, 0] = rt[:, k * TILE:(k + 1) * TILE]


def _out_proj(l, boff, xs, ya, hc, proj, lru, mod, nrm, w_out, rwc, rb):
    rev = lambda i: jnp.where(i == 0, 0, N_TILES - i)
    tok = lambda width, col=0: pl.BlockSpec((PAIR, TILE, width), lambda bp, i: (bp, rev(i), col))
    mod_specs = [pl.BlockSpec((1, 1, N_MOD, D_MODEL),
                              lambda bp, i, k=k: (l, _mod_row(boff + PAIR * bp + k, i), 0, 0)) for k in range(PAIR)]
    n_cb = LRU_WIDTH // LRU_CH
    lru_specs = [_full_spec((2, n_cb, LRU_CH, LRU_CH)), _full_spec((2, n_cb, LRU_CH, LRU_CH)),
                 _full_spec((2, LRU_WIDTH)), _full_spec((2, LRU_WIDTH)), _full_spec((2, LRU_WIDTH))]
    return pl.pallas_call(
        _out_kernel,
        grid=(SB // PAIR, N_TILES),
        in_specs=[tok(D_MODEL), tok(NA_WIDTH), tok(2 * LRU_WIDTH), tok(LRU_WIDTH, COL_G)]
        + mod_specs + lru_specs
        + [_full_spec((1, D_MODEL)), _full_spec((D_MODEL, D_MODEL)), _full_spec((2 * N_EXPERTS, D_MODEL)),
           _full_spec((N_EXPERTS, 1))],
        out_specs=[tok(D_MODEL), tok(PACK_W),
                   pl.BlockSpec((PAIR, 1, ROUTE_ROWS, TILE), lambda bp, i: (bp, rev(i), 0, 0))],
        out_shape=[jax.ShapeDtypeStruct((SB, L_TOT, D_MODEL), F32),
                   jax.ShapeDtypeStruct((SB, L_TOT, PACK_W), I32),
                   jax.ShapeDtypeStruct((SB, N_TILES, ROUTE_ROWS, TILE), F32)],
        scratch_shapes=[pltpu.VMEM((PAIR, 1, LRU_WIDTH), F32), pltpu.VMEM((TILE, LRU_WIDTH), F32),
                        pltpu.VMEM((TILE, LRU_WIDTH), F32)],
        compiler_params=_cparams(("parallel", "arbitrary")),
        name="out_proj_router",
    )(xs, ya, hc, proj, *([mod] * PAIR), *lru[2:], nrm, w_out, rwc, rb)


def _pos_kernel(rt_ref, tri_ref, start_ref, pos_ref, wt_ref, run_scr):
    @pl.when(pl.program_id(0) == 0)
    def _():
        run_scr[...] = jnp.zeros_like(run_scr)

    base = start_ref[...] + run_scr[...]
    for k in range(POS_TILES):
        gates = rt_ref[k, 0:N_EXPERTS, :]
        chosen = rt_ref[k, N_EXPERTS:, :]
        rank = jnp.dot(chosen.astype(BF16), tri_ref[...], preferred_element_type=F32)
        posf = rank + base
        seen = jnp.zeros((1, TILE), F32)
        p0 = jnp.zeros((1, TILE), F32)
        p1 = jnp.zeros((1, TILE), F32)
        w0 = jnp.zeros((1, TILE), F32)
        w1 = jnp.zeros((1, TILE), F32)
        for e in range(N_EXPERTS):
            ch = chosen[e:e + 1, :]
            first = ch * (1.0 - seen)
            second = ch * seen
            p0 = p0 + first * posf[e:e + 1, :]
            p1 = p1 + second * posf[e:e + 1, :]
            w0 = w0 + first * gates[e:e + 1, :]
            w1 = w1 + second * gates[e:e + 1, :]
            seen = jnp.minimum(seen + ch, 1.0)
        pos_ref[k] = jnp.concatenate([p0, p1], axis=0).astype(I32)
        wpad = jnp.concatenate([w0, w1, jnp.zeros((LANES - TOP_K, TILE), F32)], axis=0)
        wt_ref[k * TILE:(k + 1) * TILE, :] = jnp.transpose(wpad)
        base = base + jnp.sum(chosen, axis=1, keepdims=True)
    run_scr[...] = base - start_ref[...]


def _positions(route, tri, start):
    return pl.pallas_call(
        _pos_kernel,
        grid=(N_TOK_TILES // POS_TILES,),
        in_specs=[pl.BlockSpec((POS_TILES, ROUTE_ROWS, TILE), lambda i: (i, 0, 0)),
                  pl.BlockSpec((TILE, TILE), lambda i: (0, 0)),
                  pl.BlockSpec((N_EXPERTS, 1), lambda i: (0, 0))],
        out_specs=[pl.BlockSpec((POS_TILES, TOP_K, TILE), lambda i: (i, 0, 0)),
                   pl.BlockSpec((POS_TILES * TILE, LANES), lambda i: (i, 0))],
        out_shape=[jax.ShapeDtypeStruct((N_TOK_TILES, TOP_K, TILE), I32),
                   jax.ShapeDtypeStruct((N_TOK, LANES), F32)],
        scratch_shapes=[pltpu.VMEM((N_EXPERTS, 1), F32)],
        compiler_params=_cparams(("arbitrary",)),
        name="moe_positions",
    )(route, tri, start)


def _work_items(counts):
    smem = pl.BlockSpec(memory_space=pltpu.SMEM)
    item = jax.ShapeDtypeStruct((N_ITEMS,), I32)
    return pl.pallas_call(
        _items_kernel,
        in_specs=[smem],
        out_specs=[smem] * 4,
        out_shape=[item] * 4,
        name="moe_work_items",
    )(counts)


def _items_kernel(cnt_ref, tile_ref, exp_ref, lo_ref, hi_ref):
    n = jnp.int32(0)
    start = jnp.int32(0)
    last_e = jnp.int32(0)
    for e in range(N_EXPERTS):
        cnt = cnt_ref[e]
        end = start + cnt
        first = start // EXP_TILE
        n_tiles = jnp.where(cnt > 0, (end - 1) // EXP_TILE - first + 1, 0)

        def put(j, carry, e=e, n=n, start=start, end=end, first=first):
            tile = first + j
            tile_ref[n + j] = tile
            exp_ref[n + j] = jnp.int32(e)
            lo_ref[n + j] = jnp.maximum(start - tile * EXP_TILE, 0)
            hi_ref[n + j] = jnp.minimum(end - tile * EXP_TILE, EXP_TILE)
            return carry

        lax.fori_loop(0, n_tiles, put, 0)
        n = n + n_tiles
        start = end
        last_e = jnp.where(cnt > 0, e, last_e)

    def pad(j, carry):
        tile_ref[j] = jnp.int32(N_EXP_TILES - 1)
        exp_ref[j] = last_e
        lo_ref[j] = jnp.int32(0)
        hi_ref[j] = jnp.int32(0)
        return carry

    lax.fori_loop(n, N_ITEMS, pad, 0)


def _sc_worker_base():
    return (lax.axis_index("s") * SC_CORES + lax.axis_index("c")) * SC_ROWS


def _sc_dispatch_body(h_hbm, p0_hbm, p1_hbm, out_hbm, i0_v, i1_v, rows_v, sem):
    base = _sc_worker_base()

    @pl.loop(0, SC_N_CHUNKS)
    def _(j):
        off = base + j * SC_CHUNK
        pltpu.sync_copy(h_hbm.at[pl.ds(off, SC_CHUNK)], rows_v)
        pltpu.sync_copy(p0_hbm.at[pl.ds(off, SC_CHUNK)], i0_v)
        pltpu.sync_copy(p1_hbm.at[pl.ds(off, SC_CHUNK)], i1_v)
        c0 = pltpu.async_copy(rows_v, out_hbm.at[i0_v], sem)
        c1 = pltpu.async_copy(rows_v, out_hbm.at[i1_v], sem)
        c0.wait()
        c1.wait()


def _sc_combine_body(y_hbm, p0_hbm, p1_hbm, g0_hbm, g1_hbm, i_v, rows_v, sem):
    base = _sc_worker_base()

    @pl.loop(0, SC_N_CHUNKS)
    def _(j):
        off = base + j * SC_CHUNK
        for p_hbm, g_hbm in ((p0_hbm, g0_hbm), (p1_hbm, g1_hbm)):
            pltpu.sync_copy(p_hbm.at[pl.ds(off, SC_CHUNK)], i_v)
            pltpu.async_copy(y_hbm.at[i_v], rows_v, sem).wait()
            pltpu.sync_copy(rows_v, g_hbm.at[pl.ds(off, SC_CHUNK)])


def _sc_mesh():
    return plsc.VectorSubcoreMesh(core_axis_name="c", subcore_axis_name="s")


def _dispatch(hp, pos0, pos1):
    return pl.kernel(
        _sc_dispatch_body,
        out_type=jax.ShapeDtypeStruct((N_ASSIGN, PACK_W), I32),
        mesh=_sc_mesh(),
        scratch_types=[pltpu.VMEM((SC_CHUNK,), I32), pltpu.VMEM((SC_CHUNK,), I32),
                       pltpu.VMEM((SC_CHUNK, PACK_W), I32), pltpu.SemaphoreType.DMA],
        name="moe_dispatch",
    )(hp, pos0, pos1)


def _combine(ys, pos0, pos1):
    row = jax.ShapeDtypeStruct((N_TOK, PACK_W), I32)
    return pl.kernel(
        _sc_combine_body,
        out_type=[row, row],
        mesh=_sc_mesh(),
        scratch_types=[pltpu.VMEM((SC_CHUNK,), I32), pltpu.VMEM((SC_CHUNK, PACK_W), I32),
                       pltpu.SemaphoreType.DMA],
        name="moe_combine",
    )(ys, pos0, pos1)


def _expert_kernel(tile_ref, exp_ref, lo_ref, hi_ref, x_ref, w1_ref, w3_ref, w2_ref, o_ref, w13_scr, w2_scr):
    i = pl.program_id(0)
    prev = jnp.maximum(i - 1, 0)

    @pl.when(jnp.logical_or(i == 0, exp_ref[i] != exp_ref[prev]))
    def _():
        w13_scr[:, 0:EXPERT_FF] = w1_ref[0, 0].astype(BF16)
        w13_scr[:, EXPERT_FF:] = w3_ref[0, 0].astype(BF16)
        w2_scr[...] = w2_ref[0, 0].astype(BF16)

    lo_f, hi_f = _unpack_rows(x_ref[...])
    h = jnp.concatenate([lo_f.astype(BF16), hi_f.astype(BF16)], axis=1)
    ab = jnp.dot(h, w13_scr[...], preferred_element_type=F32)
    a = ab[:, 0:EXPERT_FF]
    b = ab[:, EXPERT_FF:]
    t = ((a * jax.nn.sigmoid(a)) * b).astype(BF16)
    y = _pack_rows(jnp.dot(t, w2_scr[...], preferred_element_type=F32))
    row = lax.broadcasted_iota(jnp.int32, (EXP_TILE, PACK_W), 0)
    mine = jnp.logical_and(row >= lo_ref[i], row < hi_ref[i])
    revisit = jnp.logical_and(i > 0, tile_ref[i] == tile_ref[prev])

    @pl.when(jnp.logical_not(revisit))
    def _():
        o_ref[...] = jnp.where(mine, y, 0)

    @pl.when(revisit)
    def _():
        o_ref[...] = jnp.where(mine, y, o_ref[...])


def _experts(l, xs_sorted, items, w1, w3, w2):
    tile, e, lo, hi = items
    grid_spec = pltpu.PrefetchScalarGridSpec(
        num_scalar_prefetch=4,
        grid=(N_ITEMS,),
        in_specs=[pl.BlockSpec((EXP_TILE, PACK_W), lambda i, t, e, lo, hi: (t[i], 0)),
                  pl.BlockSpec((1, 1, D_MODEL, EXPERT_FF), lambda i, t, e, lo, hi: (l, e[i], 0, 0)),
                  pl.BlockSpec((1, 1, D_MODEL, EXPERT_FF), lambda i, t, e, lo, hi: (l, e[i], 0, 0)),
                  pl.BlockSpec((1, 1, EXPERT_FF, D_MODEL), lambda i, t, e, lo, hi: (l, e[i], 0, 0))],
        out_specs=pl.BlockSpec((EXP_TILE, PACK_W), lambda i, t, e, lo, hi: (t[i], 0)),
        scratch_shapes=[pltpu.VMEM((D_MODEL, 2 * EXPERT_FF), BF16), pltpu.VMEM((EXPERT_FF, D_MODEL), BF16)],
    )
    return pl.pallas_call(
        _expert_kernel,
        grid_spec=grid_spec,
        out_shape=jax.ShapeDtypeStruct((N_ASSIGN, PACK_W), I32),
        compiler_params=_cparams(("arbitrary",)),
        name="moe_experts",
    )(tile, e, lo, hi, xs_sorted, w1, w3, w2)


def _moe(l, hp, route, tri, w1, w3, w2):
    counts = jnp.sum(route[:, N_EXPERTS:, :], axis=(0, 2))
    start = (jnp.cumsum(counts) - counts).reshape(N_EXPERTS, 1)
    pos, wt = _positions(route, tri, start)
    pos0 = pos[:, 0, :].reshape(N_TOK)
    pos1 = pos[:, 1, :].reshape(N_TOK)
    items = _work_items(counts.astype(I32))
    xs_sorted = _dispatch(hp.reshape(N_TOK, PACK_W), pos0, pos1)
    ys = _experts(l, xs_sorted, items, w1, w3, w2)
    g0, g1 = _combine(ys, pos0, pos1)
    shape = (SB, L_TOT, PACK_W)
    return g0.reshape(shape), g1.reshape(shape), wt.reshape(SB, L_TOT, LANES)


def _final_kernel(x_ref, g0_ref, g1_ref, wt_ref, mod_ref, *refs):
    o_ref, scr = refs[-2:]
    for b in range(SB):
        x = _moe_residual(x_ref[b], g0_ref[b], g1_ref[b], wt_ref[b], mod_ref[0, b][5:6, :])
        o_ref[b] = _from_scan_major(x, scr)


def _final(streams, mod):
    lat = lambda width: pl.BlockSpec((SB, TILE, width), lambda j: (0, j + 1, 0))
    out = None
    for k, st in enumerate(streams):
        in_specs = [lat(D_MODEL), lat(PACK_W), lat(PACK_W), lat(LANES),
                    pl.BlockSpec((1, SB, N_MOD, D_MODEL), lambda j, k=k: (DEPTH - 1, k, 0, 0))]
        args = [st["xs"]] + list(st["moe"]) + [mod]
        aliases = {}
        if out is not None:
            in_specs.append(pl.BlockSpec(memory_space=pl.ANY))
            args.append(out)
            aliases = {len(args) - 1: 0}
        out = pl.pallas_call(
            _final_kernel,
            grid=(N_LAT_TILES,),
            in_specs=in_specs,
            out_specs=pl.BlockSpec((SB, TILE, D_MODEL), lambda j, k=k: (k, j, 0)),
            out_shape=jax.ShapeDtypeStruct((BATCH, SEQ, D_MODEL), F32),
            scratch_shapes=[pltpu.VMEM((D_MODEL // LANES, TILE, LANES), F32)],
            input_output_aliases=aliases,
            compiler_params=_cparams(("arbitrary",)),
            name="final_residual",
        )(*args)
    return out


def _bias_kernel(tab_ref, perm_ref, o_ref, scr):
    rows_q = TILE // GRID_W
    lane = lax.broadcasted_iota(jnp.int32, (GROUPS, TILE), 1)
    ka = (lane >> 1) & (rows_q - 1)
    neg = jnp.full((GROUPS, TILE), NEG_INF, F32)
    qc = lax.broadcasted_iota(jnp.int32, (GRID_W, GRID_W), 0)
    kc = lax.broadcasted_iota(jnp.int32, (GRID_W, GRID_W), 1)
    col0 = jnp.clip(qc - NA_WIN_COLS // 2, 0, GRID_W - NA_WIN_COLS)
    in_win = jnp.logical_and(kc >= col0, kc < col0 + NA_WIN_COLS)
    toeplitz = []
    for r in range(2 * NA_WIN_ROWS - 1):
        row = jnp.broadcast_to(tab_ref[0, r:r + 1, :], (GRID_W, LANES))
        shifted = pltpu.roll(row, LANES - (GRID_W - 1), 1, stride=1, stride_axis=0)
        toeplitz.append(jnp.where(in_win, shifted[:, 0:GRID_W], NEG_INF))

    def interleaved(r0):
        acc = None
        for k in range(rows_q):
            val = toeplitz[r0 + k]
            t1 = val.astype(BF16)
            r1 = val - t1.astype(F32)
            t2 = r1.astype(BF16)
            t3 = (r1 - t2.astype(F32)).astype(BF16)
            for term in (t1, t2, t3):
                moved = jnp.dot(term, perm_ref[k], preferred_element_type=F32)
                acc = moved if acc is None else acc + moved
        return acc

    t4 = [interleaved(r0) for r0 in range(3 * rows_q)]
    for a in range(rows_q):
        for qs in range(SUB // rows_q):
            s = (SUB // rows_q) * a + qs
            for kt in range(3):
                src = t4[rows_q * kt - a + 3][qs * GROUPS:(qs + 1) * GROUPS, :]
                c = rows_q * kt + ka
                variants = (
                    src if kt >= 1 else neg,
                    jnp.where(jnp.logical_and(c >= a, c <= a + NA_WIN_ROWS - 1), src, neg),
                    src if kt <= 1 else neg,
                )
                for v, val in enumerate(variants):
                    for half in range(TILE // LANES):
                        scr[v, 2 * kt + half, pl.ds(s, GROUPS, stride=SUB), :] = val[:, half * LANES:(half + 1) * LANES]
    for v in range(3):
        o_ref[v, 0] = jnp.concatenate([scr[v, j] for j in range(3 * TILE // LANES)], axis=1)
    o_ref[3, 0] = jnp.full((TILE, 3 * TILE), NEG_INF, F32)


def _bias_tiles(table):
    rows_q = TILE // GRID_W
    n_r = 2 * NA_WIN_ROWS - 1
    tab = jnp.pad(table.astype(F32) * LOG2E, ((0, 0), (0, 0), (BIAS_PAD, LANES - BIAS_PAD - (2 * NA_WIN_COLS - 1))))
    n_h = table.shape[0]
    half = GRID_W // GROUPS
    perm = np.zeros((rows_q, GRID_W, TILE), np.float32)
    for k in range(rows_q):
        for ks in range(half):
            for kg in range(GROUPS):
                perm[k, ks * GROUPS + kg, kg * SUB + k * half + ks] = 1.0
    return pl.pallas_call(
        _bias_kernel,
        grid=(n_h,),
        in_specs=[pl.BlockSpec((1, n_r, LANES), lambda h: (h, 0, 0)),
                  pl.BlockSpec((rows_q, GRID_W, TILE), lambda h: (0, 0, 0))],
        out_specs=pl.BlockSpec((4, 1, TILE, 3 * TILE), lambda h: (0, h, 0, 0)),
        out_shape=jax.ShapeDtypeStruct((4, n_h, TILE, 3 * TILE), F32),
        scratch_shapes=[pltpu.VMEM((3, 3 * TILE // LANES, TILE, LANES), F32)],
        compiler_params=_cparams(("parallel",)),
        name="bias_tiles",
    )(tab, jnp.asarray(perm, BF16))


def _block_diag(w, n_chunks):
    per = LRU_BLOCKS // n_chunks
    w = w.reshape(2, n_chunks, per, LRU_BLOCK, LRU_BLOCK)
    eye = jnp.eye(per, dtype=w.dtype)
    out = jnp.einsum('dcpij,pq->dcpiqj', w, eye)
    return out.reshape(2, n_chunks, per * LRU_BLOCK, per * LRU_BLOCK)


def kernel(x, c, ctx, c_ctx, w_mod, b_mod, norm_mix, norm_ffn, w_in, w_out, q_gain, k_gain, na_bias,
           conv_w, conv_b, lru_w_r, lru_b_r, lru_w_i, lru_b_i, lru_lambda, router_w, router_b,
           exp_w1, exp_w3, exp_w2):
    cs = jnp.concatenate([c, c_ctx[None, :], jnp.zeros((MOD_ROWS - BATCH - 1, D_MODEL), F32)], axis=0)
    mod = _modulation(cs, w_mod, b_mod).reshape(DEPTH, MOD_ROWS, N_MOD, D_MODEL)

    head_of = np.arange(NA_WIDTH) // HEAD_DIM
    bd = jnp.asarray((head_of[:, None] == head_of[None, :]).astype(np.float32) / HEAD_DIM, BF16)
    tri = jnp.asarray(np.triu(np.ones((TILE, TILE), np.float32), 1), BF16)
    rwt = router_w.T
    rwh = rwt.astype(BF16)
    rwc = jnp.concatenate([rwh, (rwt - rwh.astype(F32)).astype(BF16)], axis=0)
    rb = router_b.reshape(N_EXPERTS, 1)
    n_cb = LRU_WIDTH // LRU_CH

    bias_tiles = _bias_tiles(na_bias.reshape(DEPTH * NA_HEADS, 2 * NA_WIN_ROWS - 1, 2 * NA_WIN_COLS - 1))
    streams = [{"xs": (x, ctx), "moe": None, "boff": sidx * SB} for sidx in range(STREAMS)]
    for l in range(DEPTH):
        qg = jnp.tile(q_gain[l] * (ATTN_SCALE * LOG2E), NA_HEADS)[None, :]
        kg = jnp.tile(k_gain[l], NA_HEADS)[None, :]
        w_in_l = w_in[l].astype(BF16)
        w_out_l = w_out[l].astype(BF16)
        wr = (0.5 * _block_diag(lru_w_r[l], n_cb)).astype(BF16)
        wi = (0.5 * _block_diag(lru_w_i[l], n_cb)).astype(BF16)
        lru = (conv_w[l], conv_b[l][None, :], wr, wi, 0.5 * lru_b_r[l], 0.5 * lru_b_i[l], lru_lambda[l])
        for st in streams:
            boff = st["boff"]
            xs, proj = _in_proj(l, boff, st["xs"], st["moe"], mod, norm_mix[l][None, :], w_in_l, bd, qg, kg)
            ya, hc = _attention(l, proj, bias_tiles, lru)
            xs, hp, route = _out_proj(l, boff, xs, ya, hc, proj, lru, mod, norm_ffn[l][None, :], w_out_l, rwc, rb)
            st["xs"] = xs
            st["moe"] = _moe(l, hp, route.reshape(N_TOK_TILES, ROUTE_ROWS, TILE), tri, exp_w1, exp_w3, exp_w2)
    return _final(streams, mod)
```

```python
import functools

import jax
import jax.numpy as jnp
import numpy as np
from jax import lax
from jax.experimental import pallas as pl
from jax.experimental.pallas import tpu as pltpu
from jax.experimental.pallas import tpu_sc as plsc

F32 = jnp.float32
BF16 = jnp.bfloat16
I32 = jnp.int32

D_MODEL = 1024
BATCH = 4
SEQ = 8192
DEPTH = 4
GRID_W = 64
CTX_LEN = 256
HEAD_DIM = 64
NA_WIDTH = 512
NA_HEADS = 8
NA_WIN_ROWS = 8
NA_WIN_COLS = 16
LRU_WIDTH = 512
LRU_BLOCKS = 8
LRU_BLOCK = 64
CONV_W = 4
LRU_C = 8.0
IN_COLS = 3 * NA_WIDTH + 2 * LRU_WIDTH
COL_KV, COL_Q, COL_U, COL_G = 0, 2, 3, 4
N_EXPERTS = 16
N_GROUPS = 4
EXPERTS_PER_GROUP = 4
TOP_K = 2
EXPERT_FF = 512
N_MOD = 6
ATTN_SCALE = HEAD_DIM ** -0.5
LOG2E = 1.4426950408889634
EPS = 1e-6
NEG_INF = -1e30
TINY = 1e-30
GELU_C = 0.7978845608028654

TILE = 256
WIN_TILES = 3
BIAS_VARIANTS = 3
HALO = 16
SUB = 8
LANES = 128
GROUPS = TILE // SUB
L_TOT = CTX_LEN + SEQ
N_TILES = L_TOT // TILE
N_LAT_TILES = SEQ // TILE
STREAMS = 2
SB = BATCH // STREAMS
N_TOK = SB * L_TOT
N_TOK_TILES = N_TOK // TILE
PAIR = 2
LRU_CH = 256
MOD_ROWS = 8
VMEM_LIMIT = 56 * 1024 * 1024

PACK_W = D_MODEL // 2
HI_MASK = -65536
N_ASSIGN = TOP_K * N_TOK
EXP_TILE = 512
N_EXP_TILES = N_ASSIGN // EXP_TILE
N_ITEMS = N_EXP_TILES + N_EXPERTS - 1
BIAS_PAD = GRID_W - NA_WIN_COLS
ROUTE_ROWS = 2 * N_EXPERTS
POS_TILES = 22

SC_CORES = 2
SC_SUBCORES = 16
SC_WORKERS = SC_CORES * SC_SUBCORES
SC_ROWS = N_TOK // SC_WORKERS
SC_CHUNK = 88
SC_N_CHUNKS = SC_ROWS // SC_CHUNK


def _cparams(sem):
    return pltpu.CompilerParams(dimension_semantics=sem, vmem_limit_bytes=VMEM_LIMIT)


def _pack_rows(v):
    lo = pltpu.bitcast(v[:, :PACK_W].astype(BF16).astype(F32), I32)
    hi = pltpu.bitcast(v[:, PACK_W:].astype(BF16).astype(F32), I32)
    return ((lo >> 16) & 0xFFFF) | (hi & HI_MASK)


def _unpack_rows(p):
    return pltpu.bitcast(p << 16, F32), pltpu.bitcast(p & HI_MASK, F32)


def _mod_kernel(c_ref, w_ref, b_ref, o_ref):
    c = c_ref[...]
    s = c * jax.nn.sigmoid(c)
    o_ref[0] = jnp.dot(s.astype(BF16), w_ref[0].astype(BF16), preferred_element_type=F32) + b_ref[0]


def _modulation(cs, w_mod, b_mod):
    return pl.pallas_call(
        _mod_kernel,
        grid=(DEPTH, N_MOD),
        in_specs=[
            pl.BlockSpec((MOD_ROWS, D_MODEL), lambda l, n: (0, 0)),
            pl.BlockSpec((1, D_MODEL, D_MODEL), lambda l, n: (l, 0, n)),
            pl.BlockSpec((1, 1, D_MODEL), lambda l, n: (l, 0, n)),
        ],
        out_specs=pl.BlockSpec((1, MOD_ROWS, D_MODEL), lambda l, n: (l, 0, n)),
        out_shape=jax.ShapeDtypeStruct((DEPTH, MOD_ROWS, N_MOD * D_MODEL), F32),
        compiler_params=_cparams(("arbitrary", "arbitrary")),
        name="modulation",
    )(cs, w_mod, b_mod.reshape(DEPTH, 1, N_MOD * D_MODEL))


def _mod_row(b, i):
    return jnp.where(i == 0, BATCH, b)


def _moe_residual(x, g0, g1, wt, gate_row):
    lo0, hi0 = _unpack_rows(g0)
    lo1, hi1 = _unpack_rows(g1)
    w0 = wt[:, 0:1]
    w1 = wt[:, 1:2]
    f = jnp.concatenate([w0 * lo0 + w1 * lo1, w0 * hi0 + w1 * hi1], axis=1)
    return x + gate_row * f


def _pair_tok_spec(width):
    return pl.BlockSpec((PAIR, TILE, width), lambda bp, i: (bp, i, 0))


def _pair_mod_specs(layer, boff):
    return [pl.BlockSpec((1, 1, N_MOD, D_MODEL),
                         lambda bp, i, k=k: (layer, _mod_row(boff + PAIR * bp + k, i), 0, 0))
            for k in range(PAIR)]


def _full_spec(shape):
    return pl.BlockSpec(shape, lambda bp, i: tuple(0 for _ in shape))


def _to_scan_major(src_ref, scr):
    n_slab = D_MODEL // LANES
    for s in range(SUB):
        for j in range(n_slab):
            scr[j, pl.ds(s, GROUPS, stride=SUB), :] = src_ref[s * GROUPS:(s + 1) * GROUPS, j * LANES:(j + 1) * LANES]
    return jnp.concatenate([scr[j] for j in range(n_slab)], axis=1)


def _from_scan_major(val, scr):
    n_slab = D_MODEL // LANES
    for j in range(n_slab):
        scr[j] = val[:, j * LANES:(j + 1) * LANES]
    blocks = [jnp.concatenate([scr[j, pl.ds(s, GROUPS, stride=SUB), :] for j in range(n_slab)], axis=1)
              for s in range(SUB)]
    return jnp.concatenate(blocks, axis=0)


def _in_kernel(has_prev, *refs):
    if has_prev:
        x_ref, g0_ref, g1_ref, wt_ref = refs[:4]
        mprev_refs = refs[4:4 + PAIR]
        refs = (x_ref,) + refs[4 + PAIR:]
    else:
        x_ref, ctx_ref = refs[:2]
        refs = (x_ref,) + refs[2:]
    mod_refs = refs[1:1 + PAIR]
    nrm_ref, w_ref, bd_ref, qg_ref, kg_ref = refs[1 + PAIR:6 + PAIR]
    outs = refs[6 + PAIR:]
    xo_ref, proj_ref = outs[:2]
    x_ref = refs[0]
    hs = []
    for k in range(PAIR):
        if has_prev:
            x = _moe_residual(x_ref[k], g0_ref[k], g1_ref[k], wt_ref[k], mprev_refs[k][0, 0][5:6, :])
            xo_ref[k] = x
        else:
            perm_scr = outs[2]

            @pl.when(pl.program_id(1) == 0)
            def _():
                xo_ref[k] = _to_scan_major(ctx_ref.at[k], perm_scr)

            @pl.when(pl.program_id(1) > 0)
            def _():
                xo_ref[k] = _to_scan_major(x_ref.at[k], perm_scr)

            x = xo_ref[k]
        m = mod_refs[k][0, 0]
        ms = jnp.mean(x * x, axis=-1, keepdims=True)
        h = (x * lax.rsqrt(ms + EPS)) * nrm_ref[...]
        hs.append((h * (1.0 + m[1:2, :]) + m[0:1, :]).astype(BF16))
    acc = jnp.dot(jnp.concatenate(hs, axis=0), w_ref[...], preferred_element_type=F32)
    bd = bd_ref[...]

    def head_norm(t, gain):
        ss = jnp.dot((t * t).astype(BF16), bd, preferred_element_type=F32)
        return (t * lax.rsqrt(ss + EPS)) * gain

    def put(col_block, val):
        for k in range(PAIR):
            proj_ref[k, :, col_block * NA_WIDTH:(col_block + 1) * NA_WIDTH] = val[k * TILE:(k + 1) * TILE, :].astype(BF16)

    put(COL_Q, head_norm(acc[:, 0:NA_WIDTH], qg_ref[...]))
    put(COL_KV, head_norm(acc[:, NA_WIDTH:2 * NA_WIDTH], kg_ref[...]))
    put(COL_KV + 1, acc[:, 2 * NA_WIDTH:3 * NA_WIDTH])
    put(COL_U, acc[:, 3 * NA_WIDTH:3 * NA_WIDTH + LRU_WIDTH])
    put(COL_G, acc[:, 3 * NA_WIDTH + LRU_WIDTH:])


def _in_proj(l, boff, xs, moe_prev, mod, nrm, w_in, bd, qg, kg):
    has_prev = moe_prev is not None
    if has_prev:
        in_specs = [_pair_tok_spec(D_MODEL)]
        args = [xs]
        in_specs += [_pair_tok_spec(PACK_W), _pair_tok_spec(PACK_W), _pair_tok_spec(LANES)]
        in_specs += _pair_mod_specs(l - 1, boff)
        args += list(moe_prev) + [mod] * PAIR
    else:
        pair0 = boff // PAIR
        in_specs = [pl.BlockSpec((PAIR, TILE, D_MODEL), lambda bp, i: (pair0 + bp, jnp.maximum(i - 1, 0), 0)),
                    pl.BlockSpec((PAIR, TILE, D_MODEL), lambda bp, i: (pair0 + bp, 0, 0))]
        args = list(xs)
    in_specs += _pair_mod_specs(l, boff) + [_full_spec((1, D_MODEL)), _full_spec((D_MODEL, IN_COLS)),
                                            _full_spec((NA_WIDTH, NA_WIDTH)), _full_spec((1, NA_WIDTH)),
                                            _full_spec((1, NA_WIDTH))]
    args += [mod] * PAIR + [nrm, w_in, bd, qg, kg]
    out_shape = [jax.ShapeDtypeStruct((SB, L_TOT, D_MODEL), F32), jax.ShapeDtypeStruct((SB, L_TOT, IN_COLS), BF16)]
    out_specs = [_pair_tok_spec(D_MODEL), _pair_tok_spec(IN_COLS)]
    scratch = [] if has_prev else [pltpu.VMEM((D_MODEL // LANES, TILE, LANES), F32)]
    outs = pl.pallas_call(
        functools.partial(_in_kernel, has_prev),
        grid=(SB // PAIR, N_TILES),
        in_specs=in_specs,
        out_specs=out_specs,
        out_shape=out_shape,
        scratch_shapes=scratch,
        compiler_params=_cparams(("parallel", "arbitrary")),
        name="in_proj",
    )(*args)
    return outs[0], outs[1]


def _attn_kernel(q_ref, kvp_ref, kvc_ref, kvn_ref, kvx_ref, bias_ref,
                 u_ref, up_ref, un_ref, cw_ref, cb_ref, wr_ref, wi_ref, br_ref, bi_ref, lam_ref,
                 o_ref, hc_ref, h_scr, hl_scr, p_scr):
    j = pl.program_id(1)

    @pl.when(j == 0)
    def _():
        h_scr[...] = jnp.zeros_like(h_scr)

    has_prev = jnp.where(j >= 2, 1.0, 0.0).astype(F32)
    has_next = jnp.where(jnp.logical_and(j >= 1, j <= N_TILES - 2), 1.0, 0.0).astype(F32)
    for b in range(PAIR):
        cv = _lru_conv(u_ref[b].astype(F32), up_ref[b].astype(F32), un_ref[b].astype(F32), has_prev, has_next,
                       cw_ref[...], cb_ref[...])
        hc_ref[b, :, LRU_WIDTH:] = cv.astype(BF16)
        a_f, b_f = _lru_gates(cv, 0, wr_ref, wi_ref, br_ref, bi_ref, lam_ref)
        hfull, h_last = _lru_scan(a_f, b_f, h_scr[b], False, hl_scr, p_scr)
        hc_ref[b, :, 0:LRU_WIDTH] = hfull.astype(BF16)
        h_scr[b] = h_last

    pair_w = 2 * HEAD_DIM
    lane = lax.broadcasted_iota(jnp.int32, (TILE, pair_w), 1)
    low = lane < HEAD_DIM
    nt = (((1,), (1,)), ((), ()))
    n_win = WIN_TILES * TILE

    def attend(kv_refs, windowed):
        low_kv = lax.broadcasted_iota(jnp.int32, (len(kv_refs) * TILE, pair_w), 1) < HEAD_DIM
        for b, hp in [(b, hp) for b in range(PAIR) for hp in range(NA_HEADS // 2)]:
            cols = slice(hp * pair_w, (hp + 1) * pair_w)
            vcols = slice(NA_WIDTH + hp * pair_w, NA_WIDTH + (hp + 1) * pair_w)
            q = q_ref[b, :, cols]
            kb = jnp.concatenate([r[b, :, cols] for r in kv_refs], axis=0)
            vb = jnp.concatenate([r[b, :, vcols] for r in kv_refs], axis=0)
            outs = []
            for hh in range(2):
                own = low if hh == 0 else jnp.logical_not(low)
                own_kv = low_kv if hh == 0 else jnp.logical_not(low_kv)
                qh = jnp.where(own, q, jnp.zeros_like(q))
                s = lax.dot_general(qh, kb, nt, preferred_element_type=F32)
                if windowed:
                    s_win = s[:, 0:n_win] + bias_ref[0, 2 * hp + hh]
                    s_ctx = s[:, n_win:]
                    m = jnp.maximum(jnp.max(s_win, axis=-1, keepdims=True), jnp.max(s_ctx, axis=-1, keepdims=True))
                    p = jnp.concatenate([jnp.exp2((s_win - m).astype(BF16)), jnp.exp2((s_ctx - m).astype(BF16))],
                                        axis=1)
                else:
                    p = jnp.exp2((s - jnp.max(s, axis=-1, keepdims=True)).astype(BF16))
                va = jnp.where(own_kv, vb, jnp.ones_like(vb))
                o = jnp.dot(p, va, preferred_element_type=F32)
                outs.append(o / pltpu.roll(o, HEAD_DIM, 1))
            o_ref[b, :, cols] = jnp.where(low, outs[0], outs[1]).astype(BF16)

    @pl.when(j == 0)
    def _():
        attend((kvx_ref,), False)

    @pl.when(j > 0)
    def _():
        attend((kvp_ref, kvc_ref, kvn_ref, kvx_ref), True)


def _attention(l, proj, bias_tiles, lru):
    last = N_LAT_TILES - 1
    blk = (PAIR, TILE, 2 * NA_WIDTH)
    prev_map = lambda b, j: (b, 1 + jnp.clip(j - 2, 0, last), COL_KV // 2)
    cur_map = lambda b, j: (b, jnp.maximum(j, 1), COL_KV // 2)
    next_map = lambda b, j: (b, 1 + jnp.clip(j, 0, last), COL_KV // 2)
    ctx_map = lambda b, j: (b, 0, COL_KV // 2)
    var_map = lambda b, j: (jnp.where(j <= 1, 0, jnp.where(j == N_TILES - 1, 2, 1)), l, 0, 0)
    kv_specs = [pl.BlockSpec(blk, prev_map), pl.BlockSpec(blk, cur_map), pl.BlockSpec(blk, next_map),
                pl.BlockSpec(blk, ctx_map)]
    tok = lambda col: pl.BlockSpec((PAIR, TILE, NA_WIDTH), lambda b, j: (b, j, col))
    halo = TILE // HALO
    n_halo = L_TOT // HALO
    prev16 = pl.BlockSpec((PAIR, HALO, LRU_WIDTH), lambda b, j: (b, jnp.maximum(j * halo - 1, 0), COL_U))
    next16 = pl.BlockSpec((PAIR, HALO, LRU_WIDTH), lambda b, j: (b, jnp.minimum((j + 1) * halo, n_halo - 1), COL_U))
    whole = lambda shape: pl.BlockSpec(shape, lambda b, j: tuple(0 for _ in shape))
    n_cb = LRU_WIDTH // LRU_CH
    lru_specs = [whole((CONV_W, LRU_WIDTH)), whole((1, LRU_WIDTH)), whole((2, n_cb, LRU_CH, LRU_CH)),
                 whole((2, n_cb, LRU_CH, LRU_CH)), whole((2, LRU_WIDTH)), whole((2, LRU_WIDTH)), whole((2, LRU_WIDTH))]
    return pl.pallas_call(
        _attn_kernel,
        grid=(SB // PAIR, N_TILES),
        in_specs=[tok(COL_Q)] + kv_specs + [pl.BlockSpec((1, NA_HEADS, TILE, WIN_TILES * TILE), var_map)]
        + [tok(COL_U), prev16, next16] + lru_specs,
        out_specs=[tok(0), pl.BlockSpec((PAIR, TILE, 2 * LRU_WIDTH), lambda b, j: (b, j, 0))],
        out_shape=[jax.ShapeDtypeStruct((SB, L_TOT, NA_WIDTH), BF16),
                   jax.ShapeDtypeStruct((SB, L_TOT, 2 * LRU_WIDTH), BF16)],
        scratch_shapes=[pltpu.VMEM((PAIR, 1, LRU_WIDTH), F32), pltpu.VMEM((TILE, LRU_WIDTH), F32),
                        pltpu.VMEM((TILE, LRU_WIDTH), F32)],
        compiler_params=_cparams(("parallel", "arbitrary")),
        name="na_attention",
    )(proj, proj, proj, proj, proj, bias_tiles, proj, proj, proj, *lru)


def _softplus(x):
    return jnp.maximum(x, 0.0) + jnp.log1p(jnp.exp(-jnp.abs(x)))


def _lru_conv(u, prev16, next16, has_prev, has_next, cw, cb):
    sub = lax.broadcasted_iota(jnp.int32, (SUB, u.shape[1]), 0)
    prow = prev16[HALO - 1:HALO, :] * has_prev
    n0 = next16[0:1, :] * has_next
    n8 = next16[8:9, :] * has_next
    first8 = jnp.where(sub == 0, prow, pltpu.roll(u[TILE - SUB:TILE, :], 1, 0))
    last_a = jnp.where(sub == SUB - 1, n0, pltpu.roll(u[0:SUB, :], SUB - 1, 0))
    last_b = jnp.where(sub == SUB - 1, n8, pltpu.roll(u[SUB:2 * SUB, :], SUB - 1, 0))
    um1 = jnp.concatenate([first8, u[0:TILE - SUB, :]], axis=0)
    up1 = jnp.concatenate([u[SUB:TILE, :], last_a], axis=0)
    up2 = jnp.concatenate([u[2 * SUB:TILE, :], last_a, last_b], axis=0)
    return cw[0:1, :] * um1 + cw[1:2, :] * u + cw[2:3, :] * up1 + cw[3:4, :] * up2 + cb


def _lru_gates(v, d, wr_ref, wi_ref, br_ref, bi_ref, lam_ref):
    vb = v.astype(BF16)
    n_cb = LRU_WIDTH // LRU_CH

    def gate(w_ref, b_ref):
        z = [jnp.dot(vb[:, c * LRU_CH:(c + 1) * LRU_CH], w_ref[d, c], preferred_element_type=F32) for c in range(n_cb)]
        return jnp.tanh(jnp.concatenate(z, axis=1) + b_ref[d:d + 1, :])

    tr = gate(wr_ref, br_ref)
    ti = gate(wi_ref, bi_ref)
    half = (-0.5 * LRU_C * LOG2E) * _softplus(-lam_ref[d:d + 1, :])
    a = jnp.exp2(half + half * tr)
    om = 1.0 - a * a
    b = (om * lax.rsqrt(jnp.maximum(om, TINY))) * ((0.5 * v) * (1.0 + ti))
    return a, b


def _lru_scan(a, b, h_in, reverse, hl_scr, p_scr):
    order = range(GROUPS - 1, -1, -1) if reverse else range(GROUPS)
    hl = None
    for g in order:
        ag = a[g * SUB:(g + 1) * SUB, :]
        bg = b[g * SUB:(g + 1) * SUB, :]
        if hl is None:
            hl, p = bg, ag
        else:
            hl = ag * hl + bg
            p = ag * p
        hl_scr[g * SUB:(g + 1) * SUB, :] = hl
        p_scr[g * SUB:(g + 1) * SUB, :] = p
    blocks = range(SUB - 1, -1, -1) if reverse else range(SUB)
    carry = h_in
    cins = {}
    for s in blocks:
        cins[s] = carry
        carry = hl[s:s + 1, :] + p[s:s + 1, :] * carry
    cin = jnp.concatenate([cins[s] for s in range(SUB)], axis=0)
    hfull = hl_scr[...] + p_scr[...] * jnp.tile(cin, (GROUPS, 1))
    return hfull, carry


def _route(sel, aff):
    def top2_sum(a, b, c, d):
        hi1, lo1 = jnp.maximum(a, b), jnp.minimum(a, b)
        hi2, lo2 = jnp.maximum(c, d), jnp.minimum(c, d)
        return jnp.maximum(hi1, hi2) + jnp.maximum(jnp.minimum(hi1, hi2), jnp.maximum(lo1, lo2))

    scores = [top2_sum(*sel[EXPERTS_PER_GROUP * g:EXPERTS_PER_GROUP * (g + 1)]) for g in range(N_GROUPS)]
    best = jnp.zeros_like(scores[0], dtype=jnp.int32)
    best_v = scores[0]
    for g in range(1, N_GROUPS):
        upd = scores[g] > best_v
        best = jnp.where(upd, g, best)
        best_v = jnp.where(upd, scores[g], best_v)
    chosen = []
    for e in range(N_EXPERTS):
        g = e // EXPERTS_PER_GROUP
        rank = jnp.zeros_like(best)
        for o in range(EXPERTS_PER_GROUP * g, EXPERTS_PER_GROUP * (g + 1)):
            if o == e:
                continue
            ahead = sel[o] > sel[e]
            if o < e:
                ahead = jnp.logical_or(ahead, sel[o] == sel[e])
            rank = rank + ahead.astype(jnp.int32)
        chosen.append(jnp.logical_and(best == g, rank < TOP_K))
    total = jnp.zeros_like(aff[0])
    for e in range(N_EXPERTS):
        total = total + jnp.where(chosen[e], aff[e], 0.0)
    gates = [jnp.where(chosen[e], aff[e] / total, 0.0) for e in range(N_EXPERTS)]
    return gates, [c.astype(F32) for c in chosen]


def _out_kernel(x_ref, ya_ref, hc_ref, g_ref, *refs):
    mod_refs = refs[:PAIR]
    (wr_ref, wi_ref, br_ref, bi_ref, lam_ref, nrm_ref, w_ref, rwc_ref, rb_ref,
     xo_ref, hp_ref, rt_ref, h_scr, hl_scr, p_scr) = refs[PAIR:]
    rows = PAIR * TILE

    @pl.when(pl.program_id(1) == 0)
    def _():
        h_scr[...] = jnp.zeros_like(h_scr)

    ybs = []
    for k in range(PAIR):
        a_r, b_r = _lru_gates(hc_ref[k, :, LRU_WIDTH:].astype(F32), 1, wr_ref, wi_ref, br_ref, bi_ref, lam_ref)
        hrev, h_last = _lru_scan(a_r, b_r, h_scr[k], True, hl_scr, p_scr)
        h_scr[k] = h_last
        gx = g_ref[k].astype(F32)
        gate = (0.5 * gx) * (1.0 + jnp.tanh(gx * (GELU_C + (GELU_C * 0.044715) * (gx * gx))))
        ybs.append((gate * (hc_ref[k, :, 0:LRU_WIDTH].astype(F32) + hrev)).astype(BF16))
    ya = ya_ref[...].reshape(rows, NA_WIDTH)
    yb = jnp.concatenate(ybs, axis=0)
    y = jnp.dot(ya, w_ref[0:NA_WIDTH, :], preferred_element_type=F32)
    y = y + jnp.dot(yb, w_ref[NA_WIDTH:, :], preferred_element_type=F32)
    hs = []
    for k in range(PAIR):
        m = mod_refs[k][0, 0]
        x = x_ref[k] + m[2:3, :] * y[k * TILE:(k + 1) * TILE, :]
        xo_ref[k] = x
        ms = jnp.mean(x * x, axis=-1, keepdims=True)
        h = (x * lax.rsqrt(ms + EPS)) * nrm_ref[...]
        h = h * (1.0 + m[4:5, :]) + m[3:4, :]
        hp_ref[k] = _pack_rows(h)
        hs.append(h)
    h = jnp.concatenate(hs, axis=0)
    h_hi = h.astype(BF16)
    h_lo = (h - h_hi.astype(F32)).astype(BF16)
    nt = (((1,), (1,)), ((), ()))
    rwc = rwc_ref[...]
    both = lax.dot_general(rwc, h_hi, nt, preferred_element_type=F32)
    lg = (both[0:N_EXPERTS, :] + both[N_EXPERTS:, :]
          + lax.dot_general(rwc[0:N_EXPERTS, :], h_lo, nt, preferred_element_type=F32))
    aff_all = jax.nn.sigmoid(lg)
    sel_all = aff_all + rb_ref[...]
    aff = [aff_all[e:e + 1, :] for e in range(N_EXPERTS)]
    sel = [sel_all[e:e + 1, :] for e in range(N_EXPERTS)]
    gates, chosen = _route(sel, aff)
    rt = jnp.concatenate(gates + chosen, axis=0)
    for k in range(PAIR):
        rt_ref[k, 0] = rt[:, k * TILE:(k + 1) * TILE]


def _out_proj(l, boff, xs, ya, hc, proj, lru, mod, nrm, w_out, rwc, rb):
    rev = lambda i: jnp.where(i == 0, 0, N_TILES - i)
    tok = lambda width, col=0: pl.BlockSpec((PAIR, TILE, width), lambda bp, i: (bp, rev(i), col))
    mod_specs = [pl.BlockSpec((1, 1, N_MOD, D_MODEL),
                              lambda bp, i, k=k: (l, _mod_row(boff + PAIR * bp + k, i), 0, 0)) for k in range(PAIR)]
    n_cb = LRU_WIDTH // LRU_CH
    lru_specs = [_full_spec((2, n_cb, LRU_CH, LRU_CH)), _full_spec((2, n_cb, LRU_CH, LRU_CH)),
                 _full_spec((2, LRU_WIDTH)), _full_spec((2, LRU_WIDTH)), _full_spec((2, LRU_WIDTH))]
    return pl.pallas_call(
        _out_kernel,
        grid=(SB // PAIR, N_TILES),
        in_specs=[tok(D_MODEL), tok(NA_WIDTH), tok(2 * LRU_WIDTH), tok(LRU_WIDTH, COL_G)]
        + mod_specs + lru_specs
        + [_full_spec((1, D_MODEL)), _full_spec((D_MODEL, D_MODEL)), _full_spec((2 * N_EXPERTS, D_MODEL)),
           _full_spec((N_EXPERTS, 1))],
        out_specs=[tok(D_MODEL), tok(PACK_W),
                   pl.BlockSpec((PAIR, 1, ROUTE_ROWS, TILE), lambda bp, i: (bp, rev(i), 0, 0))],
        out_shape=[jax.ShapeDtypeStruct((SB, L_TOT, D_MODEL), F32),
                   jax.ShapeDtypeStruct((SB, L_TOT, PACK_W), I32),
                   jax.ShapeDtypeStruct((SB, N_TILES, ROUTE_ROWS, TILE), F32)],
        scratch_shapes=[pltpu.VMEM((PAIR, 1, LRU_WIDTH), F32), pltpu.VMEM((TILE, LRU_WIDTH), F32),
                        pltpu.VMEM((TILE, LRU_WIDTH), F32)],
        compiler_params=_cparams(("parallel", "arbitrary")),
        name="out_proj_router",
    )(xs, ya, hc, proj, *([mod] * PAIR), *lru[2:], nrm, w_out, rwc, rb)


def _pos_kernel(rt_ref, tri_ref, start_ref, pos_ref, wt_ref, run_scr):
    @pl.when(pl.program_id(0) == 0)
    def _():
        run_scr[...] = jnp.zeros_like(run_scr)

    base = start_ref[...] + run_scr[...]
    for k in range(POS_TILES):
        gates = rt_ref[k, 0:N_EXPERTS, :]
        chosen = rt_ref[k, N_EXPERTS:, :]
        rank = jnp.dot(chosen.astype(BF16), tri_ref[...], preferred_element_type=F32)
        posf = rank + base
        seen = jnp.zeros((1, TILE), F32)
        p0 = jnp.zeros((1, TILE), F32)
        p1 = jnp.zeros((1, TILE), F32)
        w0 = jnp.zeros((1, TILE), F32)
        w1 = jnp.zeros((1, TILE), F32)
        for e in range(N_EXPERTS):
            ch = chosen[e:e + 1, :]
            first = ch * (1.0 - seen)
            second = ch * seen
            p0 = p0 + first * posf[e:e + 1, :]
            p1 = p1 + second * posf[e:e + 1, :]
            w0 = w0 + first * gates[e:e + 1, :]
            w1 = w1 + second * gates[e:e + 1, :]
            seen = jnp.minimum(seen + ch, 1.0)
        pos_ref[k] = jnp.concatenate([p0, p1], axis=0).astype(I32)
        wpad = jnp.concatenate([w0, w1, jnp.zeros((LANES - TOP_K, TILE), F32)], axis=0)
        wt_ref[k * TILE:(k + 1) * TILE, :] = jnp.transpose(wpad)
        base = base + jnp.sum(chosen, axis=1, keepdims=True)
    run_scr[...] = base - start_ref[...]


def _positions(route, tri, start):
    return pl.pallas_call(
        _pos_kernel,
        grid=(N_TOK_TILES // POS_TILES,),
        in_specs=[pl.BlockSpec((POS_TILES, ROUTE_ROWS, TILE), lambda i: (i, 0, 0)),
                  pl.BlockSpec((TILE, TILE), lambda i: (0, 0)),
                  pl.BlockSpec((N_EXPERTS, 1), lambda i: (0, 0))],
        out_specs=[pl.BlockSpec((POS_TILES, TOP_K, TILE), lambda i: (i, 0, 0)),
                   pl.BlockSpec((POS_TILES * TILE, LANES), lambda i: (i, 0))],
        out_shape=[jax.ShapeDtypeStruct((N_TOK_TILES, TOP_K, TILE), I32),
                   jax.ShapeDtypeStruct((N_TOK, LANES), F32)],
        scratch_shapes=[pltpu.VMEM((N_EXPERTS, 1), F32)],
        compiler_params=_cparams(("arbitrary",)),
        name="moe_positions",
    )(route, tri, start)


def _work_items(counts):
    smem = pl.BlockSpec(memory_space=pltpu.SMEM)
    item = jax.ShapeDtypeStruct((N_ITEMS,), I32)
    return pl.pallas_call(
        _items_kernel,
        in_specs=[smem],
        out_specs=[smem] * 4,
        out_shape=[item] * 4,
        name="moe_work_items",
    )(counts)


def _items_kernel(cnt_ref, tile_ref, exp_ref, lo_ref, hi_ref):
    n = jnp.int32(0)
    start = jnp.int32(0)
    last_e = jnp.int32(0)
    for e in range(N_EXPERTS):
        cnt = cnt_ref[e]
        end = start + cnt
        first = start // EXP_TILE
        n_tiles = jnp.where(cnt > 0, (end - 1) // EXP_TILE - first + 1, 0)

        def put(j, carry, e=e, n=n, start=start, end=end, first=first):
            tile = first + j
            tile_ref[n + j] = tile
            exp_ref[n + j] = jnp.int32(e)
            lo_ref[n + j] = jnp.maximum(start - tile * EXP_TILE, 0)
            hi_ref[n + j] = jnp.minimum(end - tile * EXP_TILE, EXP_TILE)
            return carry

        lax.fori_loop(0, n_tiles, put, 0)
        n = n + n_tiles
        start = end
        last_e = jnp.where(cnt > 0, e, last_e)

    def pad(j, carry):
        tile_ref[j] = jnp.int32(N_EXP_TILES - 1)
        exp_ref[j] = last_e
        lo_ref[j] = jnp.int32(0)
        hi_ref[j] = jnp.int32(0)
        return carry

    lax.fori_loop(n, N_ITEMS, pad, 0)


def _sc_worker_base():
    return (lax.axis_index("s") * SC_CORES + lax.axis_index("c")) * SC_ROWS


def _sc_dispatch_body(h_hbm, p0_hbm, p1_hbm, out_hbm, i0_v, i1_v, rows_v, sem):
    base = _sc_worker_base()

    @pl.loop(0, SC_N_CHUNKS)
    def _(j):
        off = base + j * SC_CHUNK
        pltpu.sync_copy(h_hbm.at[pl.ds(off, SC_CHUNK)], rows_v)
        pltpu.sync_copy(p0_hbm.at[pl.ds(off, SC_CHUNK)], i0_v)
        pltpu.sync_copy(p1_hbm.at[pl.ds(off, SC_CHUNK)], i1_v)
        c0 = pltpu.async_copy(rows_v, out_hbm.at[i0_v], sem)
        c1 = pltpu.async_copy(rows_v, out_hbm.at[i1_v], sem)
        c0.wait()
        c1.wait()


def _sc_combine_body(y_hbm, p0_hbm, p1_hbm, g0_hbm, g1_hbm, i_v, rows_v, sem):
    base = _sc_worker_base()

    @pl.loop(0, SC_N_CHUNKS)
    def _(j):
        off = base + j * SC_CHUNK
        for p_hbm, g_hbm in ((p0_hbm, g0_hbm), (p1_hbm, g1_hbm)):
            pltpu.sync_copy(p_hbm.at[pl.ds(off, SC_CHUNK)], i_v)
            pltpu.async_copy(y_hbm.at[i_v], rows_v, sem).wait()
            pltpu.sync_copy(rows_v, g_hbm.at[pl.ds(off, SC_CHUNK)])


def _sc_mesh():
    return plsc.VectorSubcoreMesh(core_axis_name="c", subcore_axis_name="s")


def _dispatch(hp, pos0, pos1):
    return pl.kernel(
        _sc_dispatch_body,
        out_type=jax.ShapeDtypeStruct((N_ASSIGN, PACK_W), I32),
        mesh=_sc_mesh(),
        scratch_types=[pltpu.VMEM((SC_CHUNK,), I32), pltpu.VMEM((SC_CHUNK,), I32),
                       pltpu.VMEM((SC_CHUNK, PACK_W), I32), pltpu.SemaphoreType.DMA],
        name="moe_dispatch",
    )(hp, pos0, pos1)


def _combine(ys, pos0, pos1):
    row = jax.ShapeDtypeStruct((N_TOK, PACK_W), I32)
    return pl.kernel(
        _sc_combine_body,
        out_type=[row, row],
        mesh=_sc_mesh(),
        scratch_types=[pltpu.VMEM((SC_CHUNK,), I32), pltpu.VMEM((SC_CHUNK, PACK_W), I32),
                       pltpu.SemaphoreType.DMA],
        name="moe_combine",
    )(ys, pos0, pos1)


def _expert_kernel(tile_ref, exp_ref, lo_ref, hi_ref, x_ref, w1_ref, w3_ref, w2_ref, o_ref, w13_scr, w2_scr):
    i = pl.program_id(0)
    prev = jnp.maximum(i - 1, 0)

    @pl.when(jnp.logical_or(i == 0, exp_ref[i] != exp_ref[prev]))
    def _():
        w13_scr[:, 0:EXPERT_FF] = w1_ref[0, 0].astype(BF16)
        w13_scr[:, EXPERT_FF:] = w3_ref[0, 0].astype(BF16)
        w2_scr[...] = w2_ref[0, 0].astype(BF16)

    lo_f, hi_f = _unpack_rows(x_ref[...])
    h = jnp.concatenate([lo_f.astype(BF16), hi_f.astype(BF16)], axis=1)
    ab = jnp.dot(h, w13_scr[...], preferred_element_type=F32)
    a = ab[:, 0:EXPERT_FF]
    b = ab[:, EXPERT_FF:]
    t = ((a * jax.nn.sigmoid(a)) * b).astype(BF16)
    y = _pack_rows(jnp.dot(t, w2_scr[...], preferred_element_type=F32))
    row = lax.broadcasted_iota(jnp.int32, (EXP_TILE, PACK_W), 0)
    mine = jnp.logical_and(row >= lo_ref[i], row < hi_ref[i])
    revisit = jnp.logical_and(i > 0, tile_ref[i] == tile_ref[prev])

    @pl.when(jnp.logical_not(revisit))
    def _():
        o_ref[...] = jnp.where(mine, y, 0)

    @pl.when(revisit)
    def _():
        o_ref[...] = jnp.where(mine, y, o_ref[...])


def _experts(l, xs_sorted, items, w1, w3, w2):
    tile, e, lo, hi = items
    grid_spec = pltpu.PrefetchScalarGridSpec(
        num_scalar_prefetch=4,
        grid=(N_ITEMS,),
        in_specs=[pl.BlockSpec((EXP_TILE, PACK_W), lambda i, t, e, lo, hi: (t[i], 0)),
                  pl.BlockSpec((1, 1, D_MODEL, EXPERT_FF), lambda i, t, e, lo, hi: (l, e[i], 0, 0)),
                  pl.BlockSpec((1, 1, D_MODEL, EXPERT_FF), lambda i, t, e, lo, hi: (l, e[i], 0, 0)),
                  pl.BlockSpec((1, 1, EXPERT_FF, D_MODEL), lambda i, t, e, lo, hi: (l, e[i], 0, 0))],
        out_specs=pl.BlockSpec((EXP_TILE, PACK_W), lambda i, t, e, lo, hi: (t[i], 0)),
        scratch_shapes=[pltpu.VMEM((D_MODEL, 2 * EXPERT_FF), BF16), pltpu.VMEM((EXPERT_FF, D_MODEL), BF16)],
    )
    return pl.pallas_call(
        _expert_kernel,
        grid_spec=grid_spec,
        out_shape=jax.ShapeDtypeStruct((N_ASSIGN, PACK_W), I32),
        compiler_params=_cparams(("arbitrary",)),
        name="moe_experts",
    )(tile, e, lo, hi, xs_sorted, w1, w3, w2)


def _moe(l, hp, route, tri, w1, w3, w2):
    counts = jnp.sum(route[:, N_EXPERTS:, :], axis=(0, 2))
    start = (jnp.cumsum(counts) - counts).reshape(N_EXPERTS, 1)
    pos, wt = _positions(route, tri, start)
    pos0 = pos[:, 0, :].reshape(N_TOK)
    pos1 = pos[:, 1, :].reshape(N_TOK)
    items = _work_items(counts.astype(I32))
    xs_sorted = _dispatch(hp.reshape(N_TOK, PACK_W), pos0, pos1)
    ys = _experts(l, xs_sorted, items, w1, w3, w2)
    g0, g1 = _combine(ys, pos0, pos1)
    shape = (SB, L_TOT, PACK_W)
    return g0.reshape(shape), g1.reshape(shape), wt.reshape(SB, L_TOT, LANES)


def _final_kernel(x_ref, g0_ref, g1_ref, wt_ref, mod_ref, *refs):
    o_ref, scr = refs[-2:]
    for b in range(SB):
        x = _moe_residual(x_ref[b], g0_ref[b], g1_ref[b], wt_ref[b], mod_ref[0, b][5:6, :])
        o_ref[b] = _from_scan_major(x, scr)


def _final(streams, mod):
    lat = lambda width: pl.BlockSpec((SB, TILE, width), lambda j: (0, j + 1, 0))
    out = None
    for k, st in enumerate(streams):
        in_specs = [lat(D_MODEL), lat(PACK_W), lat(PACK_W), lat(LANES),
                    pl.BlockSpec((1, SB, N_MOD, D_MODEL), lambda j, k=k: (DEPTH - 1, k, 0, 0))]
        args = [st["xs"]] + list(st["moe"]) + [mod]
        aliases = {}
        if out is not None:
            in_specs.append(pl.BlockSpec(memory_space=pl.ANY))
            args.append(out)
            aliases = {len(args) - 1: 0}
        out = pl.pallas_call(
            _final_kernel,
            grid=(N_LAT_TILES,),
            in_specs=in_specs,
            out_specs=pl.BlockSpec((SB, TILE, D_MODEL), lambda j, k=k: (k, j, 0)),
            out_shape=jax.ShapeDtypeStruct((BATCH, SEQ, D_MODEL), F32),
            scratch_shapes=[pltpu.VMEM((D_MODEL // LANES, TILE, LANES), F32)],
            input_output_aliases=aliases,
            compiler_params=_cparams(("arbitrary",)),
            name="final_residual",
        )(*args)
    return out


def _bias_kernel(tab_ref, perm_ref, o_ref, scr):
    rows_q = TILE // GRID_W
    lane = lax.broadcasted_iota(jnp.int32, (GROUPS, TILE), 1)
    ka = (lane >> 1) & (rows_q - 1)
    neg = jnp.full((GROUPS, TILE), NEG_INF, F32)
    qc = lax.broadcasted_iota(jnp.int32, (GRID_W, GRID_W), 0)
    kc = lax.broadcasted_iota(jnp.int32, (GRID_W, GRID_W), 1)
    col0 = jnp.clip(qc - NA_WIN_COLS // 2, 0, GRID_W - NA_WIN_COLS)
    in_win = jnp.logical_and(kc >= col0, kc < col0 + NA_WIN_COLS)
    toeplitz = []
    for r in range(2 * NA_WIN_ROWS - 1):
        row = jnp.broadcast_to(tab_ref[0, r:r + 1, :], (GRID_W, LANES))
        shifted = pltpu.roll(row, LANES - (GRID_W - 1), 1, stride=1, stride_axis=0)
        toeplitz.append(jnp.where(in_win, shifted[:, 0:GRID_W], NEG_INF))

    def interleaved(r0):
        acc = None
        for k in range(rows_q):
            val = toeplitz[r0 + k]
            t1 = val.astype(BF16)
            r1 = val - t1.astype(F32)
            t2 = r1.astype(BF16)
            t3 = (r1 - t2.astype(F32)).astype(BF16)
            for term in (t1, t2, t3):
                moved = jnp.dot(term, perm_ref[k], preferred_element_type=F32)
                acc = moved if acc is None else acc + moved
        return acc

    t4 = [interleaved(r0) for r0 in range(WIN_TILES * rows_q)]
    for a in range(rows_q):
        for qs in range(SUB // rows_q):
            s = (SUB // rows_q) * a + qs
            for kt in range(WIN_TILES):
                src = t4[rows_q * kt - a + 3][qs * GROUPS:(qs + 1) * GROUPS, :]
                c = rows_q * kt + ka
                variants = (
                    src if kt >= 1 else neg,
                    jnp.where(jnp.logical_and(c >= a, c <= a + NA_WIN_ROWS - 1), src, neg),
                    src if kt <= 1 else neg,
                )
                for v, val in enumerate(variants):
                    for half in range(TILE // LANES):
                        scr[v, 2 * kt + half, pl.ds(s, GROUPS, stride=SUB), :] = val[:, half * LANES:(half + 1) * LANES]
    for v in range(BIAS_VARIANTS):
        o_ref[v, 0] = jnp.concatenate([scr[v, j] for j in range(WIN_TILES * TILE // LANES)], axis=1)


def _bias_tiles(table):
    rows_q = TILE // GRID_W
    n_r = 2 * NA_WIN_ROWS - 1
    tab = jnp.pad(table.astype(F32) * LOG2E, ((0, 0), (0, 0), (BIAS_PAD, LANES - BIAS_PAD - (2 * NA_WIN_COLS - 1))))
    n_h = table.shape[0]
    half = GRID_W // GROUPS
    perm = np.zeros((rows_q, GRID_W, TILE), np.float32)
    for k in range(rows_q):
        for ks in range(half):
            for kg in range(GROUPS):
                perm[k, ks * GROUPS + kg, kg * SUB + k * half + ks] = 1.0
    return pl.pallas_call(
        _bias_kernel,
        grid=(n_h,),
        in_specs=[pl.BlockSpec((1, n_r, LANES), lambda h: (h, 0, 0)),
                  pl.BlockSpec((rows_q, GRID_W, TILE), lambda h: (0, 0, 0))],
        out_specs=pl.BlockSpec((BIAS_VARIANTS, 1, TILE, WIN_TILES * TILE), lambda h: (0, h, 0, 0)),
        out_shape=jax.ShapeDtypeStruct((BIAS_VARIANTS, n_h, TILE, WIN_TILES * TILE), F32),
        scratch_shapes=[pltpu.VMEM((BIAS_VARIANTS, WIN_TILES * TILE // LANES, TILE, LANES), F32)],
        compiler_params=_cparams(("parallel",)),
        name="bias_tiles",
    )(tab, jnp.asarray(perm, BF16))


def _block_diag(w, n_chunks):
    per = LRU_BLOCKS // n_chunks
    w = w.reshape(2, n_chunks, per, LRU_BLOCK, LRU_BLOCK)
    eye = jnp.eye(per, dtype=w.dtype)
    out = jnp.einsum('dcpij,pq->dcpiqj', w, eye)
    return out.reshape(2, n_chunks, per * LRU_BLOCK, per * LRU_BLOCK)


def kernel(x, c, ctx, c_ctx, w_mod, b_mod, norm_mix, norm_ffn, w_in, w_out, q_gain, k_gain, na_bias,
           conv_w, conv_b, lru_w_r, lru_b_r, lru_w_i, lru_b_i, lru_lambda, router_w, router_b,
           exp_w1, exp_w3, exp_w2):
    cs = jnp.concatenate([c, c_ctx[None, :], jnp.zeros((MOD_ROWS - BATCH - 1, D_MODEL), F32)], axis=0)
    mod = _modulation(cs, w_mod, b_mod).reshape(DEPTH, MOD_ROWS, N_MOD, D_MODEL)

    head_of = np.arange(NA_WIDTH) // HEAD_DIM
    bd = jnp.asarray((head_of[:, None] == head_of[None, :]).astype(np.float32) / HEAD_DIM, BF16)
    tri = jnp.asarray(np.triu(np.ones((TILE, TILE), np.float32), 1), BF16)
    rwt = router_w.T
    rwh = rwt.astype(BF16)
    rwc = jnp.concatenate([rwh, (rwt - rwh.astype(F32)).astype(BF16)], axis=0)
    rb = router_b.reshape(N_EXPERTS, 1)
    n_cb = LRU_WIDTH // LRU_CH

    bias_tiles = _bias_tiles(na_bias.reshape(DEPTH * NA_HEADS, 2 * NA_WIN_ROWS - 1, 2 * NA_WIN_COLS - 1))
    streams = [{"xs": (x, ctx), "moe": None, "boff": sidx * SB} for sidx in range(STREAMS)]
    for l in range(DEPTH):
        qg = jnp.tile(q_gain[l] * (ATTN_SCALE * LOG2E), NA_HEADS)[None, :]
        kg = jnp.tile(k_gain[l], NA_HEADS)[None, :]
        w_in_l = w_in[l].astype(BF16)
        w_out_l = w_out[l].astype(BF16)
        wr = (0.5 * _block_diag(lru_w_r[l], n_cb)).astype(BF16)
        wi = (0.5 * _block_diag(lru_w_i[l], n_cb)).astype(BF16)
        lru = (conv_w[l], conv_b[l][None, :], wr, wi, 0.5 * lru_b_r[l], 0.5 * lru_b_i[l], lru_lambda[l])
        for st in streams:
            boff = st["boff"]
            xs, proj = _in_proj(l, boff, st["xs"], st["moe"], mod, norm_mix[l][None, :], w_in_l, bd, qg, kg)
            ya, hc = _attention(l, proj, bias_tiles, lru)
            xs, hp, route = _out_proj(l, boff, xs, ya, hc, proj, lru, mod, norm_ffn[l][None, :], w_out_l, rwc, rb)
            st["xs"] = xs
            st["moe"] = _moe(l, hp, route.reshape(N_TOK_TILES, ROUTE_ROWS, TILE), tri, exp_w1, exp_w3, exp_w2)
    return _final(streams, mod)
```

```python
import functools

import jax
import jax.numpy as jnp
import numpy as np
from jax import lax
from jax.experimental import pallas as pl
from jax.experimental.pallas import tpu as pltpu
from jax.experimental.pallas import tpu_sc as plsc

F32 = jnp.float32
BF16 = jnp.bfloat16
I32 = jnp.int32

D_MODEL = 1024
BATCH = 4
SEQ = 8192
DEPTH = 4
GRID_W = 64
CTX_LEN = 256
HEAD_DIM = 64
NA_WIDTH = 512
NA_HEADS = 8
NA_WIN_ROWS = 8
NA_WIN_COLS = 16
LRU_WIDTH = 512
LRU_BLOCKS = 8
LRU_BLOCK = 64
CONV_W = 4
LRU_C = 8.0
IN_COLS = 3 * NA_WIDTH + 2 * LRU_WIDTH
COL_KV, COL_Q, COL_U, COL_G = 0, 2, 3, 4
N_EXPERTS = 16
N_GROUPS = 4
EXPERTS_PER_GROUP = 4
TOP_K = 2
EXPERT_FF = 512
N_MOD = 6
ATTN_SCALE = HEAD_DIM ** -0.5
LOG2E = 1.4426950408889634
EPS = 1e-6
NEG_INF = -1e30
TINY = 1e-30
GELU_C = 0.7978845608028654

TILE = 256
WIN_TILES = 3
BIAS_VARIANTS = 3
HALO = 16
SUB = 8
LANES = 128
GROUPS = TILE // SUB
L_TOT = CTX_LEN + SEQ
N_TILES = L_TOT // TILE
N_LAT_TILES = SEQ // TILE
STREAMS = 2
SB = BATCH // STREAMS
N_TOK = SB * L_TOT
N_TOK_TILES = N_TOK // TILE
PAIR = 2
LRU_CH = 256
MOD_ROWS = 8
VMEM_LIMIT = 56 * 1024 * 1024

PACK_W = D_MODEL // 2
HI_MASK = -65536
N_ASSIGN = TOP_K * N_TOK
EXP_TILE = 512
N_EXP_TILES = N_ASSIGN // EXP_TILE
N_ITEMS = N_EXP_TILES + N_EXPERTS - 1
BIAS_PAD = GRID_W - NA_WIN_COLS
ROUTE_ROWS = 2 * N_EXPERTS
POS_TILES = 22

SC_CORES = 2
SC_SUBCORES = 16
SC_WORKERS = SC_CORES * SC_SUBCORES
SC_ROWS = N_TOK // SC_WORKERS
SC_CHUNK = 88
SC_N_CHUNKS = SC_ROWS // SC_CHUNK


def _cparams(sem):
    return pltpu.CompilerParams(dimension_semantics=sem, vmem_limit_bytes=VMEM_LIMIT)


def _pack_rows(v):
    lo = pltpu.bitcast(v[:, :PACK_W].astype(BF16).astype(F32), I32)
    hi = pltpu.bitcast(v[:, PACK_W:].astype(BF16).astype(F32), I32)
    return ((lo >> 16) & 0xFFFF) | (hi & HI_MASK)


def _unpack_rows(p):
    return pltpu.bitcast(p << 16, F32), pltpu.bitcast(p & HI_MASK, F32)


def _mod_kernel(c_ref, w_ref, b_ref, o_ref):
    c = c_ref[...]
    s = c * jax.nn.sigmoid(c)
    o_ref[0] = jnp.dot(s.astype(BF16), w_ref[0].astype(BF16), preferred_element_type=F32) + b_ref[0]


def _modulation(cs, w_mod, b_mod):
    return pl.pallas_call(
        _mod_kernel,
        grid=(DEPTH, N_MOD),
        in_specs=[
            pl.BlockSpec((MOD_ROWS, D_MODEL), lambda l, n: (0, 0)),
            pl.BlockSpec((1, D_MODEL, D_MODEL), lambda l, n: (l, 0, n)),
            pl.BlockSpec((1, 1, D_MODEL), lambda l, n: (l, 0, n)),
        ],
        out_specs=pl.BlockSpec((1, MOD_ROWS, D_MODEL), lambda l, n: (l, 0, n)),
        out_shape=jax.ShapeDtypeStruct((DEPTH, MOD_ROWS, N_MOD * D_MODEL), F32),
        compiler_params=_cparams(("arbitrary", "arbitrary")),
        name="modulation",
    )(cs, w_mod, b_mod.reshape(DEPTH, 1, N_MOD * D_MODEL))


def _mod_row(b, i):
    return jnp.where(i == 0, BATCH, b)


def _moe_residual(x, g0, g1, wt, gate_row):
    lo0, hi0 = _unpack_rows(g0)
    lo1, hi1 = _unpack_rows(g1)
    w0 = wt[:, 0:1]
    w1 = wt[:, 1:2]
    f = jnp.concatenate([w0 * lo0 + w1 * lo1, w0 * hi0 + w1 * hi1], axis=1)
    return x + gate_row * f


def _pair_tok_spec(width):
    return pl.BlockSpec((PAIR, TILE, width), lambda bp, i: (bp, i, 0))


def _pair_mod_specs(layer, boff):
    return [pl.BlockSpec((1, 1, N_MOD, D_MODEL),
                         lambda bp, i, k=k: (layer, _mod_row(boff + PAIR * bp + k, i), 0, 0))
            for k in range(PAIR)]


def _full_spec(shape):
    return pl.BlockSpec(shape, lambda bp, i: tuple(0 for _ in shape))


def _to_scan_major(src_ref, scr):
    n_slab = D_MODEL // LANES
    for s in range(SUB):
        for j in range(n_slab):
            scr[j, pl.ds(s, GROUPS, stride=SUB), :] = src_ref[s * GROUPS:(s + 1) * GROUPS, j * LANES:(j + 1) * LANES]
    return jnp.concatenate([scr[j] for j in range(n_slab)], axis=1)


def _from_scan_major(val, scr):
    n_slab = D_MODEL // LANES
    for j in range(n_slab):
        scr[j] = val[:, j * LANES:(j + 1) * LANES]
    blocks = [jnp.concatenate([scr[j, pl.ds(s, GROUPS, stride=SUB), :] for j in range(n_slab)], axis=1)
              for s in range(SUB)]
    return jnp.concatenate(blocks, axis=0)


def _in_kernel(has_prev, *refs):
    if has_prev:
        x_ref, g0_ref, g1_ref, wt_ref = refs[:4]
        mprev_refs = refs[4:4 + PAIR]
        refs = (x_ref,) + refs[4 + PAIR:]
    else:
        x_ref, ctx_ref = refs[:2]
        refs = (x_ref,) + refs[2:]
    mod_refs = refs[1:1 + PAIR]
    nrm_ref, w_ref, bd_ref, qg_ref, kg_ref = refs[1 + PAIR:6 + PAIR]
    outs = refs[6 + PAIR:]
    xo_ref, proj_ref = outs[:2]
    x_ref = refs[0]
    hs = []
    for k in range(PAIR):
        if has_prev:
            x = _moe_residual(x_ref[k], g0_ref[k], g1_ref[k], wt_ref[k], mprev_refs[k][0, 0][5:6, :])
            xo_ref[k] = x
        else:
            perm_scr = outs[2]

            @pl.when(pl.program_id(1) == 0)
            def _():
                xo_ref[k] = _to_scan_major(ctx_ref.at[k], perm_scr)

            @pl.when(pl.program_id(1) > 0)
            def _():
                xo_ref[k] = _to_scan_major(x_ref.at[k], perm_scr)

            x = xo_ref[k]
        m = mod_refs[k][0, 0]
        ms = jnp.mean(x * x, axis=-1, keepdims=True)
        h = (x * lax.rsqrt(ms + EPS)) * nrm_ref[...]
        hs.append((h * (1.0 + m[1:2, :]) + m[0:1, :]).astype(BF16))
    acc = jnp.dot(jnp.concatenate(hs, axis=0), w_ref[...], preferred_element_type=F32)
    bd = bd_ref[...]

    def head_norm(t, gain):
        ss = jnp.dot((t * t).astype(BF16), bd, preferred_element_type=F32)
        return (t * lax.rsqrt(ss + EPS)) * gain

    def put(col_block, val):
        for k in range(PAIR):
            proj_ref[k, :, col_block * NA_WIDTH:(col_block + 1) * NA_WIDTH] = val[k * TILE:(k + 1) * TILE, :].astype(BF16)

    put(COL_Q, head_norm(acc[:, 0:NA_WIDTH], qg_ref[...]))
    put(COL_KV, head_norm(acc[:, NA_WIDTH:2 * NA_WIDTH], kg_ref[...]))
    put(COL_KV + 1, acc[:, 2 * NA_WIDTH:3 * NA_WIDTH])
    put(COL_U, acc[:, 3 * NA_WIDTH:3 * NA_WIDTH + LRU_WIDTH])
    put(COL_G, acc[:, 3 * NA_WIDTH + LRU_WIDTH:])


def _in_proj(l, boff, xs, moe_prev, mod, nrm, w_in, bd, qg, kg):
    has_prev = moe_prev is not None
    if has_prev:
        in_specs = [_pair_tok_spec(D_MODEL)]
        args = [xs]
        in_specs += [_pair_tok_spec(PACK_W), _pair_tok_spec(PACK_W), _pair_tok_spec(LANES)]
        in_specs += _pair_mod_specs(l - 1, boff)
        args += list(moe_prev) + [mod] * PAIR
    else:
        pair0 = boff // PAIR
        in_specs = [pl.BlockSpec((PAIR, TILE, D_MODEL), lambda bp, i: (pair0 + bp, jnp.maximum(i - 1, 0), 0)),
                    pl.BlockSpec((PAIR, TILE, D_MODEL), lambda bp, i: (pair0 + bp, 0, 0))]
        args = list(xs)
    in_specs += _pair_mod_specs(l, boff) + [_full_spec((1, D_MODEL)), _full_spec((D_MODEL, IN_COLS)),
                                            _full_spec((NA_WIDTH, NA_WIDTH)), _full_spec((1, NA_WIDTH)),
                                            _full_spec((1, NA_WIDTH))]
    args += [mod] * PAIR + [nrm, w_in, bd, qg, kg]
    out_shape = [jax.ShapeDtypeStruct((SB, L_TOT, D_MODEL), F32), jax.ShapeDtypeStruct((SB, L_TOT, IN_COLS), BF16)]
    out_specs = [_pair_tok_spec(D_MODEL), _pair_tok_spec(IN_COLS)]
    scratch = [] if has_prev else [pltpu.VMEM((D_MODEL // LANES, TILE, LANES), F32)]
    outs = pl.pallas_call(
        functools.partial(_in_kernel, has_prev),
        grid=(SB // PAIR, N_TILES),
        in_specs=in_specs,
        out_specs=out_specs,
        out_shape=out_shape,
        scratch_shapes=scratch,
        compiler_params=_cparams(("parallel", "arbitrary")),
        name="in_proj",
    )(*args)
    return outs[0], outs[1]


def _attn_kernel(q_ref, kvp_ref, kvc_ref, kvn_ref, kvx_ref, bias_ref,
                 u_ref, up_ref, un_ref, cw_ref, cb_ref, wr_ref, wi_ref, br_ref, bi_ref, lam_ref,
                 o_ref, hc_ref, h_scr, hl_scr, p_scr):
    j = pl.program_id(1)

    @pl.when(j == 0)
    def _():
        h_scr[...] = jnp.zeros_like(h_scr)

    has_prev = jnp.where(j >= 2, 1.0, 0.0).astype(F32)
    has_next = jnp.where(jnp.logical_and(j >= 1, j <= N_TILES - 2), 1.0, 0.0).astype(F32)

    def forward_lru():
        for b in range(PAIR):
            cv = _lru_conv(u_ref[b].astype(F32), up_ref[b].astype(F32), un_ref[b].astype(F32), has_prev, has_next,
                           cw_ref[...], cb_ref[...])
            hc_ref[b, :, LRU_WIDTH:] = cv.astype(BF16)
            a_f, b_f = _lru_gates(cv, 0, wr_ref, wi_ref, br_ref, bi_ref, lam_ref)
            hfull, h_last = _lru_scan(a_f, b_f, h_scr[b], False, hl_scr, p_scr)
            hc_ref[b, :, 0:LRU_WIDTH] = hfull.astype(BF16)
            h_scr[b] = h_last

    pair_w = 2 * HEAD_DIM
    lane = lax.broadcasted_iota(jnp.int32, (TILE, pair_w), 1)
    low = lane < HEAD_DIM
    nt = (((1,), (1,)), ((), ()))
    n_win = WIN_TILES * TILE

    def attend(kv_refs, windowed):
        low_kv = lax.broadcasted_iota(jnp.int32, (len(kv_refs) * TILE, pair_w), 1) < HEAD_DIM
        for b, hp in [(b, hp) for b in range(PAIR) for hp in range(NA_HEADS // 2)]:
            cols = slice(hp * pair_w, (hp + 1) * pair_w)
            vcols = slice(NA_WIDTH + hp * pair_w, NA_WIDTH + (hp + 1) * pair_w)
            q = q_ref[b, :, cols]
            kb = jnp.concatenate([r[b, :, cols] for r in kv_refs], axis=0)
            vb = jnp.concatenate([r[b, :, vcols] for r in kv_refs], axis=0)
            outs = []
            for hh in range(2):
                own = low if hh == 0 else jnp.logical_not(low)
                own_kv = low_kv if hh == 0 else jnp.logical_not(low_kv)
                qh = jnp.where(own, q, jnp.zeros_like(q))
                s = lax.dot_general(qh, kb, nt, preferred_element_type=F32)
                if windowed:
                    s_win = s[:, 0:n_win] + bias_ref[0, 2 * hp + hh]
                    s_ctx = s[:, n_win:]
                    m = jnp.maximum(jnp.max(s_win, axis=-1, keepdims=True), jnp.max(s_ctx, axis=-1, keepdims=True))
                    p = jnp.concatenate([jnp.exp2((s_win - m).astype(BF16)), jnp.exp2((s_ctx - m).astype(BF16))],
                                        axis=1)
                else:
                    p = jnp.exp2((s - jnp.max(s, axis=-1, keepdims=True)).astype(BF16))
                va = jnp.where(own_kv, vb, jnp.ones_like(vb))
                o = jnp.dot(p, va, preferred_element_type=F32)
                outs.append(o / pltpu.roll(o, HEAD_DIM, 1))
            o_ref[b, :, cols] = jnp.where(low, outs[0], outs[1]).astype(BF16)

    @pl.when(j == 0)
    def _():
        forward_lru()
        attend((kvx_ref,), False)

    @pl.when(j > 0)
    def _():
        forward_lru()
        attend((kvp_ref, kvc_ref, kvn_ref, kvx_ref), True)


def _attention(l, proj, bias_tiles, lru):
    last = N_LAT_TILES - 1
    blk = (PAIR, TILE, 2 * NA_WIDTH)
    prev_map = lambda b, j: (b, 1 + jnp.clip(j - 2, 0, last), COL_KV // 2)
    cur_map = lambda b, j: (b, jnp.maximum(j, 1), COL_KV // 2)
    next_map = lambda b, j: (b, 1 + jnp.clip(j, 0, last), COL_KV // 2)
    ctx_map = lambda b, j: (b, 0, COL_KV // 2)
    var_map = lambda b, j: (jnp.where(j <= 1, 0, jnp.where(j == N_TILES - 1, 2, 1)), l, 0, 0)
    kv_specs = [pl.BlockSpec(blk, prev_map), pl.BlockSpec(blk, cur_map), pl.BlockSpec(blk, next_map),
                pl.BlockSpec(blk, ctx_map)]
    tok = lambda col: pl.BlockSpec((PAIR, TILE, NA_WIDTH), lambda b, j: (b, j, col))
    halo = TILE // HALO
    n_halo = L_TOT // HALO
    prev16 = pl.BlockSpec((PAIR, HALO, LRU_WIDTH), lambda b, j: (b, jnp.maximum(j * halo - 1, 0), COL_U))
    next16 = pl.BlockSpec((PAIR, HALO, LRU_WIDTH), lambda b, j: (b, jnp.minimum((j + 1) * halo, n_halo - 1), COL_U))
    whole = lambda shape: pl.BlockSpec(shape, lambda b, j: tuple(0 for _ in shape))
    n_cb = LRU_WIDTH // LRU_CH
    lru_specs = [whole((CONV_W, LRU_WIDTH)), whole((1, LRU_WIDTH)), whole((2, n_cb, LRU_CH, LRU_CH)),
                 whole((2, n_cb, LRU_CH, LRU_CH)), whole((2, LRU_WIDTH)), whole((2, LRU_WIDTH)), whole((2, LRU_WIDTH))]
    return pl.pallas_call(
        _attn_kernel,
        grid=(SB // PAIR, N_TILES),
        in_specs=[tok(COL_Q)] + kv_specs + [pl.BlockSpec((1, NA_HEADS, TILE, WIN_TILES * TILE), var_map)]
        + [tok(COL_U), prev16, next16] + lru_specs,
        out_specs=[tok(0), pl.BlockSpec((PAIR, TILE, 2 * LRU_WIDTH), lambda b, j: (b, j, 0))],
        out_shape=[jax.ShapeDtypeStruct((SB, L_TOT, NA_WIDTH), BF16),
                   jax.ShapeDtypeStruct((SB, L_TOT, 2 * LRU_WIDTH), BF16)],
        scratch_shapes=[pltpu.VMEM((PAIR, 1, LRU_WIDTH), F32), pltpu.VMEM((TILE, LRU_WIDTH), F32),
                        pltpu.VMEM((TILE, LRU_WIDTH), F32)],
        compiler_params=_cparams(("parallel", "arbitrary")),
        name="na_attention",
    )(proj, proj, proj, proj, proj, bias_tiles, proj, proj, proj, *lru)


def _softplus(x):
    return jnp.maximum(x, 0.0) + jnp.log1p(jnp.exp(-jnp.abs(x)))


def _lru_conv(u, prev16, next16, has_prev, has_next, cw, cb):
    sub = lax.broadcasted_iota(jnp.int32, (SUB, u.shape[1]), 0)
    prow = prev16[HALO - 1:HALO, :] * has_prev
    n0 = next16[0:1, :] * has_next
    n8 = next16[8:9, :] * has_next
    first8 = jnp.where(sub == 0, prow, pltpu.roll(u[TILE - SUB:TILE, :], 1, 0))
    last_a = jnp.where(sub == SUB - 1, n0, pltpu.roll(u[0:SUB, :], SUB - 1, 0))
    last_b = jnp.where(sub == SUB - 1, n8, pltpu.roll(u[SUB:2 * SUB, :], SUB - 1, 0))
    um1 = jnp.concatenate([first8, u[0:TILE - SUB, :]], axis=0)
    up1 = jnp.concatenate([u[SUB:TILE, :], last_a], axis=0)
    up2 = jnp.concatenate([u[2 * SUB:TILE, :], last_a, last_b], axis=0)
    return cw[0:1, :] * um1 + cw[1:2, :] * u + cw[2:3, :] * up1 + cw[3:4, :] * up2 + cb


def _lru_gates(v, d, wr_ref, wi_ref, br_ref, bi_ref, lam_ref):
    vb = v.astype(BF16)
    n_cb = LRU_WIDTH // LRU_CH

    def gate(w_ref, b_ref):
        z = [jnp.dot(vb[:, c * LRU_CH:(c + 1) * LRU_CH], w_ref[d, c], preferred_element_type=F32) for c in range(n_cb)]
        return jnp.tanh(jnp.concatenate(z, axis=1) + b_ref[d:d + 1, :])

    tr = gate(wr_ref, br_ref)
    ti = gate(wi_ref, bi_ref)
    half = (-0.5 * LRU_C * LOG2E) * _softplus(-lam_ref[d:d + 1, :])
    a = jnp.exp2(half + half * tr)
    om = 1.0 - a * a
    b = (om * lax.rsqrt(jnp.maximum(om, TINY))) * ((0.5 * v) * (1.0 + ti))
    return a, b


def _lru_scan(a, b, h_in, reverse, hl_scr, p_scr):
    order = range(GROUPS - 1, -1, -1) if reverse else range(GROUPS)
    hl = None
    for g in order:
        ag = a[g * SUB:(g + 1) * SUB, :]
        bg = b[g * SUB:(g + 1) * SUB, :]
        if hl is None:
            hl, p = bg, ag
        else:
            hl = ag * hl + bg
            p = ag * p
        hl_scr[g * SUB:(g + 1) * SUB, :] = hl
        p_scr[g * SUB:(g + 1) * SUB, :] = p
    blocks = range(SUB - 1, -1, -1) if reverse else range(SUB)
    carry = h_in
    cins = {}
    for s in blocks:
        cins[s] = carry
        carry = hl[s:s + 1, :] + p[s:s + 1, :] * carry
    cin = jnp.concatenate([cins[s] for s in range(SUB)], axis=0)
    hfull = hl_scr[...] + p_scr[...] * jnp.tile(cin, (GROUPS, 1))
    return hfull, carry


def _route(sel, aff):
    def top2_sum(a, b, c, d):
        hi1, lo1 = jnp.maximum(a, b), jnp.minimum(a, b)
        hi2, lo2 = jnp.maximum(c, d), jnp.minimum(c, d)
        return jnp.maximum(hi1, hi2) + jnp.maximum(jnp.minimum(hi1, hi2), jnp.maximum(lo1, lo2))

    scores = [top2_sum(*sel[EXPERTS_PER_GROUP * g:EXPERTS_PER_GROUP * (g + 1)]) for g in range(N_GROUPS)]
    best = jnp.zeros_like(scores[0], dtype=jnp.int32)
    best_v = scores[0]
    for g in range(1, N_GROUPS):
        upd = scores[g] > best_v
        best = jnp.where(upd, g, best)
        best_v = jnp.where(upd, scores[g], best_v)
    chosen = []
    for e in range(N_EXPERTS):
        g = e // EXPERTS_PER_GROUP
        rank = jnp.zeros_like(best)
        for o in range(EXPERTS_PER_GROUP * g, EXPERTS_PER_GROUP * (g + 1)):
            if o == e:
                continue
            ahead = sel[o] > sel[e]
            if o < e:
                ahead = jnp.logical_or(ahead, sel[o] == sel[e])
            rank = rank + ahead.astype(jnp.int32)
        chosen.append(jnp.logical_and(best == g, rank < TOP_K))
    total = jnp.zeros_like(aff[0])
    for e in range(N_EXPERTS):
        total = total + jnp.where(chosen[e], aff[e], 0.0)
    gates = [jnp.where(chosen[e], aff[e] / total, 0.0) for e in range(N_EXPERTS)]
    return gates, [c.astype(F32) for c in chosen]


def _out_kernel(x_ref, ya_ref, hc_ref, g_ref, *refs):
    mod_refs = refs[:PAIR]
    (wr_ref, wi_ref, br_ref, bi_ref, lam_ref, nrm_ref, w_ref, rwc_ref, rb_ref,
     xo_ref, hp_ref, rt_ref, h_scr, hl_scr, p_scr) = refs[PAIR:]
    rows = PAIR * TILE

    @pl.when(pl.program_id(1) == 0)
    def _():
        h_scr[...] = jnp.zeros_like(h_scr)

    ybs = []
    for k in range(PAIR):
        a_r, b_r = _lru_gates(hc_ref[k, :, LRU_WIDTH:].astype(F32), 1, wr_ref, wi_ref, br_ref, bi_ref, lam_ref)
        hrev, h_last = _lru_scan(a_r, b_r, h_scr[k], True, hl_scr, p_scr)
        h_scr[k] = h_last
        gx = g_ref[k].astype(F32)
        gate = (0.5 * gx) * (1.0 + jnp.tanh(gx * (GELU_C + (GELU_C * 0.044715) * (gx * gx))))
        ybs.append((gate * (hc_ref[k, :, 0:LRU_WIDTH].astype(F32) + hrev)).astype(BF16))
    ya = ya_ref[...].reshape(rows, NA_WIDTH)
    yb = jnp.concatenate(ybs, axis=0)
    y = jnp.dot(ya, w_ref[0:NA_WIDTH, :], preferred_element_type=F32)
    y = y + jnp.dot(yb, w_ref[NA_WIDTH:, :], preferred_element_type=F32)
    hs = []
    for k in range(PAIR):
        m = mod_refs[k][0, 0]
        x = x_ref[k] + m[2:3, :] * y[k * TILE:(k + 1) * TILE, :]
        xo_ref[k] = x
        ms = jnp.mean(x * x, axis=-1, keepdims=True)
        h = (x * lax.rsqrt(ms + EPS)) * nrm_ref[...]
        h = h * (1.0 + m[4:5, :]) + m[3:4, :]
        hp_ref[k] = _pack_rows(h)
        hs.append(h)
    h = jnp.concatenate(hs, axis=0)
    h_hi = h.astype(BF16)
    h_lo = (h - h_hi.astype(F32)).astype(BF16)
    nt = (((1,), (1,)), ((), ()))
    rwc = rwc_ref[...]
    both = lax.dot_general(rwc, h_hi, nt, preferred_element_type=F32)
    lg = (both[0:N_EXPERTS, :] + both[N_EXPERTS:, :]
          + lax.dot_general(rwc[0:N_EXPERTS, :], h_lo, nt, preferred_element_type=F32))
    aff_all = jax.nn.sigmoid(lg)
    sel_all = aff_all + rb_ref[...]
    aff = [aff_all[e:e + 1, :] for e in range(N_EXPERTS)]
    sel = [sel_all[e:e + 1, :] for e in range(N_EXPERTS)]
    gates, chosen = _route(sel, aff)
    rt = jnp.concatenate(gates + chosen, axis=0)
    for k in range(PAIR):
        rt_ref[k, 0] = rt[:, k * TILE:(k + 1) * TILE]


def _out_proj(l, boff, xs, ya, hc, proj, lru, mod, nrm, w_out, rwc, rb):
    rev = lambda i: jnp.where(i == 0, 0, N_TILES - i)
    tok = lambda width, col=0: pl.BlockSpec((PAIR, TILE, width), lambda bp, i: (bp, rev(i), col))
    mod_specs = [pl.BlockSpec((1, 1, N_MOD, D_MODEL),
                              lambda bp, i, k=k: (l, _mod_row(boff + PAIR * bp + k, i), 0, 0)) for k in range(PAIR)]
    n_cb = LRU_WIDTH // LRU_CH
    lru_specs = [_full_spec((2, n_cb, LRU_CH, LRU_CH)), _full_spec((2, n_cb, LRU_CH, LRU_CH)),
                 _full_spec((2, LRU_WIDTH)), _full_spec((2, LRU_WIDTH)), _full_spec((2, LRU_WIDTH))]
    return pl.pallas_call(
        _out_kernel,
        grid=(SB // PAIR, N_TILES),
        in_specs=[tok(D_MODEL), tok(NA_WIDTH), tok(2 * LRU_WIDTH), tok(LRU_WIDTH, COL_G)]
        + mod_specs + lru_specs
        + [_full_spec((1, D_MODEL)), _full_spec((D_MODEL, D_MODEL)), _full_spec((2 * N_EXPERTS, D_MODEL)),
           _full_spec((N_EXPERTS, 1))],
        out_specs=[tok(D_MODEL), tok(PACK_W),
                   pl.BlockSpec((PAIR, 1, ROUTE_ROWS, TILE), lambda bp, i: (bp, rev(i), 0, 0))],
        out_shape=[jax.ShapeDtypeStruct((SB, L_TOT, D_MODEL), F32),
                   jax.ShapeDtypeStruct((SB, L_TOT, PACK_W), I32),
                   jax.ShapeDtypeStruct((SB, N_TILES, ROUTE_ROWS, TILE), F32)],
        scratch_shapes=[pltpu.VMEM((PAIR, 1, LRU_WIDTH), F32), pltpu.VMEM((TILE, LRU_WIDTH), F32),
                        pltpu.VMEM((TILE, LRU_WIDTH), F32)],
        compiler_params=_cparams(("parallel", "arbitrary")),
        name="out_proj_router",
    )(xs, ya, hc, proj, *([mod] * PAIR), *lru[2:], nrm, w_out, rwc, rb)


def _pos_kernel(rt_ref, tri_ref, start_ref, pos_ref, wt_ref, run_scr):
    @pl.when(pl.program_id(0) == 0)
    def _():
        run_scr[...] = jnp.zeros_like(run_scr)

    base = start_ref[...] + run_scr[...]
    for k in range(POS_TILES):
        gates = rt_ref[k, 0:N_EXPERTS, :]
        chosen = rt_ref[k, N_EXPERTS:, :]
        rank = jnp.dot(chosen.astype(BF16), tri_ref[...], preferred_element_type=F32)
        posf = rank + base
        seen = jnp.zeros((1, TILE), F32)
        p0 = jnp.zeros((1, TILE), F32)
        p1 = jnp.zeros((1, TILE), F32)
        w0 = jnp.zeros((1, TILE), F32)
        w1 = jnp.zeros((1, TILE), F32)
        for e in range(N_EXPERTS):
            ch = chosen[e:e + 1, :]
            first = ch * (1.0 - seen)
            second = ch * seen
            p0 = p0 + first * posf[e:e + 1, :]
            p1 = p1 + second * posf[e:e + 1, :]
            w0 = w0 + first * gates[e:e + 1, :]
            w1 = w1 + second * gates[e:e + 1, :]
            seen = jnp.minimum(seen + ch, 1.0)
        pos_ref[k] = jnp.concatenate([p0, p1], axis=0).astype(I32)
        wpad = jnp.concatenate([w0, w1, jnp.zeros((LANES - TOP_K, TILE), F32)], axis=0)
        wt_ref[k * TILE:(k + 1) * TILE, :] = jnp.transpose(wpad)
        base = base + jnp.sum(chosen, axis=1, keepdims=True)
    run_scr[...] = base - start_ref[...]


def _positions(route, tri, start):
    return pl.pallas_call(
        _pos_kernel,
        grid=(N_TOK_TILES // POS_TILES,),
        in_specs=[pl.BlockSpec((POS_TILES, ROUTE_ROWS, TILE), lambda i: (i, 0, 0)),
                  pl.BlockSpec((TILE, TILE), lambda i: (0, 0)),
                  pl.BlockSpec((N_EXPERTS, 1), lambda i: (0, 0))],
        out_specs=[pl.BlockSpec((POS_TILES, TOP_K, TILE), lambda i: (i, 0, 0)),
                   pl.BlockSpec((POS_TILES * TILE, LANES), lambda i: (i, 0))],
        out_shape=[jax.ShapeDtypeStruct((N_TOK_TILES, TOP_K, TILE), I32),
                   jax.ShapeDtypeStruct((N_TOK, LANES), F32)],
        scratch_shapes=[pltpu.VMEM((N_EXPERTS, 1), F32)],
        compiler_params=_cparams(("arbitrary",)),
        name="moe_positions",
    )(route, tri, start)


def _work_items(counts):
    smem = pl.BlockSpec(memory_space=pltpu.SMEM)
    item = jax.ShapeDtypeStruct((N_ITEMS,), I32)
    return pl.pallas_call(
        _items_kernel,
        in_specs=[smem],
        out_specs=[smem] * 4,
        out_shape=[item] * 4,
        name="moe_work_items",
    )(counts)


def _items_kernel(cnt_ref, tile_ref, exp_ref, lo_ref, hi_ref):
    n = jnp.int32(0)
    start = jnp.int32(0)
    last_e = jnp.int32(0)
    for e in range(N_EXPERTS):
        cnt = cnt_ref[e]
        end = start + cnt
        first = start // EXP_TILE
        n_tiles = jnp.where(cnt > 0, (end - 1) // EXP_TILE - first + 1, 0)

        def put(j, carry, e=e, n=n, start=start, end=end, first=first):
            tile = first + j
            tile_ref[n + j] = tile
            exp_ref[n + j] = jnp.int32(e)
            lo_ref[n + j] = jnp.maximum(start - tile * EXP_TILE, 0)
            hi_ref[n + j] = jnp.minimum(end - tile * EXP_TILE, EXP_TILE)
            return carry

        lax.fori_loop(0, n_tiles, put, 0)
        n = n + n_tiles
        start = end
        last_e = jnp.where(cnt > 0, e, last_e)

    def pad(j, carry):
        tile_ref[j] = jnp.int32(N_EXP_TILES - 1)
        exp_ref[j] = last_e
        lo_ref[j] = jnp.int32(0)
        hi_ref[j] = jnp.int32(0)
        return carry

    lax.fori_loop(n, N_ITEMS, pad, 0)


def _sc_worker_base():
    return (lax.axis_index("s") * SC_CORES + lax.axis_index("c")) * SC_ROWS


def _sc_dispatch_body(h_hbm, p0_hbm, p1_hbm, out_hbm, i0_v, i1_v, rows_v, sem):
    base = _sc_worker_base()

    @pl.loop(0, SC_N_CHUNKS)
    def _(j):
        off = base + j * SC_CHUNK
        pltpu.sync_copy(h_hbm.at[pl.ds(off, SC_CHUNK)], rows_v)
        pltpu.sync_copy(p0_hbm.at[pl.ds(off, SC_CHUNK)], i0_v)
        pltpu.sync_copy(p1_hbm.at[pl.ds(off, SC_CHUNK)], i1_v)
        c0 = pltpu.async_copy(rows_v, out_hbm.at[i0_v], sem)
        c1 = pltpu.async_copy(rows_v, out_hbm.at[i1_v], sem)
        c0.wait()
        c1.wait()


def _sc_combine_body(y_hbm, p0_hbm, p1_hbm, g0_hbm, g1_hbm, i_v, rows_v, sem):
    base = _sc_worker_base()

    @pl.loop(0, SC_N_CHUNKS)
    def _(j):
        off = base + j * SC_CHUNK
        for p_hbm, g_hbm in ((p0_hbm, g0_hbm), (p1_hbm, g1_hbm)):
            pltpu.sync_copy(p_hbm.at[pl.ds(off, SC_CHUNK)], i_v)
            pltpu.async_copy(y_hbm.at[i_v], rows_v, sem).wait()
            pltpu.sync_copy(rows_v, g_hbm.at[pl.ds(off, SC_CHUNK)])


def _sc_mesh():
    return plsc.VectorSubcoreMesh(core_axis_name="c", subcore_axis_name="s")


def _dispatch(hp, pos0, pos1):
    return pl.kernel(
        _sc_dispatch_body,
        out_type=jax.ShapeDtypeStruct((N_ASSIGN, PACK_W), I32),
        mesh=_sc_mesh(),
        scratch_types=[pltpu.VMEM((SC_CHUNK,), I32), pltpu.VMEM((SC_CHUNK,), I32),
                       pltpu.VMEM((SC_CHUNK, PACK_W), I32), pltpu.SemaphoreType.DMA],
        name="moe_dispatch",
    )(hp, pos0, pos1)


def _combine(ys, pos0, pos1):
    row = jax.ShapeDtypeStruct((N_TOK, PACK_W), I32)
    return pl.kernel(
        _sc_combine_body,
        out_type=[row, row],
        mesh=_sc_mesh(),
        scratch_types=[pltpu.VMEM((SC_CHUNK,), I32), pltpu.VMEM((SC_CHUNK, PACK_W), I32),
                       pltpu.SemaphoreType.DMA],
        name="moe_combine",
    )(ys, pos0, pos1)


def _expert_kernel(tile_ref, exp_ref, lo_ref, hi_ref, x_ref, w1_ref, w3_ref, w2_ref, o_ref, w13_scr, w2_scr):
    i = pl.program_id(0)
    prev = jnp.maximum(i - 1, 0)

    @pl.when(jnp.logical_or(i == 0, exp_ref[i] != exp_ref[prev]))
    def _():
        w13_scr[:, 0:EXPERT_FF] = w1_ref[0, 0].astype(BF16)
        w13_scr[:, EXPERT_FF:] = w3_ref[0, 0].astype(BF16)
        w2_scr[...] = w2_ref[0, 0].astype(BF16)

    lo_f, hi_f = _unpack_rows(x_ref[...])
    h = jnp.concatenate([lo_f.astype(BF16), hi_f.astype(BF16)], axis=1)
    ab = jnp.dot(h, w13_scr[...], preferred_element_type=F32)
    a = ab[:, 0:EXPERT_FF]
    b = ab[:, EXPERT_FF:]
    t = ((a * jax.nn.sigmoid(a)) * b).astype(BF16)
    y = _pack_rows(jnp.dot(t, w2_scr[...], preferred_element_type=F32))
    row = lax.broadcasted_iota(jnp.int32, (EXP_TILE, PACK_W), 0)
    mine = jnp.logical_and(row >= lo_ref[i], row < hi_ref[i])
    revisit = jnp.logical_and(i > 0, tile_ref[i] == tile_ref[prev])

    @pl.when(jnp.logical_not(revisit))
    def _():
        o_ref[...] = jnp.where(mine, y, 0)

    @pl.when(revisit)
    def _():
        o_ref[...] = jnp.where(mine, y, o_ref[...])


def _experts(l, xs_sorted, items, w1, w3, w2):
    tile, e, lo, hi = items
    grid_spec = pltpu.PrefetchScalarGridSpec(
        num_scalar_prefetch=4,
        grid=(N_ITEMS,),
        in_specs=[pl.BlockSpec((EXP_TILE, PACK_W), lambda i, t, e, lo, hi: (t[i], 0)),
                  pl.BlockSpec((1, 1, D_MODEL, EXPERT_FF), lambda i, t, e, lo, hi: (l, e[i], 0, 0)),
                  pl.BlockSpec((1, 1, D_MODEL, EXPERT_FF), lambda i, t, e, lo, hi: (l, e[i], 0, 0)),
                  pl.BlockSpec((1, 1, EXPERT_FF, D_MODEL), lambda i, t, e, lo, hi: (l, e[i], 0, 0))],
        out_specs=pl.BlockSpec((EXP_TILE, PACK_W), lambda i, t, e, lo, hi: (t[i], 0)),
        scratch_shapes=[pltpu.VMEM((D_MODEL, 2 * EXPERT_FF), BF16), pltpu.VMEM((EXPERT_FF, D_MODEL), BF16)],
    )
    return pl.pallas_call(
        _expert_kernel,
        grid_spec=grid_spec,
        out_shape=jax.ShapeDtypeStruct((N_ASSIGN, PACK_W), I32),
        compiler_params=_cparams(("arbitrary",)),
        name="moe_experts",
    )(tile, e, lo, hi, xs_sorted, w1, w3, w2)


def _moe(l, hp, route, tri, w1, w3, w2):
    counts = jnp.sum(route[:, N_EXPERTS:, :], axis=(0, 2))
    start = (jnp.cumsum(counts) - counts).reshape(N_EXPERTS, 1)
    pos, wt = _positions(route, tri, start)
    pos0 = pos[:, 0, :].reshape(N_TOK)
    pos1 = pos[:, 1, :].reshape(N_TOK)
    items = _work_items(counts.astype(I32))
    xs_sorted = _dispatch(hp.reshape(N_TOK, PACK_W), pos0, pos1)
    ys = _experts(l, xs_sorted, items, w1, w3, w2)
    g0, g1 = _combine(ys, pos0, pos1)
    shape = (SB, L_TOT, PACK_W)
    return g0.reshape(shape), g1.reshape(shape), wt.reshape(SB, L_TOT, LANES)


def _final_kernel(x_ref, g0_ref, g1_ref, wt_ref, mod_ref, *refs):
    o_ref, scr = refs[-2:]
    for b in range(SB):
        x = _moe_residual(x_ref[b], g0_ref[b], g1_ref[b], wt_ref[b], mod_ref[0, b][5:6, :])
        o_ref[b] = _from_scan_major(x, scr)


def _final(streams, mod):
    lat = lambda width: pl.BlockSpec((SB, TILE, width), lambda j: (0, j + 1, 0))
    out = None
    for k, st in enumerate(streams):
        in_specs = [lat(D_MODEL), lat(PACK_W), lat(PACK_W), lat(LANES),
                    pl.BlockSpec((1, SB, N_MOD, D_MODEL), lambda j, k=k: (DEPTH - 1, k, 0, 0))]
        args = [st["xs"]] + list(st["moe"]) + [mod]
        aliases = {}
        if out is not None:
            in_specs.append(pl.BlockSpec(memory_space=pl.ANY))
            args.append(out)
            aliases = {len(args) - 1: 0}
        out = pl.pallas_call(
            _final_kernel,
            grid=(N_LAT_TILES,),
            in_specs=in_specs,
            out_specs=pl.BlockSpec((SB, TILE, D_MODEL), lambda j, k=k: (k, j, 0)),
            out_shape=jax.ShapeDtypeStruct((BATCH, SEQ, D_MODEL), F32),
            scratch_shapes=[pltpu.VMEM((D_MODEL // LANES, TILE, LANES), F32)],
            input_output_aliases=aliases,
            compiler_params=_cparams(("arbitrary",)),
            name="final_residual",
        )(*args)
    return out


def _bias_kernel(tab_ref, perm_ref, o_ref, scr):
    rows_q = TILE // GRID_W
    lane = lax.broadcasted_iota(jnp.int32, (GROUPS, TILE), 1)
    ka = (lane >> 1) & (rows_q - 1)
    neg = jnp.full((GROUPS, TILE), NEG_INF, F32)
    qc = lax.broadcasted_iota(jnp.int32, (GRID_W, GRID_W), 0)
    kc = lax.broadcasted_iota(jnp.int32, (GRID_W, GRID_W), 1)
    col0 = jnp.clip(qc - NA_WIN_COLS // 2, 0, GRID_W - NA_WIN_COLS)
    in_win = jnp.logical_and(kc >= col0, kc < col0 + NA_WIN_COLS)
    toeplitz = []
    for r in range(2 * NA_WIN_ROWS - 1):
        row = jnp.broadcast_to(tab_ref[0, r:r + 1, :], (GRID_W, LANES))
        shifted = pltpu.roll(row, LANES - (GRID_W - 1), 1, stride=1, stride_axis=0)
        toeplitz.append(jnp.where(in_win, shifted[:, 0:GRID_W], NEG_INF))

    def interleaved(r0):
        acc = None
        for k in range(rows_q):
            val = toeplitz[r0 + k]
            t1 = val.astype(BF16)
            r1 = val - t1.astype(F32)
            t2 = r1.astype(BF16)
            t3 = (r1 - t2.astype(F32)).astype(BF16)
            for term in (t1, t2, t3):
                moved = jnp.dot(term, perm_ref[k], preferred_element_type=F32)
                acc = moved if acc is None else acc + moved
        return acc

    t4 = [interleaved(r0) for r0 in range(WIN_TILES * rows_q)]
    for a in range(rows_q):
        for qs in range(SUB // rows_q):
            s = (SUB // rows_q) * a + qs
            for kt in range(WIN_TILES):
                src = t4[rows_q * kt - a + 3][qs * GROUPS:(qs + 1) * GROUPS, :]
                c = rows_q * kt + ka
                variants = (
                    src if kt >= 1 else neg,
                    jnp.where(jnp.logical_and(c >= a, c <= a + NA_WIN_ROWS - 1), src, neg),
                    src if kt <= 1 else neg,
                )
                for v, val in enumerate(variants):
                    for half in range(TILE // LANES):
                        scr[v, 2 * kt + half, pl.ds(s, GROUPS, stride=SUB), :] = val[:, half * LANES:(half + 1) * LANES]
    for v in range(BIAS_VARIANTS):
        o_ref[v, 0] = jnp.concatenate([scr[v, j] for j in range(WIN_TILES * TILE // LANES)], axis=1)


def _bias_tiles(table):
    rows_q = TILE // GRID_W
    n_r = 2 * NA_WIN_ROWS - 1
    tab = jnp.pad(table.astype(F32) * LOG2E, ((0, 0), (0, 0), (BIAS_PAD, LANES - BIAS_PAD - (2 * NA_WIN_COLS - 1))))
    n_h = table.shape[0]
    half = GRID_W // GROUPS
    perm = np.zeros((rows_q, GRID_W, TILE), np.float32)
    for k in range(rows_q):
        for ks in range(half):
            for kg in range(GROUPS):
                perm[k, ks * GROUPS + kg, kg * SUB + k * half + ks] = 1.0
    return pl.pallas_call(
        _bias_kernel,
        grid=(n_h,),
        in_specs=[pl.BlockSpec((1, n_r, LANES), lambda h: (h, 0, 0)),
                  pl.BlockSpec((rows_q, GRID_W, TILE), lambda h: (0, 0, 0))],
        out_specs=pl.BlockSpec((BIAS_VARIANTS, 1, TILE, WIN_TILES * TILE), lambda h: (0, h, 0, 0)),
        out_shape=jax.ShapeDtypeStruct((BIAS_VARIANTS, n_h, TILE, WIN_TILES * TILE), F32),
        scratch_shapes=[pltpu.VMEM((BIAS_VARIANTS, WIN_TILES * TILE // LANES, TILE, LANES), F32)],
        compiler_params=_cparams(("parallel",)),
        name="bias_tiles",
    )(tab, jnp.asarray(perm, BF16))


def _block_diag(w, n_chunks):
    per = LRU_BLOCKS // n_chunks
    w = w.reshape(2, n_chunks, per, LRU_BLOCK, LRU_BLOCK)
    eye = jnp.eye(per, dtype=w.dtype)
    out = jnp.einsum('dcpij,pq->dcpiqj', w, eye)
    return out.reshape(2, n_chunks, per * LRU_BLOCK, per * LRU_BLOCK)


def kernel(x, c, ctx, c_ctx, w_mod, b_mod, norm_mix, norm_ffn, w_in, w_out, q_gain, k_gain, na_bias,
           conv_w, conv_b, lru_w_r, lru_b_r, lru_w_i, lru_b_i, lru_lambda, router_w, router_b,
           exp_w1, exp_w3, exp_w2):
    cs = jnp.concatenate([c, c_ctx[None, :], jnp.zeros((MOD_ROWS - BATCH - 1, D_MODEL), F32)], axis=0)
    mod = _modulation(cs, w_mod, b_mod).reshape(DEPTH, MOD_ROWS, N_MOD, D_MODEL)

    head_of = np.arange(NA_WIDTH) // HEAD_DIM
    bd = jnp.asarray((head_of[:, None] == head_of[None, :]).astype(np.float32) / HEAD_DIM, BF16)
    tri = jnp.asarray(np.triu(np.ones((TILE, TILE), np.float32), 1), BF16)
    rwt = router_w.T
    rwh = rwt.astype(BF16)
    rwc = jnp.concatenate([rwh, (rwt - rwh.astype(F32)).astype(BF16)], axis=0)
    rb = router_b.reshape(N_EXPERTS, 1)
    n_cb = LRU_WIDTH // LRU_CH

    bias_tiles = _bias_tiles(na_bias.reshape(DEPTH * NA_HEADS, 2 * NA_WIN_ROWS - 1, 2 * NA_WIN_COLS - 1))
    streams = [{"xs": (x, ctx), "moe": None, "boff": sidx * SB} for sidx in range(STREAMS)]
    for l in range(DEPTH):
        qg = jnp.tile(q_gain[l] * (ATTN_SCALE * LOG2E), NA_HEADS)[None, :]
        kg = jnp.tile(k_gain[l], NA_HEADS)[None, :]
        w_in_l = w_in[l].astype(BF16)
        w_out_l = w_out[l].astype(BF16)
        wr = (0.5 * _block_diag(lru_w_r[l], n_cb)).astype(BF16)
        wi = (0.5 * _block_diag(lru_w_i[l], n_cb)).astype(BF16)
        lru = (conv_w[l], conv_b[l][None, :], wr, wi, 0.5 * lru_b_r[l], 0.5 * lru_b_i[l], lru_lambda[l])
        for st in streams:
            boff = st["boff"]
            xs, proj = _in_proj(l, boff, st["xs"], st["moe"], mod, norm_mix[l][None, :], w_in_l, bd, qg, kg)
            ya, hc = _attention(l, proj, bias_tiles, lru)
            xs, hp, route = _out_proj(l, boff, xs, ya, hc, proj, lru, mod, norm_ffn[l][None, :], w_out_l, rwc, rb)
            st["xs"] = xs
            st["moe"] = _moe(l, hp, route.reshape(N_TOK_TILES, ROUTE_ROWS, TILE), tri, exp_w1, exp_w3, exp_w2)
    return _final(streams, mod)
```

```python
import functools

import jax
import jax.numpy as jnp
import numpy as np
from jax import lax
from jax.experimental import pallas as pl
from jax.experimental.pallas import tpu as pltpu
from jax.experimental.pallas import tpu_sc as plsc

F32 = jnp.float32
BF16 = jnp.bfloat16
I32 = jnp.int32

D_MODEL = 1024
BATCH = 4
SEQ = 8192
DEPTH = 4
GRID_W = 64
CTX_LEN = 256
HEAD_DIM = 64
NA_WIDTH = 512
NA_HEADS = 8
NA_WIN_ROWS = 8
NA_WIN_COLS = 16
LRU_WIDTH = 512
LRU_BLOCKS = 8
LRU_BLOCK = 64
CONV_W = 4
LRU_C = 8.0
IN_COLS = 3 * NA_WIDTH + 2 * LRU_WIDTH
COL_KV, COL_Q, COL_U, COL_G = 0, 2, 3, 4
N_EXPERTS = 16
N_GROUPS = 4
EXPERTS_PER_GROUP = 4
TOP_K = 2
EXPERT_FF = 512
N_MOD = 6
ATTN_SCALE = HEAD_DIM ** -0.5
LOG2E = 1.4426950408889634
EPS = 1e-6
NEG_INF = -1e30
TINY = 1e-30
GELU_C = 0.7978845608028654

TILE = 256
WIN_TILES = 3
BIAS_VARIANTS = 3
HALO = 16
SUB = 8
LANES = 128
GROUPS = TILE // SUB
L_TOT = CTX_LEN + SEQ
N_TILES = L_TOT // TILE
N_LAT_TILES = SEQ // TILE
STREAMS = 2
SB = BATCH // STREAMS
N_TOK = SB * L_TOT
N_TOK_TILES = N_TOK // TILE
PAIR = 2
LRU_CH = 256
MOD_ROWS = 8
VMEM_LIMIT = 56 * 1024 * 1024

PACK_W = D_MODEL // 2
HI_MASK = -65536
N_ASSIGN = TOP_K * N_TOK
EXP_TILE = 512
N_EXP_TILES = N_ASSIGN // EXP_TILE
N_ITEMS = N_EXP_TILES + N_EXPERTS - 1
BIAS_PAD = GRID_W - NA_WIN_COLS
ROUTE_ROWS = 2 * N_EXPERTS
POS_TILES = 22

SC_CORES = 2
SC_SUBCORES = 16
SC_WORKERS = SC_CORES * SC_SUBCORES
SC_ROWS = N_TOK // SC_WORKERS
SC_CHUNK = 88
SC_N_CHUNKS = SC_ROWS // SC_CHUNK


def _cparams(sem):
    return pltpu.CompilerParams(dimension_semantics=sem, vmem_limit_bytes=VMEM_LIMIT)


def _pack_rows(v):
    lo = pltpu.bitcast(v[:, :PACK_W].astype(BF16).astype(F32), I32)
    hi = pltpu.bitcast(v[:, PACK_W:].astype(BF16).astype(F32), I32)
    return ((lo >> 16) & 0xFFFF) | (hi & HI_MASK)


def _unpack_rows(p):
    return pltpu.bitcast(p << 16, F32), pltpu.bitcast(p & HI_MASK, F32)


def _mod_kernel(c_ref, w_ref, b_ref, o_ref):
    c = c_ref[...]
    s = c * jax.nn.sigmoid(c)
    o_ref[0] = jnp.dot(s.astype(BF16), w_ref[0].astype(BF16), preferred_element_type=F32) + b_ref[0]


def _modulation(cs, w_mod, b_mod):
    return pl.pallas_call(
        _mod_kernel,
        grid=(DEPTH, N_MOD),
        in_specs=[
            pl.BlockSpec((MOD_ROWS, D_MODEL), lambda l, n: (0, 0)),
            pl.BlockSpec((1, D_MODEL, D_MODEL), lambda l, n: (l, 0, n)),
            pl.BlockSpec((1, 1, D_MODEL), lambda l, n: (l, 0, n)),
        ],
        out_specs=pl.BlockSpec((1, MOD_ROWS, D_MODEL), lambda l, n: (l, 0, n)),
        out_shape=jax.ShapeDtypeStruct((DEPTH, MOD_ROWS, N_MOD * D_MODEL), F32),
        compiler_params=_cparams(("arbitrary", "arbitrary")),
        name="modulation",
    )(cs, w_mod, b_mod.reshape(DEPTH, 1, N_MOD * D_MODEL))


def _mod_row(b, i):
    return jnp.where(i == 0, BATCH, b)


def _moe_residual(x, g0, g1, wt, gate_row):
    lo0, hi0 = _unpack_rows(g0)
    lo1, hi1 = _unpack_rows(g1)
    w0 = wt[:, 0:1]
    w1 = wt[:, 1:2]
    f = jnp.concatenate([w0 * lo0 + w1 * lo1, w0 * hi0 + w1 * hi1], axis=1)
    return x + gate_row * f


def _pair_tok_spec(width):
    return pl.BlockSpec((PAIR, TILE, width), lambda bp, i: (bp, i, 0))


def _pair_mod_specs(layer, boff):
    return [pl.BlockSpec((1, 1, N_MOD, D_MODEL),
                         lambda bp, i, k=k: (layer, _mod_row(boff + PAIR * bp + k, i), 0, 0))
            for k in range(PAIR)]


def _full_spec(shape):
    return pl.BlockSpec(shape, lambda bp, i: tuple(0 for _ in shape))


def _to_scan_major(src_ref, scr):
    n_slab = D_MODEL // LANES
    for s in range(SUB):
        for j in range(n_slab):
            scr[j, pl.ds(s, GROUPS, stride=SUB), :] = src_ref[s * GROUPS:(s + 1) * GROUPS, j * LANES:(j + 1) * LANES]
    return jnp.concatenate([scr[j] for j in range(n_slab)], axis=1)


def _from_scan_major(val, scr):
    n_slab = D_MODEL // LANES
    for j in range(n_slab):
        scr[j] = val[:, j * LANES:(j + 1) * LANES]
    blocks = [jnp.concatenate([scr[j, pl.ds(s, GROUPS, stride=SUB), :] for j in range(n_slab)], axis=1)
              for s in range(SUB)]
    return jnp.concatenate(blocks, axis=0)


def _in_kernel(has_prev, *refs):
    if has_prev:
        x_ref, g0_ref, g1_ref, wt_ref = refs[:4]
        mprev_refs = refs[4:4 + PAIR]
        refs = (x_ref,) + refs[4 + PAIR:]
    else:
        x_ref, ctx_ref = refs[:2]
        refs = (x_ref,) + refs[2:]
    mod_refs = refs[1:1 + PAIR]
    nrm_ref, w_ref, bd_ref, qg_ref, kg_ref = refs[1 + PAIR:6 + PAIR]
    outs = refs[6 + PAIR:]
    xo_ref, proj_ref = outs[:2]
    x_ref = refs[0]
    hs = []
    for k in range(PAIR):
        if has_prev:
            x = _moe_residual(x_ref[k], g0_ref[k], g1_ref[k], wt_ref[k], mprev_refs[k][0, 0][5:6, :])
            xo_ref[k] = x
        else:
            perm_scr = outs[2]

            @pl.when(pl.program_id(1) == 0)
            def _():
                xo_ref[k] = _to_scan_major(ctx_ref.at[k], perm_scr)

            @pl.when(pl.program_id(1) > 0)
            def _():
                xo_ref[k] = _to_scan_major(x_ref.at[k], perm_scr)

            x = xo_ref[k]
        m = mod_refs[k][0, 0]
        ms = jnp.mean(x * x, axis=-1, keepdims=True)
        h = (x * lax.rsqrt(ms + EPS)) * nrm_ref[...]
        hs.append((h * (1.0 + m[1:2, :]) + m[0:1, :]).astype(BF16))
    acc = jnp.dot(jnp.concatenate(hs, axis=0), w_ref[...], preferred_element_type=F32)
    bd = bd_ref[...]

    def head_norm(t, gain):
        ss = jnp.dot((t * t).astype(BF16), bd, preferred_element_type=F32)
        return (t * lax.rsqrt(ss + EPS)) * gain

    def put(col_block, val):
        for k in range(PAIR):
            proj_ref[k, :, col_block * NA_WIDTH:(col_block + 1) * NA_WIDTH] = val[k * TILE:(k + 1) * TILE, :].astype(BF16)

    put(COL_Q, head_norm(acc[:, 0:NA_WIDTH], qg_ref[...]))
    put(COL_KV, head_norm(acc[:, NA_WIDTH:2 * NA_WIDTH], kg_ref[...]))
    put(COL_KV + 1, acc[:, 2 * NA_WIDTH:3 * NA_WIDTH])
    put(COL_U, acc[:, 3 * NA_WIDTH:3 * NA_WIDTH + LRU_WIDTH])
    put(COL_G, acc[:, 3 * NA_WIDTH + LRU_WIDTH:])


def _in_proj(l, boff, xs, moe_prev, mod, nrm, w_in, bd, qg, kg):
    has_prev = moe_prev is not None
    if has_prev:
        in_specs = [_pair_tok_spec(D_MODEL)]
        args = [xs]
        in_specs += [_pair_tok_spec(PACK_W), _pair_tok_spec(PACK_W), _pair_tok_spec(LANES)]
        in_specs += _pair_mod_specs(l - 1, boff)
        args += list(moe_prev) + [mod] * PAIR
    else:
        pair0 = boff // PAIR
        in_specs = [pl.BlockSpec((PAIR, TILE, D_MODEL), lambda bp, i: (pair0 + bp, jnp.maximum(i - 1, 0), 0)),
                    pl.BlockSpec((PAIR, TILE, D_MODEL), lambda bp, i: (pair0 + bp, 0, 0))]
        args = list(xs)
    in_specs += _pair_mod_specs(l, boff) + [_full_spec((1, D_MODEL)), _full_spec((D_MODEL, IN_COLS)),
                                            _full_spec((NA_WIDTH, NA_WIDTH)), _full_spec((1, NA_WIDTH)),
                                            _full_spec((1, NA_WIDTH))]
    args += [mod] * PAIR + [nrm, w_in, bd, qg, kg]
    out_shape = [jax.ShapeDtypeStruct((SB, L_TOT, D_MODEL), F32), jax.ShapeDtypeStruct((SB, L_TOT, IN_COLS), BF16)]
    out_specs = [_pair_tok_spec(D_MODEL), _pair_tok_spec(IN_COLS)]
    scratch = [] if has_prev else [pltpu.VMEM((D_MODEL // LANES, TILE, LANES), F32)]
    outs = pl.pallas_call(
        functools.partial(_in_kernel, has_prev),
        grid=(SB // PAIR, N_TILES),
        in_specs=in_specs,
        out_specs=out_specs,
        out_shape=out_shape,
        scratch_shapes=scratch,
        compiler_params=_cparams(("parallel", "arbitrary")),
        name="in_proj",
    )(*args)
    return outs[0], outs[1]


def _attn_kernel(q_ref, kvp_ref, kvc_ref, kvn_ref, kvx_ref, bias_ref,
                 u_ref, up_ref, un_ref, cw_ref, cb_ref, wr_ref, wi_ref, br_ref, bi_ref, lam_ref,
                 o_ref, hc_ref, h_scr, hl_scr, p_scr):
    j = pl.program_id(1)

    @pl.when(j == 0)
    def _():
        h_scr[...] = jnp.zeros_like(h_scr)

    has_prev = jnp.where(j >= 2, 1.0, 0.0).astype(F32)
    has_next = jnp.where(jnp.logical_and(j >= 1, j <= N_TILES - 2), 1.0, 0.0).astype(F32)

    def forward_lru():
        for b in range(PAIR):
            cv = _lru_conv(u_ref[b].astype(F32), up_ref[b].astype(F32), un_ref[b].astype(F32), has_prev, has_next,
                           cw_ref[...], cb_ref[...])
            hc_ref[b, :, LRU_WIDTH:] = cv.astype(BF16)
            a_f, b_f = _lru_gates(cv, 0, wr_ref, wi_ref, br_ref, bi_ref, lam_ref)
            hfull, h_last = _lru_scan(a_f, b_f, h_scr[b], False, hl_scr, p_scr)
            hc_ref[b, :, 0:LRU_WIDTH] = hfull.astype(BF16)
            h_scr[b] = h_last

    pair_w = 2 * HEAD_DIM
    lane = lax.broadcasted_iota(jnp.int32, (TILE, pair_w), 1)
    low = lane < HEAD_DIM
    nt = (((1,), (1,)), ((), ()))
    n_win = WIN_TILES * TILE

    def attend(kv_refs, windowed):
        low_kv = lax.broadcasted_iota(jnp.int32, (len(kv_refs) * TILE, pair_w), 1) < HEAD_DIM
        for b, hp in [(b, hp) for b in range(PAIR) for hp in range(NA_HEADS // 2)]:
            cols = slice(hp * pair_w, (hp + 1) * pair_w)
            vcols = slice(NA_WIDTH + hp * pair_w, NA_WIDTH + (hp + 1) * pair_w)
            q = q_ref[b, :, cols]
            kb = jnp.concatenate([r[b, :, cols] for r in kv_refs], axis=0)
            vb = jnp.concatenate([r[b, :, vcols] for r in kv_refs], axis=0)
            outs = []
            for hh in range(2):
                own = low if hh == 0 else jnp.logical_not(low)
                own_kv = low_kv if hh == 0 else jnp.logical_not(low_kv)
                qh = jnp.where(own, q, jnp.zeros_like(q))
                s = lax.dot_general(qh, kb, nt, preferred_element_type=F32)
                if windowed:
                    s_win = s[:, 0:n_win] + bias_ref[0, 2 * hp + hh]
                    s_ctx = s[:, n_win:]
                    m = jnp.maximum(jnp.max(s_win, axis=-1, keepdims=True), jnp.max(s_ctx, axis=-1, keepdims=True))
                    p = jnp.concatenate([jnp.exp2((s_win - m).astype(BF16)), jnp.exp2((s_ctx - m).astype(BF16))],
                                        axis=1)
                else:
                    p = jnp.exp2((s - jnp.max(s, axis=-1, keepdims=True)).astype(BF16))
                va = jnp.where(own_kv, vb, jnp.ones_like(vb))
                o = jnp.dot(p, va, preferred_element_type=F32)
                outs.append(o / pltpu.roll(o, HEAD_DIM, 1))
            o_ref[b, :, cols] = jnp.where(low, outs[0], outs[1]).astype(BF16)

    @pl.when(j == 0)
    def _():
        forward_lru()
        attend((kvx_ref,), False)

    @pl.when(j > 0)
    def _():
        forward_lru()
        attend((kvp_ref, kvc_ref, kvn_ref, kvx_ref), True)


def _attention(l, proj, bias_tiles, lru):
    last = N_LAT_TILES - 1
    blk = (PAIR, TILE, 2 * NA_WIDTH)
    prev_map = lambda b, j: (b, 1 + jnp.clip(j - 2, 0, last), COL_KV // 2)
    cur_map = lambda b, j: (b, jnp.maximum(j, 1), COL_KV // 2)
    next_map = lambda b, j: (b, 1 + jnp.clip(j, 0, last), COL_KV // 2)
    ctx_map = lambda b, j: (b, 0, COL_KV // 2)
    var_map = lambda b, j: (jnp.where(j <= 1, 0, jnp.where(j == N_TILES - 1, 2, 1)), l, 0, 0)
    kv_specs = [pl.BlockSpec(blk, prev_map), pl.BlockSpec(blk, cur_map), pl.BlockSpec(blk, next_map),
                pl.BlockSpec(blk, ctx_map)]
    tok = lambda col: pl.BlockSpec((PAIR, TILE, NA_WIDTH), lambda b, j: (b, j, col))
    halo = TILE // HALO
    n_halo = L_TOT // HALO
    prev16 = pl.BlockSpec((PAIR, HALO, LRU_WIDTH), lambda b, j: (b, jnp.maximum(j * halo - 1, 0), COL_U))
    next16 = pl.BlockSpec((PAIR, HALO, LRU_WIDTH), lambda b, j: (b, jnp.minimum((j + 1) * halo, n_halo - 1), COL_U))
    whole = lambda shape: pl.BlockSpec(shape, lambda b, j: tuple(0 for _ in shape))
    n_cb = LRU_WIDTH // LRU_CH
    lru_specs = [whole((CONV_W, LRU_WIDTH)), whole((1, LRU_WIDTH)), whole((2, n_cb, LRU_CH, LRU_CH)),
                 whole((2, n_cb, LRU_CH, LRU_CH)), whole((2, LRU_WIDTH)), whole((2, LRU_WIDTH)), whole((2, LRU_WIDTH))]
    return pl.pallas_call(
        _attn_kernel,
        grid=(SB // PAIR, N_TILES),
        in_specs=[tok(COL_Q)] + kv_specs + [pl.BlockSpec((1, NA_HEADS, TILE, WIN_TILES * TILE), var_map)]
        + [tok(COL_U), prev16, next16] + lru_specs,
        out_specs=[tok(0), pl.BlockSpec((PAIR, TILE, 2 * LRU_WIDTH), lambda b, j: (b, j, 0))],
        out_shape=[jax.ShapeDtypeStruct((SB, L_TOT, NA_WIDTH), BF16),
                   jax.ShapeDtypeStruct((SB, L_TOT, 2 * LRU_WIDTH), BF16)],
        scratch_shapes=[pltpu.VMEM((PAIR, 1, LRU_WIDTH), F32), pltpu.VMEM((TILE, LRU_WIDTH), F32),
                        pltpu.VMEM((TILE, LRU_WIDTH), F32)],
        compiler_params=_cparams(("parallel", "arbitrary")),
        name="na_attention",
    )(proj, proj, proj, proj, proj, bias_tiles, proj, proj, proj, *lru)


def _softplus(x):
    return jnp.maximum(x, 0.0) + jnp.log1p(jnp.exp(-jnp.abs(x)))


def _lru_conv(u, prev16, next16, has_prev, has_next, cw, cb):
    sub = lax.broadcasted_iota(jnp.int32, (SUB, u.shape[1]), 0)
    prow = prev16[HALO - 1:HALO, :] * has_prev
    n0 = next16[0:1, :] * has_next
    n8 = next16[8:9, :] * has_next
    first8 = jnp.where(sub == 0, prow, pltpu.roll(u[TILE - SUB:TILE, :], 1, 0))
    last_a = jnp.where(sub == SUB - 1, n0, pltpu.roll(u[0:SUB, :], SUB - 1, 0))
    last_b = jnp.where(sub == SUB - 1, n8, pltpu.roll(u[SUB:2 * SUB, :], SUB - 1, 0))
    um1 = jnp.concatenate([first8, u[0:TILE - SUB, :]], axis=0)
    up1 = jnp.concatenate([u[SUB:TILE, :], last_a], axis=0)
    up2 = jnp.concatenate([u[2 * SUB:TILE, :], last_a, last_b], axis=0)
    return cw[0:1, :] * um1 + cw[1:2, :] * u + cw[2:3, :] * up1 + cw[3:4, :] * up2 + cb


def _lru_gates(v, d, wr_ref, wi_ref, br_ref, bi_ref, lam_ref):
    vb = v.astype(BF16)
    n_cb = LRU_WIDTH // LRU_CH

    def gate(w_ref, b_ref):
        z = [jnp.dot(vb[:, c * LRU_CH:(c + 1) * LRU_CH], w_ref[d, c], preferred_element_type=F32) for c in range(n_cb)]
        return jnp.tanh(jnp.concatenate(z, axis=1) + b_ref[d:d + 1, :])

    tr = gate(wr_ref, br_ref)
    ti = gate(wi_ref, bi_ref)
    half = (-0.5 * LRU_C * LOG2E) * _softplus(-lam_ref[d:d + 1, :])
    a = jnp.exp2(half + half * tr)
    om = 1.0 - a * a
    b = (om * lax.rsqrt(jnp.maximum(om, TINY))) * ((0.5 * v) * (1.0 + ti))
    return a, b


def _lru_scan(a, b, h_in, reverse, hl_scr, p_scr):
    order = range(GROUPS - 1, -1, -1) if reverse else range(GROUPS)
    hl = None
    for g in order:
        ag = a[g * SUB:(g + 1) * SUB, :]
        bg = b[g * SUB:(g + 1) * SUB, :]
        if hl is None:
            hl, p = bg, ag
        else:
            hl = ag * hl + bg
            p = ag * p
        hl_scr[g * SUB:(g + 1) * SUB, :] = hl
        p_scr[g * SUB:(g + 1) * SUB, :] = p
    blocks = range(SUB - 1, -1, -1) if reverse else range(SUB)
    carry = h_in
    cins = {}
    for s in blocks:
        cins[s] = carry
        carry = hl[s:s + 1, :] + p[s:s + 1, :] * carry
    cin = jnp.concatenate([cins[s] for s in range(SUB)], axis=0)
    hfull = hl_scr[...] + p_scr[...] * jnp.tile(cin, (GROUPS, 1))
    return hfull, carry


def _route(sel, aff):
    def top2_sum(a, b, c, d):
        hi1, lo1 = jnp.maximum(a, b), jnp.minimum(a, b)
        hi2, lo2 = jnp.maximum(c, d), jnp.minimum(c, d)
        return jnp.maximum(hi1, hi2) + jnp.maximum(jnp.minimum(hi1, hi2), jnp.maximum(lo1, lo2))

    scores = [top2_sum(*sel[EXPERTS_PER_GROUP * g:EXPERTS_PER_GROUP * (g + 1)]) for g in range(N_GROUPS)]
    best = jnp.zeros_like(scores[0], dtype=jnp.int32)
    best_v = scores[0]
    for g in range(1, N_GROUPS):
        upd = scores[g] > best_v
        best = jnp.where(upd, g, best)
        best_v = jnp.where(upd, scores[g], best_v)
    chosen = []
    for e in range(N_EXPERTS):
        g = e // EXPERTS_PER_GROUP
        rank = jnp.zeros_like(best)
        for o in range(EXPERTS_PER_GROUP * g, EXPERTS_PER_GROUP * (g + 1)):
            if o == e:
                continue
            ahead = sel[o] > sel[e]
            if o < e:
                ahead = jnp.logical_or(ahead, sel[o] == sel[e])
            rank = rank + ahead.astype(jnp.int32)
        chosen.append(jnp.logical_and(best == g, rank < TOP_K))
    total = jnp.zeros_like(aff[0])
    for e in range(N_EXPERTS):
        total = total + jnp.where(chosen[e], aff[e], 0.0)
    gates = [jnp.where(chosen[e], aff[e] / total, 0.0) for e in range(N_EXPERTS)]
    return gates, [c.astype(F32) for c in chosen]


def _out_kernel(x_ref, ya_ref, hc_ref, g_ref, *refs):
    mod_refs = refs[:PAIR]
    (wr_ref, wi_ref, br_ref, bi_ref, lam_ref, nrm_ref, w_ref, rwc_ref, rb_ref,
     xo_ref, hp_ref, rt_ref, h_scr, hl_scr, p_scr) = refs[PAIR:]
    rows = PAIR * TILE

    @pl.when(pl.program_id(1) == 0)
    def _():
        h_scr[...] = jnp.zeros_like(h_scr)

    ybs = []
    for k in range(PAIR):
        a_r, b_r = _lru_gates(hc_ref[k, :, LRU_WIDTH:].astype(F32), 1, wr_ref, wi_ref, br_ref, bi_ref, lam_ref)
        hrev, h_last = _lru_scan(a_r, b_r, h_scr[k], True, hl_scr, p_scr)
        h_scr[k] = h_last
        gx = g_ref[k].astype(F32)
        gate = (0.5 * gx) * (1.0 + jnp.tanh(gx * (GELU_C + (GELU_C * 0.044715) * (gx * gx))))
        ybs.append((gate * (hc_ref[k, :, 0:LRU_WIDTH].astype(F32) + hrev)).astype(BF16))
    ya = ya_ref[...].reshape(rows, NA_WIDTH)
    yb = jnp.concatenate(ybs, axis=0)
    y = jnp.dot(ya, w_ref[0:NA_WIDTH, :], preferred_element_type=F32)
    y = y + jnp.dot(yb, w_ref[NA_WIDTH:, :], preferred_element_type=F32)
    hs = []
    for k in range(PAIR):
        m = mod_refs[k][0, 0]
        x = x_ref[k] + m[2:3, :] * y[k * TILE:(k + 1) * TILE, :]
        xo_ref[k] = x
        ms = jnp.mean(x * x, axis=-1, keepdims=True)
        h = (x * lax.rsqrt(ms + EPS)) * nrm_ref[...]
        h = h * (1.0 + m[4:5, :]) + m[3:4, :]
        hp_ref[k] = _pack_rows(h)
        hs.append(h)
    h = jnp.concatenate(hs, axis=0)
    h_hi = h.astype(BF16)
    h_lo = (h - h_hi.astype(F32)).astype(BF16)
    nt = (((1,), (1,)), ((), ()))
    rwc = rwc_ref[...]
    both = lax.dot_general(rwc, h_hi, nt, preferred_element_type=F32)
    lg = (both[0:N_EXPERTS, :] + both[N_EXPERTS:, :]
          + lax.dot_general(rwc[0:N_EXPERTS, :], h_lo, nt, preferred_element_type=F32))
    aff_all = jax.nn.sigmoid(lg)
    sel_all = aff_all + rb_ref[...]
    aff = [aff_all[e:e + 1, :] for e in range(N_EXPERTS)]
    sel = [sel_all[e:e + 1, :] for e in range(N_EXPERTS)]
    gates, chosen = _route(sel, aff)
    rt = jnp.concatenate(gates + chosen, axis=0)
    for k in range(PAIR):
        rt_ref[k, 0] = rt[:, k * TILE:(k + 1) * TILE]


def _out_proj(l, boff, xs, ya, hc, proj, lru, mod, nrm, w_out, rwc, rb):
    rev = lambda i: jnp.where(i == 0, 0, N_TILES - i)
    tok = lambda width, col=0: pl.BlockSpec((PAIR, TILE, width), lambda bp, i: (bp, rev(i), col))
    mod_specs = [pl.BlockSpec((1, 1, N_MOD, D_MODEL),
                              lambda bp, i, k=k: (l, _mod_row(boff + PAIR * bp + k, i), 0, 0)) for k in range(PAIR)]
    n_cb = LRU_WIDTH // LRU_CH
    lru_specs = [_full_spec((2, n_cb, LRU_CH, LRU_CH)), _full_spec((2, n_cb, LRU_CH, LRU_CH)),
                 _full_spec((2, LRU_WIDTH)), _full_spec((2, LRU_WIDTH)), _full_spec((2, LRU_WIDTH))]
    return pl.pallas_call(
        _out_kernel,
        grid=(SB // PAIR, N_TILES),
        in_specs=[tok(D_MODEL), tok(NA_WIDTH), tok(2 * LRU_WIDTH), tok(LRU_WIDTH, COL_G)]
        + mod_specs + lru_specs
        + [_full_spec((1, D_MODEL)), _full_spec((D_MODEL, D_MODEL)), _full_spec((2 * N_EXPERTS, D_MODEL)),
           _full_spec((N_EXPERTS, 1))],
        out_specs=[tok(D_MODEL), tok(PACK_W),
                   pl.BlockSpec((PAIR, 1, ROUTE_ROWS, TILE), lambda bp, i: (bp, rev(i), 0, 0))],
        out_shape=[jax.ShapeDtypeStruct((SB, L_TOT, D_MODEL), F32),
                   jax.ShapeDtypeStruct((SB, L_TOT, PACK_W), I32),
                   jax.ShapeDtypeStruct((SB, N_TILES, ROUTE_ROWS, TILE), F32)],
        scratch_shapes=[pltpu.VMEM((PAIR, 1, LRU_WIDTH), F32), pltpu.VMEM((TILE, LRU_WIDTH), F32),
                        pltpu.VMEM((TILE, LRU_WIDTH), F32)],
        compiler_params=_cparams(("parallel", "arbitrary")),
        name="out_proj_router",
    )(xs, ya, hc, proj, *([mod] * PAIR), *lru[2:], nrm, w_out, rwc, rb)


def _pos_kernel(rt_ref, tri_ref, start_ref, pos_ref, wt_ref, run_scr):
    @pl.when(pl.program_id(0) == 0)
    def _():
        run_scr[...] = jnp.zeros_like(run_scr)

    base = start_ref[...] + run_scr[...]
    for k in range(POS_TILES):
        gates = rt_ref[k, 0:N_EXPERTS, :]
        chosen = rt_ref[k, N_EXPERTS:, :]
        rank = jnp.dot(chosen.astype(BF16), tri_ref[...], preferred_element_type=F32)
        posf = rank + base
        seen = jnp.zeros((1, TILE), F32)
        p0 = jnp.zeros((1, TILE), F32)
        p1 = jnp.zeros((1, TILE), F32)
        w0 = jnp.zeros((1, TILE), F32)
        w1 = jnp.zeros((1, TILE), F32)
        for e in range(N_EXPERTS):
            ch = chosen[e:e + 1, :]
            first = ch * (1.0 - seen)
            second = ch * seen
            p0 = p0 + first * posf[e:e + 1, :]
            p1 = p1 + second * posf[e:e + 1, :]
            w0 = w0 + first * gates[e:e + 1, :]
            w1 = w1 + second * gates[e:e + 1, :]
            seen = jnp.minimum(seen + ch, 1.0)
        pos_ref[k] = jnp.concatenate([p0, p1], axis=0).astype(I32)
        wpad = jnp.concatenate([w0, w1, jnp.zeros((LANES - TOP_K, TILE), F32)], axis=0)
        wt_ref[k * TILE:(k + 1) * TILE, :] = jnp.transpose(wpad)
        base = base + jnp.sum(chosen, axis=1, keepdims=True)
    run_scr[...] = base - start_ref[...]


def _positions(route, tri, start):
    return pl.pallas_call(
        _pos_kernel,
        grid=(N_TOK_TILES // POS_TILES,),
        in_specs=[pl.BlockSpec((POS_TILES, ROUTE_ROWS, TILE), lambda i: (i, 0, 0)),
                  pl.BlockSpec((TILE, TILE), lambda i: (0, 0)),
                  pl.BlockSpec((N_EXPERTS, 1), lambda i: (0, 0))],
        out_specs=[pl.BlockSpec((POS_TILES, TOP_K, TILE), lambda i: (i, 0, 0)),
                   pl.BlockSpec((POS_TILES * TILE, LANES), lambda i: (i, 0))],
        out_shape=[jax.ShapeDtypeStruct((N_TOK_TILES, TOP_K, TILE), I32),
                   jax.ShapeDtypeStruct((N_TOK, LANES), F32)],
        scratch_shapes=[pltpu.VMEM((N_EXPERTS, 1), F32)],
        compiler_params=_cparams(("arbitrary",)),
        name="moe_positions",
    )(route, tri, start)


def _work_items(counts):
    smem = pl.BlockSpec(memory_space=pltpu.SMEM)
    item = jax.ShapeDtypeStruct((N_ITEMS,), I32)
    return pl.pallas_call(
        _items_kernel,
        in_specs=[smem],
        out_specs=[smem] * 4,
        out_shape=[item] * 4,
        name="moe_work_items",
    )(counts)


def _items_kernel(cnt_ref, tile_ref, exp_ref, lo_ref, hi_ref):
    n = jnp.int32(0)
    start = jnp.int32(0)
    last_e = jnp.int32(0)
    for e in range(N_EXPERTS):
        cnt = cnt_ref[e]
        end = start + cnt
        first = start // EXP_TILE
        n_tiles = jnp.where(cnt > 0, (end - 1) // EXP_TILE - first + 1, 0)

        def put(j, carry, e=e, n=n, start=start, end=end, first=first):
            tile = first + j
            tile_ref[n + j] = tile
            exp_ref[n + j] = jnp.int32(e)
            lo_ref[n + j] = jnp.maximum(start - tile * EXP_TILE, 0)
            hi_ref[n + j] = jnp.minimum(end - tile * EXP_TILE, EXP_TILE)
            return carry

        lax.fori_loop(0, n_tiles, put, 0)
        n = n + n_tiles
        start = end
        last_e = jnp.where(cnt > 0, e, last_e)

    def pad(j, carry):
        tile_ref[j] = jnp.int32(N_EXP_TILES - 1)
        exp_ref[j] = last_e
        lo_ref[j] = jnp.int32(0)
        hi_ref[j] = jnp.int32(0)
        return carry

    lax.fori_loop(n, N_ITEMS, pad, 0)


def _sc_worker_base():
    return (lax.axis_index("s") * SC_CORES + lax.axis_index("c")) * SC_ROWS


def _sc_dispatch_body(h_hbm, p0_hbm, p1_hbm, out_hbm, i0_v, i1_v, rows_v, sem):
    base = _sc_worker_base()

    @pl.loop(0, SC_N_CHUNKS)
    def _(j):
        off = base + j * SC_CHUNK
        pltpu.sync_copy(h_hbm.at[pl.ds(off, SC_CHUNK)], rows_v)
        pltpu.sync_copy(p0_hbm.at[pl.ds(off, SC_CHUNK)], i0_v)
        pltpu.sync_copy(p1_hbm.at[pl.ds(off, SC_CHUNK)], i1_v)
        c0 = pltpu.async_copy(rows_v, out_hbm.at[i0_v], sem)
        c1 = pltpu.async_copy(rows_v, out_hbm.at[i1_v], sem)
        c0.wait()
        c1.wait()


def _sc_combine_body(y_hbm, p0_hbm, p1_hbm, g0_hbm, g1_hbm, i_v, rows_v, sem):
    base = _sc_worker_base()

    @pl.loop(0, SC_N_CHUNKS)
    def _(j):
        off = base + j * SC_CHUNK
        for p_hbm, g_hbm in ((p0_hbm, g0_hbm), (p1_hbm, g1_hbm)):
            pltpu.sync_copy(p_hbm.at[pl.ds(off, SC_CHUNK)], i_v)
            pltpu.async_copy(y_hbm.at[i_v], rows_v, sem).wait()
            pltpu.sync_copy(rows_v, g_hbm.at[pl.ds(off, SC_CHUNK)])


def _sc_mesh():
    return plsc.VectorSubcoreMesh(core_axis_name="c", subcore_axis_name="s")


def _dispatch(hp, pos0, pos1):
    return pl.kernel(
        _sc_dispatch_body,
        out_type=jax.ShapeDtypeStruct((N_ASSIGN, PACK_W), I32),
        mesh=_sc_mesh(),
        scratch_types=[pltpu.VMEM((SC_CHUNK,), I32), pltpu.VMEM((SC_CHUNK,), I32),
                       pltpu.VMEM((SC_CHUNK, PACK_W), I32), pltpu.SemaphoreType.DMA],
        name="moe_dispatch",
    )(hp, pos0, pos1)


def _combine(ys, pos0, pos1):
    row = jax.ShapeDtypeStruct((N_TOK, PACK_W), I32)
    return pl.kernel(
        _sc_combine_body,
        out_type=[row, row],
        mesh=_sc_mesh(),
        scratch_types=[pltpu.VMEM((SC_CHUNK,), I32), pltpu.VMEM((SC_CHUNK, PACK_W), I32),
                       pltpu.SemaphoreType.DMA],
        name="moe_combine",
    )(ys, pos0, pos1)


def _expert_kernel(tile_ref, exp_ref, lo_ref, hi_ref, x_ref, w1_ref, w3_ref, w2_ref, o_ref, w13_scr, w2_scr):
    i = pl.program_id(0)
    prev = jnp.maximum(i - 1, 0)

    @pl.when(jnp.logical_or(i == 0, exp_ref[i] != exp_ref[prev]))
    def _():
        w13_scr[:, 0:EXPERT_FF] = w1_ref[0, 0].astype(BF16)
        w13_scr[:, EXPERT_FF:] = w3_ref[0, 0].astype(BF16)
        w2_scr[...] = w2_ref[0, 0].astype(BF16)

    lo = lo_ref[i]
    hi = hi_ref[i]
    revisit = jnp.logical_and(i > 0, tile_ref[i] == tile_ref[prev])
    first_visit = jnp.logical_not(revisit)

    def run(r0, n):
        rows = slice(r0, r0 + n)
        lo_f, hi_f = _unpack_rows(x_ref[rows, :])
        h = jnp.concatenate([lo_f.astype(BF16), hi_f.astype(BF16)], axis=1)
        ab = jnp.dot(h, w13_scr[...], preferred_element_type=F32)
        a = ab[:, 0:EXPERT_FF]
        b = ab[:, EXPERT_FF:]
        t = ((a * jax.nn.sigmoid(a)) * b).astype(BF16)
        y = _pack_rows(jnp.dot(t, w2_scr[...], preferred_element_type=F32))
        row = r0 + lax.broadcasted_iota(jnp.int32, (n, PACK_W), 0)
        mine = jnp.logical_and(row >= lo, row < hi)

        @pl.when(first_visit)
        def _():
            o_ref[rows, :] = jnp.where(mine, y, 0)

        @pl.when(revisit)
        def _():
            o_ref[rows, :] = jnp.where(mine, y, o_ref[rows, :])

    def clear(r0, n):
        @pl.when(first_visit)
        def _():
            o_ref[r0:r0 + n, :] = jnp.zeros((n, PACK_W), I32)

    half = EXP_TILE // 2
    low = jnp.logical_and(lo < half, hi > lo)
    high = hi > half

    @pl.when(jnp.logical_and(low, high))
    def _():
        run(0, EXP_TILE)

    @pl.when(jnp.logical_and(low, jnp.logical_not(high)))
    def _():
        run(0, half)
        clear(half, half)

    @pl.when(jnp.logical_and(jnp.logical_not(low), high))
    def _():
        run(half, half)
        clear(0, half)


def _experts(l, xs_sorted, items, w1, w3, w2):
    tile, e, lo, hi = items
    grid_spec = pltpu.PrefetchScalarGridSpec(
        num_scalar_prefetch=4,
        grid=(N_ITEMS,),
        in_specs=[pl.BlockSpec((EXP_TILE, PACK_W), lambda i, t, e, lo, hi: (t[i], 0)),
                  pl.BlockSpec((1, 1, D_MODEL, EXPERT_FF), lambda i, t, e, lo, hi: (l, e[i], 0, 0)),
                  pl.BlockSpec((1, 1, D_MODEL, EXPERT_FF), lambda i, t, e, lo, hi: (l, e[i], 0, 0)),
                  pl.BlockSpec((1, 1, EXPERT_FF, D_MODEL), lambda i, t, e, lo, hi: (l, e[i], 0, 0))],
        out_specs=pl.BlockSpec((EXP_TILE, PACK_W), lambda i, t, e, lo, hi: (t[i], 0)),
        scratch_shapes=[pltpu.VMEM((D_MODEL, 2 * EXPERT_FF), BF16), pltpu.VMEM((EXPERT_FF, D_MODEL), BF16)],
    )
    return pl.pallas_call(
        _expert_kernel,
        grid_spec=grid_spec,
        out_shape=jax.ShapeDtypeStruct((N_ASSIGN, PACK_W), I32),
        compiler_params=_cparams(("arbitrary",)),
        name="moe_experts",
    )(tile, e, lo, hi, xs_sorted, w1, w3, w2)


def _moe(l, hp, route, tri, w1, w3, w2):
    counts = jnp.sum(route[:, N_EXPERTS:, :], axis=(0, 2))
    start = (jnp.cumsum(counts) - counts).reshape(N_EXPERTS, 1)
    pos, wt = _positions(route, tri, start)
    pos0 = pos[:, 0, :].reshape(N_TOK)
    pos1 = pos[:, 1, :].reshape(N_TOK)
    items = _work_items(counts.astype(I32))
    xs_sorted = _dispatch(hp.reshape(N_TOK, PACK_W), pos0, pos1)
    ys = _experts(l, xs_sorted, items, w1, w3, w2)
    g0, g1 = _combine(ys, pos0, pos1)
    shape = (SB, L_TOT, PACK_W)
    return g0.reshape(shape), g1.reshape(shape), wt.reshape(SB, L_TOT, LANES)


def _final_kernel(x_ref, g0_ref, g1_ref, wt_ref, mod_ref, *refs):
    o_ref, scr = refs[-2:]
    for b in range(SB):
        x = _moe_residual(x_ref[b], g0_ref[b], g1_ref[b], wt_ref[b], mod_ref[0, b][5:6, :])
        o_ref[b] = _from_scan_major(x, scr)


def _final(streams, mod):
    lat = lambda width: pl.BlockSpec((SB, TILE, width), lambda j: (0, j + 1, 0))
    out = None
    for k, st in enumerate(streams):
        in_specs = [lat(D_MODEL), lat(PACK_W), lat(PACK_W), lat(LANES),
                    pl.BlockSpec((1, SB, N_MOD, D_MODEL), lambda j, k=k: (DEPTH - 1, k, 0, 0))]
        args = [st["xs"]] + list(st["moe"]) + [mod]
        aliases = {}
        if out is not None:
            in_specs.append(pl.BlockSpec(memory_space=pl.ANY))
            args.append(out)
            aliases = {len(args) - 1: 0}
        out = pl.pallas_call(
            _final_kernel,
            grid=(N_LAT_TILES,),
            in_specs=in_specs,
            out_specs=pl.BlockSpec((SB, TILE, D_MODEL), lambda j, k=k: (k, j, 0)),
            out_shape=jax.ShapeDtypeStruct((BATCH, SEQ, D_MODEL), F32),
            scratch_shapes=[pltpu.VMEM((D_MODEL // LANES, TILE, LANES), F32)],
            input_output_aliases=aliases,
            compiler_params=_cparams(("arbitrary",)),
            name="final_residual",
        )(*args)
    return out


def _bias_kernel(tab_ref, perm_ref, o_ref, scr):
    rows_q = TILE // GRID_W
    lane = lax.broadcasted_iota(jnp.int32, (GROUPS, TILE), 1)
    ka = (lane >> 1) & (rows_q - 1)
    neg = jnp.full((GROUPS, TILE), NEG_INF, F32)
    qc = lax.broadcasted_iota(jnp.int32, (GRID_W, GRID_W), 0)
    kc = lax.broadcasted_iota(jnp.int32, (GRID_W, GRID_W), 1)
    col0 = jnp.clip(qc - NA_WIN_COLS // 2, 0, GRID_W - NA_WIN_COLS)
    in_win = jnp.logical_and(kc >= col0, kc < col0 + NA_WIN_COLS)
    toeplitz = []
    for r in range(2 * NA_WIN_ROWS - 1):
        row = jnp.broadcast_to(tab_ref[0, r:r + 1, :], (GRID_W, LANES))
        shifted = pltpu.roll(row, LANES - (GRID_W - 1), 1, stride=1, stride_axis=0)
        toeplitz.append(jnp.where(in_win, shifted[:, 0:GRID_W], NEG_INF))

    def interleaved(r0):
        acc = None
        for k in range(rows_q):
            val = toeplitz[r0 + k]
            t1 = val.astype(BF16)
            r1 = val - t1.astype(F32)
            t2 = r1.astype(BF16)
            t3 = (r1 - t2.astype(F32)).astype(BF16)
            for term in (t1, t2, t3):
                moved = jnp.dot(term, perm_ref[k], preferred_element_type=F32)
                acc = moved if acc is None else acc + moved
        return acc

    t4 = [interleaved(r0) for r0 in range(WIN_TILES * rows_q)]
    for a in range(rows_q):
        for qs in range(SUB // rows_q):
            s = (SUB // rows_q) * a + qs
            for kt in range(WIN_TILES):
                src = t4[rows_q * kt - a + 3][qs * GROUPS:(qs + 1) * GROUPS, :]
                c = rows_q * kt + ka
                variants = (
                    src if kt >= 1 else neg,
                    jnp.where(jnp.logical_and(c >= a, c <= a + NA_WIN_ROWS - 1), src, neg),
                    src if kt <= 1 else neg,
                )
                for v, val in enumerate(variants):
                    for half in range(TILE // LANES):
                        scr[v, 2 * kt + half, pl.ds(s, GROUPS, stride=SUB), :] = val[:, half * LANES:(half + 1) * LANES]
    for v in range(BIAS_VARIANTS):
        o_ref[v, 0] = jnp.concatenate([scr[v, j] for j in range(WIN_TILES * TILE // LANES)], axis=1)


def _bias_tiles(table):
    rows_q = TILE // GRID_W
    n_r = 2 * NA_WIN_ROWS - 1
    tab = jnp.pad(table.astype(F32) * LOG2E, ((0, 0), (0, 0), (BIAS_PAD, LANES - BIAS_PAD - (2 * NA_WIN_COLS - 1))))
    n_h = table.shape[0]
    half = GRID_W // GROUPS
    perm = np.zeros((rows_q, GRID_W, TILE), np.float32)
    for k in range(rows_q):
        for ks in range(half):
            for kg in range(GROUPS):
                perm[k, ks * GROUPS + kg, kg * SUB + k * half + ks] = 1.0
    return pl.pallas_call(
        _bias_kernel,
        grid=(n_h,),
        in_specs=[pl.BlockSpec((1, n_r, LANES), lambda h: (h, 0, 0)),
                  pl.BlockSpec((rows_q, GRID_W, TILE), lambda h: (0, 0, 0))],
        out_specs=pl.BlockSpec((BIAS_VARIANTS, 1, TILE, WIN_TILES * TILE), lambda h: (0, h, 0, 0)),
        out_shape=jax.ShapeDtypeStruct((BIAS_VARIANTS, n_h, TILE, WIN_TILES * TILE), F32),
        scratch_shapes=[pltpu.VMEM((BIAS_VARIANTS, WIN_TILES * TILE // LANES, TILE, LANES), F32)],
        compiler_params=_cparams(("parallel",)),
        name="bias_tiles",
    )(tab, jnp.asarray(perm, BF16))


def _block_diag(w, n_chunks):
    per = LRU_BLOCKS // n_chunks
    w = w.reshape(2, n_chunks, per, LRU_BLOCK, LRU_BLOCK)
    eye = jnp.eye(per, dtype=w.dtype)
    out = jnp.einsum('dcpij,pq->dcpiqj', w, eye)
    return out.reshape(2, n_chunks, per * LRU_BLOCK, per * LRU_BLOCK)


def kernel(x, c, ctx, c_ctx, w_mod, b_mod, norm_mix, norm_ffn, w_in, w_out, q_gain, k_gain, na_bias,
           conv_w, conv_b, lru_w_r, lru_b_r, lru_w_i, lru_b_i, lru_lambda, router_w, router_b,
           exp_w1, exp_w3, exp_w2):
    cs = jnp.concatenate([c, c_ctx[None, :], jnp.zeros((MOD_ROWS - BATCH - 1, D_MODEL), F32)], axis=0)
    mod = _modulation(cs, w_mod, b_mod).reshape(DEPTH, MOD_ROWS, N_MOD, D_MODEL)

    head_of = np.arange(NA_WIDTH) // HEAD_DIM
    bd = jnp.asarray((head_of[:, None] == head_of[None, :]).astype(np.float32) / HEAD_DIM, BF16)
    tri = jnp.asarray(np.triu(np.ones((TILE, TILE), np.float32), 1), BF16)
    rwt = router_w.T
    rwh = rwt.astype(BF16)
    rwc = jnp.concatenate([rwh, (rwt - rwh.astype(F32)).astype(BF16)], axis=0)
    rb = router_b.reshape(N_EXPERTS, 1)
    n_cb = LRU_WIDTH // LRU_CH

    bias_tiles = _bias_tiles(na_bias.reshape(DEPTH * NA_HEADS, 2 * NA_WIN_ROWS - 1, 2 * NA_WIN_COLS - 1))
    streams = [{"xs": (x, ctx), "moe": None, "boff": sidx * SB} for sidx in range(STREAMS)]
    for l in range(DEPTH):
        qg = jnp.tile(q_gain[l] * (ATTN_SCALE * LOG2E), NA_HEADS)[None, :]
        kg = jnp.tile(k_gain[l], NA_HEADS)[None, :]
        w_in_l = w_in[l].astype(BF16)
        w_out_l = w_out[l].astype(BF16)
        wr = (0.5 * _block_diag(lru_w_r[l], n_cb)).astype(BF16)
        wi = (0.5 * _block_diag(lru_w_i[l], n_cb)).astype(BF16)
        lru = (conv_w[l], conv_b[l][None, :], wr, wi, 0.5 * lru_b_r[l], 0.5 * lru_b_i[l], lru_lambda[l])
        for st in streams:
            boff = st["boff"]
            xs, proj = _in_proj(l, boff, st["xs"], st["moe"], mod, norm_mix[l][None, :], w_in_l, bd, qg, kg)
            ya, hc = _attention(l, proj, bias_tiles, lru)
            xs, hp, route = _out_proj(l, boff, xs, ya, hc, proj, lru, mod, norm_ffn[l][None, :], w_out_l, rwc, rb)
            st["xs"] = xs
            st["moe"] = _moe(l, hp, route.reshape(N_TOK_TILES, ROUTE_ROWS, TILE), tri, exp_w1, exp_w3, exp_w2)
    return _final(streams, mod)
```

```python
import functools

import jax
import jax.numpy as jnp
import numpy as np
from jax import lax
from jax.experimental import pallas as pl
from jax.experimental.pallas import tpu as pltpu
from jax.experimental.pallas import tpu_sc as plsc

F32 = jnp.float32
BF16 = jnp.bfloat16
I32 = jnp.int32

D_MODEL = 1024
BATCH = 4
SEQ = 8192
DEPTH = 4
GRID_W = 64
CTX_LEN = 256
HEAD_DIM = 64
NA_WIDTH = 512
NA_HEADS = 8
NA_WIN_ROWS = 8
NA_WIN_COLS = 16
LRU_WIDTH = 512
LRU_BLOCKS = 8
LRU_BLOCK = 64
CONV_W = 4
LRU_C = 8.0
IN_COLS = 3 * NA_WIDTH + 2 * LRU_WIDTH
COL_KV, COL_Q, COL_U, COL_G = 0, 2, 3, 4
N_EXPERTS = 16
N_GROUPS = 4
EXPERTS_PER_GROUP = 4
TOP_K = 2
EXPERT_FF = 512
N_MOD = 6
ATTN_SCALE = HEAD_DIM ** -0.5
LOG2E = 1.4426950408889634
EPS = 1e-6
NEG_INF = -1e30
TINY = 1e-30
GELU_C = 0.7978845608028654

TILE = 256
WIN_TILES = 3
BIAS_VARIANTS = 3
HALO = 16
SUB = 8
LANES = 128
GROUPS = TILE // SUB
L_TOT = CTX_LEN + SEQ
N_TILES = L_TOT // TILE
N_LAT_TILES = SEQ // TILE
STREAMS = 2
SB = BATCH // STREAMS
N_TOK = SB * L_TOT
N_TOK_TILES = N_TOK // TILE
PAIR = 2
LRU_CH = 256
MOD_ROWS = 8
VMEM_LIMIT = 56 * 1024 * 1024

PACK_W = D_MODEL // 2
HI_MASK = -65536
N_ASSIGN = TOP_K * N_TOK
EXP_TILE = 512
N_EXP_TILES = N_ASSIGN // EXP_TILE
N_ITEMS = N_EXP_TILES + N_EXPERTS - 1
BIAS_PAD = GRID_W - NA_WIN_COLS
ROUTE_ROWS = 2 * N_EXPERTS
POS_TILES = 22

SC_CORES = 2
SC_SUBCORES = 16
SC_WORKERS = SC_CORES * SC_SUBCORES
SC_ROWS = N_TOK // SC_WORKERS
SC_CHUNK = 88
SC_N_CHUNKS = SC_ROWS // SC_CHUNK


def _cparams(sem):
    return pltpu.CompilerParams(dimension_semantics=sem, vmem_limit_bytes=VMEM_LIMIT)


def _pack_rows(v):
    lo = pltpu.bitcast(v[:, :PACK_W].astype(BF16).astype(F32), I32)
    hi = pltpu.bitcast(v[:, PACK_W:].astype(BF16).astype(F32), I32)
    return ((lo >> 16) & 0xFFFF) | (hi & HI_MASK)


def _unpack_rows(p):
    return pltpu.bitcast(p << 16, F32), pltpu.bitcast(p & HI_MASK, F32)


def _mod_kernel(c_ref, w_ref, b_ref, o_ref):
    c = c_ref[...]
    s = c * jax.nn.sigmoid(c)
    o_ref[0] = jnp.dot(s.astype(BF16), w_ref[0].astype(BF16), preferred_element_type=F32) + b_ref[0]


def _modulation(cs, w_mod, b_mod):
    return pl.pallas_call(
        _mod_kernel,
        grid=(DEPTH, N_MOD),
        in_specs=[
            pl.BlockSpec((MOD_ROWS, D_MODEL), lambda l, n: (0, 0)),
            pl.BlockSpec((1, D_MODEL, D_MODEL), lambda l, n: (l, 0, n)),
            pl.BlockSpec((1, 1, D_MODEL), lambda l, n: (l, 0, n)),
        ],
        out_specs=pl.BlockSpec((1, MOD_ROWS, D_MODEL), lambda l, n: (l, 0, n)),
        out_shape=jax.ShapeDtypeStruct((DEPTH, MOD_ROWS, N_MOD * D_MODEL), F32),
        compiler_params=_cparams(("arbitrary", "arbitrary")),
        name="modulation",
    )(cs, w_mod, b_mod.reshape(DEPTH, 1, N_MOD * D_MODEL))


def _mod_row(b, i):
    return jnp.where(i == 0, BATCH, b)


def _moe_residual(x, g0, g1, wt, gate_row):
    lo0, hi0 = _unpack_rows(g0)
    lo1, hi1 = _unpack_rows(g1)
    w0 = wt[:, 0:1]
    w1 = wt[:, 1:2]
    f = jnp.concatenate([w0 * lo0 + w1 * lo1, w0 * hi0 + w1 * hi1], axis=1)
    return x + gate_row * f


def _pair_tok_spec(width):
    return pl.BlockSpec((PAIR, TILE, width), lambda bp, i: (bp, i, 0))


def _pair_mod_specs(layer, boff):
    return [pl.BlockSpec((1, 1, N_MOD, D_MODEL),
                         lambda bp, i, k=k: (layer, _mod_row(boff + PAIR * bp + k, i), 0, 0))
            for k in range(PAIR)]


def _full_spec(shape):
    return pl.BlockSpec(shape, lambda bp, i: tuple(0 for _ in shape))


def _to_scan_major(src_ref, scr):
    n_slab = D_MODEL // LANES
    for s in range(SUB):
        for j in range(n_slab):
            scr[j, pl.ds(s, GROUPS, stride=SUB), :] = src_ref[s * GROUPS:(s + 1) * GROUPS, j * LANES:(j + 1) * LANES]
    return jnp.concatenate([scr[j] for j in range(n_slab)], axis=1)


def _from_scan_major(val, scr):
    n_slab = D_MODEL // LANES
    for j in range(n_slab):
        scr[j] = val[:, j * LANES:(j + 1) * LANES]
    blocks = [jnp.concatenate([scr[j, pl.ds(s, GROUPS, stride=SUB), :] for j in range(n_slab)], axis=1)
              for s in range(SUB)]
    return jnp.concatenate(blocks, axis=0)


def _in_kernel(has_prev, *refs):
    if has_prev:
        x_ref, g0_ref, g1_ref, wt_ref = refs[:4]
        mprev_refs = refs[4:4 + PAIR]
        refs = (x_ref,) + refs[4 + PAIR:]
    else:
        x_ref, ctx_ref = refs[:2]
        refs = (x_ref,) + refs[2:]
    mod_refs = refs[1:1 + PAIR]
    nrm_ref, w_ref, bd_ref, qg_ref, kg_ref = refs[1 + PAIR:6 + PAIR]
    outs = refs[6 + PAIR:]
    xo_ref, proj_ref = outs[:2]
    x_ref = refs[0]
    hs = []
    for k in range(PAIR):
        if has_prev:
            x = _moe_residual(x_ref[k], g0_ref[k], g1_ref[k], wt_ref[k], mprev_refs[k][0, 0][5:6, :])
            xo_ref[k] = x
        else:
            perm_scr = outs[2]

            @pl.when(pl.program_id(1) == 0)
            def _():
                xo_ref[k] = _to_scan_major(ctx_ref.at[k], perm_scr)

            @pl.when(pl.program_id(1) > 0)
            def _():
                xo_ref[k] = _to_scan_major(x_ref.at[k], perm_scr)

            x = xo_ref[k]
        m = mod_refs[k][0, 0]
        ms = jnp.mean(x * x, axis=-1, keepdims=True)
        h = (x * lax.rsqrt(ms + EPS)) * nrm_ref[...]
        hs.append((h * (1.0 + m[1:2, :]) + m[0:1, :]).astype(BF16))
    acc = jnp.dot(jnp.concatenate(hs, axis=0), w_ref[...], preferred_element_type=F32)
    bd = bd_ref[...]

    def head_norm(t, gain):
        ss = jnp.dot((t * t).astype(BF16), bd, preferred_element_type=F32)
        return (t * lax.rsqrt(ss + EPS)) * gain

    def put(col_block, val):
        for k in range(PAIR):
            proj_ref[k, :, col_block * NA_WIDTH:(col_block + 1) * NA_WIDTH] = val[k * TILE:(k + 1) * TILE, :].astype(BF16)

    put(COL_Q, head_norm(acc[:, 0:NA_WIDTH], qg_ref[...]))
    put(COL_KV, head_norm(acc[:, NA_WIDTH:2 * NA_WIDTH], kg_ref[...]))
    put(COL_KV + 1, acc[:, 2 * NA_WIDTH:3 * NA_WIDTH])
    put(COL_U, acc[:, 3 * NA_WIDTH:3 * NA_WIDTH + LRU_WIDTH])
    put(COL_G, acc[:, 3 * NA_WIDTH + LRU_WIDTH:])


def _in_proj(l, boff, xs, moe_prev, mod, nrm, w_in, bd, qg, kg):
    has_prev = moe_prev is not None
    if has_prev:
        in_specs = [_pair_tok_spec(D_MODEL)]
        args = [xs]
        in_specs += [_pair_tok_spec(PACK_W), _pair_tok_spec(PACK_W), _pair_tok_spec(LANES)]
        in_specs += _pair_mod_specs(l - 1, boff)
        args += list(moe_prev) + [mod] * PAIR
    else:
        pair0 = boff // PAIR
        in_specs = [pl.BlockSpec((PAIR, TILE, D_MODEL), lambda bp, i: (pair0 + bp, jnp.maximum(i - 1, 0), 0)),
                    pl.BlockSpec((PAIR, TILE, D_MODEL), lambda bp, i: (pair0 + bp, 0, 0))]
        args = list(xs)
    in_specs += _pair_mod_specs(l, boff) + [_full_spec((1, D_MODEL)), _full_spec((D_MODEL, IN_COLS)),
                                            _full_spec((NA_WIDTH, NA_WIDTH)), _full_spec((1, NA_WIDTH)),
                                            _full_spec((1, NA_WIDTH))]
    args += [mod] * PAIR + [nrm, w_in, bd, qg, kg]
    out_shape = [jax.ShapeDtypeStruct((SB, L_TOT, D_MODEL), F32), jax.ShapeDtypeStruct((SB, L_TOT, IN_COLS), BF16)]
    out_specs = [_pair_tok_spec(D_MODEL), _pair_tok_spec(IN_COLS)]
    scratch = [] if has_prev else [pltpu.VMEM((D_MODEL // LANES, TILE, LANES), F32)]
    outs = pl.pallas_call(
        functools.partial(_in_kernel, has_prev),
        grid=(SB // PAIR, N_TILES),
        in_specs=in_specs,
        out_specs=out_specs,
        out_shape=out_shape,
        scratch_shapes=scratch,
        compiler_params=_cparams(("parallel", "arbitrary")),
        name="in_proj",
    )(*args)
    return outs[0], outs[1]


def _attn_kernel(q_ref, kvp_ref, kvc_ref, kvn_ref, kvx_ref, bias_ref,
                 u_ref, up_ref, un_ref, cw_ref, cb_ref, wr_ref, wi_ref, br_ref, bi_ref, lam_ref,
                 o_ref, hc_ref, h_scr, hl_scr, p_scr):
    j = pl.program_id(1)

    @pl.when(j == 0)
    def _():
        h_scr[...] = jnp.zeros_like(h_scr)

    has_prev = jnp.where(j >= 2, 1.0, 0.0).astype(F32)
    has_next = jnp.where(jnp.logical_and(j >= 1, j <= N_TILES - 2), 1.0, 0.0).astype(F32)

    def forward_lru():
        for b in range(PAIR):
            cv = _lru_conv(u_ref[b].astype(F32), up_ref[b].astype(F32), un_ref[b].astype(F32), has_prev, has_next,
                           cw_ref[...], cb_ref[...])
            hc_ref[b, :, LRU_WIDTH:] = cv.astype(BF16)
            a_f, b_f = _lru_gates(cv, 0, wr_ref, wi_ref, br_ref, bi_ref, lam_ref)
            hfull, h_last = _lru_scan(a_f, b_f, h_scr[b], False, hl_scr, p_scr)
            hc_ref[b, :, 0:LRU_WIDTH] = hfull.astype(BF16)
            h_scr[b] = h_last

    pair_w = 2 * HEAD_DIM
    lane = lax.broadcasted_iota(jnp.int32, (TILE, pair_w), 1)
    low = lane < HEAD_DIM
    nt = (((1,), (1,)), ((), ()))
    n_win = WIN_TILES * TILE

    def attend(kv_refs, windowed):
        low_kv = lax.broadcasted_iota(jnp.int32, (len(kv_refs) * TILE, pair_w), 1) < HEAD_DIM
        for b, hp in [(b, hp) for b in range(PAIR) for hp in range(NA_HEADS // 2)]:
            cols = slice(hp * pair_w, (hp + 1) * pair_w)
            vcols = slice(NA_WIDTH + hp * pair_w, NA_WIDTH + (hp + 1) * pair_w)
            q = q_ref[b, :, cols]
            kb = jnp.concatenate([r[b, :, cols] for r in kv_refs], axis=0)
            vb = jnp.concatenate([r[b, :, vcols] for r in kv_refs], axis=0)
            outs = []
            for hh in range(2):
                own = low if hh == 0 else jnp.logical_not(low)
                own_kv = low_kv if hh == 0 else jnp.logical_not(low_kv)
                qh = jnp.where(own, q, jnp.zeros_like(q))
                s = lax.dot_general(qh, kb, nt, preferred_element_type=F32)
                if windowed:
                    s_win = s[:, 0:n_win] + bias_ref[0, 2 * hp + hh]
                    s_ctx = s[:, n_win:]
                    m = jnp.maximum(jnp.max(s_win, axis=-1, keepdims=True), jnp.max(s_ctx, axis=-1, keepdims=True))
                    p = jnp.concatenate([jnp.exp2((s_win - m).astype(BF16)), jnp.exp2((s_ctx - m).astype(BF16))],
                                        axis=1)
                else:
                    p = jnp.exp2((s - jnp.max(s, axis=-1, keepdims=True)).astype(BF16))
                va = jnp.where(own_kv, vb, jnp.ones_like(vb))
                o = jnp.dot(p, va, preferred_element_type=F32)
                outs.append(o / pltpu.roll(o, HEAD_DIM, 1))
            o_ref[b, :, cols] = jnp.where(low, outs[0], outs[1]).astype(BF16)

    @pl.when(j == 0)
    def _():
        forward_lru()
        attend((kvx_ref,), False)

    @pl.when(j > 0)
    def _():
        forward_lru()
        attend((kvp_ref, kvc_ref, kvn_ref, kvx_ref), True)


def _attention(l, proj, bias_tiles, lru):
    last = N_LAT_TILES - 1
    blk = (PAIR, TILE, 2 * NA_WIDTH)
    prev_map = lambda b, j: (b, 1 + jnp.clip(j - 2, 0, last), COL_KV // 2)
    cur_map = lambda b, j: (b, jnp.maximum(j, 1), COL_KV // 2)
    next_map = lambda b, j: (b, 1 + jnp.clip(j, 0, last), COL_KV // 2)
    ctx_map = lambda b, j: (b, 0, COL_KV // 2)
    var_map = lambda b, j: (jnp.where(j <= 1, 0, jnp.where(j == N_TILES - 1, 2, 1)), l, 0, 0)
    kv_specs = [pl.BlockSpec(blk, prev_map), pl.BlockSpec(blk, cur_map), pl.BlockSpec(blk, next_map),
                pl.BlockSpec(blk, ctx_map)]
    tok = lambda col: pl.BlockSpec((PAIR, TILE, NA_WIDTH), lambda b, j: (b, j, col))
    halo = TILE // HALO
    n_halo = L_TOT // HALO
    prev16 = pl.BlockSpec((PAIR, HALO, LRU_WIDTH), lambda b, j: (b, jnp.maximum(j * halo - 1, 0), COL_U))
    next16 = pl.BlockSpec((PAIR, HALO, LRU_WIDTH), lambda b, j: (b, jnp.minimum((j + 1) * halo, n_halo - 1), COL_U))
    whole = lambda shape: pl.BlockSpec(shape, lambda b, j: tuple(0 for _ in shape))
    n_cb = LRU_WIDTH // LRU_CH
    lru_specs = [whole((CONV_W, LRU_WIDTH)), whole((1, LRU_WIDTH)), whole((2, n_cb, LRU_CH, LRU_CH)),
                 whole((2, n_cb, LRU_CH, LRU_CH)), whole((2, LRU_WIDTH)), whole((2, LRU_WIDTH)), whole((2, LRU_WIDTH))]
    return pl.pallas_call(
        _attn_kernel,
        grid=(SB // PAIR, N_TILES),
        in_specs=[tok(COL_Q)] + kv_specs + [pl.BlockSpec((1, NA_HEADS, TILE, WIN_TILES * TILE), var_map)]
        + [tok(COL_U), prev16, next16] + lru_specs,
        out_specs=[tok(0), pl.BlockSpec((PAIR, TILE, 2 * LRU_WIDTH), lambda b, j: (b, j, 0))],
        out_shape=[jax.ShapeDtypeStruct((SB, L_TOT, NA_WIDTH), BF16),
                   jax.ShapeDtypeStruct((SB, L_TOT, 2 * LRU_WIDTH), BF16)],
        scratch_shapes=[pltpu.VMEM((PAIR, 1, LRU_WIDTH), F32), pltpu.VMEM((TILE, LRU_WIDTH), F32),
                        pltpu.VMEM((TILE, LRU_WIDTH), F32)],
        compiler_params=_cparams(("parallel", "arbitrary")),
        name="na_attention",
    )(proj, proj, proj, proj, proj, bias_tiles, proj, proj, proj, *lru)


def _softplus(x):
    return jnp.maximum(x, 0.0) + jnp.log1p(jnp.exp(-jnp.abs(x)))


def _lru_conv(u, prev16, next16, has_prev, has_next, cw, cb):
    sub = lax.broadcasted_iota(jnp.int32, (SUB, u.shape[1]), 0)
    prow = prev16[HALO - 1:HALO, :] * has_prev
    n0 = next16[0:1, :] * has_next
    n8 = next16[8:9, :] * has_next
    first8 = jnp.where(sub == 0, prow, pltpu.roll(u[TILE - SUB:TILE, :], 1, 0))
    last_a = jnp.where(sub == SUB - 1, n0, pltpu.roll(u[0:SUB, :], SUB - 1, 0))
    last_b = jnp.where(sub == SUB - 1, n8, pltpu.roll(u[SUB:2 * SUB, :], SUB - 1, 0))
    um1 = jnp.concatenate([first8, u[0:TILE - SUB, :]], axis=0)
    up1 = jnp.concatenate([u[SUB:TILE, :], last_a], axis=0)
    up2 = jnp.concatenate([u[2 * SUB:TILE, :], last_a, last_b], axis=0)
    return cw[0:1, :] * um1 + cw[1:2, :] * u + cw[2:3, :] * up1 + cw[3:4, :] * up2 + cb


def _lru_gates(v, d, wr_ref, wi_ref, br_ref, bi_ref, lam_ref):
    vb = v.astype(BF16)
    n_cb = LRU_WIDTH // LRU_CH

    def gate(w_ref, b_ref):
        z = [jnp.dot(vb[:, c * LRU_CH:(c + 1) * LRU_CH], w_ref[d, c], preferred_element_type=F32) for c in range(n_cb)]
        return jnp.tanh(jnp.concatenate(z, axis=1) + b_ref[d:d + 1, :])

    tr = gate(wr_ref, br_ref)
    ti = gate(wi_ref, bi_ref)
    half = (-0.5 * LRU_C * LOG2E) * _softplus(-lam_ref[d:d + 1, :])
    a = jnp.exp2(half + half * tr)
    om = 1.0 - a * a
    b = (om * lax.rsqrt(jnp.maximum(om, TINY))) * ((0.5 * v) * (1.0 + ti))
    return a, b


def _lru_scan(a, b, h_in, reverse, hl_scr, p_scr):
    order = range(GROUPS - 1, -1, -1) if reverse else range(GROUPS)
    hl = None
    for g in order:
        ag = a[g * SUB:(g + 1) * SUB, :]
        bg = b[g * SUB:(g + 1) * SUB, :]
        if hl is None:
            hl, p = bg, ag
        else:
            hl = ag * hl + bg
            p = ag * p
        hl_scr[g * SUB:(g + 1) * SUB, :] = hl
        p_scr[g * SUB:(g + 1) * SUB, :] = p
    blocks = range(SUB - 1, -1, -1) if reverse else range(SUB)
    carry = h_in
    cins = {}
    for s in blocks:
        cins[s] = carry
        carry = hl[s:s + 1, :] + p[s:s + 1, :] * carry
    cin = jnp.concatenate([cins[s] for s in range(SUB)], axis=0)
    hfull = hl_scr[...] + p_scr[...] * jnp.tile(cin, (GROUPS, 1))
    return hfull, carry


def _route(sel, aff):
    def top2_sum(a, b, c, d):
        hi1, lo1 = jnp.maximum(a, b), jnp.minimum(a, b)
        hi2, lo2 = jnp.maximum(c, d), jnp.minimum(c, d)
        return jnp.maximum(hi1, hi2) + jnp.maximum(jnp.minimum(hi1, hi2), jnp.maximum(lo1, lo2))

    scores = [top2_sum(*sel[EXPERTS_PER_GROUP * g:EXPERTS_PER_GROUP * (g + 1)]) for g in range(N_GROUPS)]
    best = jnp.zeros_like(scores[0], dtype=jnp.int32)
    best_v = scores[0]
    for g in range(1, N_GROUPS):
        upd = scores[g] > best_v
        best = jnp.where(upd, g, best)
        best_v = jnp.where(upd, scores[g], best_v)
    chosen = []
    for e in range(N_EXPERTS):
        g = e // EXPERTS_PER_GROUP
        rank = jnp.zeros_like(best)
        for o in range(EXPERTS_PER_GROUP * g, EXPERTS_PER_GROUP * (g + 1)):
            if o == e:
                continue
            ahead = sel[o] > sel[e]
            if o < e:
                ahead = jnp.logical_or(ahead, sel[o] == sel[e])
            rank = rank + ahead.astype(jnp.int32)
        chosen.append(jnp.logical_and(best == g, rank < TOP_K))
    total = jnp.zeros_like(aff[0])
    for e in range(N_EXPERTS):
        total = total + jnp.where(chosen[e], aff[e], 0.0)
    gates = [jnp.where(chosen[e], aff[e] / total, 0.0) for e in range(N_EXPERTS)]
    return gates, [c.astype(F32) for c in chosen]


def _out_kernel(x_ref, ya_ref, hc_ref, g_ref, *refs):
    mod_refs = refs[:PAIR]
    (wr_ref, wi_ref, br_ref, bi_ref, lam_ref, nrm_ref, w_ref, rwc_ref, rb_ref,
     xo_ref, hp_ref, rt_ref, h_scr, hl_scr, p_scr) = refs[PAIR:]
    rows = PAIR * TILE

    @pl.when(pl.program_id(1) == 0)
    def _():
        h_scr[...] = jnp.zeros_like(h_scr)

    ybs = []
    for k in range(PAIR):
        a_r, b_r = _lru_gates(hc_ref[k, :, LRU_WIDTH:].astype(F32), 1, wr_ref, wi_ref, br_ref, bi_ref, lam_ref)
        hrev, h_last = _lru_scan(a_r, b_r, h_scr[k], True, hl_scr, p_scr)
        h_scr[k] = h_last
        gx = g_ref[k].astype(F32)
        gate = (0.5 * gx) * (1.0 + jnp.tanh(gx * (GELU_C + (GELU_C * 0.044715) * (gx * gx))))
        ybs.append((gate * (hc_ref[k, :, 0:LRU_WIDTH].astype(F32) + hrev)).astype(BF16))
    ya = ya_ref[...].reshape(rows, NA_WIDTH)
    yb = jnp.concatenate(ybs, axis=0)
    y = jnp.dot(ya, w_ref[0:NA_WIDTH, :], preferred_element_type=F32)
    y = y + jnp.dot(yb, w_ref[NA_WIDTH:, :], preferred_element_type=F32)
    hs = []
    for k in range(PAIR):
        m = mod_refs[k][0, 0]
        x = x_ref[k] + m[2:3, :] * y[k * TILE:(k + 1) * TILE, :]
        xo_ref[k] = x
        ms = jnp.mean(x * x, axis=-1, keepdims=True)
        h = (x * lax.rsqrt(ms + EPS)) * nrm_ref[...]
        h = h * (1.0 + m[4:5, :]) + m[3:4, :]
        hp_ref[k] = _pack_rows(h)
        hs.append(h)
    h = jnp.concatenate(hs, axis=0)
    h_hi = h.astype(BF16)
    h_lo = (h - h_hi.astype(F32)).astype(BF16)
    nt = (((1,), (1,)), ((), ()))
    rwc = rwc_ref[...]
    both = lax.dot_general(rwc, h_hi, nt, preferred_element_type=F32)
    lg = (both[0:N_EXPERTS, :] + both[N_EXPERTS:, :]
          + lax.dot_general(rwc[0:N_EXPERTS, :], h_lo, nt, preferred_element_type=F32))
    aff_all = jax.nn.sigmoid(lg)
    sel_all = aff_all + rb_ref[...]
    aff = [aff_all[e:e + 1, :] for e in range(N_EXPERTS)]
    sel = [sel_all[e:e + 1, :] for e in range(N_EXPERTS)]
    gates, chosen = _route(sel, aff)
    rt = jnp.concatenate(gates + chosen, axis=0)
    for k in range(PAIR):
        rt_ref[k, 0] = rt[:, k * TILE:(k + 1) * TILE]


def _out_proj(l, boff, xs, ya, hc, proj, lru, mod, nrm, w_out, rwc, rb):
    rev = lambda i: jnp.where(i == 0, 0, N_TILES - i)
    tok = lambda width, col=0: pl.BlockSpec((PAIR, TILE, width), lambda bp, i: (bp, rev(i), col))
    mod_specs = [pl.BlockSpec((1, 1, N_MOD, D_MODEL),
                              lambda bp, i, k=k: (l, _mod_row(boff + PAIR * bp + k, i), 0, 0)) for k in range(PAIR)]
    n_cb = LRU_WIDTH // LRU_CH
    lru_specs = [_full_spec((2, n_cb, LRU_CH, LRU_CH)), _full_spec((2, n_cb, LRU_CH, LRU_CH)),
                 _full_spec((2, LRU_WIDTH)), _full_spec((2, LRU_WIDTH)), _full_spec((2, LRU_WIDTH))]
    return pl.pallas_call(
        _out_kernel,
        grid=(SB // PAIR, N_TILES),
        in_specs=[tok(D_MODEL), tok(NA_WIDTH), tok(2 * LRU_WIDTH), tok(LRU_WIDTH, COL_G)]
        + mod_specs + lru_specs
        + [_full_spec((1, D_MODEL)), _full_spec((D_MODEL, D_MODEL)), _full_spec((2 * N_EXPERTS, D_MODEL)),
           _full_spec((N_EXPERTS, 1))],
        out_specs=[tok(D_MODEL), tok(PACK_W),
                   pl.BlockSpec((PAIR, 1, ROUTE_ROWS, TILE), lambda bp, i: (bp, rev(i), 0, 0))],
        out_shape=[jax.ShapeDtypeStruct((SB, L_TOT, D_MODEL), F32),
                   jax.ShapeDtypeStruct((SB, L_TOT, PACK_W), I32),
                   jax.ShapeDtypeStruct((SB, N_TILES, ROUTE_ROWS, TILE), F32)],
        scratch_shapes=[pltpu.VMEM((PAIR, 1, LRU_WIDTH), F32), pltpu.VMEM((TILE, LRU_WIDTH), F32),
                        pltpu.VMEM((TILE, LRU_WIDTH), F32)],
        compiler_params=_cparams(("parallel", "arbitrary")),
        name="out_proj_router",
    )(xs, ya, hc, proj, *([mod] * PAIR), *lru[2:], nrm, w_out, rwc, rb)


def _pos_kernel(rt_ref, tri_ref, start_ref, pos_ref, wt_ref, run_scr):
    @pl.when(pl.program_id(0) == 0)
    def _():
        run_scr[...] = jnp.zeros_like(run_scr)

    base = start_ref[...] + run_scr[...]
    for k in range(POS_TILES):
        gates = rt_ref[k, 0:N_EXPERTS, :]
        chosen = rt_ref[k, N_EXPERTS:, :]
        rank = jnp.dot(chosen.astype(BF16), tri_ref[...], preferred_element_type=F32)
        posf = rank + base
        seen = jnp.zeros((1, TILE), F32)
        p0 = jnp.zeros((1, TILE), F32)
        p1 = jnp.zeros((1, TILE), F32)
        w0 = jnp.zeros((1, TILE), F32)
        w1 = jnp.zeros((1, TILE), F32)
        for e in range(N_EXPERTS):
            ch = chosen[e:e + 1, :]
            first = ch * (1.0 - seen)
            second = ch * seen
            p0 = p0 + first * posf[e:e + 1, :]
            p1 = p1 + second * posf[e:e + 1, :]
            w0 = w0 + first * gates[e:e + 1, :]
            w1 = w1 + second * gates[e:e + 1, :]
            seen = jnp.minimum(seen + ch, 1.0)
        pos_ref[k] = jnp.concatenate([p0, p1], axis=0).astype(I32)
        wpad = jnp.concatenate([w0, w1, jnp.zeros((LANES - TOP_K, TILE), F32)], axis=0)
        wt_ref[k * TILE:(k + 1) * TILE, :] = jnp.transpose(wpad)
        base = base + jnp.sum(chosen, axis=1, keepdims=True)
    run_scr[...] = base - start_ref[...]


def _positions(route, tri, start):
    return pl.pallas_call(
        _pos_kernel,
        grid=(N_TOK_TILES // POS_TILES,),
        in_specs=[pl.BlockSpec((POS_TILES, ROUTE_ROWS, TILE), lambda i: (i, 0, 0)),
                  pl.BlockSpec((TILE, TILE), lambda i: (0, 0)),
                  pl.BlockSpec((N_EXPERTS, 1), lambda i: (0, 0))],
        out_specs=[pl.BlockSpec((POS_TILES, TOP_K, TILE), lambda i: (i, 0, 0)),
                   pl.BlockSpec((POS_TILES * TILE, LANES), lambda i: (i, 0))],
        out_shape=[jax.ShapeDtypeStruct((N_TOK_TILES, TOP_K, TILE), I32),
                   jax.ShapeDtypeStruct((N_TOK, LANES), F32)],
        scratch_shapes=[pltpu.VMEM((N_EXPERTS, 1), F32)],
        compiler_params=_cparams(("arbitrary",)),
        name="moe_positions",
    )(route, tri, start)


def _work_items(counts):
    smem = pl.BlockSpec(memory_space=pltpu.SMEM)
    item = jax.ShapeDtypeStruct((N_ITEMS,), I32)
    return pl.pallas_call(
        _items_kernel,
        in_specs=[smem],
        out_specs=[smem] * 4,
        out_shape=[item] * 4,
        name="moe_work_items",
    )(counts)


def _items_kernel(cnt_ref, tile_ref, exp_ref, lo_ref, hi_ref):
    n = jnp.int32(0)
    start = jnp.int32(0)
    last_e = jnp.int32(0)
    for e in range(N_EXPERTS):
        cnt = cnt_ref[e]
        end = start + cnt
        first = start // EXP_TILE
        n_tiles = jnp.where(cnt > 0, (end - 1) // EXP_TILE - first + 1, 0)

        def put(j, carry, e=e, n=n, start=start, end=end, first=first):
            tile = first + j
            tile_ref[n + j] = tile
            exp_ref[n + j] = jnp.int32(e)
            lo_ref[n + j] = jnp.maximum(start - tile * EXP_TILE, 0)
            hi_ref[n + j] = jnp.minimum(end - tile * EXP_TILE, EXP_TILE)
            return carry

        lax.fori_loop(0, n_tiles, put, 0)
        n = n + n_tiles
        start = end
        last_e = jnp.where(cnt > 0, e, last_e)

    def pad(j, carry):
        tile_ref[j] = jnp.int32(N_EXP_TILES - 1)
        exp_ref[j] = last_e
        lo_ref[j] = jnp.int32(0)
        hi_ref[j] = jnp.int32(0)
        return carry

    lax.fori_loop(n, N_ITEMS, pad, 0)


def _sc_worker_base():
    return (lax.axis_index("s") * SC_CORES + lax.axis_index("c")) * SC_ROWS


def _sc_dispatch_body(h_hbm, p0_hbm, p1_hbm, out_hbm, i0_v, i1_v, rows_v, sem):
    base = _sc_worker_base()

    @pl.loop(0, SC_N_CHUNKS)
    def _(j):
        off = base + j * SC_CHUNK
        pltpu.sync_copy(h_hbm.at[pl.ds(off, SC_CHUNK)], rows_v)
        pltpu.sync_copy(p0_hbm.at[pl.ds(off, SC_CHUNK)], i0_v)
        pltpu.sync_copy(p1_hbm.at[pl.ds(off, SC_CHUNK)], i1_v)
        c0 = pltpu.async_copy(rows_v, out_hbm.at[i0_v], sem)
        c1 = pltpu.async_copy(rows_v, out_hbm.at[i1_v], sem)
        c0.wait()
        c1.wait()


def _sc_combine_body(y_hbm, p0_hbm, p1_hbm, g0_hbm, g1_hbm, i_v, rows_v, sem):
    base = _sc_worker_base()

    @pl.loop(0, SC_N_CHUNKS)
    def _(j):
        off = base + j * SC_CHUNK
        for p_hbm, g_hbm in ((p0_hbm, g0_hbm), (p1_hbm, g1_hbm)):
            pltpu.sync_copy(p_hbm.at[pl.ds(off, SC_CHUNK)], i_v)
            pltpu.async_copy(y_hbm.at[i_v], rows_v, sem).wait()
            pltpu.sync_copy(rows_v, g_hbm.at[pl.ds(off, SC_CHUNK)])


def _sc_mesh():
    return plsc.VectorSubcoreMesh(core_axis_name="c", subcore_axis_name="s")


def _dispatch(hp, pos0, pos1):
    return pl.kernel(
        _sc_dispatch_body,
        out_type=jax.ShapeDtypeStruct((N_ASSIGN, PACK_W), I32),
        mesh=_sc_mesh(),
        scratch_types=[pltpu.VMEM((SC_CHUNK,), I32), pltpu.VMEM((SC_CHUNK,), I32),
                       pltpu.VMEM((SC_CHUNK, PACK_W), I32), pltpu.SemaphoreType.DMA],
        name="moe_dispatch",
    )(hp, pos0, pos1)


def _combine(ys, pos0, pos1):
    row = jax.ShapeDtypeStruct((N_TOK, PACK_W), I32)
    return pl.kernel(
        _sc_combine_body,
        out_type=[row, row],
        mesh=_sc_mesh(),
        scratch_types=[pltpu.VMEM((SC_CHUNK,), I32), pltpu.VMEM((SC_CHUNK, PACK_W), I32),
                       pltpu.SemaphoreType.DMA],
        name="moe_combine",
    )(ys, pos0, pos1)


def _expert_kernel(tile_ref, exp_ref, lo_ref, hi_ref, sched_ref, x_ref, w1_ref, w3_ref, w2_ref, o_ref,
                   w13_scr, w2_scr):
    del sched_ref
    i = pl.program_id(0)
    prev = jnp.maximum(i - 1, 0)

    @pl.when(jnp.logical_or(i == 0, exp_ref[i] != exp_ref[prev]))
    def _():
        w13_scr[:, 0:EXPERT_FF] = w1_ref[0, 0].astype(BF16)
        w13_scr[:, EXPERT_FF:] = w3_ref[0, 0].astype(BF16)
        w2_scr[...] = w2_ref[0, 0].astype(BF16)

    lo = lo_ref[i]
    hi = hi_ref[i]
    revisit = jnp.logical_and(i > 0, tile_ref[i] == tile_ref[prev])
    first_visit = jnp.logical_not(revisit)

    def run(r0, n):
        rows = slice(r0, r0 + n)
        lo_f, hi_f = _unpack_rows(x_ref[rows, :])
        h = jnp.concatenate([lo_f.astype(BF16), hi_f.astype(BF16)], axis=1)
        ab = jnp.dot(h, w13_scr[...], preferred_element_type=F32)
        a = ab[:, 0:EXPERT_FF]
        b = ab[:, EXPERT_FF:]
        t = ((a * jax.nn.sigmoid(a)) * b).astype(BF16)
        y = _pack_rows(jnp.dot(t, w2_scr[...], preferred_element_type=F32))
        row = r0 + lax.broadcasted_iota(jnp.int32, (n, PACK_W), 0)
        mine = jnp.logical_and(row >= lo, row < hi)

        @pl.when(first_visit)
        def _():
            o_ref[rows, :] = jnp.where(mine, y, 0)

        @pl.when(revisit)
        def _():
            o_ref[rows, :] = jnp.where(mine, y, o_ref[rows, :])

    def clear(r0, n):
        @pl.when(first_visit)
        def _():
            o_ref[r0:r0 + n, :] = jnp.zeros((n, PACK_W), I32)

    half = EXP_TILE // 2
    low = jnp.logical_and(lo < half, hi > lo)
    high = hi > half

    @pl.when(jnp.logical_and(low, high))
    def _():
        run(0, EXP_TILE)

    @pl.when(jnp.logical_and(low, jnp.logical_not(high)))
    def _():
        run(0, half)
        clear(half, half)

    @pl.when(jnp.logical_and(jnp.logical_not(low), high))
    def _():
        run(half, half)
        clear(0, half)


def _weight_schedule(e):
    idx = jnp.arange(N_ITEMS, dtype=I32)
    later = jnp.logical_not(jnp.concatenate([jnp.ones((1,), jnp.bool_), e[1:] != e[:-1]]))
    run_end = jnp.sum((e[None, :] <= e[:, None]).astype(I32), axis=1)
    nxt = jnp.min(jnp.where(e[None, :] > e[:, None], e[None, :], N_EXPERTS), axis=1)
    nxt = jnp.where(nxt == N_EXPERTS, e, nxt)
    return jnp.stack([jnp.where(jnp.logical_and(later, idx + k >= run_end), nxt, e) for k in (1, 2, 3)])


def _experts(l, xs_sorted, items, w1, w3, w2):
    tile, e, lo, hi = items
    grid_spec = pltpu.PrefetchScalarGridSpec(
        num_scalar_prefetch=5,
        grid=(N_ITEMS,),
        in_specs=[pl.BlockSpec((EXP_TILE, PACK_W), lambda i, t, e, lo, hi, ws: (t[i], 0)),
                  pl.BlockSpec((1, 1, D_MODEL, EXPERT_FF), lambda i, t, e, lo, hi, ws: (l, ws[2, i], 0, 0)),
                  pl.BlockSpec((1, 1, D_MODEL, EXPERT_FF), lambda i, t, e, lo, hi, ws: (l, ws[1, i], 0, 0)),
                  pl.BlockSpec((1, 1, EXPERT_FF, D_MODEL), lambda i, t, e, lo, hi, ws: (l, ws[0, i], 0, 0))],
        out_specs=pl.BlockSpec((EXP_TILE, PACK_W), lambda i, t, e, lo, hi, ws: (t[i], 0)),
        scratch_shapes=[pltpu.VMEM((D_MODEL, 2 * EXPERT_FF), BF16), pltpu.VMEM((EXPERT_FF, D_MODEL), BF16)],
    )
    return pl.pallas_call(
        _expert_kernel,
        grid_spec=grid_spec,
        out_shape=jax.ShapeDtypeStruct((N_ASSIGN, PACK_W), I32),
        compiler_params=_cparams(("arbitrary",)),
        name="moe_experts",
    )(tile, e, lo, hi, _weight_schedule(e), xs_sorted, w1, w3, w2)


def _moe(l, hp, route, tri, w1, w3, w2):
    counts = jnp.sum(route[:, N_EXPERTS:, :], axis=(0, 2))
    start = (jnp.cumsum(counts) - counts).reshape(N_EXPERTS, 1)
    pos, wt = _positions(route, tri, start)
    pos0 = pos[:, 0, :].reshape(N_TOK)
    pos1 = pos[:, 1, :].reshape(N_TOK)
    items = _work_items(counts.astype(I32))
    xs_sorted = _dispatch(hp.reshape(N_TOK, PACK_W), pos0, pos1)
    ys = _experts(l, xs_sorted, items, w1, w3, w2)
    g0, g1 = _combine(ys, pos0, pos1)
    shape = (SB, L_TOT, PACK_W)
    return g0.reshape(shape), g1.reshape(shape), wt.reshape(SB, L_TOT, LANES)


def _final_kernel(x_ref, g0_ref, g1_ref, wt_ref, mod_ref, *refs):
    o_ref, scr = refs[-2:]
    for b in range(SB):
        x = _moe_residual(x_ref[b], g0_ref[b], g1_ref[b], wt_ref[b], mod_ref[0, b][5:6, :])
        o_ref[b] = _from_scan_major(x, scr)


def _final(streams, mod):
    lat = lambda width: pl.BlockSpec((SB, TILE, width), lambda j: (0, j + 1, 0))
    out = None
    for k, st in enumerate(streams):
        in_specs = [lat(D_MODEL), lat(PACK_W), lat(PACK_W), lat(LANES),
                    pl.BlockSpec((1, SB, N_MOD, D_MODEL), lambda j, k=k: (DEPTH - 1, k, 0, 0))]
        args = [st["xs"]] + list(st["moe"]) + [mod]
        aliases = {}
        if out is not None:
            in_specs.append(pl.BlockSpec(memory_space=pl.ANY))
            args.append(out)
            aliases = {len(args) - 1: 0}
        out = pl.pallas_call(
            _final_kernel,
            grid=(N_LAT_TILES,),
            in_specs=in_specs,
            out_specs=pl.BlockSpec((SB, TILE, D_MODEL), lambda j, k=k: (k, j, 0)),
            out_shape=jax.ShapeDtypeStruct((BATCH, SEQ, D_MODEL), F32),
            scratch_shapes=[pltpu.VMEM((D_MODEL // LANES, TILE, LANES), F32)],
            input_output_aliases=aliases,
            compiler_params=_cparams(("arbitrary",)),
            name="final_residual",
        )(*args)
    return out


def _bias_kernel(tab_ref, perm_ref, o_ref, scr):
    rows_q = TILE // GRID_W
    lane = lax.broadcasted_iota(jnp.int32, (GROUPS, TILE), 1)
    ka = (lane >> 1) & (rows_q - 1)
    neg = jnp.full((GROUPS, TILE), NEG_INF, F32)
    qc = lax.broadcasted_iota(jnp.int32, (GRID_W, GRID_W), 0)
    kc = lax.broadcasted_iota(jnp.int32, (GRID_W, GRID_W), 1)
    col0 = jnp.clip(qc - NA_WIN_COLS // 2, 0, GRID_W - NA_WIN_COLS)
    in_win = jnp.logical_and(kc >= col0, kc < col0 + NA_WIN_COLS)
    toeplitz = []
    for r in range(2 * NA_WIN_ROWS - 1):
        row = jnp.broadcast_to(tab_ref[0, r:r + 1, :], (GRID_W, LANES))
        shifted = pltpu.roll(row, LANES - (GRID_W - 1), 1, stride=1, stride_axis=0)
        toeplitz.append(jnp.where(in_win, shifted[:, 0:GRID_W], NEG_INF))

    def interleaved(r0):
        acc = None
        for k in range(rows_q):
            val = toeplitz[r0 + k]
            t1 = val.astype(BF16)
            r1 = val - t1.astype(F32)
            t2 = r1.astype(BF16)
            t3 = (r1 - t2.astype(F32)).astype(BF16)
            for term in (t1, t2, t3):
                moved = jnp.dot(term, perm_ref[k], preferred_element_type=F32)
                acc = moved if acc is None else acc + moved
        return acc

    t4 = [interleaved(r0) for r0 in range(WIN_TILES * rows_q)]
    for a in range(rows_q):
        for qs in range(SUB // rows_q):
            s = (SUB // rows_q) * a + qs
            for kt in range(WIN_TILES):
                src = t4[rows_q * kt - a + 3][qs * GROUPS:(qs + 1) * GROUPS, :]
                c = rows_q * kt + ka
                variants = (
                    src if kt >= 1 else neg,
                    jnp.where(jnp.logical_and(c >= a, c <= a + NA_WIN_ROWS - 1), src, neg),
                    src if kt <= 1 else neg,
                )
                for v, val in enumerate(variants):
                    for half in range(TILE // LANES):
                        scr[v, 2 * kt + half, pl.ds(s, GROUPS, stride=SUB), :] = val[:, half * LANES:(half + 1) * LANES]
    for v in range(BIAS_VARIANTS):
        o_ref[v, 0] = jnp.concatenate([scr[v, j] for j in range(WIN_TILES * TILE // LANES)], axis=1)


def _bias_tiles(table):
    rows_q = TILE // GRID_W
    n_r = 2 * NA_WIN_ROWS - 1
    tab = jnp.pad(table.astype(F32) * LOG2E, ((0, 0), (0, 0), (BIAS_PAD, LANES - BIAS_PAD - (2 * NA_WIN_COLS - 1))))
    n_h = table.shape[0]
    half = GRID_W // GROUPS
    perm = np.zeros((rows_q, GRID_W, TILE), np.float32)
    for k in range(rows_q):
        for ks in range(half):
            for kg in range(GROUPS):
                perm[k, ks * GROUPS + kg, kg * SUB + k * half + ks] = 1.0
    return pl.pallas_call(
        _bias_kernel,
        grid=(n_h,),
        in_specs=[pl.BlockSpec((1, n_r, LANES), lambda h: (h, 0, 0)),
                  pl.BlockSpec((rows_q, GRID_W, TILE), lambda h: (0, 0, 0))],
        out_specs=pl.BlockSpec((BIAS_VARIANTS, 1, TILE, WIN_TILES * TILE), lambda h: (0, h, 0, 0)),
        out_shape=jax.ShapeDtypeStruct((BIAS_VARIANTS, n_h, TILE, WIN_TILES * TILE), F32),
        scratch_shapes=[pltpu.VMEM((BIAS_VARIANTS, WIN_TILES * TILE // LANES, TILE, LANES), F32)],
        compiler_params=_cparams(("parallel",)),
        name="bias_tiles",
    )(tab, jnp.asarray(perm, BF16))


def _block_diag(w, n_chunks):
    per = LRU_BLOCKS // n_chunks
    w = w.reshape(2, n_chunks, per, LRU_BLOCK, LRU_BLOCK)
    eye = jnp.eye(per, dtype=w.dtype)
    out = jnp.einsum('dcpij,pq->dcpiqj', w, eye)
    return out.reshape(2, n_chunks, per * LRU_BLOCK, per * LRU_BLOCK)


def kernel(x, c, ctx, c_ctx, w_mod, b_mod, norm_mix, norm_ffn, w_in, w_out, q_gain, k_gain, na_bias,
           conv_w, conv_b, lru_w_r, lru_b_r, lru_w_i, lru_b_i, lru_lambda, router_w, router_b,
           exp_w1, exp_w3, exp_w2):
    cs = jnp.concatenate([c, c_ctx[None, :], jnp.zeros((MOD_ROWS - BATCH - 1, D_MODEL), F32)], axis=0)
    mod = _modulation(cs, w_mod, b_mod).reshape(DEPTH, MOD_ROWS, N_MOD, D_MODEL)

    head_of = np.arange(NA_WIDTH) // HEAD_DIM
    bd = jnp.asarray((head_of[:, None] == head_of[None, :]).astype(np.float32) / HEAD_DIM, BF16)
    tri = jnp.asarray(np.triu(np.ones((TILE, TILE), np.float32), 1), BF16)
    rwt = router_w.T
    rwh = rwt.astype(BF16)
    rwc = jnp.concatenate([rwh, (rwt - rwh.astype(F32)).astype(BF16)], axis=0)
    rb = router_b.reshape(N_EXPERTS, 1)
    n_cb = LRU_WIDTH // LRU_CH

    bias_tiles = _bias_tiles(na_bias.reshape(DEPTH * NA_HEADS, 2 * NA_WIN_ROWS - 1, 2 * NA_WIN_COLS - 1))
    streams = [{"xs": (x, ctx), "moe": None, "boff": sidx * SB} for sidx in range(STREAMS)]
    for l in range(DEPTH):
        qg = jnp.tile(q_gain[l] * (ATTN_SCALE * LOG2E), NA_HEADS)[None, :]
        kg = jnp.tile(k_gain[l], NA_HEADS)[None, :]
        w_in_l = w_in[l].astype(BF16)
        w_out_l = w_out[l].astype(BF16)
        wr = (0.5 * _block_diag(lru_w_r[l], n_cb)).astype(BF16)
        wi = (0.5 * _block_diag(lru_w_i[l], n_cb)).astype(BF16)
        lru = (conv_w[l], conv_b[l][None, :], wr, wi, 0.5 * lru_b_r[l], 0.5 * lru_b_i[l], lru_lambda[l])
        for st in streams:
            boff = st["boff"]
            xs, proj = _in_proj(l, boff, st["xs"], st["moe"], mod, norm_mix[l][None, :], w_in_l, bd, qg, kg)
            ya, hc = _attention(l, proj, bias_tiles, lru)
            xs, hp, route = _out_proj(l, boff, xs, ya, hc, proj, lru, mod, norm_ffn[l][None, :], w_out_l, rwc, rb)
            st["xs"] = xs
            st["moe"] = _moe(l, hp, route.reshape(N_TOK_TILES, ROUTE_ROWS, TILE), tri, exp_w1, exp_w3, exp_w2)
    return _final(streams, mod)
```

```python
import functools

import jax
import jax.numpy as jnp
import numpy as np
from jax import lax
from jax.experimental import pallas as pl
from jax.experimental.pallas import tpu as pltpu
from jax.experimental.pallas import tpu_sc as plsc

F32 = jnp.float32
BF16 = jnp.bfloat16
I32 = jnp.int32

D_MODEL = 1024
BATCH = 4
SEQ = 8192
DEPTH = 4
GRID_W = 64
CTX_LEN = 256
HEAD_DIM = 64
NA_WIDTH = 512
NA_HEADS = 8
NA_WIN_ROWS = 8
NA_WIN_COLS = 16
LRU_WIDTH = 512
LRU_BLOCKS = 8
LRU_BLOCK = 64
CONV_W = 4
LRU_C = 8.0
IN_COLS = 3 * NA_WIDTH + 2 * LRU_WIDTH
COL_KV, COL_Q, COL_U, COL_G = 0, 2, 3, 4
N_EXPERTS = 16
N_GROUPS = 4
EXPERTS_PER_GROUP = 4
TOP_K = 2
EXPERT_FF = 512
N_MOD = 6
ATTN_SCALE = HEAD_DIM ** -0.5
LOG2E = 1.4426950408889634
EPS = 1e-6
NEG_INF = -1e30
TINY = 1e-30
GELU_C = 0.7978845608028654

TILE = 256
WIN_TILES = 3
BIAS_VARIANTS = 3
HALO = 16
SUB = 8
LANES = 128
GROUPS = TILE // SUB
L_TOT = CTX_LEN + SEQ
N_TILES = L_TOT // TILE
N_LAT_TILES = SEQ // TILE
STREAMS = 2
SB = BATCH // STREAMS
N_TOK = SB * L_TOT
N_TOK_TILES = N_TOK // TILE
PAIR = 2
LRU_CH = 256
MOD_ROWS = 8
VMEM_LIMIT = 56 * 1024 * 1024

PACK_W = D_MODEL // 2
HI_MASK = -65536
N_ASSIGN = TOP_K * N_TOK
EXP_TILE = 512
N_EXP_TILES = N_ASSIGN // EXP_TILE
N_ITEMS = N_EXP_TILES + N_EXPERTS - 1
BIAS_PAD = GRID_W - NA_WIN_COLS
ROUTE_ROWS = 2 * N_EXPERTS
POS_TILES = 22

SC_CORES = 2
SC_SUBCORES = 16
SC_WORKERS = SC_CORES * SC_SUBCORES
SC_ROWS = N_TOK // SC_WORKERS
SC_CHUNK = 88
SC_N_CHUNKS = SC_ROWS // SC_CHUNK


def _cparams(sem):
    return pltpu.CompilerParams(dimension_semantics=sem, vmem_limit_bytes=VMEM_LIMIT)


def _pack_rows(v):
    lo = pltpu.bitcast(v[:, :PACK_W].astype(BF16).astype(F32), I32)
    hi = pltpu.bitcast(v[:, PACK_W:].astype(BF16).astype(F32), I32)
    return ((lo >> 16) & 0xFFFF) | (hi & HI_MASK)


def _unpack_rows(p):
    return pltpu.bitcast(p << 16, F32), pltpu.bitcast(p & HI_MASK, F32)


def _mod_kernel(c_ref, w_ref, b_ref, o_ref):
    c = c_ref[...]
    s = c * jax.nn.sigmoid(c)
    o_ref[0] = jnp.dot(s.astype(BF16), w_ref[0].astype(BF16), preferred_element_type=F32) + b_ref[0]


def _modulation(cs, w_mod, b_mod):
    return pl.pallas_call(
        _mod_kernel,
        grid=(DEPTH, N_MOD),
        in_specs=[
            pl.BlockSpec((MOD_ROWS, D_MODEL), lambda l, n: (0, 0)),
            pl.BlockSpec((1, D_MODEL, D_MODEL), lambda l, n: (l, 0, n)),
            pl.BlockSpec((1, 1, D_MODEL), lambda l, n: (l, 0, n)),
        ],
        out_specs=pl.BlockSpec((1, MOD_ROWS, D_MODEL), lambda l, n: (l, 0, n)),
        out_shape=jax.ShapeDtypeStruct((DEPTH, MOD_ROWS, N_MOD * D_MODEL), F32),
        compiler_params=_cparams(("arbitrary", "arbitrary")),
        name="modulation",
    )(cs, w_mod, b_mod.reshape(DEPTH, 1, N_MOD * D_MODEL))


def _mod_row(b, i):
    return jnp.where(i == 0, BATCH, b)


def _moe_residual(x, g0, g1, wt, gate_row):
    lo0, hi0 = _unpack_rows(g0)
    lo1, hi1 = _unpack_rows(g1)
    w0 = wt[:, 0:1]
    w1 = wt[:, 1:2]
    f = jnp.concatenate([w0 * lo0 + w1 * lo1, w0 * hi0 + w1 * hi1], axis=1)
    return x + gate_row * f


def _pair_tok_spec(width):
    return pl.BlockSpec((PAIR, TILE, width), lambda bp, i: (bp, i, 0))


def _pair_mod_specs(layer, boff):
    return [pl.BlockSpec((1, 1, N_MOD, D_MODEL),
                         lambda bp, i, k=k: (layer, _mod_row(boff + PAIR * bp + k, i), 0, 0))
            for k in range(PAIR)]


def _full_spec(shape):
    return pl.BlockSpec(shape, lambda bp, i: tuple(0 for _ in shape))


def _to_scan_major(src_ref, scr):
    n_slab = D_MODEL // LANES
    for s in range(SUB):
        for j in range(n_slab):
            scr[j, pl.ds(s, GROUPS, stride=SUB), :] = src_ref[s * GROUPS:(s + 1) * GROUPS, j * LANES:(j + 1) * LANES]
    return jnp.concatenate([scr[j] for j in range(n_slab)], axis=1)


def _from_scan_major(val, scr):
    n_slab = D_MODEL // LANES
    for j in range(n_slab):
        scr[j] = val[:, j * LANES:(j + 1) * LANES]
    blocks = [jnp.concatenate([scr[j, pl.ds(s, GROUPS, stride=SUB), :] for j in range(n_slab)], axis=1)
              for s in range(SUB)]
    return jnp.concatenate(blocks, axis=0)


def _in_kernel(has_prev, *refs):
    if has_prev:
        x_ref, g0_ref, g1_ref, wt_ref = refs[:4]
        mprev_refs = refs[4:4 + PAIR]
        refs = (x_ref,) + refs[4 + PAIR:]
    else:
        x_ref, ctx_ref = refs[:2]
        refs = (x_ref,) + refs[2:]
    mod_refs = refs[1:1 + PAIR]
    nrm_ref, w_ref, bd_ref, qg_ref, kg_ref = refs[1 + PAIR:6 + PAIR]
    outs = refs[6 + PAIR:]
    xo_ref, proj_ref = outs[:2]
    x_ref = refs[0]
    hs = []
    for k in range(PAIR):
        if has_prev:
            x = _moe_residual(x_ref[k], g0_ref[k], g1_ref[k], wt_ref[k], mprev_refs[k][0, 0][5:6, :])
            xo_ref[k] = x
        else:
            perm_scr = outs[2]

            @pl.when(pl.program_id(1) == 0)
            def _():
                xo_ref[k] = _to_scan_major(ctx_ref.at[k], perm_scr)

            @pl.when(pl.program_id(1) > 0)
            def _():
                xo_ref[k] = _to_scan_major(x_ref.at[k], perm_scr)

            x = xo_ref[k]
        m = mod_refs[k][0, 0]
        ms = jnp.mean(x * x, axis=-1, keepdims=True)
        h = (x * lax.rsqrt(ms + EPS)) * nrm_ref[...]
        hs.append((h * (1.0 + m[1:2, :]) + m[0:1, :]).astype(BF16))
    acc = jnp.dot(jnp.concatenate(hs, axis=0), w_ref[...], preferred_element_type=F32)
    bd = bd_ref[...]

    def head_norm(t, gain):
        ss = jnp.dot((t * t).astype(BF16), bd, preferred_element_type=F32)
        return (t * lax.rsqrt(ss + EPS)) * gain

    def put(col_block, val):
        for k in range(PAIR):
            proj_ref[k, :, col_block * NA_WIDTH:(col_block + 1) * NA_WIDTH] = val[k * TILE:(k + 1) * TILE, :].astype(BF16)

    put(COL_Q, head_norm(acc[:, 0:NA_WIDTH], qg_ref[...]))
    put(COL_KV, head_norm(acc[:, NA_WIDTH:2 * NA_WIDTH], kg_ref[...]))
    put(COL_KV + 1, acc[:, 2 * NA_WIDTH:3 * NA_WIDTH])
    put(COL_U, acc[:, 3 * NA_WIDTH:3 * NA_WIDTH + LRU_WIDTH])
    put(COL_G, acc[:, 3 * NA_WIDTH + LRU_WIDTH:])


def _in_proj(l, boff, xs, moe_prev, mod, nrm, w_in, bd, qg, kg):
    has_prev = moe_prev is not None
    if has_prev:
        in_specs = [_pair_tok_spec(D_MODEL)]
        args = [xs]
        in_specs += [_pair_tok_spec(PACK_W), _pair_tok_spec(PACK_W), _pair_tok_spec(LANES)]
        in_specs += _pair_mod_specs(l - 1, boff)
        args += list(moe_prev) + [mod] * PAIR
    else:
        pair0 = boff // PAIR
        in_specs = [pl.BlockSpec((PAIR, TILE, D_MODEL), lambda bp, i: (pair0 + bp, jnp.maximum(i - 1, 0), 0)),
                    pl.BlockSpec((PAIR, TILE, D_MODEL), lambda bp, i: (pair0 + bp, 0, 0))]
        args = list(xs)
    in_specs += _pair_mod_specs(l, boff) + [_full_spec((1, D_MODEL)), _full_spec((D_MODEL, IN_COLS)),
                                            _full_spec((NA_WIDTH, NA_WIDTH)), _full_spec((1, NA_WIDTH)),
                                            _full_spec((1, NA_WIDTH))]
    args += [mod] * PAIR + [nrm, w_in, bd, qg, kg]
    out_shape = [jax.ShapeDtypeStruct((SB, L_TOT, D_MODEL), F32), jax.ShapeDtypeStruct((SB, L_TOT, IN_COLS), BF16)]
    out_specs = [_pair_tok_spec(D_MODEL), _pair_tok_spec(IN_COLS)]
    scratch = [] if has_prev else [pltpu.VMEM((D_MODEL // LANES, TILE, LANES), F32)]
    outs = pl.pallas_call(
        functools.partial(_in_kernel, has_prev),
        grid=(SB // PAIR, N_TILES),
        in_specs=in_specs,
        out_specs=out_specs,
        out_shape=out_shape,
        scratch_shapes=scratch,
        compiler_params=_cparams(("parallel", "arbitrary")),
        name="in_proj",
    )(*args)
    return outs[0], outs[1]


def _attn_kernel(q_ref, kvp_ref, kvc_ref, kvn_ref, kvx_ref, bias_ref,
                 u_ref, up_ref, un_ref, cw_ref, cb_ref, wr_ref, wi_ref, br_ref, bi_ref, lam_ref,
                 o_ref, hc_ref, h_scr, hl_scr, p_scr):
    j = pl.program_id(1)

    @pl.when(j == 0)
    def _():
        h_scr[...] = jnp.zeros_like(h_scr)

    has_prev = jnp.where(j >= 2, 1.0, 0.0).astype(F32)
    has_next = jnp.where(jnp.logical_and(j >= 1, j <= N_TILES - 2), 1.0, 0.0).astype(F32)

    def forward_lru():
        for b in range(PAIR):
            cv = _lru_conv(u_ref[b].astype(F32), up_ref[b].astype(F32), un_ref[b].astype(F32), has_prev, has_next,
                           cw_ref[...], cb_ref[...])
            hc_ref[b, :, LRU_WIDTH:] = cv.astype(BF16)
            a_f, b_f = _lru_gates(cv, 0, wr_ref, wi_ref, br_ref, bi_ref, lam_ref)
            hfull, h_last = _lru_scan(a_f, b_f, h_scr[b], False, hl_scr, p_scr)
            hc_ref[b, :, 0:LRU_WIDTH] = hfull.astype(BF16)
            h_scr[b] = h_last

    pair_w = 2 * HEAD_DIM
    lane = lax.broadcasted_iota(jnp.int32, (TILE, pair_w), 1)
    low = lane < HEAD_DIM
    nt = (((1,), (1,)), ((), ()))
    n_win = WIN_TILES * TILE

    def attend(kv_refs, windowed):
        low_kv = lax.broadcasted_iota(jnp.int32, (len(kv_refs) * TILE, pair_w), 1) < HEAD_DIM
        for b, hp in [(b, hp) for b in range(PAIR) for hp in range(NA_HEADS // 2)]:
            cols = slice(hp * pair_w, (hp + 1) * pair_w)
            vcols = slice(NA_WIDTH + hp * pair_w, NA_WIDTH + (hp + 1) * pair_w)
            q = q_ref[b, :, cols]
            kb = jnp.concatenate([r[b, :, cols] for r in kv_refs], axis=0)
            vb = jnp.concatenate([r[b, :, vcols] for r in kv_refs], axis=0)
            outs = []
            for hh in range(2):
                own = low if hh == 0 else jnp.logical_not(low)
                own_kv = low_kv if hh == 0 else jnp.logical_not(low_kv)
                qh = jnp.where(own, q, jnp.zeros_like(q))
                s = lax.dot_general(qh, kb, nt, preferred_element_type=F32)
                if windowed:
                    s_win = s[:, 0:n_win] + bias_ref[0, 2 * hp + hh]
                    s_ctx = s[:, n_win:]
                    m = jnp.maximum(jnp.max(s_win, axis=-1, keepdims=True), jnp.max(s_ctx, axis=-1, keepdims=True))
                    p = jnp.concatenate([jnp.exp2((s_win - m).astype(BF16)), jnp.exp2((s_ctx - m).astype(BF16))],
                                        axis=1)
                else:
                    p = jnp.exp2((s - jnp.max(s, axis=-1, keepdims=True)).astype(BF16))
                va = jnp.where(own_kv, vb, jnp.ones_like(vb))
                o = jnp.dot(p, va, preferred_element_type=F32)
                outs.append(o / pltpu.roll(o, HEAD_DIM, 1))
            o_ref[b, :, cols] = jnp.where(low, outs[0], outs[1]).astype(BF16)

    @pl.when(j == 0)
    def _():
        forward_lru()
        attend((kvx_ref,), False)

    @pl.when(j > 0)
    def _():
        forward_lru()
        attend((kvp_ref, kvc_ref, kvn_ref, kvx_ref), True)


def _attention(l, proj, bias_tiles, lru):
    last = N_LAT_TILES - 1
    blk = (PAIR, TILE, 2 * NA_WIDTH)
    prev_map = lambda b, j: (b, 1 + jnp.clip(j - 2, 0, last), COL_KV // 2)
    cur_map = lambda b, j: (b, jnp.maximum(j, 1), COL_KV // 2)
    next_map = lambda b, j: (b, 1 + jnp.clip(j, 0, last), COL_KV // 2)
    ctx_map = lambda b, j: (b, 0, COL_KV // 2)
    var_map = lambda b, j: (jnp.where(j <= 1, 0, jnp.where(j == N_TILES - 1, 2, 1)), l, 0, 0)
    kv_specs = [pl.BlockSpec(blk, prev_map), pl.BlockSpec(blk, cur_map), pl.BlockSpec(blk, next_map),
                pl.BlockSpec(blk, ctx_map)]
    tok = lambda col: pl.BlockSpec((PAIR, TILE, NA_WIDTH), lambda b, j: (b, j, col))
    halo = TILE // HALO
    n_halo = L_TOT // HALO
    prev16 = pl.BlockSpec((PAIR, HALO, LRU_WIDTH), lambda b, j: (b, jnp.maximum(j * halo - 1, 0), COL_U))
    next16 = pl.BlockSpec((PAIR, HALO, LRU_WIDTH), lambda b, j: (b, jnp.minimum((j + 1) * halo, n_halo - 1), COL_U))
    whole = lambda shape: pl.BlockSpec(shape, lambda b, j: tuple(0 for _ in shape))
    n_cb = LRU_WIDTH // LRU_CH
    lru_specs = [whole((CONV_W, LRU_WIDTH)), whole((1, LRU_WIDTH)), whole((2, n_cb, LRU_CH, LRU_CH)),
                 whole((2, n_cb, LRU_CH, LRU_CH)), whole((2, LRU_WIDTH)), whole((2, LRU_WIDTH)), whole((2, LRU_WIDTH))]
    return pl.pallas_call(
        _attn_kernel,
        grid=(SB // PAIR, N_TILES),
        in_specs=[tok(COL_Q)] + kv_specs + [pl.BlockSpec((1, NA_HEADS, TILE, WIN_TILES * TILE), var_map)]
        + [tok(COL_U), prev16, next16] + lru_specs,
        out_specs=[tok(0), pl.BlockSpec((PAIR, TILE, 2 * LRU_WIDTH), lambda b, j: (b, j, 0))],
        out_shape=[jax.ShapeDtypeStruct((SB, L_TOT, NA_WIDTH), BF16),
                   jax.ShapeDtypeStruct((SB, L_TOT, 2 * LRU_WIDTH), BF16)],
        scratch_shapes=[pltpu.VMEM((PAIR, 1, LRU_WIDTH), F32), pltpu.VMEM((TILE, LRU_WIDTH), F32),
                        pltpu.VMEM((TILE, LRU_WIDTH), F32)],
        compiler_params=_cparams(("parallel", "arbitrary")),
        name="na_attention",
    )(proj, proj, proj, proj, proj, bias_tiles, proj, proj, proj, *lru)


def _softplus(x):
    return jnp.maximum(x, 0.0) + jnp.log1p(jnp.exp(-jnp.abs(x)))


def _lru_conv(u, prev16, next16, has_prev, has_next, cw, cb):
    sub = lax.broadcasted_iota(jnp.int32, (SUB, u.shape[1]), 0)
    prow = prev16[HALO - 1:HALO, :] * has_prev
    n0 = next16[0:1, :] * has_next
    n8 = next16[8:9, :] * has_next
    first8 = jnp.where(sub == 0, prow, pltpu.roll(u[TILE - SUB:TILE, :], 1, 0))
    last_a = jnp.where(sub == SUB - 1, n0, pltpu.roll(u[0:SUB, :], SUB - 1, 0))
    last_b = jnp.where(sub == SUB - 1, n8, pltpu.roll(u[SUB:2 * SUB, :], SUB - 1, 0))
    um1 = jnp.concatenate([first8, u[0:TILE - SUB, :]], axis=0)
    up1 = jnp.concatenate([u[SUB:TILE, :], last_a], axis=0)
    up2 = jnp.concatenate([u[2 * SUB:TILE, :], last_a, last_b], axis=0)
    return cw[0:1, :] * um1 + cw[1:2, :] * u + cw[2:3, :] * up1 + cw[3:4, :] * up2 + cb


def _lru_gates(v, d, wr_ref, wi_ref, br_ref, bi_ref, lam_ref):
    vb = v.astype(BF16)
    n_cb = LRU_WIDTH // LRU_CH

    def gate(w_ref, b_ref):
        z = [jnp.dot(vb[:, c * LRU_CH:(c + 1) * LRU_CH], w_ref[d, c], preferred_element_type=F32) for c in range(n_cb)]
        return jnp.tanh(jnp.concatenate(z, axis=1) + b_ref[d:d + 1, :])

    tr = gate(wr_ref, br_ref)
    ti = gate(wi_ref, bi_ref)
    half = (-0.5 * LRU_C * LOG2E) * _softplus(-lam_ref[d:d + 1, :])
    a = jnp.exp2(half + half * tr)
    om = 1.0 - a * a
    b = (om * lax.rsqrt(jnp.maximum(om, TINY))) * ((0.5 * v) * (1.0 + ti))
    return a, b


def _lru_scan(a, b, h_in, reverse, hl_scr, p_scr):
    order = range(GROUPS - 1, -1, -1) if reverse else range(GROUPS)
    hl = None
    for g in order:
        ag = a[g * SUB:(g + 1) * SUB, :]
        bg = b[g * SUB:(g + 1) * SUB, :]
        if hl is None:
            hl, p = bg, ag
        else:
            hl = ag * hl + bg
            p = ag * p
        hl_scr[g * SUB:(g + 1) * SUB, :] = hl
        p_scr[g * SUB:(g + 1) * SUB, :] = p
    blocks = range(SUB - 1, -1, -1) if reverse else range(SUB)
    carry = h_in
    cins = {}
    for s in blocks:
        cins[s] = carry
        carry = hl[s:s + 1, :] + p[s:s + 1, :] * carry
    cin = jnp.concatenate([cins[s] for s in range(SUB)], axis=0)
    hfull = hl_scr[...] + p_scr[...] * jnp.tile(cin, (GROUPS, 1))
    return hfull, carry


def _route(sel, aff):
    def top2_sum(a, b, c, d):
        hi1, lo1 = jnp.maximum(a, b), jnp.minimum(a, b)
        hi2, lo2 = jnp.maximum(c, d), jnp.minimum(c, d)
        return jnp.maximum(hi1, hi2) + jnp.maximum(jnp.minimum(hi1, hi2), jnp.maximum(lo1, lo2))

    scores = [top2_sum(*sel[EXPERTS_PER_GROUP * g:EXPERTS_PER_GROUP * (g + 1)]) for g in range(N_GROUPS)]
    best = jnp.zeros_like(scores[0], dtype=jnp.int32)
    best_v = scores[0]
    for g in range(1, N_GROUPS):
        upd = scores[g] > best_v
        best = jnp.where(upd, g, best)
        best_v = jnp.where(upd, scores[g], best_v)
    chosen = []
    for e in range(N_EXPERTS):
        g = e // EXPERTS_PER_GROUP
        rank = jnp.zeros_like(best)
        for o in range(EXPERTS_PER_GROUP * g, EXPERTS_PER_GROUP * (g + 1)):
            if o == e:
                continue
            ahead = sel[o] > sel[e]
            if o < e:
                ahead = jnp.logical_or(ahead, sel[o] == sel[e])
            rank = rank + ahead.astype(jnp.int32)
        chosen.append(jnp.logical_and(best == g, rank < TOP_K))
    total = jnp.zeros_like(aff[0])
    for e in range(N_EXPERTS):
        total = total + jnp.where(chosen[e], aff[e], 0.0)
    gates = [jnp.where(chosen[e], aff[e] / total, 0.0) for e in range(N_EXPERTS)]
    return gates, [c.astype(F32) for c in chosen]


def _out_kernel(x_ref, ya_ref, hc_ref, g_ref, *refs):
    mod_refs = refs[:PAIR]
    (wr_ref, wi_ref, br_ref, bi_ref, lam_ref, nrm_ref, w_ref, rwc_ref, rb_ref,
     xo_ref, hp_ref, rt_ref, h_scr, hl_scr, p_scr) = refs[PAIR:]
    rows = PAIR * TILE

    @pl.when(pl.program_id(1) == 0)
    def _():
        h_scr[...] = jnp.zeros_like(h_scr)

    ybs = []
    for k in range(PAIR):
        a_r, b_r = _lru_gates(hc_ref[k, :, LRU_WIDTH:].astype(F32), 1, wr_ref, wi_ref, br_ref, bi_ref, lam_ref)
        hrev, h_last = _lru_scan(a_r, b_r, h_scr[k], True, hl_scr, p_scr)
        h_scr[k] = h_last
        gx = g_ref[k].astype(F32)
        gate = (0.5 * gx) * (1.0 + jnp.tanh(gx * (GELU_C + (GELU_C * 0.044715) * (gx * gx))))
        ybs.append((gate * (hc_ref[k, :, 0:LRU_WIDTH].astype(F32) + hrev)).astype(BF16))
    ya = ya_ref[...].reshape(rows, NA_WIDTH)
    yb = jnp.concatenate(ybs, axis=0)
    y = jnp.dot(ya, w_ref[0:NA_WIDTH, :], preferred_element_type=F32)
    y = y + jnp.dot(yb, w_ref[NA_WIDTH:, :], preferred_element_type=F32)
    hs = []
    for k in range(PAIR):
        m = mod_refs[k][0, 0]
        x = x_ref[k] + m[2:3, :] * y[k * TILE:(k + 1) * TILE, :]
        xo_ref[k] = x
        ms = jnp.mean(x * x, axis=-1, keepdims=True)
        h = (x * lax.rsqrt(ms + EPS)) * nrm_ref[...]
        h = h * (1.0 + m[4:5, :]) + m[3:4, :]
        hp_ref[k] = _pack_rows(h)
        hs.append(h)
    h = jnp.concatenate(hs, axis=0)
    h_hi = h.astype(BF16)
    h_lo = (h - h_hi.astype(F32)).astype(BF16)
    nt = (((1,), (1,)), ((), ()))
    rwc = rwc_ref[...]
    both = lax.dot_general(rwc, h_hi, nt, preferred_element_type=F32)
    lg = (both[0:N_EXPERTS, :] + both[N_EXPERTS:, :]
          + lax.dot_general(rwc[0:N_EXPERTS, :], h_lo, nt, preferred_element_type=F32))
    aff_all = jax.nn.sigmoid(lg)
    sel_all = aff_all + rb_ref[...]
    aff = [aff_all[e:e + 1, :] for e in range(N_EXPERTS)]
    sel = [sel_all[e:e + 1, :] for e in range(N_EXPERTS)]
    gates, chosen = _route(sel, aff)
    rt = jnp.concatenate(gates + chosen, axis=0)
    for k in range(PAIR):
        rt_ref[k, 0] = rt[:, k * TILE:(k + 1) * TILE]


def _out_proj(l, boff, xs, ya, hc, proj, lru, mod, nrm, w_out, rwc, rb):
    rev = lambda i: jnp.where(i == 0, 0, N_TILES - i)
    tok = lambda width, col=0: pl.BlockSpec((PAIR, TILE, width), lambda bp, i: (bp, rev(i), col))
    mod_specs = [pl.BlockSpec((1, 1, N_MOD, D_MODEL),
                              lambda bp, i, k=k: (l, _mod_row(boff + PAIR * bp + k, i), 0, 0)) for k in range(PAIR)]
    n_cb = LRU_WIDTH // LRU_CH
    lru_specs = [_full_spec((2, n_cb, LRU_CH, LRU_CH)), _full_spec((2, n_cb, LRU_CH, LRU_CH)),
                 _full_spec((2, LRU_WIDTH)), _full_spec((2, LRU_WIDTH)), _full_spec((2, LRU_WIDTH))]
    return pl.pallas_call(
        _out_kernel,
        grid=(SB // PAIR, N_TILES),
        in_specs=[tok(D_MODEL), tok(NA_WIDTH), tok(2 * LRU_WIDTH), tok(LRU_WIDTH, COL_G)]
        + mod_specs + lru_specs
        + [_full_spec((1, D_MODEL)), _full_spec((D_MODEL, D_MODEL)), _full_spec((2 * N_EXPERTS, D_MODEL)),
           _full_spec((N_EXPERTS, 1))],
        out_specs=[tok(D_MODEL), tok(PACK_W),
                   pl.BlockSpec((PAIR, 1, ROUTE_ROWS, TILE), lambda bp, i: (bp, rev(i), 0, 0))],
        out_shape=[jax.ShapeDtypeStruct((SB, L_TOT, D_MODEL), F32),
                   jax.ShapeDtypeStruct((SB, L_TOT, PACK_W), I32),
                   jax.ShapeDtypeStruct((SB, N_TILES, ROUTE_ROWS, TILE), F32)],
        scratch_shapes=[pltpu.VMEM((PAIR, 1, LRU_WIDTH), F32), pltpu.VMEM((TILE, LRU_WIDTH), F32),
                        pltpu.VMEM((TILE, LRU_WIDTH), F32)],
        compiler_params=_cparams(("parallel", "arbitrary")),
        name="out_proj_router",
    )(xs, ya, hc, proj, *([mod] * PAIR), *lru[2:], nrm, w_out, rwc, rb)


def _pos_kernel(rt_ref, tri_ref, start_ref, pos_ref, wt_ref, run_scr):
    @pl.when(pl.program_id(0) == 0)
    def _():
        run_scr[...] = jnp.zeros_like(run_scr)

    base = start_ref[...] + run_scr[...]
    for k in range(POS_TILES):
        gates = rt_ref[k, 0:N_EXPERTS, :]
        chosen = rt_ref[k, N_EXPERTS:, :]
        rank = jnp.dot(chosen.astype(BF16), tri_ref[...], preferred_element_type=F32)
        posf = rank + base
        seen = jnp.zeros((1, TILE), F32)
        p0 = jnp.zeros((1, TILE), F32)
        p1 = jnp.zeros((1, TILE), F32)
        w0 = jnp.zeros((1, TILE), F32)
        w1 = jnp.zeros((1, TILE), F32)
        for e in range(N_EXPERTS):
            ch = chosen[e:e + 1, :]
            first = ch * (1.0 - seen)
            second = ch * seen
            p0 = p0 + first * posf[e:e + 1, :]
            p1 = p1 + second * posf[e:e + 1, :]
            w0 = w0 + first * gates[e:e + 1, :]
            w1 = w1 + second * gates[e:e + 1, :]
            seen = jnp.minimum(seen + ch, 1.0)
        pos_ref[k] = jnp.concatenate([p0, p1], axis=0).astype(I32)
        wpad = jnp.concatenate([w0, w1, jnp.zeros((LANES - TOP_K, TILE), F32)], axis=0)
        wt_ref[k * TILE:(k + 1) * TILE, :] = jnp.transpose(wpad)
        base = base + jnp.sum(chosen, axis=1, keepdims=True)
    run_scr[...] = base - start_ref[...]


def _positions(route, tri, start):
    return pl.pallas_call(
        _pos_kernel,
        grid=(N_TOK_TILES // POS_TILES,),
        in_specs=[pl.BlockSpec((POS_TILES, ROUTE_ROWS, TILE), lambda i: (i, 0, 0)),
                  pl.BlockSpec((TILE, TILE), lambda i: (0, 0)),
                  pl.BlockSpec((N_EXPERTS, 1), lambda i: (0, 0))],
        out_specs=[pl.BlockSpec((POS_TILES, TOP_K, TILE), lambda i: (i, 0, 0)),
                   pl.BlockSpec((POS_TILES * TILE, LANES), lambda i: (i, 0))],
        out_shape=[jax.ShapeDtypeStruct((N_TOK_TILES, TOP_K, TILE), I32),
                   jax.ShapeDtypeStruct((N_TOK, LANES), F32)],
        scratch_shapes=[pltpu.VMEM((N_EXPERTS, 1), F32)],
        compiler_params=_cparams(("arbitrary",)),
        name="moe_positions",
    )(route, tri, start)


def _work_items(counts):
    smem = pl.BlockSpec(memory_space=pltpu.SMEM)
    item = jax.ShapeDtypeStruct((N_ITEMS,), I32)
    return pl.pallas_call(
        _items_kernel,
        in_specs=[smem],
        out_specs=[smem] * 4,
        out_shape=[item] * 4,
        name="moe_work_items",
    )(counts)


def _items_kernel(cnt_ref, tile_ref, exp_ref, lo_ref, hi_ref):
    n = jnp.int32(0)
    start = jnp.int32(0)
    last_e = jnp.int32(0)
    for e in range(N_EXPERTS):
        cnt = cnt_ref[e]
        end = start + cnt
        first = start // EXP_TILE
        n_tiles = jnp.where(cnt > 0, (end - 1) // EXP_TILE - first + 1, 0)

        def put(j, carry, e=e, n=n, start=start, end=end, first=first):
            tile = first + j
            tile_ref[n + j] = tile
            exp_ref[n + j] = jnp.int32(e)
            lo_ref[n + j] = jnp.maximum(start - tile * EXP_TILE, 0)
            hi_ref[n + j] = jnp.minimum(end - tile * EXP_TILE, EXP_TILE)
            return carry

        lax.fori_loop(0, n_tiles, put, 0)
        n = n + n_tiles
        start = end
        last_e = jnp.where(cnt > 0, e, last_e)

    def pad(j, carry):
        tile_ref[j] = jnp.int32(N_EXP_TILES - 1)
        exp_ref[j] = last_e
        lo_ref[j] = jnp.int32(0)
        hi_ref[j] = jnp.int32(0)
        return carry

    lax.fori_loop(n, N_ITEMS, pad, 0)


def _sc_worker_base():
    return (lax.axis_index("s") * SC_CORES + lax.axis_index("c")) * SC_ROWS


def _sc_dispatch_body(h_hbm, p0_hbm, p1_hbm, out_hbm, i0_v, i1_v, rows_v, sem):
    base = _sc_worker_base()

    @pl.loop(0, SC_N_CHUNKS)
    def _(j):
        off = base + j * SC_CHUNK
        pltpu.sync_copy(h_hbm.at[pl.ds(off, SC_CHUNK)], rows_v)
        pltpu.sync_copy(p0_hbm.at[pl.ds(off, SC_CHUNK)], i0_v)
        pltpu.sync_copy(p1_hbm.at[pl.ds(off, SC_CHUNK)], i1_v)
        c0 = pltpu.async_copy(rows_v, out_hbm.at[i0_v], sem)
        c1 = pltpu.async_copy(rows_v, out_hbm.at[i1_v], sem)
        c0.wait()
        c1.wait()


def _sc_combine_body(y_hbm, p0_hbm, p1_hbm, g0_hbm, g1_hbm, i_v, rows_v, sem):
    base = _sc_worker_base()

    @pl.loop(0, SC_N_CHUNKS)
    def _(j):
        off = base + j * SC_CHUNK
        for p_hbm, g_hbm in ((p0_hbm, g0_hbm), (p1_hbm, g1_hbm)):
            pltpu.sync_copy(p_hbm.at[pl.ds(off, SC_CHUNK)], i_v)
            pltpu.async_copy(y_hbm.at[i_v], rows_v, sem).wait()
            pltpu.sync_copy(rows_v, g_hbm.at[pl.ds(off, SC_CHUNK)])


def _sc_mesh():
    return plsc.VectorSubcoreMesh(core_axis_name="c", subcore_axis_name="s")


def _dispatch(hp, pos0, pos1):
    return pl.kernel(
        _sc_dispatch_body,
        out_type=jax.ShapeDtypeStruct((N_ASSIGN, PACK_W), I32),
        mesh=_sc_mesh(),
        scratch_types=[pltpu.VMEM((SC_CHUNK,), I32), pltpu.VMEM((SC_CHUNK,), I32),
                       pltpu.VMEM((SC_CHUNK, PACK_W), I32), pltpu.SemaphoreType.DMA],
        name="moe_dispatch",
    )(hp, pos0, pos1)


def _combine(ys, pos0, pos1):
    row = jax.ShapeDtypeStruct((N_TOK, PACK_W), I32)
    return pl.kernel(
        _sc_combine_body,
        out_type=[row, row],
        mesh=_sc_mesh(),
        scratch_types=[pltpu.VMEM((SC_CHUNK,), I32), pltpu.VMEM((SC_CHUNK, PACK_W), I32),
                       pltpu.SemaphoreType.DMA],
        name="moe_combine",
    )(ys, pos0, pos1)


def _expert_kernel(tile_ref, exp_ref, lo_ref, hi_ref, sched_ref, x_ref, w1_ref, w3_ref, w2_ref, o_ref,
                   w13_scr, w2_scr):
    del sched_ref
    i = pl.program_id(0)
    prev = jnp.maximum(i - 1, 0)

    @pl.when(jnp.logical_or(i == 0, exp_ref[i] != exp_ref[prev]))
    def _():
        w13_scr[:, 0:EXPERT_FF] = w1_ref[0, 0].astype(BF16)
        w13_scr[:, EXPERT_FF:] = w3_ref[0, 0].astype(BF16)
        w2_scr[...] = w2_ref[0, 0].astype(BF16)

    lo = lo_ref[i]
    hi = hi_ref[i]
    revisit = jnp.logical_and(i > 0, tile_ref[i] == tile_ref[prev])
    first_visit = jnp.logical_not(revisit)

    def run(r0, n):
        rows = pl.ds(r0, n)
        lo_f, hi_f = _unpack_rows(x_ref[rows, :])
        h = jnp.concatenate([lo_f.astype(BF16), hi_f.astype(BF16)], axis=1)
        ab = jnp.dot(h, w13_scr[...], preferred_element_type=F32)
        a = ab[:, 0:EXPERT_FF]
        b = ab[:, EXPERT_FF:]
        t = ((a * jax.nn.sigmoid(a)) * b).astype(BF16)
        y = _pack_rows(jnp.dot(t, w2_scr[...], preferred_element_type=F32))
        row = r0 + lax.broadcasted_iota(jnp.int32, (n, PACK_W), 0)
        mine = jnp.logical_and(row >= lo, row < hi)

        @pl.when(first_visit)
        def _():
            o_ref[rows, :] = jnp.where(mine, y, 0)

        @pl.when(revisit)
        def _():
            o_ref[rows, :] = jnp.where(mine, y, o_ref[rows, :])

    quarter = EXP_TILE // 4
    first_q = lo // quarter
    n_q = jnp.where(hi > lo, (hi + quarter - 1) // quarter - first_q, 0)
    r0 = pl.multiple_of(first_q * quarter, quarter)
    for k in range(1, 5):
        @pl.when(n_q == k)
        def _(k=k):
            n = k * quarter
            run(r0 if n < EXP_TILE else 0, n)
            if n < EXP_TILE:
                @pl.when(first_visit)
                def _():
                    o_ref[n:, :] = jnp.zeros((EXP_TILE - n, PACK_W), I32)


def _weight_schedule(e):
    idx = jnp.arange(N_ITEMS, dtype=I32)
    later = jnp.logical_not(jnp.concatenate([jnp.ones((1,), jnp.bool_), e[1:] != e[:-1]]))
    run_end = jnp.sum((e[None, :] <= e[:, None]).astype(I32), axis=1)
    nxt = jnp.min(jnp.where(e[None, :] > e[:, None], e[None, :], N_EXPERTS), axis=1)
    nxt = jnp.where(nxt == N_EXPERTS, e, nxt)
    return jnp.stack([jnp.where(jnp.logical_and(later, idx + k >= run_end), nxt, e) for k in (1, 2, 3)])


def _experts(l, xs_sorted, items, w1, w3, w2):
    tile, e, lo, hi = items
    grid_spec = pltpu.PrefetchScalarGridSpec(
        num_scalar_prefetch=5,
        grid=(N_ITEMS,),
        in_specs=[pl.BlockSpec((EXP_TILE, PACK_W), lambda i, t, e, lo, hi, ws: (t[i], 0)),
                  pl.BlockSpec((1, 1, D_MODEL, EXPERT_FF), lambda i, t, e, lo, hi, ws: (l, ws[2, i], 0, 0)),
                  pl.BlockSpec((1, 1, D_MODEL, EXPERT_FF), lambda i, t, e, lo, hi, ws: (l, ws[1, i], 0, 0)),
                  pl.BlockSpec((1, 1, EXPERT_FF, D_MODEL), lambda i, t, e, lo, hi, ws: (l, ws[0, i], 0, 0))],
        out_specs=pl.BlockSpec((EXP_TILE, PACK_W), lambda i, t, e, lo, hi, ws: (t[i], 0)),
        scratch_shapes=[pltpu.VMEM((D_MODEL, 2 * EXPERT_FF), BF16), pltpu.VMEM((EXPERT_FF, D_MODEL), BF16)],
    )
    return pl.pallas_call(
        _expert_kernel,
        grid_spec=grid_spec,
        out_shape=jax.ShapeDtypeStruct((N_ASSIGN, PACK_W), I32),
        compiler_params=_cparams(("arbitrary",)),
        name="moe_experts",
    )(tile, e, lo, hi, _weight_schedule(e), xs_sorted, w1, w3, w2)


def _moe(l, hp, route, tri, w1, w3, w2):
    counts = jnp.sum(route[:, N_EXPERTS:, :], axis=(0, 2))
    start = (jnp.cumsum(counts) - counts).reshape(N_EXPERTS, 1)
    pos, wt = _positions(route, tri, start)
    pos0 = pos[:, 0, :].reshape(N_TOK)
    pos1 = pos[:, 1, :].reshape(N_TOK)
    items = _work_items(counts.astype(I32))
    xs_sorted = _dispatch(hp.reshape(N_TOK, PACK_W), pos0, pos1)
    ys = _experts(l, xs_sorted, items, w1, w3, w2)
    g0, g1 = _combine(ys, pos0, pos1)
    shape = (SB, L_TOT, PACK_W)
    return g0.reshape(shape), g1.reshape(shape), wt.reshape(SB, L_TOT, LANES)


def _final_kernel(x_ref, g0_ref, g1_ref, wt_ref, mod_ref, *refs):
    o_ref, scr = refs[-2:]
    for b in range(SB):
        x = _moe_residual(x_ref[b], g0_ref[b], g1_ref[b], wt_ref[b], mod_ref[0, b][5:6, :])
        o_ref[b] = _from_scan_major(x, scr)


def _final(streams, mod):
    lat = lambda width: pl.BlockSpec((SB, TILE, width), lambda j: (0, j + 1, 0))
    out = None
    for k, st in enumerate(streams):
        in_specs = [lat(D_MODEL), lat(PACK_W), lat(PACK_W), lat(LANES),
                    pl.BlockSpec((1, SB, N_MOD, D_MODEL), lambda j, k=k: (DEPTH - 1, k, 0, 0))]
        args = [st["xs"]] + list(st["moe"]) + [mod]
        aliases = {}
        if out is not None:
            in_specs.append(pl.BlockSpec(memory_space=pl.ANY))
            args.append(out)
            aliases = {len(args) - 1: 0}
        out = pl.pallas_call(
            _final_kernel,
            grid=(N_LAT_TILES,),
            in_specs=in_specs,
            out_specs=pl.BlockSpec((SB, TILE, D_MODEL), lambda j, k=k: (k, j, 0)),
            out_shape=jax.ShapeDtypeStruct((BATCH, SEQ, D_MODEL), F32),
            scratch_shapes=[pltpu.VMEM((D_MODEL // LANES, TILE, LANES), F32)],
            input_output_aliases=aliases,
            compiler_params=_cparams(("arbitrary",)),
            name="final_residual",
        )(*args)
    return out


def _bias_kernel(tab_ref, perm_ref, o_ref, scr):
    rows_q = TILE // GRID_W
    lane = lax.broadcasted_iota(jnp.int32, (GROUPS, TILE), 1)
    ka = (lane >> 1) & (rows_q - 1)
    neg = jnp.full((GROUPS, TILE), NEG_INF, F32)
    qc = lax.broadcasted_iota(jnp.int32, (GRID_W, GRID_W), 0)
    kc = lax.broadcasted_iota(jnp.int32, (GRID_W, GRID_W), 1)
    col0 = jnp.clip(qc - NA_WIN_COLS // 2, 0, GRID_W - NA_WIN_COLS)
    in_win = jnp.logical_and(kc >= col0, kc < col0 + NA_WIN_COLS)
    toeplitz = []
    for r in range(2 * NA_WIN_ROWS - 1):
        row = jnp.broadcast_to(tab_ref[0, r:r + 1, :], (GRID_W, LANES))
        shifted = pltpu.roll(row, LANES - (GRID_W - 1), 1, stride=1, stride_axis=0)
        toeplitz.append(jnp.where(in_win, shifted[:, 0:GRID_W], NEG_INF))

    def interleaved(r0):
        acc = None
        for k in range(rows_q):
            val = toeplitz[r0 + k]
            t1 = val.astype(BF16)
            r1 = val - t1.astype(F32)
            t2 = r1.astype(BF16)
            t3 = (r1 - t2.astype(F32)).astype(BF16)
            for term in (t1, t2, t3):
                moved = jnp.dot(term, perm_ref[k], preferred_element_type=F32)
                acc = moved if acc is None else acc + moved
        return acc

    t4 = [interleaved(r0) for r0 in range(WIN_TILES * rows_q)]
    for a in range(rows_q):
        for qs in range(SUB // rows_q):
            s = (SUB // rows_q) * a + qs
            for kt in range(WIN_TILES):
                src = t4[rows_q * kt - a + 3][qs * GROUPS:(qs + 1) * GROUPS, :]
                c = rows_q * kt + ka
                variants = (
                    src if kt >= 1 else neg,
                    jnp.where(jnp.logical_and(c >= a, c <= a + NA_WIN_ROWS - 1), src, neg),
                    src if kt <= 1 else neg,
                )
                for v, val in enumerate(variants):
                    for half in range(TILE // LANES):
                        scr[v, 2 * kt + half, pl.ds(s, GROUPS, stride=SUB), :] = val[:, half * LANES:(half + 1) * LANES]
    for v in range(BIAS_VARIANTS):
        o_ref[v, 0] = jnp.concatenate([scr[v, j] for j in range(WIN_TILES * TILE // LANES)], axis=1)


def _bias_tiles(table):
    rows_q = TILE // GRID_W
    n_r = 2 * NA_WIN_ROWS - 1
    tab = jnp.pad(table.astype(F32) * LOG2E, ((0, 0), (0, 0), (BIAS_PAD, LANES - BIAS_PAD - (2 * NA_WIN_COLS - 1))))
    n_h = table.shape[0]
    half = GRID_W // GROUPS
    perm = np.zeros((rows_q, GRID_W, TILE), np.float32)
    for k in range(rows_q):
        for ks in range(half):
            for kg in range(GROUPS):
                perm[k, ks * GROUPS + kg, kg * SUB + k * half + ks] = 1.0
    return pl.pallas_call(
        _bias_kernel,
        grid=(n_h,),
        in_specs=[pl.BlockSpec((1, n_r, LANES), lambda h: (h, 0, 0)),
                  pl.BlockSpec((rows_q, GRID_W, TILE), lambda h: (0, 0, 0))],
        out_specs=pl.BlockSpec((BIAS_VARIANTS, 1, TILE, WIN_TILES * TILE), lambda h: (0, h, 0, 0)),
        out_shape=jax.ShapeDtypeStruct((BIAS_VARIANTS, n_h, TILE, WIN_TILES * TILE), F32),
        scratch_shapes=[pltpu.VMEM((BIAS_VARIANTS, WIN_TILES * TILE // LANES, TILE, LANES), F32)],
        compiler_params=_cparams(("parallel",)),
        name="bias_tiles",
    )(tab, jnp.asarray(perm, BF16))


def _block_diag(w, n_chunks):
    per = LRU_BLOCKS // n_chunks
    w = w.reshape(2, n_chunks, per, LRU_BLOCK, LRU_BLOCK)
    eye = jnp.eye(per, dtype=w.dtype)
    out = jnp.einsum('dcpij,pq->dcpiqj', w, eye)
    return out.reshape(2, n_chunks, per * LRU_BLOCK, per * LRU_BLOCK)


def kernel(x, c, ctx, c_ctx, w_mod, b_mod, norm_mix, norm_ffn, w_in, w_out, q_gain, k_gain, na_bias,
           conv_w, conv_b, lru_w_r, lru_b_r, lru_w_i, lru_b_i, lru_lambda, router_w, router_b,
           exp_w1, exp_w3, exp_w2):
    cs = jnp.concatenate([c, c_ctx[None, :], jnp.zeros((MOD_ROWS - BATCH - 1, D_MODEL), F32)], axis=0)
    mod = _modulation(cs, w_mod, b_mod).reshape(DEPTH, MOD_ROWS, N_MOD, D_MODEL)

    head_of = np.arange(NA_WIDTH) // HEAD_DIM
    bd = jnp.asarray((head_of[:, None] == head_of[None, :]).astype(np.float32) / HEAD_DIM, BF16)
    tri = jnp.asarray(np.triu(np.ones((TILE, TILE), np.float32), 1), BF16)
    rwt = router_w.T
    rwh = rwt.astype(BF16)
    rwc = jnp.concatenate([rwh, (rwt - rwh.astype(F32)).astype(BF16)], axis=0)
    rb = router_b.reshape(N_EXPERTS, 1)
    n_cb = LRU_WIDTH // LRU_CH

    bias_tiles = _bias_tiles(na_bias.reshape(DEPTH * NA_HEADS, 2 * NA_WIN_ROWS - 1, 2 * NA_WIN_COLS - 1))
    streams = [{"xs": (x, ctx), "moe": None, "boff": sidx * SB} for sidx in range(STREAMS)]
    for l in range(DEPTH):
        qg = jnp.tile(q_gain[l] * (ATTN_SCALE * LOG2E), NA_HEADS)[None, :]
        kg = jnp.tile(k_gain[l], NA_HEADS)[None, :]
        w_in_l = w_in[l].astype(BF16)
        w_out_l = w_out[l].astype(BF16)
        wr = (0.5 * _block_diag(lru_w_r[l], n_cb)).astype(BF16)
        wi = (0.5 * _block_diag(lru_w_i[l], n_cb)).astype(BF16)
        lru = (conv_w[l], conv_b[l][None, :], wr, wi, 0.5 * lru_b_r[l], 0.5 * lru_b_i[l], lru_lambda[l])
        for st in streams:
            boff = st["boff"]
            xs, proj = _in_proj(l, boff, st["xs"], st["moe"], mod, norm_mix[l][None, :], w_in_l, bd, qg, kg)
            ya, hc = _attention(l, proj, bias_tiles, lru)
            xs, hp, route = _out_proj(l, boff, xs, ya, hc, proj, lru, mod, norm_ffn[l][None, :], w_out_l, rwc, rb)
            st["xs"] = xs
            st["moe"] = _moe(l, hp, route.reshape(N_TOK_TILES, ROUTE_ROWS, TILE), tri, exp_w1, exp_w3, exp_w2)
    return _final(streams, mod)
```

```python
import functools

import jax
import jax.numpy as jnp
import numpy as np
from jax import lax
from jax.experimental import pallas as pl
from jax.experimental.pallas import tpu as pltpu
from jax.experimental.pallas import tpu_sc as plsc

F32 = jnp.float32
BF16 = jnp.bfloat16
I32 = jnp.int32

D_MODEL = 1024
BATCH = 4
SEQ = 8192
DEPTH = 4
GRID_W = 64
CTX_LEN = 256
HEAD_DIM = 64
NA_WIDTH = 512
NA_HEADS = 8
NA_WIN_ROWS = 8
NA_WIN_COLS = 16
LRU_WIDTH = 512
LRU_BLOCKS = 8
LRU_BLOCK = 64
CONV_W = 4
LRU_C = 8.0
IN_COLS = 3 * NA_WIDTH + 2 * LRU_WIDTH
COL_KV, COL_Q, COL_U, COL_G = 0, 2, 3, 4
N_EXPERTS = 16
N_GROUPS = 4
EXPERTS_PER_GROUP = 4
TOP_K = 2
EXPERT_FF = 512
N_MOD = 6
ATTN_SCALE = HEAD_DIM ** -0.5
LOG2E = 1.4426950408889634
EPS = 1e-6
NEG_INF = -1e30
TINY = 1e-30
GELU_C = 0.7978845608028654

TILE = 256
WIN_TILES = 3
BIAS_VARIANTS = 3
HALO = 16
SUB = 8
LANES = 128
GROUPS = TILE // SUB
L_TOT = CTX_LEN + SEQ
N_TILES = L_TOT // TILE
N_LAT_TILES = SEQ // TILE
STREAMS = 2
SB = BATCH // STREAMS
N_TOK = SB * L_TOT
N_TOK_TILES = N_TOK // TILE
PAIR = 2
LRU_CH = 256
MOD_ROWS = 8
VMEM_LIMIT = 56 * 1024 * 1024

PACK_W = D_MODEL // 2
HI_MASK = -65536
N_ASSIGN = TOP_K * N_TOK
EXP_TILE = 1024
EXP_UNIT = 128
N_EXP_TILES = N_ASSIGN // EXP_TILE
N_ITEMS = N_EXP_TILES + N_EXPERTS - 1
BIAS_PAD = GRID_W - NA_WIN_COLS
ROUTE_ROWS = 2 * N_EXPERTS
POS_TILES = 22

SC_CORES = 2
SC_SUBCORES = 16
SC_WORKERS = SC_CORES * SC_SUBCORES
SC_ROWS = N_TOK // SC_WORKERS
SC_CHUNK = 88
SC_N_CHUNKS = SC_ROWS // SC_CHUNK


def _cparams(sem):
    return pltpu.CompilerParams(dimension_semantics=sem, vmem_limit_bytes=VMEM_LIMIT)


def _pack_rows(v):
    lo = pltpu.bitcast(v[:, :PACK_W].astype(BF16).astype(F32), I32)
    hi = pltpu.bitcast(v[:, PACK_W:].astype(BF16).astype(F32), I32)
    return ((lo >> 16) & 0xFFFF) | (hi & HI_MASK)


def _unpack_rows(p):
    return pltpu.bitcast(p << 16, F32), pltpu.bitcast(p & HI_MASK, F32)


def _mod_kernel(c_ref, w_ref, b_ref, o_ref):
    c = c_ref[...]
    s = c * jax.nn.sigmoid(c)
    o_ref[0] = jnp.dot(s.astype(BF16), w_ref[0].astype(BF16), preferred_element_type=F32) + b_ref[0]


def _modulation(cs, w_mod, b_mod):
    return pl.pallas_call(
        _mod_kernel,
        grid=(DEPTH, N_MOD),
        in_specs=[
            pl.BlockSpec((MOD_ROWS, D_MODEL), lambda l, n: (0, 0)),
            pl.BlockSpec((1, D_MODEL, D_MODEL), lambda l, n: (l, 0, n)),
            pl.BlockSpec((1, 1, D_MODEL), lambda l, n: (l, 0, n)),
        ],
        out_specs=pl.BlockSpec((1, MOD_ROWS, D_MODEL), lambda l, n: (l, 0, n)),
        out_shape=jax.ShapeDtypeStruct((DEPTH, MOD_ROWS, N_MOD * D_MODEL), F32),
        compiler_params=_cparams(("arbitrary", "arbitrary")),
        name="modulation",
    )(cs, w_mod, b_mod.reshape(DEPTH, 1, N_MOD * D_MODEL))


def _mod_row(b, i):
    return jnp.where(i == 0, BATCH, b)


def _moe_residual(x, g0, g1, wt, gate_row):
    lo0, hi0 = _unpack_rows(g0)
    lo1, hi1 = _unpack_rows(g1)
    w0 = wt[:, 0:1]
    w1 = wt[:, 1:2]
    f = jnp.concatenate([w0 * lo0 + w1 * lo1, w0 * hi0 + w1 * hi1], axis=1)
    return x + gate_row * f


def _pair_tok_spec(width):
    return pl.BlockSpec((PAIR, TILE, width), lambda bp, i: (bp, i, 0))


def _pair_mod_specs(layer, boff):
    return [pl.BlockSpec((1, 1, N_MOD, D_MODEL),
                         lambda bp, i, k=k: (layer, _mod_row(boff + PAIR * bp + k, i), 0, 0))
            for k in range(PAIR)]


def _full_spec(shape):
    return pl.BlockSpec(shape, lambda bp, i: tuple(0 for _ in shape))


def _to_scan_major(src_ref, scr):
    n_slab = D_MODEL // LANES
    for s in range(SUB):
        for j in range(n_slab):
            scr[j, pl.ds(s, GROUPS, stride=SUB), :] = src_ref[s * GROUPS:(s + 1) * GROUPS, j * LANES:(j + 1) * LANES]
    return jnp.concatenate([scr[j] for j in range(n_slab)], axis=1)


def _from_scan_major(val, scr):
    n_slab = D_MODEL // LANES
    for j in range(n_slab):
        scr[j] = val[:, j * LANES:(j + 1) * LANES]
    blocks = [jnp.concatenate([scr[j, pl.ds(s, GROUPS, stride=SUB), :] for j in range(n_slab)], axis=1)
              for s in range(SUB)]
    return jnp.concatenate(blocks, axis=0)


def _in_kernel(has_prev, *refs):
    if has_prev:
        x_ref, g0_ref, g1_ref, wt_ref = refs[:4]
        mprev_refs = refs[4:4 + PAIR]
        refs = (x_ref,) + refs[4 + PAIR:]
    else:
        x_ref, ctx_ref = refs[:2]
        refs = (x_ref,) + refs[2:]
    mod_refs = refs[1:1 + PAIR]
    nrm_ref, w_ref, bd_ref, qg_ref, kg_ref = refs[1 + PAIR:6 + PAIR]
    outs = refs[6 + PAIR:]
    xo_ref, proj_ref = outs[:2]
    x_ref = refs[0]
    hs = []
    for k in range(PAIR):
        if has_prev:
            x = _moe_residual(x_ref[k], g0_ref[k], g1_ref[k], wt_ref[k], mprev_refs[k][0, 0][5:6, :])
            xo_ref[k] = x
        else:
            perm_scr = outs[2]

            @pl.when(pl.program_id(1) == 0)
            def _():
                xo_ref[k] = _to_scan_major(ctx_ref.at[k], perm_scr)

            @pl.when(pl.program_id(1) > 0)
            def _():
                xo_ref[k] = _to_scan_major(x_ref.at[k], perm_scr)

            x = xo_ref[k]
        m = mod_refs[k][0, 0]
        ms = jnp.mean(x * x, axis=-1, keepdims=True)
        h = (x * lax.rsqrt(ms + EPS)) * nrm_ref[...]
        hs.append((h * (1.0 + m[1:2, :]) + m[0:1, :]).astype(BF16))
    acc = jnp.dot(jnp.concatenate(hs, axis=0), w_ref[...], preferred_element_type=F32)
    bd = bd_ref[...]

    def head_norm(t, gain):
        ss = jnp.dot((t * t).astype(BF16), bd, preferred_element_type=F32)
        return (t * lax.rsqrt(ss + EPS)) * gain

    def put(col_block, val):
        for k in range(PAIR):
            proj_ref[k, :, col_block * NA_WIDTH:(col_block + 1) * NA_WIDTH] = val[k * TILE:(k + 1) * TILE, :].astype(BF16)

    put(COL_Q, head_norm(acc[:, 0:NA_WIDTH], qg_ref[...]))
    put(COL_KV, head_norm(acc[:, NA_WIDTH:2 * NA_WIDTH], kg_ref[...]))
    put(COL_KV + 1, acc[:, 2 * NA_WIDTH:3 * NA_WIDTH])
    put(COL_U, acc[:, 3 * NA_WIDTH:3 * NA_WIDTH + LRU_WIDTH])
    put(COL_G, acc[:, 3 * NA_WIDTH + LRU_WIDTH:])


def _in_proj(l, boff, xs, moe_prev, mod, nrm, w_in, bd, qg, kg):
    has_prev = moe_prev is not None
    if has_prev:
        in_specs = [_pair_tok_spec(D_MODEL)]
        args = [xs]
        in_specs += [_pair_tok_spec(PACK_W), _pair_tok_spec(PACK_W), _pair_tok_spec(LANES)]
        in_specs += _pair_mod_specs(l - 1, boff)
        args += list(moe_prev) + [mod] * PAIR
    else:
        pair0 = boff // PAIR
        in_specs = [pl.BlockSpec((PAIR, TILE, D_MODEL), lambda bp, i: (pair0 + bp, jnp.maximum(i - 1, 0), 0)),
                    pl.BlockSpec((PAIR, TILE, D_MODEL), lambda bp, i: (pair0 + bp, 0, 0))]
        args = list(xs)
    in_specs += _pair_mod_specs(l, boff) + [_full_spec((1, D_MODEL)), _full_spec((D_MODEL, IN_COLS)),
                                            _full_spec((NA_WIDTH, NA_WIDTH)), _full_spec((1, NA_WIDTH)),
                                            _full_spec((1, NA_WIDTH))]
    args += [mod] * PAIR + [nrm, w_in, bd, qg, kg]
    out_shape = [jax.ShapeDtypeStruct((SB, L_TOT, D_MODEL), F32), jax.ShapeDtypeStruct((SB, L_TOT, IN_COLS), BF16)]
    out_specs = [_pair_tok_spec(D_MODEL), _pair_tok_spec(IN_COLS)]
    scratch = [] if has_prev else [pltpu.VMEM((D_MODEL // LANES, TILE, LANES), F32)]
    outs = pl.pallas_call(
        functools.partial(_in_kernel, has_prev),
        grid=(SB // PAIR, N_TILES),
        in_specs=in_specs,
        out_specs=out_specs,
        out_shape=out_shape,
        scratch_shapes=scratch,
        compiler_params=_cparams(("parallel", "arbitrary")),
        name="in_proj",
    )(*args)
    return outs[0], outs[1]


def _attn_kernel(q_ref, kvp_ref, kvc_ref, kvn_ref, kvx_ref, bias_ref,
                 u_ref, up_ref, un_ref, cw_ref, cb_ref, wr_ref, wi_ref, br_ref, bi_ref, lam_ref,
                 o_ref, hc_ref, h_scr, hl_scr, p_scr):
    j = pl.program_id(1)

    @pl.when(j == 0)
    def _():
        h_scr[...] = jnp.zeros_like(h_scr)

    has_prev = jnp.where(j >= 2, 1.0, 0.0).astype(F32)
    has_next = jnp.where(jnp.logical_and(j >= 1, j <= N_TILES - 2), 1.0, 0.0).astype(F32)

    def forward_lru():
        for b in range(PAIR):
            cv = _lru_conv(u_ref[b].astype(F32), up_ref[b].astype(F32), un_ref[b].astype(F32), has_prev, has_next,
                           cw_ref[...], cb_ref[...])
            hc_ref[b, :, LRU_WIDTH:] = cv.astype(BF16)
            a_f, b_f = _lru_gates(cv, 0, wr_ref, wi_ref, br_ref, bi_ref, lam_ref)
            hfull, h_last = _lru_scan(a_f, b_f, h_scr[b], False, hl_scr, p_scr)
            hc_ref[b, :, 0:LRU_WIDTH] = hfull.astype(BF16)
            h_scr[b] = h_last

    pair_w = 2 * HEAD_DIM
    lane = lax.broadcasted_iota(jnp.int32, (TILE, pair_w), 1)
    low = lane < HEAD_DIM
    nt = (((1,), (1,)), ((), ()))
    n_win = WIN_TILES * TILE

    def attend(kv_refs, windowed):
        low_kv = lax.broadcasted_iota(jnp.int32, (len(kv_refs) * TILE, pair_w), 1) < HEAD_DIM
        for b, hp in [(b, hp) for b in range(PAIR) for hp in range(NA_HEADS // 2)]:
            cols = slice(hp * pair_w, (hp + 1) * pair_w)
            vcols = slice(NA_WIDTH + hp * pair_w, NA_WIDTH + (hp + 1) * pair_w)
            q = q_ref[b, :, cols]
            kb = jnp.concatenate([r[b, :, cols] for r in kv_refs], axis=0)
            vb = jnp.concatenate([r[b, :, vcols] for r in kv_refs], axis=0)
            outs = []
            for hh in range(2):
                own = low if hh == 0 else jnp.logical_not(low)
                own_kv = low_kv if hh == 0 else jnp.logical_not(low_kv)
                qh = jnp.where(own, q, jnp.zeros_like(q))
                s = lax.dot_general(qh, kb, nt, preferred_element_type=F32)
                if windowed:
                    s_win = s[:, 0:n_win] + bias_ref[0, 2 * hp + hh]
                    s_ctx = s[:, n_win:]
                    m = jnp.maximum(jnp.max(s_win, axis=-1, keepdims=True), jnp.max(s_ctx, axis=-1, keepdims=True))
                    p = jnp.concatenate([jnp.exp2((s_win - m).astype(BF16)), jnp.exp2((s_ctx - m).astype(BF16))],
                                        axis=1)
                else:
                    p = jnp.exp2((s - jnp.max(s, axis=-1, keepdims=True)).astype(BF16))
                va = jnp.where(own_kv, vb, jnp.ones_like(vb))
                o = jnp.dot(p, va, preferred_element_type=F32)
                outs.append(o / pltpu.roll(o, HEAD_DIM, 1))
            o_ref[b, :, cols] = jnp.where(low, outs[0], outs[1]).astype(BF16)

    @pl.when(j == 0)
    def _():
        forward_lru()
        attend((kvx_ref,), False)

    @pl.when(j > 0)
    def _():
        forward_lru()
        attend((kvp_ref, kvc_ref, kvn_ref, kvx_ref), True)


def _attention(l, proj, bias_tiles, lru):
    last = N_LAT_TILES - 1
    blk = (PAIR, TILE, 2 * NA_WIDTH)
    prev_map = lambda b, j: (b, 1 + jnp.clip(j - 2, 0, last), COL_KV // 2)
    cur_map = lambda b, j: (b, jnp.maximum(j, 1), COL_KV // 2)
    next_map = lambda b, j: (b, 1 + jnp.clip(j, 0, last), COL_KV // 2)
    ctx_map = lambda b, j: (b, 0, COL_KV // 2)
    var_map = lambda b, j: (jnp.where(j <= 1, 0, jnp.where(j == N_TILES - 1, 2, 1)), l, 0, 0)
    kv_specs = [pl.BlockSpec(blk, prev_map), pl.BlockSpec(blk, cur_map), pl.BlockSpec(blk, next_map),
                pl.BlockSpec(blk, ctx_map)]
    tok = lambda col: pl.BlockSpec((PAIR, TILE, NA_WIDTH), lambda b, j: (b, j, col))
    halo = TILE // HALO
    n_halo = L_TOT // HALO
    prev16 = pl.BlockSpec((PAIR, HALO, LRU_WIDTH), lambda b, j: (b, jnp.maximum(j * halo - 1, 0), COL_U))
    next16 = pl.BlockSpec((PAIR, HALO, LRU_WIDTH), lambda b, j: (b, jnp.minimum((j + 1) * halo, n_halo - 1), COL_U))
    whole = lambda shape: pl.BlockSpec(shape, lambda b, j: tuple(0 for _ in shape))
    n_cb = LRU_WIDTH // LRU_CH
    lru_specs = [whole((CONV_W, LRU_WIDTH)), whole((1, LRU_WIDTH)), whole((2, n_cb, LRU_CH, LRU_CH)),
                 whole((2, n_cb, LRU_CH, LRU_CH)), whole((2, LRU_WIDTH)), whole((2, LRU_WIDTH)), whole((2, LRU_WIDTH))]
    return pl.pallas_call(
        _attn_kernel,
        grid=(SB // PAIR, N_TILES),
        in_specs=[tok(COL_Q)] + kv_specs + [pl.BlockSpec((1, NA_HEADS, TILE, WIN_TILES * TILE), var_map)]
        + [tok(COL_U), prev16, next16] + lru_specs,
        out_specs=[tok(0), pl.BlockSpec((PAIR, TILE, 2 * LRU_WIDTH), lambda b, j: (b, j, 0))],
        out_shape=[jax.ShapeDtypeStruct((SB, L_TOT, NA_WIDTH), BF16),
                   jax.ShapeDtypeStruct((SB, L_TOT, 2 * LRU_WIDTH), BF16)],
        scratch_shapes=[pltpu.VMEM((PAIR, 1, LRU_WIDTH), F32), pltpu.VMEM((TILE, LRU_WIDTH), F32),
                        pltpu.VMEM((TILE, LRU_WIDTH), F32)],
        compiler_params=_cparams(("parallel", "arbitrary")),
        name="na_attention",
    )(proj, proj, proj, proj, proj, bias_tiles, proj, proj, proj, *lru)


def _softplus(x):
    return jnp.maximum(x, 0.0) + jnp.log1p(jnp.exp(-jnp.abs(x)))


def _lru_conv(u, prev16, next16, has_prev, has_next, cw, cb):
    sub = lax.broadcasted_iota(jnp.int32, (SUB, u.shape[1]), 0)
    prow = prev16[HALO - 1:HALO, :] * has_prev
    n0 = next16[0:1, :] * has_next
    n8 = next16[8:9, :] * has_next
    first8 = jnp.where(sub == 0, prow, pltpu.roll(u[TILE - SUB:TILE, :], 1, 0))
    last_a = jnp.where(sub == SUB - 1, n0, pltpu.roll(u[0:SUB, :], SUB - 1, 0))
    last_b = jnp.where(sub == SUB - 1, n8, pltpu.roll(u[SUB:2 * SUB, :], SUB - 1, 0))
    um1 = jnp.concatenate([first8, u[0:TILE - SUB, :]], axis=0)
    up1 = jnp.concatenate([u[SUB:TILE, :], last_a], axis=0)
    up2 = jnp.concatenate([u[2 * SUB:TILE, :], last_a, last_b], axis=0)
    return cw[0:1, :] * um1 + cw[1:2, :] * u + cw[2:3, :] * up1 + cw[3:4, :] * up2 + cb


def _lru_gates(v, d, wr_ref, wi_ref, br_ref, bi_ref, lam_ref):
    vb = v.astype(BF16)
    n_cb = LRU_WIDTH // LRU_CH

    def gate(w_ref, b_ref):
        z = [jnp.dot(vb[:, c * LRU_CH:(c + 1) * LRU_CH], w_ref[d, c], preferred_element_type=F32) for c in range(n_cb)]
        return jnp.tanh(jnp.concatenate(z, axis=1) + b_ref[d:d + 1, :])

    tr = gate(wr_ref, br_ref)
    ti = gate(wi_ref, bi_ref)
    half = (-0.5 * LRU_C * LOG2E) * _softplus(-lam_ref[d:d + 1, :])
    a = jnp.exp2(half + half * tr)
    om = 1.0 - a * a
    b = (om * lax.rsqrt(jnp.maximum(om, TINY))) * ((0.5 * v) * (1.0 + ti))
    return a, b


def _lru_scan(a, b, h_in, reverse, hl_scr, p_scr):
    order = range(GROUPS - 1, -1, -1) if reverse else range(GROUPS)
    hl = None
    for g in order:
        ag = a[g * SUB:(g + 1) * SUB, :]
        bg = b[g * SUB:(g + 1) * SUB, :]
        if hl is None:
            hl, p = bg, ag
        else:
            hl = ag * hl + bg
            p = ag * p
        hl_scr[g * SUB:(g + 1) * SUB, :] = hl
        p_scr[g * SUB:(g + 1) * SUB, :] = p
    blocks = range(SUB - 1, -1, -1) if reverse else range(SUB)
    carry = h_in
    cins = {}
    for s in blocks:
        cins[s] = carry
        carry = hl[s:s + 1, :] + p[s:s + 1, :] * carry
    cin = jnp.concatenate([cins[s] for s in range(SUB)], axis=0)
    hfull = hl_scr[...] + p_scr[...] * jnp.tile(cin, (GROUPS, 1))
    return hfull, carry


def _route(sel, aff):
    def top2_sum(a, b, c, d):
        hi1, lo1 = jnp.maximum(a, b), jnp.minimum(a, b)
        hi2, lo2 = jnp.maximum(c, d), jnp.minimum(c, d)
        return jnp.maximum(hi1, hi2) + jnp.maximum(jnp.minimum(hi1, hi2), jnp.maximum(lo1, lo2))

    scores = [top2_sum(*sel[EXPERTS_PER_GROUP * g:EXPERTS_PER_GROUP * (g + 1)]) for g in range(N_GROUPS)]
    best = jnp.zeros_like(scores[0], dtype=jnp.int32)
    best_v = scores[0]
    for g in range(1, N_GROUPS):
        upd = scores[g] > best_v
        best = jnp.where(upd, g, best)
        best_v = jnp.where(upd, scores[g], best_v)
    chosen = []
    for e in range(N_EXPERTS):
        g = e // EXPERTS_PER_GROUP
        rank = jnp.zeros_like(best)
        for o in range(EXPERTS_PER_GROUP * g, EXPERTS_PER_GROUP * (g + 1)):
            if o == e:
                continue
            ahead = sel[o] > sel[e]
            if o < e:
                ahead = jnp.logical_or(ahead, sel[o] == sel[e])
            rank = rank + ahead.astype(jnp.int32)
        chosen.append(jnp.logical_and(best == g, rank < TOP_K))
    total = jnp.zeros_like(aff[0])
    for e in range(N_EXPERTS):
        total = total + jnp.where(chosen[e], aff[e], 0.0)
    gates = [jnp.where(chosen[e], aff[e] / total, 0.0) for e in range(N_EXPERTS)]
    return gates, [c.astype(F32) for c in chosen]


def _out_kernel(x_ref, ya_ref, hc_ref, g_ref, *refs):
    mod_refs = refs[:PAIR]
    (wr_ref, wi_ref, br_ref, bi_ref, lam_ref, nrm_ref, w_ref, rwc_ref, rb_ref,
     xo_ref, hp_ref, rt_ref, h_scr, hl_scr, p_scr) = refs[PAIR:]
    rows = PAIR * TILE

    @pl.when(pl.program_id(1) == 0)
    def _():
        h_scr[...] = jnp.zeros_like(h_scr)

    ybs = []
    for k in range(PAIR):
        a_r, b_r = _lru_gates(hc_ref[k, :, LRU_WIDTH:].astype(F32), 1, wr_ref, wi_ref, br_ref, bi_ref, lam_ref)
        hrev, h_last = _lru_scan(a_r, b_r, h_scr[k], True, hl_scr, p_scr)
        h_scr[k] = h_last
        gx = g_ref[k].astype(F32)
        gate = (0.5 * gx) * (1.0 + jnp.tanh(gx * (GELU_C + (GELU_C * 0.044715) * (gx * gx))))
        ybs.append((gate * (hc_ref[k, :, 0:LRU_WIDTH].astype(F32) + hrev)).astype(BF16))
    ya = ya_ref[...].reshape(rows, NA_WIDTH)
    yb = jnp.concatenate(ybs, axis=0)
    y = jnp.dot(ya, w_ref[0:NA_WIDTH, :], preferred_element_type=F32)
    y = y + jnp.dot(yb, w_ref[NA_WIDTH:, :], preferred_element_type=F32)
    hs = []
    for k in range(PAIR):
        m = mod_refs[k][0, 0]
        x = x_ref[k] + m[2:3, :] * y[k * TILE:(k + 1) * TILE, :]
        xo_ref[k] = x
        ms = jnp.mean(x * x, axis=-1, keepdims=True)
        h = (x * lax.rsqrt(ms + EPS)) * nrm_ref[...]
        h = h * (1.0 + m[4:5, :]) + m[3:4, :]
        hp_ref[k] = _pack_rows(h)
        hs.append(h)
    h = jnp.concatenate(hs, axis=0)
    h_hi = h.astype(BF16)
    h_lo = (h - h_hi.astype(F32)).astype(BF16)
    nt = (((1,), (1,)), ((), ()))
    rwc = rwc_ref[...]
    both = lax.dot_general(rwc, h_hi, nt, preferred_element_type=F32)
    lg = (both[0:N_EXPERTS, :] + both[N_EXPERTS:, :]
          + lax.dot_general(rwc[0:N_EXPERTS, :], h_lo, nt, preferred_element_type=F32))
    aff_all = jax.nn.sigmoid(lg)
    sel_all = aff_all + rb_ref[...]
    aff = [aff_all[e:e + 1, :] for e in range(N_EXPERTS)]
    sel = [sel_all[e:e + 1, :] for e in range(N_EXPERTS)]
    gates, chosen = _route(sel, aff)
    rt = jnp.concatenate(gates + chosen, axis=0)
    for k in range(PAIR):
        rt_ref[k, 0] = rt[:, k * TILE:(k + 1) * TILE]


def _out_proj(l, boff, xs, ya, hc, proj, lru, mod, nrm, w_out, rwc, rb):
    rev = lambda i: jnp.where(i == 0, 0, N_TILES - i)
    tok = lambda width, col=0: pl.BlockSpec((PAIR, TILE, width), lambda bp, i: (bp, rev(i), col))
    mod_specs = [pl.BlockSpec((1, 1, N_MOD, D_MODEL),
                              lambda bp, i, k=k: (l, _mod_row(boff + PAIR * bp + k, i), 0, 0)) for k in range(PAIR)]
    n_cb = LRU_WIDTH // LRU_CH
    lru_specs = [_full_spec((2, n_cb, LRU_CH, LRU_CH)), _full_spec((2, n_cb, LRU_CH, LRU_CH)),
                 _full_spec((2, LRU_WIDTH)), _full_spec((2, LRU_WIDTH)), _full_spec((2, LRU_WIDTH))]
    return pl.pallas_call(
        _out_kernel,
        grid=(SB // PAIR, N_TILES),
        in_specs=[tok(D_MODEL), tok(NA_WIDTH), tok(2 * LRU_WIDTH), tok(LRU_WIDTH, COL_G)]
        + mod_specs + lru_specs
        + [_full_spec((1, D_MODEL)), _full_spec((D_MODEL, D_MODEL)), _full_spec((2 * N_EXPERTS, D_MODEL)),
           _full_spec((N_EXPERTS, 1))],
        out_specs=[tok(D_MODEL), tok(PACK_W),
                   pl.BlockSpec((PAIR, 1, ROUTE_ROWS, TILE), lambda bp, i: (bp, rev(i), 0, 0))],
        out_shape=[jax.ShapeDtypeStruct((SB, L_TOT, D_MODEL), F32),
                   jax.ShapeDtypeStruct((SB, L_TOT, PACK_W), I32),
                   jax.ShapeDtypeStruct((SB, N_TILES, ROUTE_ROWS, TILE), F32)],
        scratch_shapes=[pltpu.VMEM((PAIR, 1, LRU_WIDTH), F32), pltpu.VMEM((TILE, LRU_WIDTH), F32),
                        pltpu.VMEM((TILE, LRU_WIDTH), F32)],
        compiler_params=_cparams(("parallel", "arbitrary")),
        name="out_proj_router",
    )(xs, ya, hc, proj, *([mod] * PAIR), *lru[2:], nrm, w_out, rwc, rb)


def _pos_kernel(rt_ref, tri_ref, start_ref, pos_ref, wt_ref, run_scr):
    @pl.when(pl.program_id(0) == 0)
    def _():
        run_scr[...] = jnp.zeros_like(run_scr)

    base = start_ref[...] + run_scr[...]
    for k in range(POS_TILES):
        gates = rt_ref[k, 0:N_EXPERTS, :]
        chosen = rt_ref[k, N_EXPERTS:, :]
        rank = jnp.dot(chosen.astype(BF16), tri_ref[...], preferred_element_type=F32)
        posf = rank + base
        seen = jnp.zeros((1, TILE), F32)
        p0 = jnp.zeros((1, TILE), F32)
        p1 = jnp.zeros((1, TILE), F32)
        w0 = jnp.zeros((1, TILE), F32)
        w1 = jnp.zeros((1, TILE), F32)
        for e in range(N_EXPERTS):
            ch = chosen[e:e + 1, :]
            first = ch * (1.0 - seen)
            second = ch * seen
            p0 = p0 + first * posf[e:e + 1, :]
            p1 = p1 + second * posf[e:e + 1, :]
            w0 = w0 + first * gates[e:e + 1, :]
            w1 = w1 + second * gates[e:e + 1, :]
            seen = jnp.minimum(seen + ch, 1.0)
        pos_ref[k] = jnp.concatenate([p0, p1], axis=0).astype(I32)
        wpad = jnp.concatenate([w0, w1, jnp.zeros((LANES - TOP_K, TILE), F32)], axis=0)
        wt_ref[k * TILE:(k + 1) * TILE, :] = jnp.transpose(wpad)
        base = base + jnp.sum(chosen, axis=1, keepdims=True)
    run_scr[...] = base - start_ref[...]


def _positions(route, tri, start):
    return pl.pallas_call(
        _pos_kernel,
        grid=(N_TOK_TILES // POS_TILES,),
        in_specs=[pl.BlockSpec((POS_TILES, ROUTE_ROWS, TILE), lambda i: (i, 0, 0)),
                  pl.BlockSpec((TILE, TILE), lambda i: (0, 0)),
                  pl.BlockSpec((N_EXPERTS, 1), lambda i: (0, 0))],
        out_specs=[pl.BlockSpec((POS_TILES, TOP_K, TILE), lambda i: (i, 0, 0)),
                   pl.BlockSpec((POS_TILES * TILE, LANES), lambda i: (i, 0))],
        out_shape=[jax.ShapeDtypeStruct((N_TOK_TILES, TOP_K, TILE), I32),
                   jax.ShapeDtypeStruct((N_TOK, LANES), F32)],
        scratch_shapes=[pltpu.VMEM((N_EXPERTS, 1), F32)],
        compiler_params=_cparams(("arbitrary",)),
        name="moe_positions",
    )(route, tri, start)


def _work_items(counts):
    smem = pl.BlockSpec(memory_space=pltpu.SMEM)
    item = jax.ShapeDtypeStruct((N_ITEMS,), I32)
    return pl.pallas_call(
        _items_kernel,
        in_specs=[smem],
        out_specs=[smem] * 4,
        out_shape=[item] * 4,
        name="moe_work_items",
    )(counts)


def _items_kernel(cnt_ref, tile_ref, exp_ref, lo_ref, hi_ref):
    n = jnp.int32(0)
    start = jnp.int32(0)
    last_e = jnp.int32(0)
    for e in range(N_EXPERTS):
        cnt = cnt_ref[e]
        end = start + cnt
        first = start // EXP_TILE
        n_tiles = jnp.where(cnt > 0, (end - 1) // EXP_TILE - first + 1, 0)

        def put(j, carry, e=e, n=n, start=start, end=end, first=first):
            tile = first + j
            tile_ref[n + j] = tile
            exp_ref[n + j] = jnp.int32(e)
            lo_ref[n + j] = jnp.maximum(start - tile * EXP_TILE, 0)
            hi_ref[n + j] = jnp.minimum(end - tile * EXP_TILE, EXP_TILE)
            return carry

        lax.fori_loop(0, n_tiles, put, 0)
        n = n + n_tiles
        start = end
        last_e = jnp.where(cnt > 0, e, last_e)

    def pad(j, carry):
        tile_ref[j] = jnp.int32(N_EXP_TILES - 1)
        exp_ref[j] = last_e
        lo_ref[j] = jnp.int32(0)
        hi_ref[j] = jnp.int32(0)
        return carry

    lax.fori_loop(n, N_ITEMS, pad, 0)


def _sc_worker_base():
    return (lax.axis_index("s") * SC_CORES + lax.axis_index("c")) * SC_ROWS


def _sc_dispatch_body(h_hbm, p0_hbm, p1_hbm, out_hbm, i0_v, i1_v, rows_v, sem):
    base = _sc_worker_base()

    @pl.loop(0, SC_N_CHUNKS)
    def _(j):
        off = base + j * SC_CHUNK
        pltpu.sync_copy(h_hbm.at[pl.ds(off, SC_CHUNK)], rows_v)
        pltpu.sync_copy(p0_hbm.at[pl.ds(off, SC_CHUNK)], i0_v)
        pltpu.sync_copy(p1_hbm.at[pl.ds(off, SC_CHUNK)], i1_v)
        c0 = pltpu.async_copy(rows_v, out_hbm.at[i0_v], sem)
        c1 = pltpu.async_copy(rows_v, out_hbm.at[i1_v], sem)
        c0.wait()
        c1.wait()


def _sc_combine_body(y_hbm, p0_hbm, p1_hbm, g0_hbm, g1_hbm, i_v, rows_v, sem):
    base = _sc_worker_base()

    @pl.loop(0, SC_N_CHUNKS)
    def _(j):
        off = base + j * SC_CHUNK
        for p_hbm, g_hbm in ((p0_hbm, g0_hbm), (p1_hbm, g1_hbm)):
            pltpu.sync_copy(p_hbm.at[pl.ds(off, SC_CHUNK)], i_v)
            pltpu.async_copy(y_hbm.at[i_v], rows_v, sem).wait()
            pltpu.sync_copy(rows_v, g_hbm.at[pl.ds(off, SC_CHUNK)])


def _sc_mesh():
    return plsc.VectorSubcoreMesh(core_axis_name="c", subcore_axis_name="s")


def _dispatch(hp, pos0, pos1):
    return pl.kernel(
        _sc_dispatch_body,
        out_type=jax.ShapeDtypeStruct((N_ASSIGN, PACK_W), I32),
        mesh=_sc_mesh(),
        scratch_types=[pltpu.VMEM((SC_CHUNK,), I32), pltpu.VMEM((SC_CHUNK,), I32),
                       pltpu.VMEM((SC_CHUNK, PACK_W), I32), pltpu.SemaphoreType.DMA],
        name="moe_dispatch",
    )(hp, pos0, pos1)


def _combine(ys, pos0, pos1):
    row = jax.ShapeDtypeStruct((N_TOK, PACK_W), I32)
    return pl.kernel(
        _sc_combine_body,
        out_type=[row, row],
        mesh=_sc_mesh(),
        scratch_types=[pltpu.VMEM((SC_CHUNK,), I32), pltpu.VMEM((SC_CHUNK, PACK_W), I32),
                       pltpu.SemaphoreType.DMA],
        name="moe_combine",
    )(ys, pos0, pos1)


def _expert_kernel(tile_ref, exp_ref, lo_ref, hi_ref, sched_ref, x_ref, w1_ref, w3_ref, w2_ref, o_ref,
                   w13_scr, w2_scr):
    del sched_ref
    i = pl.program_id(0)
    prev = jnp.maximum(i - 1, 0)

    @pl.when(jnp.logical_or(i == 0, exp_ref[i] != exp_ref[prev]))
    def _():
        w13_scr[:, 0:EXPERT_FF] = w1_ref[0, 0].astype(BF16)
        w13_scr[:, EXPERT_FF:] = w3_ref[0, 0].astype(BF16)
        w2_scr[...] = w2_ref[0, 0].astype(BF16)

    lo = lo_ref[i]
    hi = hi_ref[i]
    revisit = jnp.logical_and(i > 0, tile_ref[i] == tile_ref[prev])
    first_visit = jnp.logical_not(revisit)

    def run(r0, n):
        rows = pl.ds(r0, n)
        lo_f, hi_f = _unpack_rows(x_ref[rows, :])
        h = jnp.concatenate([lo_f.astype(BF16), hi_f.astype(BF16)], axis=1)
        ab = jnp.dot(h, w13_scr[...], preferred_element_type=F32)
        a = ab[:, 0:EXPERT_FF]
        b = ab[:, EXPERT_FF:]
        t = ((a * jax.nn.sigmoid(a)) * b).astype(BF16)
        y = _pack_rows(jnp.dot(t, w2_scr[...], preferred_element_type=F32))
        row = r0 + lax.broadcasted_iota(jnp.int32, (n, PACK_W), 0)
        mine = jnp.logical_and(row >= lo, row < hi)

        @pl.when(first_visit)
        def _():
            o_ref[rows, :] = jnp.where(mine, y, 0)

        @pl.when(revisit)
        def _():
            o_ref[rows, :] = jnp.where(mine, y, o_ref[rows, :])

    first_q = lo // EXP_UNIT
    n_q = jnp.where(hi > lo, (hi + EXP_UNIT - 1) // EXP_UNIT - first_q, 0)
    r0 = pl.multiple_of(first_q * EXP_UNIT, EXP_UNIT)
    for k in range(1, EXP_TILE // EXP_UNIT + 1):
        @pl.when(n_q == k)
        def _(k=k):
            n = k * EXP_UNIT
            run(r0 if n < EXP_TILE else 0, n)
            if n < EXP_TILE:
                @pl.when(first_visit)
                def _():
                    o_ref[n:, :] = jnp.zeros((EXP_TILE - n, PACK_W), I32)


def _weight_schedule(e):
    idx = jnp.arange(N_ITEMS, dtype=I32)
    later = jnp.logical_not(jnp.concatenate([jnp.ones((1,), jnp.bool_), e[1:] != e[:-1]]))
    run_end = jnp.sum((e[None, :] <= e[:, None]).astype(I32), axis=1)
    nxt = jnp.min(jnp.where(e[None, :] > e[:, None], e[None, :], N_EXPERTS), axis=1)
    nxt = jnp.where(nxt == N_EXPERTS, e, nxt)
    return jnp.stack([jnp.where(jnp.logical_and(later, idx + k >= run_end), nxt, e) for k in (1, 2, 3)])


def _experts(l, xs_sorted, items, w1, w3, w2):
    tile, e, lo, hi = items
    grid_spec = pltpu.PrefetchScalarGridSpec(
        num_scalar_prefetch=5,
        grid=(N_ITEMS,),
        in_specs=[pl.BlockSpec((EXP_TILE, PACK_W), lambda i, t, e, lo, hi, ws: (t[i], 0)),
                  pl.BlockSpec((1, 1, D_MODEL, EXPERT_FF), lambda i, t, e, lo, hi, ws: (l, ws[2, i], 0, 0)),
                  pl.BlockSpec((1, 1, D_MODEL, EXPERT_FF), lambda i, t, e, lo, hi, ws: (l, ws[1, i], 0, 0)),
                  pl.BlockSpec((1, 1, EXPERT_FF, D_MODEL), lambda i, t, e, lo, hi, ws: (l, ws[0, i], 0, 0))],
        out_specs=pl.BlockSpec((EXP_TILE, PACK_W), lambda i, t, e, lo, hi, ws: (t[i], 0)),
        scratch_shapes=[pltpu.VMEM((D_MODEL, 2 * EXPERT_FF), BF16), pltpu.VMEM((EXPERT_FF, D_MODEL), BF16)],
    )
    return pl.pallas_call(
        _expert_kernel,
        grid_spec=grid_spec,
        out_shape=jax.ShapeDtypeStruct((N_ASSIGN, PACK_W), I32),
        compiler_params=_cparams(("arbitrary",)),
        name="moe_experts",
    )(tile, e, lo, hi, _weight_schedule(e), xs_sorted, w1, w3, w2)


def _moe(l, hp, route, tri, w1, w3, w2):
    counts = jnp.sum(route[:, N_EXPERTS:, :], axis=(0, 2))
    start = (jnp.cumsum(counts) - counts).reshape(N_EXPERTS, 1)
    pos, wt = _positions(route, tri, start)
    pos0 = pos[:, 0, :].reshape(N_TOK)
    pos1 = pos[:, 1, :].reshape(N_TOK)
    items = _work_items(counts.astype(I32))
    xs_sorted = _dispatch(hp.reshape(N_TOK, PACK_W), pos0, pos1)
    ys = _experts(l, xs_sorted, items, w1, w3, w2)
    g0, g1 = _combine(ys, pos0, pos1)
    shape = (SB, L_TOT, PACK_W)
    return g0.reshape(shape), g1.reshape(shape), wt.reshape(SB, L_TOT, LANES)


def _final_kernel(x_ref, g0_ref, g1_ref, wt_ref, mod_ref, *refs):
    o_ref, scr = refs[-2:]
    for b in range(SB):
        x = _moe_residual(x_ref[b], g0_ref[b], g1_ref[b], wt_ref[b], mod_ref[0, b][5:6, :])
        o_ref[b] = _from_scan_major(x, scr)


def _final(streams, mod):
    lat = lambda width: pl.BlockSpec((SB, TILE, width), lambda j: (0, j + 1, 0))
    out = None
    for k, st in enumerate(streams):
        in_specs = [lat(D_MODEL), lat(PACK_W), lat(PACK_W), lat(LANES),
                    pl.BlockSpec((1, SB, N_MOD, D_MODEL), lambda j, k=k: (DEPTH - 1, k, 0, 0))]
        args = [st["xs"]] + list(st["moe"]) + [mod]
        aliases = {}
        if out is not None:
            in_specs.append(pl.BlockSpec(memory_space=pl.ANY))
            args.append(out)
            aliases = {len(args) - 1: 0}
        out = pl.pallas_call(
            _final_kernel,
            grid=(N_LAT_TILES,),
            in_specs=in_specs,
            out_specs=pl.BlockSpec((SB, TILE, D_MODEL), lambda j, k=k: (k, j, 0)),
            out_shape=jax.ShapeDtypeStruct((BATCH, SEQ, D_MODEL), F32),
            scratch_shapes=[pltpu.VMEM((D_MODEL // LANES, TILE, LANES), F32)],
            input_output_aliases=aliases,
            compiler_params=_cparams(("arbitrary",)),
            name="final_residual",
        )(*args)
    return out


def _bias_kernel(tab_ref, perm_ref, o_ref, scr):
    rows_q = TILE // GRID_W
    lane = lax.broadcasted_iota(jnp.int32, (GROUPS, TILE), 1)
    ka = (lane >> 1) & (rows_q - 1)
    neg = jnp.full((GROUPS, TILE), NEG_INF, F32)
    qc = lax.broadcasted_iota(jnp.int32, (GRID_W, GRID_W), 0)
    kc = lax.broadcasted_iota(jnp.int32, (GRID_W, GRID_W), 1)
    col0 = jnp.clip(qc - NA_WIN_COLS // 2, 0, GRID_W - NA_WIN_COLS)
    in_win = jnp.logical_and(kc >= col0, kc < col0 + NA_WIN_COLS)
    toeplitz = []
    for r in range(2 * NA_WIN_ROWS - 1):
        row = jnp.broadcast_to(tab_ref[0, r:r + 1, :], (GRID_W, LANES))
        shifted = pltpu.roll(row, LANES - (GRID_W - 1), 1, stride=1, stride_axis=0)
        toeplitz.append(jnp.where(in_win, shifted[:, 0:GRID_W], NEG_INF))

    def interleaved(r0):
        acc = None
        for k in range(rows_q):
            val = toeplitz[r0 + k]
            t1 = val.astype(BF16)
            r1 = val - t1.astype(F32)
            t2 = r1.astype(BF16)
            t3 = (r1 - t2.astype(F32)).astype(BF16)
            for term in (t1, t2, t3):
                moved = jnp.dot(term, perm_ref[k], preferred_element_type=F32)
                acc = moved if acc is None else acc + moved
        return acc

    t4 = [interleaved(r0) for r0 in range(WIN_TILES * rows_q)]
    for a in range(rows_q):
        for qs in range(SUB // rows_q):
            s = (SUB // rows_q) * a + qs
            for kt in range(WIN_TILES):
                src = t4[rows_q * kt - a + 3][qs * GROUPS:(qs + 1) * GROUPS, :]
                c = rows_q * kt + ka
                variants = (
                    src if kt >= 1 else neg,
                    jnp.where(jnp.logical_and(c >= a, c <= a + NA_WIN_ROWS - 1), src, neg),
                    src if kt <= 1 else neg,
                )
                for v, val in enumerate(variants):
                    for half in range(TILE // LANES):
                        scr[v, 2 * kt + half, pl.ds(s, GROUPS, stride=SUB), :] = val[:, half * LANES:(half + 1) * LANES]
    for v in range(BIAS_VARIANTS):
        o_ref[v, 0] = jnp.concatenate([scr[v, j] for j in range(WIN_TILES * TILE // LANES)], axis=1)


def _bias_tiles(table):
    rows_q = TILE // GRID_W
    n_r = 2 * NA_WIN_ROWS - 1
    tab = jnp.pad(table.astype(F32) * LOG2E, ((0, 0), (0, 0), (BIAS_PAD, LANES - BIAS_PAD - (2 * NA_WIN_COLS - 1))))
    n_h = table.shape[0]
    half = GRID_W // GROUPS
    perm = np.zeros((rows_q, GRID_W, TILE), np.float32)
    for k in range(rows_q):
        for ks in range(half):
            for kg in range(GROUPS):
                perm[k, ks * GROUPS + kg, kg * SUB + k * half + ks] = 1.0
    return pl.pallas_call(
        _bias_kernel,
        grid=(n_h,),
        in_specs=[pl.BlockSpec((1, n_r, LANES), lambda h: (h, 0, 0)),
                  pl.BlockSpec((rows_q, GRID_W, TILE), lambda h: (0, 0, 0))],
        out_specs=pl.BlockSpec((BIAS_VARIANTS, 1, TILE, WIN_TILES * TILE), lambda h: (0, h, 0, 0)),
        out_shape=jax.ShapeDtypeStruct((BIAS_VARIANTS, n_h, TILE, WIN_TILES * TILE), F32),
        scratch_shapes=[pltpu.VMEM((BIAS_VARIANTS, WIN_TILES * TILE // LANES, TILE, LANES), F32)],
        compiler_params=_cparams(("parallel",)),
        name="bias_tiles",
    )(tab, jnp.asarray(perm, BF16))


def _block_diag(w, n_chunks):
    per = LRU_BLOCKS // n_chunks
    w = w.reshape(2, n_chunks, per, LRU_BLOCK, LRU_BLOCK)
    eye = jnp.eye(per, dtype=w.dtype)
    out = jnp.einsum('dcpij,pq->dcpiqj', w, eye)
    return out.reshape(2, n_chunks, per * LRU_BLOCK, per * LRU_BLOCK)


def kernel(x, c, ctx, c_ctx, w_mod, b_mod, norm_mix, norm_ffn, w_in, w_out, q_gain, k_gain, na_bias,
           conv_w, conv_b, lru_w_r, lru_b_r, lru_w_i, lru_b_i, lru_lambda, router_w, router_b,
           exp_w1, exp_w3, exp_w2):
    cs = jnp.concatenate([c, c_ctx[None, :], jnp.zeros((MOD_ROWS - BATCH - 1, D_MODEL), F32)], axis=0)
    mod = _modulation(cs, w_mod, b_mod).reshape(DEPTH, MOD_ROWS, N_MOD, D_MODEL)

    head_of = np.arange(NA_WIDTH) // HEAD_DIM
    bd = jnp.asarray((head_of[:, None] == head_of[None, :]).astype(np.float32) / HEAD_DIM, BF16)
    tri = jnp.asarray(np.triu(np.ones((TILE, TILE), np.float32), 1), BF16)
    rwt = router_w.T
    rwh = rwt.astype(BF16)
    rwc = jnp.concatenate([rwh, (rwt - rwh.astype(F32)).astype(BF16)], axis=0)
    rb = router_b.reshape(N_EXPERTS, 1)
    n_cb = LRU_WIDTH // LRU_CH

    bias_tiles = _bias_tiles(na_bias.reshape(DEPTH * NA_HEADS, 2 * NA_WIN_ROWS - 1, 2 * NA_WIN_COLS - 1))
    streams = [{"xs": (x, ctx), "moe": None, "boff": sidx * SB} for sidx in range(STREAMS)]
    for l in range(DEPTH):
        qg = jnp.tile(q_gain[l] * (ATTN_SCALE * LOG2E), NA_HEADS)[None, :]
        kg = jnp.tile(k_gain[l], NA_HEADS)[None, :]
        w_in_l = w_in[l].astype(BF16)
        w_out_l = w_out[l].astype(BF16)
        wr = (0.5 * _block_diag(lru_w_r[l], n_cb)).astype(BF16)
        wi = (0.5 * _block_diag(lru_w_i[l], n_cb)).astype(BF16)
        lru = (conv_w[l], conv_b[l][None, :], wr, wi, 0.5 * lru_b_r[l], 0.5 * lru_b_i[l], lru_lambda[l])
        for st in streams:
            boff = st["boff"]
            xs, proj = _in_proj(l, boff, st["xs"], st["moe"], mod, norm_mix[l][None, :], w_in_l, bd, qg, kg)
            ya, hc = _attention(l, proj, bias_tiles, lru)
            xs, hp, route = _out_proj(l, boff, xs, ya, hc, proj, lru, mod, norm_ffn[l][None, :], w_out_l, rwc, rb)
            st["xs"] = xs
            st["moe"] = _moe(l, hp, route.reshape(N_TOK_TILES, ROUTE_ROWS, TILE), tri, exp_w1, exp_w3, exp_w2)
    return _final(streams, mod)
```
